```python
import jax
import jax.numpy as jnp
from jax import lax
import numpy as np

D_MODEL = 2048
BATCH = 4
SEQ = 4096
DEPTH = 4
DEC_BATCH = 32
DEC_SEQ = 64
PAST_LEN = 2048

CHUNK = 64
Q_BLOCK = 128
EPS = 1e-6
MLSTM_WIDTH = D_MODEL // 2
M_HEADS = 4
M_DV = MLSTM_WIDTH // M_HEADS
M_DK = M_DV // 2
MLA_WIDTH = D_MODEL - MLSTM_WIDTH
A_HEADS = 8
A_VDIM = MLA_WIDTH // A_HEADS
A_NOPE = 128
A_ROPE = 64
KV_LORA = 256
Q_LORA = 512
ROPE_THETA = 10000.0
SM_SCALE = (A_NOPE + A_ROPE) ** -0.5
MIX_WIDTH = MLSTM_WIDTH + MLA_WIDTH
D_FF = 5632
N_MOD = 9
OFF_MQ = 0
OFF_MK = OFF_MQ + M_HEADS * M_DK
OFF_MV = OFF_MK + M_HEADS * M_DK
OFF_MO = OFF_MV + M_HEADS * M_DV
OFF_MI = OFF_MO + M_HEADS * M_DV
OFF_MF = OFF_MI + M_HEADS
OFF_QA = OFF_MF + M_HEADS
OFF_KVA = OFF_QA + Q_LORA
OFF_PE = OFF_KVA + KV_LORA
IN_WIDTH = OFF_PE + A_ROPE

kernel_name = 'hymba_mlstm_mla_macaron_stream_step'


def rmsnorm(x, g):
    xf = x.astype(jnp.float32)
    y = xf * lax.rsqrt(jnp.mean(xf * xf, axis=-1, keepdims=True) + EPS)
    return (y * g.astype(jnp.float32)).astype(x.dtype)


def rope(x, pos):
    half = x.shape[-1] // 2
    freqs = ROPE_THETA ** (-jnp.arange(half, dtype=jnp.float32) / half)
    ang = pos.astype(jnp.float32)[:, None] * freqs[None, :]
    shape = (1, ang.shape[0]) + (1,) * (x.ndim - 3) + (half,)
    cos = jnp.cos(ang).reshape(shape)
    sin = jnp.sin(ang).reshape(shape)
    xf = x.astype(jnp.float32)
    x1, x2 = xf[..., :half], xf[..., half:]
    return jnp.concatenate([x1 * cos - x2 * sin, x1 * sin + x2 * cos], axis=-1).astype(x.dtype)


def swiglu(h, w_in, w_out):
    g, u = jnp.split(h @ w_in, 2, axis=-1)
    return (jax.nn.silu(g) * u) @ w_out


def mlstm_chunk(carry, inp):
    C, n, m = carry
    q, k, v, ig, lf = inp
    L = q.shape[2]
    b = jnp.cumsum(lf, axis=-1)
    causal = jnp.tril(jnp.ones((L, L), dtype=bool))
    dmat = jnp.where(causal, b[..., :, None] - b[..., None, :] + ig[..., None, :], -jnp.inf)
    g = b + m[..., None]
    m_t = jnp.maximum(g, jnp.max(dmat, axis=-1))
    w_intra = jnp.exp(dmat - m_t[..., None])
    w_inter = jnp.exp(g - m_t)
    s = jnp.einsum('bhtd,bhsd->bhts', q, k) * w_intra
    num = jnp.einsum('bhts,bhsv->bhtv', s, v) + w_inter[..., None] * jnp.einsum('bhvd,bhtd->bhtv', C, q)
    nq = jnp.sum(s, axis=-1) + w_inter * jnp.einsum('bhd,bhtd->bht', n, q)
    h = num / jnp.maximum(jnp.abs(nq), jnp.exp(-m_t))[..., None]
    m_new = m_t[..., -1]
    wa = jnp.exp(b[..., -1:] - b + ig - m_new[..., None])
    decay = jnp.exp(b[..., -1] + m - m_new)
    C_new = decay[..., None, None] * C + jnp.einsum('bhs,bhsv,bhsd->bhvd', wa, v, k)
    n_new = decay[..., None] * n + jnp.einsum('bhs,bhsd->bhd', wa, k)
    return (C_new, n_new, m_new), h


def mla_attend(q_lat, q_pe, ckv, kpe, mask):
    ckv32 = ckv.astype(jnp.float32)
    s = (jnp.einsum('bqhr,bkr->bhqk', q_lat.astype(jnp.float32), ckv32)
         + jnp.einsum('bqhp,bkp->bhqk', q_pe.astype(jnp.float32), kpe.astype(jnp.float32))) * SM_SCALE
    if mask is not None:
        s = jnp.where(mask[None, None], s, -jnp.inf)
    pr = jax.nn.softmax(s, axis=-1)
    return jnp.einsum('bhqk,bkr->bqhr', pr, ckv32)


def token_mix(h, pos, C0, n0, m0, ckv_past, kpe_past, p, l):
    B, T, _ = h.shape
    z = h @ p['w_in'][l]
    q = z[..., OFF_MQ:OFF_MK].reshape(B, T, M_HEADS, M_DK).astype(jnp.float32)
    k = z[..., OFF_MK:OFF_MV].reshape(B, T, M_HEADS, M_DK).astype(jnp.float32) * (M_DK ** -0.5)
    v = z[..., OFF_MV:OFF_MO].reshape(B, T, M_HEADS, M_DV).astype(jnp.float32)
    og = jax.nn.sigmoid(z[..., OFF_MO:OFF_MI].astype(jnp.float32))
    ig = z[..., OFF_MI:OFF_MF].astype(jnp.float32) + p['mlstm_b_i'][l].astype(jnp.float32)
    lf = jax.nn.log_sigmoid(z[..., OFF_MF:OFF_QA].astype(jnp.float32) + p['mlstm_b_f'][l].astype(jnp.float32))
    L = min(T, CHUNK)
    nC = T // L

    def to_chunks(a):
        a = a.reshape((B, nC, L) + a.shape[2:])
        return jnp.moveaxis(jnp.moveaxis(a, 1, 0), 3, 2)

    carry0 = (C0.astype(jnp.float32), n0.astype(jnp.float32), m0.astype(jnp.float32))
    (C, n, m), hs = lax.scan(mlstm_chunk, carry0, (to_chunks(q), to_chunks(k), to_chunks(v), to_chunks(ig), to_chunks(lf)))
    hs = jnp.moveaxis(jnp.moveaxis(hs, 2, 3), 0, 1).reshape(B, T, M_HEADS, M_DV)
    hm = hs * og.reshape(B, T, M_HEADS, M_DV)
    hm = hm * lax.rsqrt(jnp.mean(hm * hm, axis=-1, keepdims=True) + EPS)
    hm = (hm * p['mlstm_norm'][l].astype(jnp.float32).reshape(M_HEADS, M_DV)).reshape(B, T, MLSTM_WIDTH).astype(h.dtype)
    cq = rmsnorm(z[..., OFF_QA:OFF_KVA], p['q_norm'][l])
    qa = (cq @ p['w_uq'][l]).reshape(B, T, A_HEADS, A_NOPE + A_ROPE)
    q_nope = qa[..., :A_NOPE]
    q_pe = rope(qa[..., A_NOPE:], pos)
    q_lat = jnp.einsum('bthn,rhn->bthr', q_nope, p['w_uk'][l])
    ckv_new = rmsnorm(z[..., OFF_KVA:OFF_PE], p['kv_norm'][l])
    kpe_new = rope(z[..., OFF_PE:IN_WIDTH], pos)
    if ckv_past is None:
        nB = T // Q_BLOCK
        kchunk = jnp.arange(T) // CHUNK

        def attend_block(args):
            ql, qp, bi = args
            qchunk = (bi * Q_BLOCK + jnp.arange(Q_BLOCK)) // CHUNK
            mask = kchunk[None, :] <= qchunk[:, None]
            return mla_attend(ql, qp, ckv_new, kpe_new, mask)

        def to_blocks(a):
            return jnp.moveaxis(a.reshape((B, nB, Q_BLOCK) + a.shape[2:]), 1, 0)

        o_lat = lax.map(attend_block, (to_blocks(q_lat), to_blocks(q_pe), jnp.arange(nB)))
        o_lat = jnp.moveaxis(o_lat, 0, 1).reshape(B, T, A_HEADS, KV_LORA)
    else:
        ckv_all = jnp.concatenate([ckv_past.astype(ckv_new.dtype), ckv_new], axis=1)
        kpe_all = jnp.concatenate([kpe_past.astype(kpe_new.dtype), kpe_new], axis=1)
        o_lat = mla_attend(q_lat, q_pe, ckv_all, kpe_all, None)
    o = jnp.einsum('bthr,rhv->bthv', o_lat.astype(h.dtype), p['w_uv'][l]).reshape(B, T, MLA_WIDTH)
    y = jnp.concatenate([hm, o], axis=-1) @ p['w_out'][l]
    return y, C, n, m, ckv_new, kpe_new


def trunk(x, c, pos, C_in, n_in, m_in, ckv_past, kpe_past, p):
    B, _, D = x.shape
    ckvs, kpes, Cs, ns, ms = [], [], [], [], []
    for l in range(DEPTH):
        mod = (jax.nn.silu(c) @ p['mod_w'][l] + p['mod_b'][l]).reshape(B, N_MOD, 1, D)
        sh1, sc1, g1, sh2, sc2, g2, sh3, sc3, g3 = [mod[:, i] for i in range(N_MOD)]
        h = rmsnorm(x, p['ln_ffn1'][l]) * (1 + sc1) + sh1
        x = x + 0.5 * g1 * swiglu(h, p['ffn1_w_in'][l], p['ffn1_w_out'][l])
        h = rmsnorm(x, p['ln_mix'][l]) * (1 + sc2) + sh2
        y, C, n, m, ckv, kpe = token_mix(h, pos, C_in[l], n_in[l], m_in[l],
                                         None if ckv_past is None else ckv_past[l],
                                         None if kpe_past is None else kpe_past[l], p, l)
        x = x + g2 * y
        h = rmsnorm(x, p['ln_ffn2'][l]) * (1 + sc3) + sh3
        x = x + 0.5 * g3 * swiglu(h, p['ffn2_w_in'][l], p['ffn2_w_out'][l])
        ckvs.append(ckv)
        kpes.append(kpe)
        Cs.append(C)
        ns.append(n)
        ms.append(m)
    x = rmsnorm(x, p['final_norm'])
    return x, jnp.stack(ckvs), jnp.stack(kpes), jnp.stack(Cs), jnp.stack(ns), jnp.stack(ms)


def setup_inputs(seed: int = 0) -> dict:
    key = jax.random.key(seed)
    ks = jax.random.split(key, 32)

    def nrm(k, shape, scale):
        return jax.random.normal(k, shape, dtype=jnp.float32) * scale

    D = D_MODEL
    return {
        'x_prompt': nrm(ks[0], (BATCH, SEQ, D), 1.0),
        'x_sample': nrm(ks[1], (DEC_BATCH, DEC_SEQ, D), 1.0),
        'c_prompt': nrm(ks[2], (BATCH, D), 1.0),
        'c_sample': nrm(ks[3], (DEC_BATCH, D), 1.0),
        'cache_ckv': nrm(ks[4], (DEPTH, DEC_BATCH, PAST_LEN, KV_LORA), 1.0),
        'cache_kpe': nrm(ks[5], (DEPTH, DEC_BATCH, PAST_LEN, A_ROPE), 1.0),
        'state_C': nrm(ks[6], (DEPTH, DEC_BATCH, M_HEADS, M_DV, M_DK), 0.1),
        'state_n': nrm(ks[7], (DEPTH, DEC_BATCH, M_HEADS, M_DK), 0.1),
        'state_m': nrm(ks[8], (DEPTH, DEC_BATCH, M_HEADS), 0.5),
        'mod_w': nrm(ks[9], (DEPTH, D, N_MOD * D), 0.5 * D ** -0.5),
        'mod_b': nrm(ks[10], (DEPTH, N_MOD * D), 0.02),
        'ln_ffn1': 1.0 + nrm(ks[11], (DEPTH, D), 0.02),
        'ffn1_w_in': nrm(ks[12], (DEPTH, D, 2 * D_FF), D ** -0.5),
        'ffn1_w_out': nrm(ks[13], (DEPTH, D_FF, D), D_FF ** -0.5),
        'ln_mix': 1.0 + nrm(ks[14], (DEPTH, D), 0.02),
        'w_in': nrm(ks[15], (DEPTH, D, IN_WIDTH), D ** -0.5),
        'mlstm_b_i': nrm(ks[16], (DEPTH, M_HEADS), 0.1),
        'mlstm_b_f': 3.0 + nrm(ks[17], (DEPTH, M_HEADS), 0.5),
        'mlstm_norm': 1.0 + nrm(ks[18], (DEPTH, MLSTM_WIDTH), 0.02),
        'q_norm': 1.0 + nrm(ks[19], (DEPTH, Q_LORA), 0.02),
        'w_uq': nrm(ks[20], (DEPTH, Q_LORA, A_HEADS * (A_NOPE + A_ROPE)), Q_LORA ** -0.5),
        'kv_norm': 1.0 + nrm(ks[21], (DEPTH, KV_LORA), 0.02),
        'w_uk': nrm(ks[22], (DEPTH, KV_LORA, A_HEADS, A_NOPE), KV_LORA ** -0.5),
        'w_uv': nrm(ks[23], (DEPTH, KV_LORA, A_HEADS, A_VDIM), KV_LORA ** -0.5),
        'w_out': nrm(ks[24], (DEPTH, MIX_WIDTH, D), MIX_WIDTH ** -0.5),
        'ln_ffn2': 1.0 + nrm(ks[25], (DEPTH, D), 0.02),
        'ffn2_w_in': nrm(ks[26], (DEPTH, D, 2 * D_FF), D ** -0.5),
        'ffn2_w_out': nrm(ks[27], (DEPTH, D_FF, D), D_FF ** -0.5),
        'final_norm': 1.0 + nrm(ks[28], (D,), 0.02),
    }


def reference(x_prompt, x_sample, c_prompt, c_sample, cache_ckv, cache_kpe, state_C, state_n, state_m,
              mod_w, mod_b, ln_ffn1, ffn1_w_in, ffn1_w_out, ln_mix, w_in, mlstm_b_i, mlstm_b_f, mlstm_norm,
              q_norm, w_uq, kv_norm, w_uk, w_uv, w_out, ln_ffn2, ffn2_w_in, ffn2_w_out, final_norm):
    p = {'mod_w': mod_w, 'mod_b': mod_b, 'ln_ffn1': ln_ffn1, 'ffn1_w_in': ffn1_w_in, 'ffn1_w_out': ffn1_w_out,
         'ln_mix': ln_mix, 'w_in': w_in, 'mlstm_b_i': mlstm_b_i, 'mlstm_b_f': mlstm_b_f, 'mlstm_norm': mlstm_norm,
         'q_norm': q_norm, 'w_uq': w_uq, 'kv_norm': kv_norm, 'w_uk': w_uk, 'w_uv': w_uv, 'w_out': w_out,
         'ln_ffn2': ln_ffn2, 'ffn2_w_in': ffn2_w_in, 'ffn2_w_out': ffn2_w_out, 'final_norm': final_norm}
    Bp, Sp, _ = x_prompt.shape
    _, Ss, _ = x_sample.shape
    Past = cache_ckv.shape[2]
    C0 = jnp.zeros((DEPTH, Bp, M_HEADS, M_DV, M_DK), jnp.float32)
    n0 = jnp.zeros((DEPTH, Bp, M_HEADS, M_DK), jnp.float32)
    m0 = jnp.zeros((DEPTH, Bp, M_HEADS), jnp.float32)
    pos_p = jnp.arange(Sp)
    y_prompt, p_ckv, p_kpe, p_C, p_n, p_m = trunk(x_prompt, c_prompt, pos_p, C0, n0, m0, None, None, p)
    pos_s = Past + jnp.arange(Ss)
    y_sample, s_ckv, s_kpe, s_C, s_n, s_m = trunk(x_sample, c_sample, pos_s, state_C, state_n, state_m,
                                                  cache_ckv, cache_kpe, p)
    return (y_prompt, y_sample, p_ckv, p_kpe, p_C, p_n, p_m, s_ckv, s_kpe, s_C, s_n, s_m)
```

```python
import functools
import math

import jax
import jax.numpy as jnp
from jax import lax
from jax.experimental import pallas as pl
from jax.experimental.pallas import tpu as pltpu

CHUNK = 64
EPS = 1e-6
ROPE_THETA = 10000.0
N_MOD = 9
LANES = 128
VMEM_LIMIT = 56 * 1024 * 1024

BF = jnp.bfloat16
F32 = jnp.float32


def _cparams(*sem):
    return pltpu.CompilerParams(dimension_semantics=sem, vmem_limit_bytes=VMEM_LIMIT)


def _dot(a, b):
    return jnp.dot(a, b, preferred_element_type=F32)


def _dot_nt(a, b):
    return lax.dot_general(a, b, (((1,), (1,)), ((), ())), preferred_element_type=F32)


def _dot_tn(a, b):
    return lax.dot_general(a, b, (((0,), (0,)), ((), ())), preferred_element_type=F32)


def _rms(x, g):
    ms = jnp.mean(x * x, axis=-1, keepdims=True)
    return x * lax.rsqrt(ms + EPS) * g


def _silu(x):
    return x * jax.nn.sigmoid(x)


def _rope128(pe, cos, sin):
    lane = lax.broadcasted_iota(jnp.int32, pe.shape, 1)
    first_half = jnp.bitwise_and(lane, 63) < 32
    swapped = jnp.where(first_half, pltpu.roll(pe, 96, 1), pltpu.roll(pe, 32, 1))
    return pe * cos + swapped * sin


def _mod_kernel(c_ref, w_ref, b_ref, o_ref):
    a = _silu(c_ref[...]).astype(BF)
    o_ref[...] = _dot(a, w_ref[...].astype(BF)) + b_ref[...]


def _mod_call(cg, mod_w, mod_b):
    depth, d, nd = mod_w.shape
    g = cg.shape[0]
    tn = 1024
    return pl.pallas_call(
        _mod_kernel,
        grid=(depth, nd // tn),
        in_specs=[
            pl.BlockSpec((g, d), lambda l, j: (0, 0)),
            pl.BlockSpec((None, d, tn), lambda l, j: (l, 0, j)),
            pl.BlockSpec((None, 1, tn), lambda l, j: (l, 0, j)),
        ],
        out_specs=pl.BlockSpec((None, g, tn), lambda l, j: (l, 0, j)),
        out_shape=jax.ShapeDtypeStruct((depth, g, nd), F32),
        compiler_params=_cparams("parallel", "parallel"),
        name="adaln_mod",
    )(cg, mod_w, mod_b.reshape(depth, 1, nd))


def _norm_mod_to(h_ref, x_ref, ln_ref, sh_ref, sc_ref):
    ln = ln_ref[...]
    for g in range(x_ref.shape[0] // CHUNK):
        rows = pl.ds(g * CHUNK, CHUNK)
        y = _rms(x_ref[rows, :], ln)
        h_ref[rows, :] = (y * (1.0 + sc_ref[g:g + 1, :]) + sh_ref[g:g + 1, :]).astype(h_ref.dtype)


def _ffn_kernel(x_ref, sh_ref, sc_ref, gt_ref, ln_ref, wg_ref, wu_ref, wo_ref, o_ref, h_sc):
    j = pl.program_id(1)
    nj = pl.num_programs(1)

    @pl.when(j == 0)
    def _():
        _norm_mod_to(h_sc, x_ref, ln_ref, sh_ref, sc_ref)

    h = h_sc[...]
    a = (_silu(_dot(h, wg_ref[...])) * _dot(h, wu_ref[...])).astype(BF)
    y = _dot(a, wo_ref[...])

    @pl.when(j == 0)
    def _():
        o_ref[...] = y

    @pl.when(j > 0)
    def _():
        o_ref[...] += y

    @pl.when(j == nj - 1)
    def _():
        for g in range(x_ref.shape[0] // CHUNK):
            rows = pl.ds(g * CHUNK, CHUNK)
            o_ref[rows, :] = x_ref[rows, :] + (0.5 * gt_ref[g:g + 1, :]) * o_ref[rows, :]


def _ffn_call(x, mod, ln, w_in, w_out, layer, k0, tm, tf):
    m, d = x.shape
    f = w_out.shape[1]
    ng = tm // CHUNK
    nf = f // tf

    def mod_spec(k):
        return pl.BlockSpec((None, ng, d), lambda i, j: (layer, i, k))

    return pl.pallas_call(
        _ffn_kernel,
        grid=(m // tm, nf),
        in_specs=[
            pl.BlockSpec((tm, d), lambda i, j: (i, 0)),
            mod_spec(k0), mod_spec(k0 + 1), mod_spec(k0 + 2),
            pl.BlockSpec((None, 1, d), lambda i, j: (layer, 0, 0)),
            pl.BlockSpec((None, d, tf), lambda i, j: (layer, 0, j)),
            pl.BlockSpec((None, d, tf), lambda i, j: (layer, 0, j + nf)),
            pl.BlockSpec((None, tf, d), lambda i, j: (layer, j, 0)),
        ],
        out_specs=pl.BlockSpec((tm, d), lambda i, j: (i, 0)),
        out_shape=jax.ShapeDtypeStruct((m, d), F32),
        scratch_shapes=[pltpu.VMEM((tm, d), BF)],
        compiler_params=_cparams("parallel", "arbitrary"),
        name="ffn",
    )(x, mod, mod, mod, ln, w_in, w_in, w_out)


def _inproj_kernel(x_ref, sh_ref, sc_ref, ln_ref, wqk_ref, wv_ref, wo_ref, wa_ref, qn_ref, kvn_ref, gb_ref,
                   cos_ref, sin_ref, q_ref, k_ref, v_ref, og_ref, cq_ref, ckv_ref, kpe_ref, gt_ref, h_sc,
                   *, n_heads, k_scale):
    _norm_mod_to(h_sc, x_ref, ln_ref, sh_ref, sc_ref)
    h = h_sc[...]
    hdk = q_ref.shape[1]
    zqk = _dot(h, wqk_ref[...])
    q_ref[...] = zqk[:, :hdk].astype(BF)
    k_ref[...] = (zqk[:, hdk:] * k_scale).astype(BF)
    v_ref[...] = _dot(h, wv_ref[...]).astype(BF)
    og_ref[...] = jax.nn.sigmoid(_dot(h, wo_ref[...]))
    za = _dot(h, wa_ref[...])
    ql = cq_ref.shape[1]
    kvl = ckv_ref.shape[1]
    cq_ref[...] = _rms(za[:, :ql], qn_ref[...]).astype(BF)
    ckv_ref[...] = _rms(za[:, ql:ql + kvl], kvn_ref[...])
    pe = _rope128(za[:, ql + kvl:ql + kvl + LANES], cos_ref[...], sin_ref[...])
    kpe_ref[...] = pe[:, :kpe_ref.shape[1]]
    zg = za[:, ql + kvl + LANES:] + gb_ref[...]
    lane = lax.broadcasted_iota(jnp.int32, zg.shape, 1)
    log_sig = jnp.minimum(zg, 0.0) - jnp.log1p(jnp.exp(-jnp.abs(zg)))
    gt_ref[...] = jnp.where(lane < n_heads, zg, log_sig)


def _inproj_call(x, mod, ln, wqk, wv, wo, wa, q_norm, kv_norm, gate_bias, cos, sin, layer, tm, dims):
    m, d = x.shape
    ng = tm // CHUNK
    hdk, hdv, ql, kvl, rope = dims["hdk"], dims["hdv"], dims["ql"], dims["kvl"], dims["rope"]

    def mod_spec(k):
        return pl.BlockSpec((None, ng, d), lambda i: (layer, i, k))

    def wspec(w):
        return pl.BlockSpec((None,) + w.shape[1:], lambda i: (layer, 0, 0))

    def row(n):
        return pl.BlockSpec((tm, n), lambda i: (i, 0))

    kern = functools.partial(_inproj_kernel, n_heads=dims["mh"], k_scale=dims["dk"] ** -0.5)
    return pl.pallas_call(
        kern,
        grid=(m // tm,),
        in_specs=[
            row(d), mod_spec(3), mod_spec(4), wspec(ln), wspec(wqk), wspec(wv), wspec(wo), wspec(wa),
            wspec(q_norm), wspec(kv_norm), wspec(gate_bias), row(LANES), row(LANES),
        ],
        out_specs=[row(hdk), row(hdk), row(hdv), row(hdv), row(ql), row(kvl), row(rope), row(LANES)],
        out_shape=[
            jax.ShapeDtypeStruct((m, hdk), BF), jax.ShapeDtypeStruct((m, hdk), BF),
            jax.ShapeDtypeStruct((m, hdv), BF), jax.ShapeDtypeStruct((m, hdv), F32),
            jax.ShapeDtypeStruct((m, ql), BF), jax.ShapeDtypeStruct((m, kvl), F32),
            jax.ShapeDtypeStruct((m, rope), F32), jax.ShapeDtypeStruct((m, LANES), F32),
        ],
        scratch_shapes=[pltpu.VMEM((tm, d), BF)],
        compiler_params=_cparams("parallel"),
        name="in_proj",
    )(x, mod, mod, ln, wqk, wv, wo, wa, q_norm, kv_norm, gate_bias, cos, sin)


def _mlstm_kernel(q_ref, k_ref, v_ref, og_ref, gc_ref, gr_ref, c0_ref, n0_ref, m0_ref, nrm_ref,
                  hm_ref, c_ref, n_ref, m_ref, *, n_heads):
    c_idx = pl.program_id(1)

    @pl.when(c_idx == 0)
    def _():
        c_ref[...] = c0_ref[...]
        n_ref[...] = n0_ref[...]
        m_ref[...] = m0_ref[...]

    L = q_ref.shape[0]
    dk = q_ref.shape[1] // n_heads
    dv = v_ref.shape[1] // n_heads
    t_idx = lax.broadcasted_iota(jnp.int32, (L, L), 0)
    s_idx = lax.broadcasted_iota(jnp.int32, (L, L), 1)
    causal = s_idx <= t_idx
    for h in range(n_heads):
        qh = q_ref[:, h * dk:(h + 1) * dk]
        kh = k_ref[:, h * dk:(h + 1) * dk]
        vh = v_ref[:, h * dv:(h + 1) * dv]
        ig_c = gc_ref[:, h:h + 1]
        lf_c = gc_ref[:, n_heads + h:n_heads + h + 1]
        ig_r = gr_ref[h:h + 1, :]
        lf_r = gr_ref[n_heads + h:n_heads + h + 1, :]
        b_c = jnp.sum(jnp.where(causal, lf_r, 0.0), axis=1, keepdims=True)
        b_r = jnp.sum(jnp.where(causal, 0.0, lf_c), axis=0, keepdims=True) + lf_r
        b_last = b_c[L - 1:L, :]
        m_prev = m_ref[h:h + 1, 0:1]
        dmat = jnp.where(causal, b_c - b_r + ig_r, -jnp.inf)
        g_c = b_c + m_prev
        m_t = jnp.maximum(g_c, jnp.max(dmat, axis=1, keepdims=True))
        w_intra = jnp.exp(dmat - m_t)
        w_inter = jnp.exp(g_c - m_t)
        s = _dot_nt(qh, kh) * w_intra
        c_prev = c_ref[h]
        n_prev = n_ref[h:h + 1, :]
        num = _dot(s.astype(BF), vh) + w_inter * _dot_nt(qh, c_prev.astype(BF))
        nq = jnp.sum(s, axis=1, keepdims=True) + w_inter * jnp.sum(qh.astype(F32) * n_prev, axis=1, keepdims=True)
        hh = num / jnp.maximum(jnp.abs(nq), jnp.exp(-m_t))
        m_new = m_t[L - 1:L, :]
        wa = jnp.exp(b_last - b_c + ig_c - m_new)
        decay = jnp.exp(b_last + m_prev - m_new)
        wk = wa * kh.astype(F32)
        c_ref[h] = decay * c_prev + _dot_tn(vh, wk.astype(BF))
        n_ref[h:h + 1, :] = decay * n_prev + jnp.sum(wk, axis=0, keepdims=True)
        m_ref[h:h + 1, :] = jnp.broadcast_to(m_new, (1, m_ref.shape[1]))
        x = hh * og_ref[:, h * dv:(h + 1) * dv]
        x = x * lax.rsqrt(jnp.mean(x * x, axis=-1, keepdims=True) + EPS)
        hm_ref[:, h * dv:(h + 1) * dv] = (x * nrm_ref[:, h * dv:(h + 1) * dv]).astype(hm_ref.dtype)


def _mlstm_call(q, k, v, og, gates_c, gates_r, c0, n0, m0, norm, layer, row0, n_seq, n_chunks, n_heads):
    m, hdk = q.shape
    hdv = v.shape[1]
    dv, dk = c0.shape[-2:]
    g2 = gates_r.shape[1]

    def tok(n):
        return pl.BlockSpec((CHUNK, n), lambda b, c: (row0 + b * n_chunks + c, 0))

    def state(shape):
        return pl.BlockSpec((None,) + shape, lambda b, c: (b,) + (0,) * len(shape))

    kern = functools.partial(_mlstm_kernel, n_heads=n_heads)
    return pl.pallas_call(
        kern,
        grid=(n_seq, n_chunks),
        in_specs=[
            tok(hdk), tok(hdk), tok(hdv), tok(hdv), tok(LANES),
            pl.BlockSpec((None, g2, CHUNK), lambda b, c: (row0 + b * n_chunks + c, 0, 0)),
            state((n_heads, dv, dk)), state((n_heads, dk)), state((n_heads, LANES)),
            pl.BlockSpec((None, 1, hdv), lambda b, c: (layer, 0, 0)),
        ],
        out_specs=[
            pl.BlockSpec((CHUNK, hdv), lambda b, c: (b * n_chunks + c, 0)),
            state((n_heads, dv, dk)), state((n_heads, dk)), state((n_heads, LANES)),
        ],
        out_shape=[
            jax.ShapeDtypeStruct((n_seq * n_chunks * CHUNK, hdv), BF),
            jax.ShapeDtypeStruct((n_seq, n_heads, dv, dk), F32),
            jax.ShapeDtypeStruct((n_seq, n_heads, dk), F32),
            jax.ShapeDtypeStruct((n_seq, n_heads, LANES), F32),
        ],
        compiler_params=_cparams("parallel", "arbitrary"),
        name="mlstm",
    )(q, k, v, og, gates_c, gates_r, c0, n0, m0, norm)


def _build_queries(q_sc, cq_ref, wuq_ref, wuk_ref, cos_ref, sin_ref, n_heads, nope, rope, kvl):
    tq = cq_ref.shape[0]
    qa = _dot(cq_ref[...], wuq_ref[...])
    cos = cos_ref[...]
    sin = sin_ref[...]
    for h in range(n_heads):
        qn = qa[:, h * nope:(h + 1) * nope].astype(BF)
        q_sc[h * tq:(h + 1) * tq, 0:kvl] = _dot(qn, wuk_ref[h]).astype(BF)
    per = LANES // rope
    for p in range(n_heads // per):
        base = n_heads * nope + p * LANES
        pe = _rope128(qa[:, base:base + LANES], cos, sin)
        for u in range(per):
            h = p * per + u
            q_sc[h * tq:(h + 1) * tq, kvl:kvl + rope] = pe[:, u * rope:(u + 1) * rope].astype(BF)


def _flash_step(q_sc, kblk, m_sc, l_sc, acc_sc, kvl, sm_scale, mask):
    s = _dot_nt(q_sc[...], kblk) * sm_scale
    if mask is not None:
        s = jnp.where(mask, s, -jnp.inf)
    m_prev = m_sc[...]
    m_new = jnp.maximum(m_prev, jnp.max(s, axis=1, keepdims=True))
    alpha = jnp.exp(m_prev - m_new)
    p = jnp.exp(s - m_new)
    l_sc[...] = alpha * l_sc[...] + jnp.sum(p, axis=1, keepdims=True)
    acc_sc[...] = alpha * acc_sc[...] + _dot(p.astype(BF), kblk[:, :kvl])
    m_sc[...] = m_new


def _flash_init(m_sc, l_sc, acc_sc):
    m_sc[...] = jnp.full(m_sc.shape, -jnp.inf, F32)
    l_sc[...] = jnp.zeros(l_sc.shape, F32)
    acc_sc[...] = jnp.zeros(acc_sc.shape, F32)


def _flash_finish(o_ref, l_sc, acc_sc, wuv_ref, n_heads, tq):
    vd = wuv_ref.shape[2]
    for h in range(n_heads):
        rows = pl.ds(h * tq, tq)
        o_lat = (acc_sc[rows, :] / l_sc[rows, :]).astype(BF)
        o_ref[:, h * vd:(h + 1) * vd] = _dot(o_lat, wuv_ref[h]).astype(o_ref.dtype)


def _attn_prompt_kernel(cq_ref, ckv_ref, kpe_ref, wuq_ref, wuk_ref, wuv_ref, cos_ref, sin_ref, o_ref,
                        kcat, q_sc, m_sc, l_sc, acc_sc, *, n_heads, nope, rope, sm_scale):
    i = pl.program_id(1)
    tq = cq_ref.shape[0]
    kvl = ckv_ref.shape[1]

    @pl.when(i == 0)
    def _():
        kcat[:, 0:kvl] = ckv_ref[...].astype(BF)
        kcat[:, kvl:kvl + rope] = kpe_ref[...].astype(BF)

    _build_queries(q_sc, cq_ref, wuq_ref, wuk_ref, cos_ref, sin_ref, n_heads, nope, rope, kvl)
    _flash_init(m_sc, l_sc, acc_sc)

    def body(j, carry):
        kblk = kcat[pl.ds(pl.multiple_of(j * tq, tq), tq), :]
        _flash_step(q_sc, kblk, m_sc, l_sc, acc_sc, kvl, sm_scale, None)
        return carry

    lax.fori_loop(0, i, body, 0)

    shape = (n_heads * tq, tq)
    chunk_bits = CHUNK.bit_length() - 1
    q_chunk = jnp.bitwise_and(jnp.right_shift(lax.broadcasted_iota(jnp.int32, shape, 0), chunk_bits), tq // CHUNK - 1)
    k_chunk = jnp.right_shift(lax.broadcasted_iota(jnp.int32, shape, 1), chunk_bits)
    kblk = kcat[pl.ds(pl.multiple_of(i * tq, tq), tq), :]
    _flash_step(q_sc, kblk, m_sc, l_sc, acc_sc, kvl, sm_scale, k_chunk <= q_chunk)
    _flash_finish(o_ref, l_sc, acc_sc, wuv_ref, n_heads, tq)


def _attn_prompt_call(cq, ckv, kpe, wuq, wuk, wuv, cos, sin, layer, n_seq, seq, tq, dims):
    ah, nope, rope, kvl, vd = dims["ah"], dims["nope"], dims["rope"], dims["kvl"], dims["vd"]
    ql = cq.shape[1]
    nq = seq // tq
    assert tq % CHUNK == 0 and (tq // CHUNK) & (tq // CHUNK - 1) == 0
    kern = functools.partial(_attn_prompt_kernel, n_heads=ah, nope=nope, rope=rope,
                             sm_scale=(nope + rope) ** -0.5)

    def wspec(w):
        return pl.BlockSpec((None,) + w.shape[1:], lambda b, i: (layer,) + (0,) * (w.ndim - 1))

    return pl.pallas_call(
        kern,
        grid=(n_seq, nq),
        in_specs=[
            pl.BlockSpec((tq, ql), lambda b, i: (b * nq + i, 0)),
            pl.BlockSpec((seq, kvl), lambda b, i: (b, 0)),
            pl.BlockSpec((seq, rope), lambda b, i: (b, 0)),
            wspec(wuq), wspec(wuk), wspec(wuv),
            pl.BlockSpec((tq, LANES), lambda b, i: (b * nq + i, 0)),
            pl.BlockSpec((tq, LANES), lambda b, i: (b * nq + i, 0)),
        ],
        out_specs=pl.BlockSpec((tq, ah * vd), lambda b, i: (b * nq + i, 0)),
        out_shape=jax.ShapeDtypeStruct((n_seq * seq, ah * vd), BF),
        scratch_shapes=[
            pltpu.VMEM((seq, kvl + rope), BF),
            pltpu.VMEM((ah * tq, kvl + rope), BF),
            pltpu.VMEM((ah * tq, 1), F32),
            pltpu.VMEM((ah * tq, 1), F32),
            pltpu.VMEM((ah * tq, kvl), F32),
        ],
        compiler_params=_cparams("parallel", "arbitrary"),
        name="mla_prompt",
    )(cq, ckv, kpe, wuq, wuk, wuv, cos, sin)


def _attn_sample_kernel(cq_ref, ckv_ref, kpe_ref, pckv_ref, pkpe_ref, wuq_ref, wuk_ref, wuv_ref, cos_ref, sin_ref,
                        o_ref, kcat, q_sc, m_sc, l_sc, acc_sc, *, n_heads, nope, rope, sm_scale, kb):
    tq = cq_ref.shape[0]
    kvl = ckv_ref.shape[1]
    past = pckv_ref.shape[0]
    kcat[0:past, 0:kvl] = pckv_ref[...].astype(BF)
    kcat[0:past, kvl:kvl + rope] = pkpe_ref[...].astype(BF)
    kcat[past:past + tq, 0:kvl] = ckv_ref[...].astype(BF)
    kcat[past:past + tq, kvl:kvl + rope] = kpe_ref[...].astype(BF)
    _build_queries(q_sc, cq_ref, wuq_ref, wuk_ref, cos_ref, sin_ref, n_heads, nope, rope, kvl)
    _flash_init(m_sc, l_sc, acc_sc)
    for j in range(past // kb):
        _flash_step(q_sc, kcat[j * kb:(j + 1) * kb, :], m_sc, l_sc, acc_sc, kvl, sm_scale, None)
    _flash_step(q_sc, kcat[past:past + tq, :], m_sc, l_sc, acc_sc, kvl, sm_scale, None)
    _flash_finish(o_ref, l_sc, acc_sc, wuv_ref, n_heads, tq)


def _attn_sample_call(cq, ckv, kpe, cache_ckv, cache_kpe, wuq, wuk, wuv, cos, sin, layer, row0, n_seq, tq, dims):
    ah, nope, rope, kvl, vd = dims["ah"], dims["nope"], dims["rope"], dims["kvl"], dims["vd"]
    ql = cq.shape[1]
    past = cache_ckv.shape[2]
    kb = math.gcd(past, 512)
    kern = functools.partial(_attn_sample_kernel, n_heads=ah, nope=nope, rope=rope,
                             sm_scale=(nope + rope) ** -0.5, kb=kb)

    def wspec(w):
        return pl.BlockSpec((None,) + w.shape[1:], lambda b: (layer,) + (0,) * (w.ndim - 1))

    def tok(n):
        return pl.BlockSpec((tq, n), lambda b: (row0 + b, 0))

    return pl.pallas_call(
        kern,
        grid=(n_seq,),
        in_specs=[
            tok(ql), tok(kvl), tok(rope),
            pl.BlockSpec((None, None, past, kvl), lambda b: (layer, b, 0, 0)),
            pl.BlockSpec((None, None, past, rope), lambda b: (layer, b, 0, 0)),
            wspec(wuq), wspec(wuk), wspec(wuv), tok(LANES), tok(LANES),
        ],
        out_specs=pl.BlockSpec((tq, ah * vd), lambda b: (b, 0)),
        out_shape=jax.ShapeDtypeStruct((n_seq * tq, ah * vd), BF),
        scratch_shapes=[
            pltpu.VMEM((past + tq, kvl + rope), BF),
            pltpu.VMEM((ah * tq, kvl + rope), BF),
            pltpu.VMEM((ah * tq, 1), F32),
            pltpu.VMEM((ah * tq, 1), F32),
            pltpu.VMEM((ah * tq, kvl), F32),
        ],
        compiler_params=_cparams("parallel"),
        name="mla_sample",
    )(cq, ckv, kpe, cache_ckv, cache_kpe, wuq, wuk, wuv, cos, sin)


def _outproj_kernel(x_ref, hm_ref, oa_ref, gt_ref, wm_ref, wa_ref, o_ref):
    y = _dot(hm_ref[...], wm_ref[...]) + _dot(oa_ref[...], wa_ref[...])
    for g in range(x_ref.shape[0] // CHUNK):
        rows = pl.ds(g * CHUNK, CHUNK)
        o_ref[rows, :] = x_ref[rows, :] + gt_ref[g:g + 1, :] * y[g * CHUNK:(g + 1) * CHUNK, :]


def _outproj_call(x, hm, oa, mod, w_out, layer, tm):
    m, d = x.shape
    wm = hm.shape[1]
    wa = oa.shape[1]
    assert wm == wa
    ng = tm // CHUNK
    return pl.pallas_call(
        _outproj_kernel,
        grid=(m // tm,),
        in_specs=[
            pl.BlockSpec((tm, d), lambda i: (i, 0)),
            pl.BlockSpec((tm, wm), lambda i: (i, 0)),
            pl.BlockSpec((tm, wa), lambda i: (i, 0)),
            pl.BlockSpec((None, ng, d), lambda i: (layer, i, 5)),
            pl.BlockSpec((None, wm, d), lambda i: (layer, 0, 0)),
            pl.BlockSpec((None, wa, d), lambda i: (layer, 1, 0)),
        ],
        out_specs=pl.BlockSpec((tm, d), lambda i: (i, 0)),
        out_shape=jax.ShapeDtypeStruct((m, d), F32),
        compiler_params=_cparams("parallel"),
        name="out_proj",
    )(x, hm, oa, mod, w_out, w_out)


def _final_norm_kernel(x_ref, g_ref, o_ref):
    o_ref[...] = _rms(x_ref[...], g_ref[...])


def _final_norm_call(x, g, row0, rows, tm):
    d = x.shape[1]
    return pl.pallas_call(
        _final_norm_kernel,
        grid=(rows // tm,),
        in_specs=[pl.BlockSpec((tm, d), lambda i: (row0 + i, 0)), pl.BlockSpec((1, d), lambda i: (0, 0))],
        out_specs=pl.BlockSpec((tm, d), lambda i: (i, 0)),
        out_shape=jax.ShapeDtypeStruct((rows, d), F32),
        compiler_params=_cparams("parallel"),
        name="final_norm",
    )(x, g)


def _rope_tables(pos, rope):
    half = rope // 2
    freqs = ROPE_THETA ** (-jnp.arange(half, dtype=F32) / half)
    ang = pos.astype(F32)[:, None] * freqs[None, :]
    cos = jnp.cos(ang)
    sin = jnp.sin(ang)
    reps = LANES // rope
    return jnp.tile(jnp.concatenate([cos, cos], axis=1), (1, reps)), jnp.tile(jnp.concatenate([-sin, sin], axis=1), (1, reps))


def _pick_tile(m, cap):
    t = cap
    while m % t:
        t //= 2
    return t


def kernel(x_prompt, x_sample, c_prompt, c_sample, cache_ckv, cache_kpe, state_C, state_n, state_m, mod_w, mod_b, ln_ffn1, ffn1_w_in, ffn1_w_out, ln_mix, w_in, mlstm_b_i, mlstm_b_f, mlstm_norm, q_norm, w_uq, kv_norm, w_uk, w_uv, w_out, ln_ffn2, ffn2_w_in, ffn2_w_out, final_norm):
    bp, sp, d = x_prompt.shape
    bs, ss, _ = x_sample.shape
    depth = mod_w.shape[0]
    past = cache_ckv.shape[2]
    mh, dv, dk = state_C.shape[2:]
    kvl, ah, nope = w_uk.shape[1:]
    vd = w_uv.shape[3]
    rope = cache_kpe.shape[3]
    ql = q_norm.shape[1]
    hdk, hdv = mh * dk, mh * dv
    dff = ffn1_w_out.shape[1]
    dims = dict(mh=mh, dk=dk, dv=dv, hdk=hdk, hdv=hdv, ql=ql, kvl=kvl, rope=rope, ah=ah, nope=nope, vd=vd)
    assert sp % CHUNK == 0 and ss == CHUNK and LANES % rope == 0 and ah % (LANES // rope) == 0
    mp, ms = bp * sp, bs * ss
    m = mp + ms
    tm = _pick_tile(math.gcd(mp, ms), 512)
    tf = _pick_tile(dff, 512)
    tq = _pick_tile(sp, 256)

    x = jnp.concatenate([x_prompt.reshape(mp, d), x_sample.reshape(ms, d)], axis=0)
    cg = jnp.concatenate([jnp.repeat(c_prompt, sp // CHUNK, axis=0), jnp.repeat(c_sample, ss // CHUNK, axis=0)], axis=0)
    pos = jnp.concatenate([jnp.tile(jnp.arange(sp), bp), jnp.tile(past + jnp.arange(ss), bs)])
    cos, sin = _rope_tables(pos, rope)

    o_mq, o_mk, o_mv, o_mo = 0, hdk, 2 * hdk, 2 * hdk + hdv
    o_mi = o_mo + hdv
    o_mf, o_qa = o_mi + mh, o_mi + 2 * mh
    o_kva, o_pe = o_qa + ql, o_qa + ql + kvl
    wqk = w_in[:, :, o_mq:o_mv].astype(BF)
    wv = w_in[:, :, o_mv:o_mo].astype(BF)
    wo = w_in[:, :, o_mo:o_mi].astype(BF)
    zpad = lambda n: jnp.zeros((depth, d, n), w_in.dtype)
    wa = jnp.concatenate([w_in[:, :, o_qa:o_pe + rope], zpad(LANES - rope), w_in[:, :, o_mi:o_qa], zpad(LANES - 2 * mh)], axis=2).astype(BF)
    gate_bias = jnp.concatenate([mlstm_b_i, mlstm_b_f, jnp.zeros((depth, LANES - 2 * mh), F32)], axis=1).reshape(depth, 1, LANES)
    wuq4 = w_uq.reshape(depth, ql, ah, nope + rope)
    wuq = jnp.concatenate([wuq4[..., :nope].reshape(depth, ql, ah * nope), wuq4[..., nope:].reshape(depth, ql, ah * rope)], axis=2).astype(BF)
    wuk = jnp.transpose(w_uk, (0, 2, 3, 1)).astype(BF)
    wuv = jnp.transpose(w_uv, (0, 2, 1, 3)).astype(BF)
    w_out_b = w_out.astype(BF)
    f1_in, f1_out = ffn1_w_in.astype(BF), ffn1_w_out.astype(BF)
    f2_in, f2_out = ffn2_w_in.astype(BF), ffn2_w_out.astype(BF)
    r3 = lambda a: a.reshape(depth, 1, a.shape[1])

    mod = _mod_call(cg, mod_w, mod_b)

    zeros_c = jnp.zeros((bp, mh, dv, dk), F32)
    zeros_n = jnp.zeros((bp, mh, dk), F32)
    zeros_m = jnp.zeros((bp, mh, LANES), F32)
    m0_s = jnp.broadcast_to(state_m[..., None], state_m.shape + (LANES,))

    outs = {k: [] for k in ("p_ckv", "p_kpe", "p_C", "p_n", "p_m", "s_ckv", "s_kpe", "s_C", "s_n", "s_m")}
    for l in range(depth):
        x = _ffn_call(x, mod, r3(ln_ffn1), f1_in, f1_out, l, 0, tm, tf)
        q, k, v, og, cq, ckv, kpe, gates = _inproj_call(
            x, mod, r3(ln_mix), wqk, wv, wo, wa, r3(q_norm), r3(kv_norm), gate_bias, cos, sin, l, tm, dims)
        gates_r = jnp.transpose(gates[:, :2 * mh].reshape(m // CHUNK, CHUNK, 2 * mh), (0, 2, 1))
        nrm = r3(mlstm_norm)
        hm_p, c_p, n_p, m_p = _mlstm_call(q, k, v, og, gates, gates_r, zeros_c, zeros_n, zeros_m, nrm,
                                          l, 0, bp, sp // CHUNK, mh)
        hm_s, c_s, n_s, m_s = _mlstm_call(q, k, v, og, gates, gates_r, state_C[l], state_n[l], m0_s[l], nrm,
                                          l, mp // CHUNK, bs, ss // CHUNK, mh)
        oa_p = _attn_prompt_call(cq, ckv, kpe, wuq, wuk, wuv, cos, sin, l, bp, sp, tq, dims)
        oa_s = _attn_sample_call(cq, ckv, kpe, cache_ckv, cache_kpe, wuq, wuk, wuv, cos, sin, l, mp // ss, bs, ss, dims)
        hm = jnp.concatenate([hm_p, hm_s], axis=0)
        oa = jnp.concatenate([oa_p, oa_s], axis=0)
        x = _outproj_call(x, hm, oa, mod, w_out_b, l, tm)
        x = _ffn_call(x, mod, r3(ln_ffn2), f2_in, f2_out, l, 6, tm, tf)
        outs["p_ckv"].append(ckv[:mp].reshape(bp, sp, kvl))
        outs["p_kpe"].append(kpe[:mp].reshape(bp, sp, rope))
        outs["s_ckv"].append(ckv[mp:].reshape(bs, ss, kvl))
        outs["s_kpe"].append(kpe[mp:].reshape(bs, ss, rope))
        outs["p_C"].append(c_p)
        outs["p_n"].append(n_p)
        outs["p_m"].append(m_p[..., 0])
        outs["s_C"].append(c_s)
        outs["s_n"].append(n_s)
        outs["s_m"].append(m_s[..., 0])

    fn = final_norm.reshape(1, d)
    y_prompt = _final_norm_call(x, fn, 0, mp, tm).reshape(bp, sp, d)
    y_sample = _final_norm_call(x, fn, mp // tm, ms, tm).reshape(bs, ss, d)
    st = {k: jnp.stack(v) for k, v in outs.items()}
    return (y_prompt, y_sample, st["p_ckv"], st["p_kpe"], st["p_C"], st["p_n"], st["p_m"],
            st["s_ckv"], st["s_kpe"], st["s_C"], st["s_n"], st["s_m"])
```

```python
import functools
import math

import jax
import jax.numpy as jnp
from jax import lax
from jax.experimental import pallas as pl
from jax.experimental.pallas import tpu as pltpu

CHUNK = 64
EPS = 1e-6
ROPE_THETA = 10000.0
N_MOD = 9
LANES = 128
VMEM_LIMIT = 56 * 1024 * 1024

BF = jnp.bfloat16
F32 = jnp.float32


def _cparams(*sem):
    return pltpu.CompilerParams(dimension_semantics=sem, vmem_limit_bytes=VMEM_LIMIT)


def _dot(a, b):
    return jnp.dot(a, b, preferred_element_type=F32)


def _dot_nt(a, b):
    return lax.dot_general(a, b, (((1,), (1,)), ((), ())), preferred_element_type=F32)


def _dot_tn(a, b):
    return lax.dot_general(a, b, (((0,), (0,)), ((), ())), preferred_element_type=F32)


def _rms(x, g):
    ms = jnp.mean(x * x, axis=-1, keepdims=True)
    return x * lax.rsqrt(ms + EPS) * g


def _silu(x):
    return x * jax.nn.sigmoid(x)


def _rope128(pe, cos, sin):
    lane = lax.broadcasted_iota(jnp.int32, pe.shape, 1)
    first_half = jnp.bitwise_and(lane, 63) < 32
    swapped = jnp.where(first_half, pltpu.roll(pe, 96, 1), pltpu.roll(pe, 32, 1))
    return pe * cos + swapped * sin


def _mod_kernel(c_ref, w_ref, b_ref, o_ref):
    a = _silu(c_ref[...]).astype(BF)
    o_ref[...] = _dot(a, w_ref[...].astype(BF)) + b_ref[...]


def _mod_call(cg, mod_w, mod_b):
    depth, d, nd = mod_w.shape
    g = cg.shape[0]
    tn = 1024
    return pl.pallas_call(
        _mod_kernel,
        grid=(depth, nd // tn),
        in_specs=[
            pl.BlockSpec((g, d), lambda l, j: (0, 0)),
            pl.BlockSpec((None, d, tn), lambda l, j: (l, 0, j)),
            pl.BlockSpec((None, 1, tn), lambda l, j: (l, 0, j)),
        ],
        out_specs=pl.BlockSpec((None, g, tn), lambda l, j: (l, 0, j)),
        out_shape=jax.ShapeDtypeStruct((depth, g, nd), F32),
        compiler_params=_cparams("parallel", "parallel"),
        name="adaln_mod",
    )(cg, mod_w, mod_b.reshape(depth, 1, nd))


def _norm_mod_to(h_ref, x_ref, ln_ref, sh_ref, sc_ref):
    ln = ln_ref[...]
    for g in range(x_ref.shape[0] // CHUNK):
        rows = pl.ds(g * CHUNK, CHUNK)
        y = _rms(x_ref[rows, :], ln)
        h_ref[rows, :] = (y * (1.0 + sc_ref[g:g + 1, :]) + sh_ref[g:g + 1, :]).astype(h_ref.dtype)


def _ffn_kernel(x_ref, sh_ref, sc_ref, gt_ref, ln_ref, wg_ref, wu_ref, wo_ref, o_ref, h_sc):
    j = pl.program_id(1)
    nj = pl.num_programs(1)

    @pl.when(j == 0)
    def _():
        _norm_mod_to(h_sc, x_ref, ln_ref, sh_ref, sc_ref)

    h = h_sc[...]
    a = (_silu(_dot(h, wg_ref[...])) * _dot(h, wu_ref[...])).astype(BF)
    y = _dot(a, wo_ref[...])

    @pl.when(j == 0)
    def _():
        o_ref[...] = y

    @pl.when(j > 0)
    def _():
        o_ref[...] += y

    @pl.when(j == nj - 1)
    def _():
        for g in range(x_ref.shape[0] // CHUNK):
            rows = pl.ds(g * CHUNK, CHUNK)
            o_ref[rows, :] = x_ref[rows, :] + (0.5 * gt_ref[g:g + 1, :]) * o_ref[rows, :]


def _ffn_call(x, mod, ln, w_in, w_out, layer, k0, tm, tf):
    m, d = x.shape
    f = w_out.shape[1]
    ng = tm // CHUNK
    nf = f // tf

    def mod_spec(k):
        return pl.BlockSpec((None, ng, d), lambda i, j: (layer, i, k))

    return pl.pallas_call(
        _ffn_kernel,
        grid=(m // tm, nf),
        in_specs=[
            pl.BlockSpec((tm, d), lambda i, j: (i, 0)),
            mod_spec(k0), mod_spec(k0 + 1), mod_spec(k0 + 2),
            pl.BlockSpec((None, 1, d), lambda i, j: (layer, 0, 0)),
            pl.BlockSpec((None, d, tf), lambda i, j: (layer, 0, j)),
            pl.BlockSpec((None, d, tf), lambda i, j: (layer, 0, j + nf)),
            pl.BlockSpec((None, tf, d), lambda i, j: (layer, j, 0)),
        ],
        out_specs=pl.BlockSpec((tm, d), lambda i, j: (i, 0)),
        out_shape=jax.ShapeDtypeStruct((m, d), F32),
        scratch_shapes=[pltpu.VMEM((tm, d), BF)],
        compiler_params=_cparams("parallel", "arbitrary"),
        name="ffn",
    )(x, mod, mod, mod, ln, w_in, w_in, w_out)


def _inproj_kernel(x_ref, sh_ref, sc_ref, ln_ref, wqk_ref, wv_ref, wo_ref, wa_ref, qn_ref, kvn_ref, gb_ref,
                   cos_ref, sin_ref, q_ref, k_ref, v_ref, og_ref, cq_ref, ckv_ref, kpe_ref, gt_ref, h_sc,
                   *, n_heads, k_scale):
    _norm_mod_to(h_sc, x_ref, ln_ref, sh_ref, sc_ref)
    h = h_sc[...]
    hdk = q_ref.shape[1]
    zqk = _dot(h, wqk_ref[...])
    q_ref[...] = zqk[:, :hdk].astype(BF)
    k_ref[...] = (zqk[:, hdk:] * k_scale).astype(BF)
    v_ref[...] = _dot(h, wv_ref[...]).astype(BF)
    og_ref[...] = jax.nn.sigmoid(_dot(h, wo_ref[...]))
    za = _dot(h, wa_ref[...])
    ql = cq_ref.shape[1]
    kvl = ckv_ref.shape[1]
    cq_ref[...] = _rms(za[:, :ql], qn_ref[...]).astype(BF)
    ckv_ref[...] = _rms(za[:, ql:ql + kvl], kvn_ref[...])
    pe = _rope128(za[:, ql + kvl:ql + kvl + LANES], cos_ref[...], sin_ref[...])
    kpe_ref[...] = pe[:, :kpe_ref.shape[1]]
    zg = za[:, ql + kvl + LANES:] + gb_ref[...]
    lane = lax.broadcasted_iota(jnp.int32, zg.shape, 1)
    log_sig = jnp.minimum(zg, 0.0) - jnp.log1p(jnp.exp(-jnp.abs(zg)))
    gt_ref[...] = jnp.where(lane < n_heads, zg, log_sig)


def _inproj_call(x, mod, ln, wqk, wv, wo, wa, q_norm, kv_norm, gate_bias, cos, sin, layer, tm, dims):
    m, d = x.shape
    ng = tm // CHUNK
    hdk, hdv, ql, kvl, rope = dims["hdk"], dims["hdv"], dims["ql"], dims["kvl"], dims["rope"]

    def mod_spec(k):
        return pl.BlockSpec((None, ng, d), lambda i: (layer, i, k))

    def wspec(w):
        return pl.BlockSpec((None,) + w.shape[1:], lambda i: (layer, 0, 0))

    def row(n):
        return pl.BlockSpec((tm, n), lambda i: (i, 0))

    kern = functools.partial(_inproj_kernel, n_heads=dims["mh"], k_scale=dims["dk"] ** -0.5)
    return pl.pallas_call(
        kern,
        grid=(m // tm,),
        in_specs=[
            row(d), mod_spec(3), mod_spec(4), wspec(ln), wspec(wqk), wspec(wv), wspec(wo), wspec(wa),
            wspec(q_norm), wspec(kv_norm), wspec(gate_bias), row(LANES), row(LANES),
        ],
        out_specs=[row(hdk), row(hdk), row(hdv), row(hdv), row(ql), row(kvl), row(rope), row(LANES)],
        out_shape=[
            jax.ShapeDtypeStruct((m, hdk), BF), jax.ShapeDtypeStruct((m, hdk), BF),
            jax.ShapeDtypeStruct((m, hdv), BF), jax.ShapeDtypeStruct((m, hdv), F32),
            jax.ShapeDtypeStruct((m, ql), BF), jax.ShapeDtypeStruct((m, kvl), F32),
            jax.ShapeDtypeStruct((m, rope), F32), jax.ShapeDtypeStruct((m, LANES), F32),
        ],
        scratch_shapes=[pltpu.VMEM((tm, d), BF)],
        compiler_params=_cparams("parallel"),
        name="in_proj",
    )(x, mod, mod, ln, wqk, wv, wo, wa, q_norm, kv_norm, gate_bias, cos, sin)


def _mlstm_kernel(q_ref, k_ref, v_ref, og_ref, gc_ref, gr_ref, c0_ref, n0_ref, m0_ref, nrm_ref,
                  hm_ref, c_ref, n_ref, m_ref, *, n_heads):
    c_idx = pl.program_id(1)

    @pl.when(c_idx == 0)
    def _():
        c_ref[...] = c0_ref[...]
        n_ref[...] = n0_ref[...]
        m_ref[...] = m0_ref[...]

    L = q_ref.shape[0]
    dk = q_ref.shape[1] // n_heads
    dv = v_ref.shape[1] // n_heads
    t_idx = lax.broadcasted_iota(jnp.int32, (L, L), 0)
    s_idx = lax.broadcasted_iota(jnp.int32, (L, L), 1)
    causal = s_idx <= t_idx
    for h in range(n_heads):
        qh = q_ref[:, h * dk:(h + 1) * dk]
        kh = k_ref[:, h * dk:(h + 1) * dk]
        vh = v_ref[:, h * dv:(h + 1) * dv]
        ig_c = gc_ref[:, h:h + 1]
        lf_c = gc_ref[:, n_heads + h:n_heads + h + 1]
        ig_r = gr_ref[h:h + 1, :]
        lf_r = gr_ref[n_heads + h:n_heads + h + 1, :]
        b_c = jnp.sum(jnp.where(causal, lf_r, 0.0), axis=1, keepdims=True)
        b_r = jnp.sum(jnp.where(causal, 0.0, lf_c), axis=0, keepdims=True) + lf_r
        b_last = b_c[L - 1:L, :]
        m_prev = m_ref[h:h + 1, 0:1]
        dmat = jnp.where(causal, b_c - b_r + ig_r, -jnp.inf)
        g_c = b_c + m_prev
        m_t = jnp.maximum(g_c, jnp.max(dmat, axis=1, keepdims=True))
        w_intra = jnp.exp(dmat - m_t)
        w_inter = jnp.exp(g_c - m_t)
        s = _dot_nt(qh, kh) * w_intra
        c_prev = c_ref[h]
        n_prev = n_ref[h:h + 1, :]
        num = _dot(s.astype(BF), vh) + w_inter * _dot_nt(qh, c_prev.astype(BF))
        nq = jnp.sum(s, axis=1, keepdims=True) + w_inter * jnp.sum(qh.astype(F32) * n_prev, axis=1, keepdims=True)
        hh = num / jnp.maximum(jnp.abs(nq), jnp.exp(-m_t))
        m_new = m_t[L - 1:L, :]
        wa = jnp.exp(b_last - b_c + ig_c - m_new)
        decay = jnp.exp(b_last + m_prev - m_new)
        wk = wa * kh.astype(F32)
        c_ref[h] = decay * c_prev + _dot_tn(vh, wk.astype(BF))
        n_ref[h:h + 1, :] = decay * n_prev + jnp.sum(wk, axis=0, keepdims=True)
        m_ref[h:h + 1, :] = jnp.broadcast_to(m_new, (1, m_ref.shape[1]))
        x = hh * og_ref[:, h * dv:(h + 1) * dv]
        x = x * lax.rsqrt(jnp.mean(x * x, axis=-1, keepdims=True) + EPS)
        hm_ref[:, h * dv:(h + 1) * dv] = (x * nrm_ref[:, h * dv:(h + 1) * dv]).astype(hm_ref.dtype)


def _mlstm_call(q, k, v, og, gates_c, gates_r, c0, n0, m0, norm, layer, row0, n_seq, n_chunks, n_heads):
    m, hdk = q.shape
    hdv = v.shape[1]
    dv, dk = c0.shape[-2:]
    g2 = gates_r.shape[1]

    def tok(n):
        return pl.BlockSpec((CHUNK, n), lambda b, c: (row0 + b * n_chunks + c, 0))

    def state(shape):
        return pl.BlockSpec((None,) + shape, lambda b, c: (b,) + (0,) * len(shape))

    kern = functools.partial(_mlstm_kernel, n_heads=n_heads)
    return pl.pallas_call(
        kern,
        grid=(n_seq, n_chunks),
        in_specs=[
            tok(hdk), tok(hdk), tok(hdv), tok(hdv), tok(LANES),
            pl.BlockSpec((None, g2, CHUNK), lambda b, c: (row0 + b * n_chunks + c, 0, 0)),
            state((n_heads, dv, dk)), state((n_heads, dk)), state((n_heads, LANES)),
            pl.BlockSpec((None, 1, hdv), lambda b, c: (layer, 0, 0)),
        ],
        out_specs=[
            pl.BlockSpec((CHUNK, hdv), lambda b, c: (b * n_chunks + c, 0)),
            state((n_heads, dv, dk)), state((n_heads, dk)), state((n_heads, LANES)),
        ],
        out_shape=[
            jax.ShapeDtypeStruct((n_seq * n_chunks * CHUNK, hdv), BF),
            jax.ShapeDtypeStruct((n_seq, n_heads, dv, dk), F32),
            jax.ShapeDtypeStruct((n_seq, n_heads, dk), F32),
            jax.ShapeDtypeStruct((n_seq, n_heads, LANES), F32),
        ],
        compiler_params=_cparams("parallel", "arbitrary"),
        name="mlstm",
    )(q, k, v, og, gates_c, gates_r, c0, n0, m0, norm)


def _build_queries(q_sc, cq_ref, wuq_ref, wuk_ref, cos_ref, sin_ref, n_heads, nope, rope, kvl):
    tq = cq_ref.shape[0]
    qa = _dot(cq_ref[...], wuq_ref[...])
    cos = cos_ref[...]
    sin = sin_ref[...]
    for h in range(n_heads):
        qn = qa[:, h * nope:(h + 1) * nope].astype(BF)
        q_sc[h * tq:(h + 1) * tq, 0:kvl] = _dot(qn, wuk_ref[h]).astype(BF)
    per = LANES // rope
    for p in range(n_heads // per):
        base = n_heads * nope + p * LANES
        pe = _rope128(qa[:, base:base + LANES], cos, sin)
        for u in range(per):
            h = p * per + u
            q_sc[h * tq:(h + 1) * tq, kvl:kvl + rope] = pe[:, u * rope:(u + 1) * rope].astype(BF)


def _lane_tile(x, width):
    if width <= LANES:
        return x[:, :width]
    return jnp.concatenate([x] * (width // LANES), axis=1)


def _flash_block(q_sc, kblk, m_sc, l_sc, acc_sc, kvl, sm_scale, rs, q_chunks):
    kb = kblk.shape[0]

    def sub(r, carry):
        r0 = pl.multiple_of(r * rs, rs)
        rows = pl.ds(r0, rs)
        s = _dot_nt(q_sc[rows, :], kblk) * sm_scale
        if q_chunks is not None:
            chunk_bits = CHUNK.bit_length() - 1
            row = lax.broadcasted_iota(jnp.int32, (rs, kb), 0) + r0
            q_chunk = jnp.bitwise_and(jnp.right_shift(row, chunk_bits), q_chunks - 1)
            k_chunk = jnp.right_shift(lax.broadcasted_iota(jnp.int32, (rs, kb), 1), chunk_bits)
            s = jnp.where(k_chunk <= q_chunk, s, -jnp.inf)
        m_prev = m_sc[rows, :]
        m_new = jnp.maximum(m_prev, jnp.max(s, axis=1, keepdims=True))
        alpha = jnp.exp(m_prev - m_new)
        p = jnp.exp(s - _lane_tile(m_new, kb))
        l_sc[rows, :] = alpha * l_sc[rows, :] + jnp.sum(p, axis=1, keepdims=True)
        acc_sc[rows, :] = _lane_tile(alpha, kvl) * acc_sc[rows, :] + _dot(p.astype(BF), kblk[:, :kvl])
        m_sc[rows, :] = m_new
        return carry

    n_sub = q_sc.shape[0] // rs
    lax.fori_loop(0, n_sub, sub, 0, unroll=True)


def _flash_init(m_sc, l_sc, acc_sc):
    m_sc[...] = jnp.full(m_sc.shape, -jnp.inf, F32)
    l_sc[...] = jnp.zeros(l_sc.shape, F32)
    acc_sc[...] = jnp.zeros(acc_sc.shape, F32)


def _flash_finish(o_ref, l_sc, acc_sc, wuv_ref, n_heads, tq):
    vd = wuv_ref.shape[2]
    kvl = acc_sc.shape[1]
    for h in range(n_heads):
        rows = pl.ds(h * tq, tq)
        o_lat = (acc_sc[rows, :] / _lane_tile(l_sc[rows, :], kvl)).astype(BF)
        o_ref[:, h * vd:(h + 1) * vd] = _dot(o_lat, wuv_ref[h]).astype(o_ref.dtype)


def _attn_prompt_kernel(cq_ref, ckv_ref, kpe_ref, wuq_ref, wuk_ref, wuv_ref, cos_ref, sin_ref, o_ref,
                        kcat, q_sc, m_sc, l_sc, acc_sc, *, n_heads, nope, rope, sm_scale, rs):
    i = pl.program_id(1)
    tq = cq_ref.shape[0]
    kvl = ckv_ref.shape[1]

    @pl.when(i == 0)
    def _():
        kcat[:, 0:kvl] = ckv_ref[...].astype(BF)
        kcat[:, kvl:kvl + rope] = kpe_ref[...].astype(BF)

    _build_queries(q_sc, cq_ref, wuq_ref, wuk_ref, cos_ref, sin_ref, n_heads, nope, rope, kvl)
    _flash_init(m_sc, l_sc, acc_sc)

    def body(j, carry):
        kblk = kcat[pl.ds(pl.multiple_of(j * tq, tq), tq), :]
        _flash_block(q_sc, kblk, m_sc, l_sc, acc_sc, kvl, sm_scale, rs, None)
        return carry

    lax.fori_loop(0, i, body, 0)

    kblk = kcat[pl.ds(pl.multiple_of(i * tq, tq), tq), :]
    _flash_block(q_sc, kblk, m_sc, l_sc, acc_sc, kvl, sm_scale, rs, tq // CHUNK)
    _flash_finish(o_ref, l_sc, acc_sc, wuv_ref, n_heads, tq)


def _attn_prompt_call(cq, ckv, kpe, wuq, wuk, wuv, cos, sin, layer, n_seq, seq, tq, dims):
    ah, nope, rope, kvl, vd = dims["ah"], dims["nope"], dims["rope"], dims["kvl"], dims["vd"]
    ql = cq.shape[1]
    nq = seq // tq
    assert tq % CHUNK == 0 and (tq // CHUNK) & (tq // CHUNK - 1) == 0
    rs = _pick_tile(ah * tq, 512)
    kern = functools.partial(_attn_prompt_kernel, n_heads=ah, nope=nope, rope=rope,
                             sm_scale=(nope + rope) ** -0.5, rs=rs)

    def wspec(w):
        return pl.BlockSpec((None,) + w.shape[1:], lambda b, i: (layer,) + (0,) * (w.ndim - 1))

    return pl.pallas_call(
        kern,
        grid=(n_seq, nq),
        in_specs=[
            pl.BlockSpec((tq, ql), lambda b, i: (b * nq + i, 0)),
            pl.BlockSpec((seq, kvl), lambda b, i: (b, 0)),
            pl.BlockSpec((seq, rope), lambda b, i: (b, 0)),
            wspec(wuq), wspec(wuk), wspec(wuv),
            pl.BlockSpec((tq, LANES), lambda b, i: (b * nq + i, 0)),
            pl.BlockSpec((tq, LANES), lambda b, i: (b * nq + i, 0)),
        ],
        out_specs=pl.BlockSpec((tq, ah * vd), lambda b, i: (b * nq + i, 0)),
        out_shape=jax.ShapeDtypeStruct((n_seq * seq, ah * vd), BF),
        scratch_shapes=[
            pltpu.VMEM((seq, kvl + rope), BF),
            pltpu.VMEM((ah * tq, kvl + rope), BF),
            pltpu.VMEM((ah * tq, LANES), F32),
            pltpu.VMEM((ah * tq, LANES), F32),
            pltpu.VMEM((ah * tq, kvl), F32),
        ],
        compiler_params=_cparams("parallel", "arbitrary"),
        name="mla_prompt",
    )(cq, ckv, kpe, wuq, wuk, wuv, cos, sin)


def _attn_sample_kernel(cq_ref, ckv_ref, kpe_ref, pckv_ref, pkpe_ref, wuq_ref, wuk_ref, wuv_ref, cos_ref, sin_ref,
                        o_ref, kcat, q_sc, m_sc, l_sc, acc_sc, *, n_heads, nope, rope, sm_scale, kb):
    tq = cq_ref.shape[0]
    kvl = ckv_ref.shape[1]
    past = pckv_ref.shape[0]
    kcat[0:past, 0:kvl] = pckv_ref[...].astype(BF)
    kcat[0:past, kvl:kvl + rope] = pkpe_ref[...].astype(BF)
    kcat[past:past + tq, 0:kvl] = ckv_ref[...].astype(BF)
    kcat[past:past + tq, kvl:kvl + rope] = kpe_ref[...].astype(BF)
    _build_queries(q_sc, cq_ref, wuq_ref, wuk_ref, cos_ref, sin_ref, n_heads, nope, rope, kvl)
    _flash_init(m_sc, l_sc, acc_sc)
    rs = q_sc.shape[0]

    def body(j, carry):
        kblk = kcat[pl.ds(pl.multiple_of(j * kb, kb), kb), :]
        _flash_block(q_sc, kblk, m_sc, l_sc, acc_sc, kvl, sm_scale, rs, None)
        return carry

    lax.fori_loop(0, past // kb, body, 0)
    _flash_block(q_sc, kcat[past:past + tq, :], m_sc, l_sc, acc_sc, kvl, sm_scale, rs, None)
    _flash_finish(o_ref, l_sc, acc_sc, wuv_ref, n_heads, tq)


def _attn_sample_call(cq, ckv, kpe, cache_ckv, cache_kpe, wuq, wuk, wuv, cos, sin, layer, row0, n_seq, tq, dims):
    ah, nope, rope, kvl, vd = dims["ah"], dims["nope"], dims["rope"], dims["kvl"], dims["vd"]
    ql = cq.shape[1]
    past = cache_ckv.shape[2]
    kb = math.gcd(past, 256)
    kern = functools.partial(_attn_sample_kernel, n_heads=ah, nope=nope, rope=rope,
                             sm_scale=(nope + rope) ** -0.5, kb=kb)

    def wspec(w):
        return pl.BlockSpec((None,) + w.shape[1:], lambda b: (layer,) + (0,) * (w.ndim - 1))

    def tok(n):
        return pl.BlockSpec((tq, n), lambda b: (row0 + b, 0))

    return pl.pallas_call(
        kern,
        grid=(n_seq,),
        in_specs=[
            tok(ql), tok(kvl), tok(rope),
            pl.BlockSpec((None, None, past, kvl), lambda b: (layer, b, 0, 0)),
            pl.BlockSpec((None, None, past, rope), lambda b: (layer, b, 0, 0)),
            wspec(wuq), wspec(wuk), wspec(wuv), tok(LANES), tok(LANES),
        ],
        out_specs=pl.BlockSpec((tq, ah * vd), lambda b: (b, 0)),
        out_shape=jax.ShapeDtypeStruct((n_seq * tq, ah * vd), BF),
        scratch_shapes=[
            pltpu.VMEM((past + tq, kvl + rope), BF),
            pltpu.VMEM((ah * tq, kvl + rope), BF),
            pltpu.VMEM((ah * tq, LANES), F32),
            pltpu.VMEM((ah * tq, LANES), F32),
            pltpu.VMEM((ah * tq, kvl), F32),
        ],
        compiler_params=_cparams("parallel"),
        name="mla_sample",
    )(cq, ckv, kpe, cache_ckv, cache_kpe, wuq, wuk, wuv, cos, sin)


def _outproj_kernel(x_ref, hm_ref, oa_ref, gt_ref, wm_ref, wa_ref, o_ref):
    y = _dot(hm_ref[...], wm_ref[...]) + _dot(oa_ref[...], wa_ref[...])
    for g in range(x_ref.shape[0] // CHUNK):
        rows = pl.ds(g * CHUNK, CHUNK)
        o_ref[rows, :] = x_ref[rows, :] + gt_ref[g:g + 1, :] * y[g * CHUNK:(g + 1) * CHUNK, :]


def _outproj_call(x, hm, oa, mod, w_out, layer, tm):
    m, d = x.shape
    wm = hm.shape[1]
    wa = oa.shape[1]
    assert wm == wa
    ng = tm // CHUNK
    return pl.pallas_call(
        _outproj_kernel,
        grid=(m // tm,),
        in_specs=[
            pl.BlockSpec((tm, d), lambda i: (i, 0)),
            pl.BlockSpec((tm, wm), lambda i: (i, 0)),
            pl.BlockSpec((tm, wa), lambda i: (i, 0)),
            pl.BlockSpec((None, ng, d), lambda i: (layer, i, 5)),
            pl.BlockSpec((None, wm, d), lambda i: (layer, 0, 0)),
            pl.BlockSpec((None, wa, d), lambda i: (layer, 1, 0)),
        ],
        out_specs=pl.BlockSpec((tm, d), lambda i: (i, 0)),
        out_shape=jax.ShapeDtypeStruct((m, d), F32),
        compiler_params=_cparams("parallel"),
        name="out_proj",
    )(x, hm, oa, mod, w_out, w_out)


def _final_norm_kernel(x_ref, g_ref, o_ref):
    o_ref[...] = _rms(x_ref[...], g_ref[...])


def _final_norm_call(x, g, row0, rows, tm):
    d = x.shape[1]
    return pl.pallas_call(
        _final_norm_kernel,
        grid=(rows // tm,),
        in_specs=[pl.BlockSpec((tm, d), lambda i: (row0 + i, 0)), pl.BlockSpec((1, d), lambda i: (0, 0))],
        out_specs=pl.BlockSpec((tm, d), lambda i: (i, 0)),
        out_shape=jax.ShapeDtypeStruct((rows, d), F32),
        compiler_params=_cparams("parallel"),
        name="final_norm",
    )(x, g)


def _rope_tables(pos, rope):
    half = rope // 2
    freqs = ROPE_THETA ** (-jnp.arange(half, dtype=F32) / half)
    ang = pos.astype(F32)[:, None] * freqs[None, :]
    cos = jnp.cos(ang)
    sin = jnp.sin(ang)
    reps = LANES // rope
    return jnp.tile(jnp.concatenate([cos, cos], axis=1), (1, reps)), jnp.tile(jnp.concatenate([-sin, sin], axis=1), (1, reps))


def _pick_tile(m, cap):
    t = cap
    while m % t:
        t //= 2
    return t


def kernel(x_prompt, x_sample, c_prompt, c_sample, cache_ckv, cache_kpe, state_C, state_n, state_m, mod_w, mod_b, ln_ffn1, ffn1_w_in, ffn1_w_out, ln_mix, w_in, mlstm_b_i, mlstm_b_f, mlstm_norm, q_norm, w_uq, kv_norm, w_uk, w_uv, w_out, ln_ffn2, ffn2_w_in, ffn2_w_out, final_norm):
    bp, sp, d = x_prompt.shape
    bs, ss, _ = x_sample.shape
    depth = mod_w.shape[0]
    past = cache_ckv.shape[2]
    mh, dv, dk = state_C.shape[2:]
    kvl, ah, nope = w_uk.shape[1:]
    vd = w_uv.shape[3]
    rope = cache_kpe.shape[3]
    ql = q_norm.shape[1]
    hdk, hdv = mh * dk, mh * dv
    dff = ffn1_w_out.shape[1]
    dims = dict(mh=mh, dk=dk, dv=dv, hdk=hdk, hdv=hdv, ql=ql, kvl=kvl, rope=rope, ah=ah, nope=nope, vd=vd)
    assert sp % CHUNK == 0 and ss == CHUNK and LANES % rope == 0 and ah % (LANES // rope) == 0
    mp, ms = bp * sp, bs * ss
    m = mp + ms
    tm = _pick_tile(math.gcd(mp, ms), 512)
    tf = _pick_tile(dff, 512)
    tq = _pick_tile(sp, 256)

    x = jnp.concatenate([x_prompt.reshape(mp, d), x_sample.reshape(ms, d)], axis=0)
    cg = jnp.concatenate([jnp.repeat(c_prompt, sp // CHUNK, axis=0), jnp.repeat(c_sample, ss // CHUNK, axis=0)], axis=0)
    pos = jnp.concatenate([jnp.tile(jnp.arange(sp), bp), jnp.tile(past + jnp.arange(ss), bs)])
    cos, sin = _rope_tables(pos, rope)

    o_mq, o_mk, o_mv, o_mo = 0, hdk, 2 * hdk, 2 * hdk + hdv
    o_mi = o_mo + hdv
    o_mf, o_qa = o_mi + mh, o_mi + 2 * mh
    o_kva, o_pe = o_qa + ql, o_qa + ql + kvl
    wqk = w_in[:, :, o_mq:o_mv].astype(BF)
    wv = w_in[:, :, o_mv:o_mo].astype(BF)
    wo = w_in[:, :, o_mo:o_mi].astype(BF)
    zpad = lambda n: jnp.zeros((depth, d, n), w_in.dtype)
    wa = jnp.concatenate([w_in[:, :, o_qa:o_pe + rope], zpad(LANES - rope), w_in[:, :, o_mi:o_qa], zpad(LANES - 2 * mh)], axis=2).astype(BF)
    gate_bias = jnp.concatenate([mlstm_b_i, mlstm_b_f, jnp.zeros((depth, LANES - 2 * mh), F32)], axis=1).reshape(depth, 1, LANES)
    wuq4 = w_uq.reshape(depth, ql, ah, nope + rope)
    wuq = jnp.concatenate([wuq4[..., :nope].reshape(depth, ql, ah * nope), wuq4[..., nope:].reshape(depth, ql, ah * rope)], axis=2).astype(BF)
    wuk = jnp.transpose(w_uk, (0, 2, 3, 1)).astype(BF)
    wuv = jnp.transpose(w_uv, (0, 2, 1, 3)).astype(BF)
    w_out_b = w_out.astype(BF)
    f1_in, f1_out = ffn1_w_in.astype(BF), ffn1_w_out.astype(BF)
    f2_in, f2_out = ffn2_w_in.astype(BF), ffn2_w_out.astype(BF)
    r3 = lambda a: a.reshape(depth, 1, a.shape[1])

    mod = _mod_call(cg, mod_w, mod_b)

    zeros_c = jnp.zeros((bp, mh, dv, dk), F32)
    zeros_n = jnp.zeros((bp, mh, dk), F32)
    zeros_m = jnp.zeros((bp, mh, LANES), F32)
    m0_s = jnp.broadcast_to(state_m[..., None], state_m.shape + (LANES,))

    outs = {k: [] for k in ("p_ckv", "p_kpe", "p_C", "p_n", "p_m", "s_ckv", "s_kpe", "s_C", "s_n", "s_m")}
    for l in range(depth):
        x = _ffn_call(x, mod, r3(ln_ffn1), f1_in, f1_out, l, 0, tm, tf)
        q, k, v, og, cq, ckv, kpe, gates = _inproj_call(
            x, mod, r3(ln_mix), wqk, wv, wo, wa, r3(q_norm), r3(kv_norm), gate_bias, cos, sin, l, tm, dims)
        gates_r = jnp.transpose(gates[:, :2 * mh].reshape(m // CHUNK, CHUNK, 2 * mh), (0, 2, 1))
        nrm = r3(mlstm_norm)
        hm_p, c_p, n_p, m_p = _mlstm_call(q, k, v, og, gates, gates_r, zeros_c, zeros_n, zeros_m, nrm,
                                          l, 0, bp, sp // CHUNK, mh)
        hm_s, c_s, n_s, m_s = _mlstm_call(q, k, v, og, gates, gates_r, state_C[l], state_n[l], m0_s[l], nrm,
                                          l, mp // CHUNK, bs, ss // CHUNK, mh)
        oa_p = _attn_prompt_call(cq, ckv, kpe, wuq, wuk, wuv, cos, sin, l, bp, sp, tq, dims)
        oa_s = _attn_sample_call(cq, ckv, kpe, cache_ckv, cache_kpe, wuq, wuk, wuv, cos, sin, l, mp // ss, bs, ss, dims)
        hm = jnp.concatenate([hm_p, hm_s], axis=0)
        oa = jnp.concatenate([oa_p, oa_s], axis=0)
        x = _outproj_call(x, hm, oa, mod, w_out_b, l, tm)
        x = _ffn_call(x, mod, r3(ln_ffn2), f2_in, f2_out, l, 6, tm, tf)
        outs["p_ckv"].append(ckv[:mp].reshape(bp, sp, kvl))
        outs["p_kpe"].append(kpe[:mp].reshape(bp, sp, rope))
        outs["s_ckv"].append(ckv[mp:].reshape(bs, ss, kvl))
        outs["s_kpe"].append(kpe[mp:].reshape(bs, ss, rope))
        outs["p_C"].append(c_p)
        outs["p_n"].append(n_p)
        outs["p_m"].append(m_p[..., 0])
        outs["s_C"].append(c_s)
        outs["s_n"].append(n_s)
        outs["s_m"].append(m_s[..., 0])

    fn = final_norm.reshape(1, d)
    y_prompt = _final_norm_call(x, fn, 0, mp, tm).reshape(bp, sp, d)
    y_sample = _final_norm_call(x, fn, mp // tm, ms, tm).reshape(bs, ss, d)
    st = {k: jnp.stack(v) for k, v in outs.items()}
    return (y_prompt, y_sample, st["p_ckv"], st["p_kpe"], st["p_C"], st["p_n"], st["p_m"],
            st["s_ckv"], st["s_kpe"], st["s_C"], st["s_n"], st["s_m"])
```

```python
import functools
import math

import jax
import jax.numpy as jnp
from jax import lax
from jax.experimental import pallas as pl
from jax.experimental.pallas import tpu as pltpu

CHUNK = 64
EPS = 1e-6
ROPE_THETA = 10000.0
N_MOD = 9
LANES = 128
VMEM_LIMIT = 56 * 1024 * 1024

BF = jnp.bfloat16
F32 = jnp.float32


def _cparams(*sem):
    return pltpu.CompilerParams(dimension_semantics=sem, vmem_limit_bytes=VMEM_LIMIT)


def _dot(a, b):
    return jnp.dot(a, b, preferred_element_type=F32)


def _dot_nt(a, b):
    return lax.dot_general(a, b, (((1,), (1,)), ((), ())), preferred_element_type=F32)


def _dot_tn(a, b):
    return lax.dot_general(a, b, (((0,), (0,)), ((), ())), preferred_element_type=F32)


def _rms(x, g):
    ms = jnp.mean(x * x, axis=-1, keepdims=True)
    return x * lax.rsqrt(ms + EPS) * g


def _silu(x):
    return x * jax.nn.sigmoid(x)


def _rope128(pe, cos, sin):
    lane = lax.broadcasted_iota(jnp.int32, pe.shape, 1)
    first_half = jnp.bitwise_and(lane, 63) < 32
    swapped = jnp.where(first_half, pltpu.roll(pe, 96, 1), pltpu.roll(pe, 32, 1))
    return pe * cos + swapped * sin


def _mod_kernel(c_ref, w_ref, b_ref, o_ref):
    a = _silu(c_ref[...]).astype(BF)
    o_ref[...] = _dot(a, w_ref[...].astype(BF)) + b_ref[...]


def _mod_call(cg, mod_w, mod_b):
    depth, d, nd = mod_w.shape
    g = cg.shape[0]
    tn = 1024
    return pl.pallas_call(
        _mod_kernel,
        grid=(depth, nd // tn),
        in_specs=[
            pl.BlockSpec((g, d), lambda l, j: (0, 0)),
            pl.BlockSpec((None, d, tn), lambda l, j: (l, 0, j)),
            pl.BlockSpec((None, 1, tn), lambda l, j: (l, 0, j)),
        ],
        out_specs=pl.BlockSpec((None, g, tn), lambda l, j: (l, 0, j)),
        out_shape=jax.ShapeDtypeStruct((depth, g, nd), F32),
        compiler_params=_cparams("parallel", "parallel"),
        name="adaln_mod",
    )(cg, mod_w, mod_b.reshape(depth, 1, nd))


def _norm_mod_to(h_ref, x_ref, ln_ref, sh_ref, sc_ref):
    ln = ln_ref[...]
    for g in range(x_ref.shape[0] // CHUNK):
        rows = pl.ds(g * CHUNK, CHUNK)
        y = _rms(x_ref[rows, :], ln)
        h_ref[rows, :] = (y * (1.0 + sc_ref[g:g + 1, :]) + sh_ref[g:g + 1, :]).astype(h_ref.dtype)


def _ffn_kernel(x_ref, sh_ref, sc_ref, gt_ref, ln_ref, wg_ref, wu_ref, wo_ref, o_ref, h_sc):
    j = pl.program_id(1)
    nj = pl.num_programs(1)

    @pl.when(j == 0)
    def _():
        _norm_mod_to(h_sc, x_ref, ln_ref, sh_ref, sc_ref)
        o_ref[...] = jnp.zeros(o_ref.shape, o_ref.dtype)

    h = h_sc[...]
    a = (_silu(_dot(h, wg_ref[...])) * _dot(h, wu_ref[...])).astype(BF)
    o_ref[...] += _dot(a, wo_ref[...])

    @pl.when(j == nj - 1)
    def _():
        for g in range(x_ref.shape[0] // CHUNK):
            rows = pl.ds(g * CHUNK, CHUNK)
            o_ref[rows, :] = x_ref[rows, :] + (0.5 * gt_ref[g:g + 1, :]) * o_ref[rows, :]


def _ffn_call(x, mod, ln, w_in, w_out, layer, k0, tm, tf):
    m, d = x.shape
    f = w_out.shape[1]
    ng = tm // CHUNK
    nf = f // tf

    def mod_spec(k):
        return pl.BlockSpec((None, ng, d), lambda i, j: (layer, i, k))

    return pl.pallas_call(
        _ffn_kernel,
        grid=(m // tm, nf),
        in_specs=[
            pl.BlockSpec((tm, d), lambda i, j: (i, 0)),
            mod_spec(k0), mod_spec(k0 + 1), mod_spec(k0 + 2),
            pl.BlockSpec((None, 1, d), lambda i, j: (layer, 0, 0)),
            pl.BlockSpec((None, d, tf), lambda i, j: (layer, 0, j)),
            pl.BlockSpec((None, d, tf), lambda i, j: (layer, 0, j + nf)),
            pl.BlockSpec((None, tf, d), lambda i, j: (layer, j, 0)),
        ],
        out_specs=pl.BlockSpec((tm, d), lambda i, j: (i, 0)),
        out_shape=jax.ShapeDtypeStruct((m, d), F32),
        scratch_shapes=[pltpu.VMEM((tm, d), BF)],
        compiler_params=_cparams("parallel", "arbitrary"),
        name="ffn",
    )(x, mod, mod, mod, ln, w_in, w_in, w_out)


def _inproj_kernel(x_ref, sh_ref, sc_ref, ln_ref, wqk_ref, wv_ref, wo_ref, wa_ref, qn_ref, kvn_ref, gb_ref,
                   cos_ref, sin_ref, q_ref, k_ref, v_ref, og_ref, cq_ref, ckv_ref, kpe_ref, gt_ref, h_sc,
                   *, n_heads, k_scale):
    _norm_mod_to(h_sc, x_ref, ln_ref, sh_ref, sc_ref)
    h = h_sc[...]
    hdk = q_ref.shape[1]
    zqk = _dot(h, wqk_ref[...])
    q_ref[...] = zqk[:, :hdk].astype(BF)
    k_ref[...] = (zqk[:, hdk:] * k_scale).astype(BF)
    v_ref[...] = _dot(h, wv_ref[...]).astype(BF)
    og_ref[...] = jax.nn.sigmoid(_dot(h, wo_ref[...]))
    za = _dot(h, wa_ref[...])
    ql = cq_ref.shape[1]
    kvl = ckv_ref.shape[1]
    cq_ref[...] = _rms(za[:, :ql], qn_ref[...]).astype(BF)
    ckv_ref[...] = _rms(za[:, ql:ql + kvl], kvn_ref[...])
    pe = _rope128(za[:, ql + kvl:ql + kvl + LANES], cos_ref[...], sin_ref[...])
    kpe_ref[...] = pe[:, :kpe_ref.shape[1]]
    zg = za[:, ql + kvl + LANES:] + gb_ref[...]
    lane = lax.broadcasted_iota(jnp.int32, zg.shape, 1)
    log_sig = jnp.minimum(zg, 0.0) - jnp.log1p(jnp.exp(-jnp.abs(zg)))
    gt_ref[...] = jnp.where(lane < n_heads, zg, log_sig)


def _inproj_call(x, mod, ln, wqk, wv, wo, wa, q_norm, kv_norm, gate_bias, cos, sin, layer, tm, dims):
    m, d = x.shape
    ng = tm // CHUNK
    hdk, hdv, ql, kvl, rope = dims["hdk"], dims["hdv"], dims["ql"], dims["kvl"], dims["rope"]

    def mod_spec(k):
        return pl.BlockSpec((None, ng, d), lambda i: (layer, i, k))

    def wspec(w):
        return pl.BlockSpec((None,) + w.shape[1:], lambda i: (layer, 0, 0))

    def row(n):
        return pl.BlockSpec((tm, n), lambda i: (i, 0))

    kern = functools.partial(_inproj_kernel, n_heads=dims["mh"], k_scale=dims["dk"] ** -0.5)
    return pl.pallas_call(
        kern,
        grid=(m // tm,),
        in_specs=[
            row(d), mod_spec(3), mod_spec(4), wspec(ln), wspec(wqk), wspec(wv), wspec(wo), wspec(wa),
            wspec(q_norm), wspec(kv_norm), wspec(gate_bias), row(LANES), row(LANES),
        ],
        out_specs=[row(hdk), row(hdk), row(hdv), row(hdv), row(ql), row(kvl), row(rope), row(LANES)],
        out_shape=[
            jax.ShapeDtypeStruct((m, hdk), BF), jax.ShapeDtypeStruct((m, hdk), BF),
            jax.ShapeDtypeStruct((m, hdv), BF), jax.ShapeDtypeStruct((m, hdv), F32),
            jax.ShapeDtypeStruct((m, ql), BF), jax.ShapeDtypeStruct((m, kvl), F32),
            jax.ShapeDtypeStruct((m, rope), F32), jax.ShapeDtypeStruct((m, LANES), F32),
        ],
        scratch_shapes=[pltpu.VMEM((tm, d), BF)],
        compiler_params=_cparams("parallel"),
        name="in_proj",
    )(x, mod, mod, ln, wqk, wv, wo, wa, q_norm, kv_norm, gate_bias, cos, sin)


def _mlstm_kernel(q_ref, k_ref, v_ref, og_ref, gc_ref, gr_ref, c0_ref, n0_ref, m0_ref, nrm_ref,
                  hm_ref, c_ref, n_ref, m_ref, *, n_heads):
    c_idx = pl.program_id(1)

    @pl.when(c_idx == 0)
    def _():
        c_ref[...] = c0_ref[...]
        n_ref[...] = n0_ref[...]
        m_ref[...] = m0_ref[...]

    L = q_ref.shape[0]
    dk = q_ref.shape[1] // n_heads
    dv = v_ref.shape[1] // n_heads
    t_idx = lax.broadcasted_iota(jnp.int32, (L, L), 0)
    s_idx = lax.broadcasted_iota(jnp.int32, (L, L), 1)
    causal = s_idx <= t_idx
    for h in range(n_heads):
        qh = q_ref[:, h * dk:(h + 1) * dk]
        kh = k_ref[:, h * dk:(h + 1) * dk]
        vh = v_ref[:, h * dv:(h + 1) * dv]
        ig_c = gc_ref[:, h:h + 1]
        lf_c = gc_ref[:, n_heads + h:n_heads + h + 1]
        ig_r = gr_ref[h:h + 1, :]
        lf_r = gr_ref[n_heads + h:n_heads + h + 1, :]
        b_c = jnp.sum(jnp.where(causal, lf_r, 0.0), axis=1, keepdims=True)
        b_r = jnp.sum(jnp.where(causal, 0.0, lf_c), axis=0, keepdims=True) + lf_r
        b_last = b_c[L - 1:L, :]
        m_prev = m_ref[h:h + 1, 0:1]
        dmat = jnp.where(causal, b_c - b_r + ig_r, -jnp.inf)
        g_c = b_c + m_prev
        m_t = jnp.maximum(g_c, jnp.max(dmat, axis=1, keepdims=True))
        w_intra = jnp.exp(dmat - m_t)
        w_inter = jnp.exp(g_c - m_t)
        s = _dot_nt(qh, kh) * w_intra
        c_prev = c_ref[h]
        n_prev = n_ref[h:h + 1, :]
        num = _dot(s.astype(BF), vh) + w_inter * _dot_nt(qh, c_prev.astype(BF))
        nq = jnp.sum(s, axis=1, keepdims=True) + w_inter * jnp.sum(qh.astype(F32) * n_prev, axis=1, keepdims=True)
        hh = num / jnp.maximum(jnp.abs(nq), jnp.exp(-m_t))
        m_new = m_t[L - 1:L, :]
        wa = jnp.exp(b_last - b_c + ig_c - m_new)
        decay = jnp.exp(b_last + m_prev - m_new)
        wk = wa * kh.astype(F32)
        c_ref[h] = decay * c_prev + _dot_tn(vh, wk.astype(BF))
        n_ref[h:h + 1, :] = decay * n_prev + jnp.sum(wk, axis=0, keepdims=True)
        m_ref[h:h + 1, :] = jnp.broadcast_to(m_new, (1, m_ref.shape[1]))
        x = hh * og_ref[:, h * dv:(h + 1) * dv]
        x = x * lax.rsqrt(jnp.mean(x * x, axis=-1, keepdims=True) + EPS)
        hm_ref[:, h * dv:(h + 1) * dv] = (x * nrm_ref[:, h * dv:(h + 1) * dv]).astype(hm_ref.dtype)


def _mlstm_call(q, k, v, og, gates_c, gates_r, c0, n0, m0, norm, layer, row0, n_seq, n_chunks, n_heads):
    m, hdk = q.shape
    hdv = v.shape[1]
    dv, dk = c0.shape[-2:]
    g2 = gates_r.shape[1]

    def tok(n):
        return pl.BlockSpec((CHUNK, n), lambda b, c: (row0 + b * n_chunks + c, 0))

    def state(shape):
        return pl.BlockSpec((None,) + shape, lambda b, c: (b,) + (0,) * len(shape))

    kern = functools.partial(_mlstm_kernel, n_heads=n_heads)
    return pl.pallas_call(
        kern,
        grid=(n_seq, n_chunks),
        in_specs=[
            tok(hdk), tok(hdk), tok(hdv), tok(hdv), tok(LANES),
            pl.BlockSpec((None, g2, CHUNK), lambda b, c: (row0 + b * n_chunks + c, 0, 0)),
            state((n_heads, dv, dk)), state((n_heads, dk)), state((n_heads, LANES)),
            pl.BlockSpec((None, 1, hdv), lambda b, c: (layer, 0, 0)),
        ],
        out_specs=[
            pl.BlockSpec((CHUNK, hdv), lambda b, c: (b * n_chunks + c, 0)),
            state((n_heads, dv, dk)), state((n_heads, dk)), state((n_heads, LANES)),
        ],
        out_shape=[
            jax.ShapeDtypeStruct((n_seq * n_chunks * CHUNK, hdv), BF),
            jax.ShapeDtypeStruct((n_seq, n_heads, dv, dk), F32),
            jax.ShapeDtypeStruct((n_seq, n_heads, dk), F32),
            jax.ShapeDtypeStruct((n_seq, n_heads, LANES), F32),
        ],
        compiler_params=_cparams("parallel", "arbitrary"),
        name="mlstm",
    )(q, k, v, og, gates_c, gates_r, c0, n0, m0, norm)


def _build_queries(q_sc, cq_ref, wuq_ref, wuk_ref, cos_ref, sin_ref, n_heads, nope, rope, kvl):
    tq = cq_ref.shape[0]
    qa = _dot(cq_ref[...], wuq_ref[...])
    cos = cos_ref[...]
    sin = sin_ref[...]
    for h in range(n_heads):
        qn = qa[:, h * nope:(h + 1) * nope].astype(BF)
        q_sc[h * tq:(h + 1) * tq, 0:kvl] = _dot(qn, wuk_ref[h]).astype(BF)
    per = LANES // rope
    for p in range(n_heads // per):
        base = n_heads * nope + p * LANES
        pe = _rope128(qa[:, base:base + LANES], cos, sin)
        for u in range(per):
            h = p * per + u
            q_sc[h * tq:(h + 1) * tq, kvl:kvl + rope] = pe[:, u * rope:(u + 1) * rope].astype(BF)


def _lane_tile(x, width):
    if width <= LANES:
        return x[:, :width]
    return jnp.concatenate([x] * (width // LANES), axis=1)


def _flash_block(q_sc, kblk, m_sc, l_sc, acc_sc, kvl, sm_scale, rs, q_chunks):
    kb = kblk.shape[0]

    def sub(r, carry):
        r0 = pl.multiple_of(r * rs, rs)
        rows = pl.ds(r0, rs)
        s = _dot_nt(q_sc[rows, :], kblk) * sm_scale
        if q_chunks is not None:
            chunk_bits = CHUNK.bit_length() - 1
            row = lax.broadcasted_iota(jnp.int32, (rs, kb), 0) + r0
            q_chunk = jnp.bitwise_and(jnp.right_shift(row, chunk_bits), q_chunks - 1)
            k_chunk = jnp.right_shift(lax.broadcasted_iota(jnp.int32, (rs, kb), 1), chunk_bits)
            s = jnp.where(k_chunk <= q_chunk, s, -jnp.inf)
        m_prev = m_sc[rows, :]
        m_new = jnp.maximum(m_prev, jnp.max(s, axis=1, keepdims=True))
        alpha = jnp.exp(m_prev - m_new)
        p = jnp.exp(s - _lane_tile(m_new, kb))
        l_sc[rows, :] = alpha * l_sc[rows, :] + jnp.sum(p, axis=1, keepdims=True)
        acc_sc[rows, :] = _lane_tile(alpha, kvl) * acc_sc[rows, :] + _dot(p.astype(BF), kblk[:, :kvl])
        m_sc[rows, :] = m_new
        return carry

    n_sub = q_sc.shape[0] // rs
    lax.fori_loop(0, n_sub, sub, 0, unroll=True)


def _flash_init(m_sc, l_sc, acc_sc):
    m_sc[...] = jnp.full(m_sc.shape, -jnp.inf, F32)
    l_sc[...] = jnp.zeros(l_sc.shape, F32)
    acc_sc[...] = jnp.zeros(acc_sc.shape, F32)


def _flash_finish(o_ref, l_sc, acc_sc, wuv_ref, n_heads, tq):
    vd = wuv_ref.shape[2]
    kvl = acc_sc.shape[1]
    for h in range(n_heads):
        rows = pl.ds(h * tq, tq)
        o_lat = (acc_sc[rows, :] / _lane_tile(l_sc[rows, :], kvl)).astype(BF)
        o_ref[:, h * vd:(h + 1) * vd] = _dot(o_lat, wuv_ref[h]).astype(o_ref.dtype)


def _attn_prompt_kernel(cq_ref, ckv_ref, kpe_ref, wuq_ref, wuk_ref, wuv_ref, cos_ref, sin_ref, o_ref,
                        kcat, q_sc, m_sc, l_sc, acc_sc, *, n_heads, nope, rope, sm_scale, rs):
    i = pl.program_id(1)
    tq = cq_ref.shape[0]
    kvl = ckv_ref.shape[1]

    @pl.when(i == 0)
    def _():
        kcat[:, 0:kvl] = ckv_ref[...].astype(BF)
        kcat[:, kvl:kvl + rope] = kpe_ref[...].astype(BF)

    _build_queries(q_sc, cq_ref, wuq_ref, wuk_ref, cos_ref, sin_ref, n_heads, nope, rope, kvl)
    _flash_init(m_sc, l_sc, acc_sc)

    def body(j, carry):
        kblk = kcat[pl.ds(pl.multiple_of(j * tq, tq), tq), :]
        _flash_block(q_sc, kblk, m_sc, l_sc, acc_sc, kvl, sm_scale, rs, None)
        return carry

    lax.fori_loop(0, i, body, 0)

    kblk = kcat[pl.ds(pl.multiple_of(i * tq, tq), tq), :]
    _flash_block(q_sc, kblk, m_sc, l_sc, acc_sc, kvl, sm_scale, rs, tq // CHUNK)
    _flash_finish(o_ref, l_sc, acc_sc, wuv_ref, n_heads, tq)


def _attn_prompt_call(cq, ckv, kpe, wuq, wuk, wuv, cos, sin, layer, n_seq, seq, tq, dims):
    ah, nope, rope, kvl, vd = dims["ah"], dims["nope"], dims["rope"], dims["kvl"], dims["vd"]
    ql = cq.shape[1]
    nq = seq // tq
    assert tq % CHUNK == 0 and (tq // CHUNK) & (tq // CHUNK - 1) == 0
    rs = _pick_tile(ah * tq, 512)
    kern = functools.partial(_attn_prompt_kernel, n_heads=ah, nope=nope, rope=rope,
                             sm_scale=(nope + rope) ** -0.5, rs=rs)

    def wspec(w):
        return pl.BlockSpec((None,) + w.shape[1:], lambda b, i: (layer,) + (0,) * (w.ndim - 1))

    return pl.pallas_call(
        kern,
        grid=(n_seq, nq),
        in_specs=[
            pl.BlockSpec((tq, ql), lambda b, i: (b * nq + i, 0)),
            pl.BlockSpec((seq, kvl), lambda b, i: (b, 0)),
            pl.BlockSpec((seq, rope), lambda b, i: (b, 0)),
            wspec(wuq), wspec(wuk), wspec(wuv),
            pl.BlockSpec((tq, LANES), lambda b, i: (b * nq + i, 0)),
            pl.BlockSpec((tq, LANES), lambda b, i: (b * nq + i, 0)),
        ],
        out_specs=pl.BlockSpec((tq, ah * vd), lambda b, i: (b * nq + i, 0)),
        out_shape=jax.ShapeDtypeStruct((n_seq * seq, ah * vd), BF),
        scratch_shapes=[
            pltpu.VMEM((seq, kvl + rope), BF),
            pltpu.VMEM((ah * tq, kvl + rope), BF),
            pltpu.VMEM((ah * tq, LANES), F32),
            pltpu.VMEM((ah * tq, LANES), F32),
            pltpu.VMEM((ah * tq, kvl), F32),
        ],
        compiler_params=_cparams("parallel", "arbitrary"),
        name="mla_prompt",
    )(cq, ckv, kpe, wuq, wuk, wuv, cos, sin)


def _attn_sample_kernel(cq_ref, ckv_ref, kpe_ref, pckv_ref, pkpe_ref, wuq_ref, wuk_ref, wuv_ref, cos_ref, sin_ref,
                        o_ref, kcat, q_sc, m_sc, l_sc, acc_sc, *, n_heads, nope, rope, sm_scale, kb):
    tq = cq_ref.shape[0]
    kvl = ckv_ref.shape[1]
    past = pckv_ref.shape[0]
    kcat[0:past, 0:kvl] = pckv_ref[...].astype(BF)
    kcat[0:past, kvl:kvl + rope] = pkpe_ref[...].astype(BF)
    kcat[past:past + tq, 0:kvl] = ckv_ref[...].astype(BF)
    kcat[past:past + tq, kvl:kvl + rope] = kpe_ref[...].astype(BF)
    _build_queries(q_sc, cq_ref, wuq_ref, wuk_ref, cos_ref, sin_ref, n_heads, nope, rope, kvl)
    _flash_init(m_sc, l_sc, acc_sc)
    rs = q_sc.shape[0]

    def body(j, carry):
        kblk = kcat[pl.ds(pl.multiple_of(j * kb, kb), kb), :]
        _flash_block(q_sc, kblk, m_sc, l_sc, acc_sc, kvl, sm_scale, rs, None)
        return carry

    lax.fori_loop(0, past // kb, body, 0)
    _flash_block(q_sc, kcat[past:past + tq, :], m_sc, l_sc, acc_sc, kvl, sm_scale, rs, None)
    _flash_finish(o_ref, l_sc, acc_sc, wuv_ref, n_heads, tq)


def _attn_sample_call(cq, ckv, kpe, cache_ckv, cache_kpe, wuq, wuk, wuv, cos, sin, layer, row0, n_seq, tq, dims):
    ah, nope, rope, kvl, vd = dims["ah"], dims["nope"], dims["rope"], dims["kvl"], dims["vd"]
    ql = cq.shape[1]
    past = cache_ckv.shape[2]
    kb = math.gcd(past, 256)
    kern = functools.partial(_attn_sample_kernel, n_heads=ah, nope=nope, rope=rope,
                             sm_scale=(nope + rope) ** -0.5, kb=kb)

    def wspec(w):
        return pl.BlockSpec((None,) + w.shape[1:], lambda b: (layer,) + (0,) * (w.ndim - 1))

    def tok(n):
        return pl.BlockSpec((tq, n), lambda b: (row0 + b, 0))

    return pl.pallas_call(
        kern,
        grid=(n_seq,),
        in_specs=[
            tok(ql), tok(kvl), tok(rope),
            pl.BlockSpec((None, None, past, kvl), lambda b: (layer, b, 0, 0)),
            pl.BlockSpec((None, None, past, rope), lambda b: (layer, b, 0, 0)),
            wspec(wuq), wspec(wuk), wspec(wuv), tok(LANES), tok(LANES),
        ],
        out_specs=pl.BlockSpec((tq, ah * vd), lambda b: (b, 0)),
        out_shape=jax.ShapeDtypeStruct((n_seq * tq, ah * vd), BF),
        scratch_shapes=[
            pltpu.VMEM((past + tq, kvl + rope), BF),
            pltpu.VMEM((ah * tq, kvl + rope), BF),
            pltpu.VMEM((ah * tq, LANES), F32),
            pltpu.VMEM((ah * tq, LANES), F32),
            pltpu.VMEM((ah * tq, kvl), F32),
        ],
        compiler_params=_cparams("parallel"),
        name="mla_sample",
    )(cq, ckv, kpe, cache_ckv, cache_kpe, wuq, wuk, wuv, cos, sin)


def _outproj_kernel(x_ref, hm_ref, oa_ref, gt_ref, wm_ref, wa_ref, o_ref):
    y = _dot(hm_ref[...], wm_ref[...]) + _dot(oa_ref[...], wa_ref[...])
    for g in range(x_ref.shape[0] // CHUNK):
        rows = pl.ds(g * CHUNK, CHUNK)
        o_ref[rows, :] = x_ref[rows, :] + gt_ref[g:g + 1, :] * y[g * CHUNK:(g + 1) * CHUNK, :]


def _outproj_call(x, hm, oa, mod, w_out, layer, tm):
    m, d = x.shape
    wm = hm.shape[1]
    wa = oa.shape[1]
    assert wm == wa
    ng = tm // CHUNK
    return pl.pallas_call(
        _outproj_kernel,
        grid=(m // tm,),
        in_specs=[
            pl.BlockSpec((tm, d), lambda i: (i, 0)),
            pl.BlockSpec((tm, wm), lambda i: (i, 0)),
            pl.BlockSpec((tm, wa), lambda i: (i, 0)),
            pl.BlockSpec((None, ng, d), lambda i: (layer, i, 5)),
            pl.BlockSpec((None, wm, d), lambda i: (layer, 0, 0)),
            pl.BlockSpec((None, wa, d), lambda i: (layer, 1, 0)),
        ],
        out_specs=pl.BlockSpec((tm, d), lambda i: (i, 0)),
        out_shape=jax.ShapeDtypeStruct((m, d), F32),
        compiler_params=_cparams("parallel"),
        name="out_proj",
    )(x, hm, oa, mod, w_out, w_out)


def _final_norm_kernel(x_ref, g_ref, o_ref):
    o_ref[...] = _rms(x_ref[...], g_ref[...])


def _final_norm_call(x, g, row0, rows, tm):
    d = x.shape[1]
    return pl.pallas_call(
        _final_norm_kernel,
        grid=(rows // tm,),
        in_specs=[pl.BlockSpec((tm, d), lambda i: (row0 + i, 0)), pl.BlockSpec((1, d), lambda i: (0, 0))],
        out_specs=pl.BlockSpec((tm, d), lambda i: (i, 0)),
        out_shape=jax.ShapeDtypeStruct((rows, d), F32),
        compiler_params=_cparams("parallel"),
        name="final_norm",
    )(x, g)


def _rope_tables(pos, rope):
    half = rope // 2
    freqs = ROPE_THETA ** (-jnp.arange(half, dtype=F32) / half)
    ang = pos.astype(F32)[:, None] * freqs[None, :]
    cos = jnp.cos(ang)
    sin = jnp.sin(ang)
    reps = LANES // rope
    return jnp.tile(jnp.concatenate([cos, cos], axis=1), (1, reps)), jnp.tile(jnp.concatenate([-sin, sin], axis=1), (1, reps))


def _pick_tile(m, cap):
    t = cap
    while m % t:
        t //= 2
    return t


def kernel(x_prompt, x_sample, c_prompt, c_sample, cache_ckv, cache_kpe, state_C, state_n, state_m, mod_w, mod_b, ln_ffn1, ffn1_w_in, ffn1_w_out, ln_mix, w_in, mlstm_b_i, mlstm_b_f, mlstm_norm, q_norm, w_uq, kv_norm, w_uk, w_uv, w_out, ln_ffn2, ffn2_w_in, ffn2_w_out, final_norm):
    bp, sp, d = x_prompt.shape
    bs, ss, _ = x_sample.shape
    depth = mod_w.shape[0]
    past = cache_ckv.shape[2]
    mh, dv, dk = state_C.shape[2:]
    kvl, ah, nope = w_uk.shape[1:]
    vd = w_uv.shape[3]
    rope = cache_kpe.shape[3]
    ql = q_norm.shape[1]
    hdk, hdv = mh * dk, mh * dv
    dff = ffn1_w_out.shape[1]
    dims = dict(mh=mh, dk=dk, dv=dv, hdk=hdk, hdv=hdv, ql=ql, kvl=kvl, rope=rope, ah=ah, nope=nope, vd=vd)
    assert sp % CHUNK == 0 and ss == CHUNK and LANES % rope == 0 and ah % (LANES // rope) == 0
    mp, ms = bp * sp, bs * ss
    m = mp + ms
    tm = _pick_tile(math.gcd(mp, ms), 512)
    tf = _pick_tile(dff, 512)
    tq = _pick_tile(sp, 256)

    x = jnp.concatenate([x_prompt.reshape(mp, d), x_sample.reshape(ms, d)], axis=0)
    cg = jnp.concatenate([jnp.repeat(c_prompt, sp // CHUNK, axis=0), jnp.repeat(c_sample, ss // CHUNK, axis=0)], axis=0)
    pos = jnp.concatenate([jnp.tile(jnp.arange(sp), bp), jnp.tile(past + jnp.arange(ss), bs)])
    cos, sin = _rope_tables(pos, rope)

    o_mq, o_mk, o_mv, o_mo = 0, hdk, 2 * hdk, 2 * hdk + hdv
    o_mi = o_mo + hdv
    o_mf, o_qa = o_mi + mh, o_mi + 2 * mh
    o_kva, o_pe = o_qa + ql, o_qa + ql + kvl
    wqk = w_in[:, :, o_mq:o_mv].astype(BF)
    wv = w_in[:, :, o_mv:o_mo].astype(BF)
    wo = w_in[:, :, o_mo:o_mi].astype(BF)
    zpad = lambda n: jnp.zeros((depth, d, n), w_in.dtype)
    wa = jnp.concatenate([w_in[:, :, o_qa:o_pe + rope], zpad(LANES - rope), w_in[:, :, o_mi:o_qa], zpad(LANES - 2 * mh)], axis=2).astype(BF)
    gate_bias = jnp.concatenate([mlstm_b_i, mlstm_b_f, jnp.zeros((depth, LANES - 2 * mh), F32)], axis=1).reshape(depth, 1, LANES)
    wuq4 = w_uq.reshape(depth, ql, ah, nope + rope)
    wuq = jnp.concatenate([wuq4[..., :nope].reshape(depth, ql, ah * nope), wuq4[..., nope:].reshape(depth, ql, ah * rope)], axis=2).astype(BF)
    wuk = jnp.transpose(w_uk, (0, 2, 3, 1)).astype(BF)
    wuv = jnp.transpose(w_uv, (0, 2, 1, 3)).astype(BF)
    w_out_b = w_out.astype(BF)
    f1_in, f1_out = ffn1_w_in.astype(BF), ffn1_w_out.astype(BF)
    f2_in, f2_out = ffn2_w_in.astype(BF), ffn2_w_out.astype(BF)
    r3 = lambda a: a.reshape(depth, 1, a.shape[1])

    mod = _mod_call(cg, mod_w, mod_b)

    zeros_c = jnp.zeros((bp, mh, dv, dk), F32)
    zeros_n = jnp.zeros((bp, mh, dk), F32)
    zeros_m = jnp.zeros((bp, mh, LANES), F32)
    m0_s = jnp.broadcast_to(state_m[..., None], state_m.shape + (LANES,))

    outs = {k: [] for k in ("p_ckv", "p_kpe", "p_C", "p_n", "p_m", "s_ckv", "s_kpe", "s_C", "s_n", "s_m")}
    for l in range(depth):
        x = _ffn_call(x, mod, r3(ln_ffn1), f1_in, f1_out, l, 0, tm, tf)
        q, k, v, og, cq, ckv, kpe, gates = _inproj_call(
            x, mod, r3(ln_mix), wqk, wv, wo, wa, r3(q_norm), r3(kv_norm), gate_bias, cos, sin, l, tm, dims)
        gates_r = jnp.transpose(gates[:, :2 * mh].reshape(m // CHUNK, CHUNK, 2 * mh), (0, 2, 1))
        nrm = r3(mlstm_norm)
        hm_p, c_p, n_p, m_p = _mlstm_call(q, k, v, og, gates, gates_r, zeros_c, zeros_n, zeros_m, nrm,
                                          l, 0, bp, sp // CHUNK, mh)
        hm_s, c_s, n_s, m_s = _mlstm_call(q, k, v, og, gates, gates_r, state_C[l], state_n[l], m0_s[l], nrm,
                                          l, mp // CHUNK, bs, ss // CHUNK, mh)
        oa_p = _attn_prompt_call(cq, ckv, kpe, wuq, wuk, wuv, cos, sin, l, bp, sp, tq, dims)
        oa_s = _attn_sample_call(cq, ckv, kpe, cache_ckv, cache_kpe, wuq, wuk, wuv, cos, sin, l, mp // ss, bs, ss, dims)
        hm = jnp.concatenate([hm_p, hm_s], axis=0)
        oa = jnp.concatenate([oa_p, oa_s], axis=0)
        x = _outproj_call(x, hm, oa, mod, w_out_b, l, tm)
        x = _ffn_call(x, mod, r3(ln_ffn2), f2_in, f2_out, l, 6, tm, tf)
        outs["p_ckv"].append(ckv[:mp].reshape(bp, sp, kvl))
        outs["p_kpe"].append(kpe[:mp].reshape(bp, sp, rope))
        outs["s_ckv"].append(ckv[mp:].reshape(bs, ss, kvl))
        outs["s_kpe"].append(kpe[mp:].reshape(bs, ss, rope))
        outs["p_C"].append(c_p)
        outs["p_n"].append(n_p)
        outs["p_m"].append(m_p[..., 0])
        outs["s_C"].append(c_s)
        outs["s_n"].append(n_s)
        outs["s_m"].append(m_s[..., 0])

    fn = final_norm.reshape(1, d)
    y_prompt = _final_norm_call(x, fn, 0, mp, tm).reshape(bp, sp, d)
    y_sample = _final_norm_call(x, fn, mp // tm, ms, tm).reshape(bs, ss, d)
    st = {k: jnp.stack(v) for k, v in outs.items()}
    return (y_prompt, y_sample, st["p_ckv"], st["p_kpe"], st["p_C"], st["p_n"], st["p_m"],
            st["s_ckv"], st["s_kpe"], st["s_C"], st["s_n"], st["s_m"])
```

```python
import functools
import math

import jax
import jax.numpy as jnp
from jax import lax
from jax.experimental import pallas as pl
from jax.experimental.pallas import tpu as pltpu

CHUNK = 64
EPS = 1e-6
ROPE_THETA = 10000.0
N_MOD = 9
LANES = 128
VMEM_LIMIT = 56 * 1024 * 1024

BF = jnp.bfloat16
F32 = jnp.float32


def _cparams(*sem):
    return pltpu.CompilerParams(dimension_semantics=sem, vmem_limit_bytes=VMEM_LIMIT)


def _dot(a, b):
    return jnp.dot(a, b, preferred_element_type=F32)


def _dot_nt(a, b):
    return lax.dot_general(a, b, (((1,), (1,)), ((), ())), preferred_element_type=F32)


def _dot_tn(a, b):
    return lax.dot_general(a, b, (((0,), (0,)), ((), ())), preferred_element_type=F32)


def _rms(x, g):
    ms = jnp.mean(x * x, axis=-1, keepdims=True)
    return x * lax.rsqrt(ms + EPS) * g


def _silu(x):
    return x * jax.nn.sigmoid(x)


def _rope128(pe, cos, sin):
    lane = lax.broadcasted_iota(jnp.int32, pe.shape, 1)
    first_half = jnp.bitwise_and(lane, 63) < 32
    swapped = jnp.where(first_half, pltpu.roll(pe, 96, 1), pltpu.roll(pe, 32, 1))
    return pe * cos + swapped * sin


def _mod_kernel(c_ref, w_ref, b_ref, o_ref):
    a = _silu(c_ref[...]).astype(BF)
    o_ref[...] = _dot(a, w_ref[...].astype(BF)) + b_ref[...]


def _mod_call(cg, mod_w, mod_b):
    depth, d, nd = mod_w.shape
    g = cg.shape[0]
    tn = 1024
    return pl.pallas_call(
        _mod_kernel,
        grid=(depth, nd // tn),
        in_specs=[
            pl.BlockSpec((g, d), lambda l, j: (0, 0)),
            pl.BlockSpec((None, d, tn), lambda l, j: (l, 0, j)),
            pl.BlockSpec((None, 1, tn), lambda l, j: (l, 0, j)),
        ],
        out_specs=pl.BlockSpec((None, g, tn), lambda l, j: (l, 0, j)),
        out_shape=jax.ShapeDtypeStruct((depth, g, nd), F32),
        compiler_params=_cparams("parallel", "parallel"),
        name="adaln_mod",
    )(cg, mod_w, mod_b.reshape(depth, 1, nd))


def _norm_mod_to(h_ref, x_ref, ln_ref, sh_ref, sc_ref):
    ln = ln_ref[...]
    for g in range(x_ref.shape[0] // CHUNK):
        rows = pl.ds(g * CHUNK, CHUNK)
        y = _rms(x_ref[rows, :], ln)
        h_ref[rows, :] = (y * (1.0 + sc_ref[g:g + 1, :]) + sh_ref[g:g + 1, :]).astype(h_ref.dtype)


def _ffn_kernel(x_ref, sh_ref, sc_ref, gt_ref, ln_ref, wg_ref, wu_ref, wo_ref, o_ref, h_sc):
    j = pl.program_id(1)
    nj = pl.num_programs(1)

    @pl.when(j == 0)
    def _():
        _norm_mod_to(h_sc, x_ref, ln_ref, sh_ref, sc_ref)
        o_ref[...] = jnp.zeros(o_ref.shape, o_ref.dtype)

    h = h_sc[...]
    a = (_silu(_dot(h, wg_ref[...])) * _dot(h, wu_ref[...])).astype(BF)
    o_ref[...] += _dot(a, wo_ref[...])

    @pl.when(j == nj - 1)
    def _():
        for g in range(x_ref.shape[0] // CHUNK):
            rows = pl.ds(g * CHUNK, CHUNK)
            o_ref[rows, :] = x_ref[rows, :] + (0.5 * gt_ref[g:g + 1, :]) * o_ref[rows, :]


def _ffn_call(x, mod, ln, w_in, w_out, layer, k0, tm, tf):
    m, d = x.shape
    f = w_out.shape[1]
    ng = tm // CHUNK
    nf = f // tf

    def mod_spec(k):
        return pl.BlockSpec((None, ng, d), lambda i, j: (layer, i, k))

    return pl.pallas_call(
        _ffn_kernel,
        grid=(m // tm, nf),
        in_specs=[
            pl.BlockSpec((tm, d), lambda i, j: (i, 0)),
            mod_spec(k0), mod_spec(k0 + 1), mod_spec(k0 + 2),
            pl.BlockSpec((None, 1, d), lambda i, j: (layer, 0, 0)),
            pl.BlockSpec((None, d, tf), lambda i, j: (layer, 0, j)),
            pl.BlockSpec((None, d, tf), lambda i, j: (layer, 0, j + nf)),
            pl.BlockSpec((None, tf, d), lambda i, j: (layer, j, 0)),
        ],
        out_specs=pl.BlockSpec((tm, d), lambda i, j: (i, 0)),
        out_shape=jax.ShapeDtypeStruct((m, d), F32),
        scratch_shapes=[pltpu.VMEM((tm, d), BF)],
        compiler_params=_cparams("parallel", "arbitrary"),
        name="ffn",
    )(x, mod, mod, mod, ln, w_in, w_in, w_out)


def _inproj_kernel(x_ref, sh_ref, sc_ref, ln_ref, wqk_ref, wv_ref, wo_ref, wa_ref, qn_ref, kvn_ref, gb_ref,
                   cos_ref, sin_ref, q_ref, k_ref, v_ref, og_ref, cq_ref, ckv_ref, kpe_ref, gt_ref, h_sc,
                   *, n_heads, k_scale):
    _norm_mod_to(h_sc, x_ref, ln_ref, sh_ref, sc_ref)
    h = h_sc[...]
    hdk = q_ref.shape[1]
    zqk = _dot(h, wqk_ref[...])
    q_ref[...] = zqk[:, :hdk].astype(BF)
    k_ref[...] = (zqk[:, hdk:] * k_scale).astype(BF)
    v_ref[...] = _dot(h, wv_ref[...]).astype(BF)
    og_ref[...] = jax.nn.sigmoid(_dot(h, wo_ref[...]))
    za = _dot(h, wa_ref[...])
    ql = cq_ref.shape[1]
    kvl = ckv_ref.shape[1]
    cq_ref[...] = _rms(za[:, :ql], qn_ref[...]).astype(BF)
    ckv_ref[...] = _rms(za[:, ql:ql + kvl], kvn_ref[...])
    pe = _rope128(za[:, ql + kvl:ql + kvl + LANES], cos_ref[...], sin_ref[...])
    kpe_ref[...] = pe[:, :kpe_ref.shape[1]]
    zg = za[:, ql + kvl + LANES:] + gb_ref[...]
    lane = lax.broadcasted_iota(jnp.int32, zg.shape, 1)
    log_sig = jnp.minimum(zg, 0.0) - jnp.log1p(jnp.exp(-jnp.abs(zg)))
    gt_ref[...] = jnp.where(lane < n_heads, zg, log_sig)


def _inproj_call(x, mod, ln, wqk, wv, wo, wa, q_norm, kv_norm, gate_bias, cos, sin, layer, tm, dims):
    m, d = x.shape
    ng = tm // CHUNK
    hdk, hdv, ql, kvl, rope = dims["hdk"], dims["hdv"], dims["ql"], dims["kvl"], dims["rope"]

    def mod_spec(k):
        return pl.BlockSpec((None, ng, d), lambda i: (layer, i, k))

    def wspec(w):
        return pl.BlockSpec((None,) + w.shape[1:], lambda i: (layer, 0, 0))

    def row(n):
        return pl.BlockSpec((tm, n), lambda i: (i, 0))

    kern = functools.partial(_inproj_kernel, n_heads=dims["mh"], k_scale=dims["dk"] ** -0.5)
    return pl.pallas_call(
        kern,
        grid=(m // tm,),
        in_specs=[
            row(d), mod_spec(3), mod_spec(4), wspec(ln), wspec(wqk), wspec(wv), wspec(wo), wspec(wa),
            wspec(q_norm), wspec(kv_norm), wspec(gate_bias), row(LANES), row(LANES),
        ],
        out_specs=[row(hdk), row(hdk), row(hdv), row(hdv), row(ql), row(kvl), row(rope), row(LANES)],
        out_shape=[
            jax.ShapeDtypeStruct((m, hdk), BF), jax.ShapeDtypeStruct((m, hdk), BF),
            jax.ShapeDtypeStruct((m, hdv), BF), jax.ShapeDtypeStruct((m, hdv), F32),
            jax.ShapeDtypeStruct((m, ql), BF), jax.ShapeDtypeStruct((m, kvl), F32),
            jax.ShapeDtypeStruct((m, rope), F32), jax.ShapeDtypeStruct((m, LANES), F32),
        ],
        scratch_shapes=[pltpu.VMEM((tm, d), BF)],
        compiler_params=_cparams("parallel"),
        name="in_proj",
    )(x, mod, mod, ln, wqk, wv, wo, wa, q_norm, kv_norm, gate_bias, cos, sin)


def _mlstm_kernel(q_ref, k_ref, v_ref, og_ref, gc_ref, gr_ref, c0_ref, n0_ref, m0_ref, nrm_ref,
                  hm_ref, c_ref, n_ref, m_ref, *, n_heads):
    c_idx = pl.program_id(1)

    @pl.when(c_idx == 0)
    def _():
        c_ref[...] = c0_ref[...]
        n_ref[...] = n0_ref[...]
        m_ref[...] = m0_ref[...]

    L = q_ref.shape[0]
    dk = q_ref.shape[1] // n_heads
    dv = v_ref.shape[1] // n_heads
    t_idx = lax.broadcasted_iota(jnp.int32, (L, L), 0)
    s_idx = lax.broadcasted_iota(jnp.int32, (L, L), 1)
    causal = s_idx <= t_idx
    for h in range(n_heads):
        qh = q_ref[:, h * dk:(h + 1) * dk]
        kh = k_ref[:, h * dk:(h + 1) * dk]
        vh = v_ref[:, h * dv:(h + 1) * dv]
        ig_c = gc_ref[:, h:h + 1]
        lf_c = gc_ref[:, n_heads + h:n_heads + h + 1]
        ig_r = gr_ref[h:h + 1, :]
        lf_r = gr_ref[n_heads + h:n_heads + h + 1, :]
        b_c = jnp.sum(jnp.where(causal, lf_r, 0.0), axis=1, keepdims=True)
        b_r = jnp.sum(jnp.where(causal, 0.0, lf_c), axis=0, keepdims=True) + lf_r
        b_last = b_c[L - 1:L, :]
        m_prev = m_ref[h:h + 1, 0:1]
        dmat = jnp.where(causal, b_c - b_r + ig_r, -jnp.inf)
        g_c = b_c + m_prev
        m_t = jnp.maximum(g_c, jnp.max(dmat, axis=1, keepdims=True))
        w_intra = jnp.exp(dmat - m_t)
        w_inter = jnp.exp(g_c - m_t)
        s = _dot_nt(qh, kh) * w_intra
        c_prev = c_ref[h]
        n_prev = n_ref[h:h + 1, :]
        num = _dot(s.astype(BF), vh) + w_inter * _dot_nt(qh, c_prev.astype(BF))
        nq = jnp.sum(s, axis=1, keepdims=True) + w_inter * jnp.sum(qh.astype(F32) * n_prev, axis=1, keepdims=True)
        hh = num / jnp.maximum(jnp.abs(nq), jnp.exp(-m_t))
        m_new = m_t[L - 1:L, :]
        wa = jnp.exp(b_last - b_c + ig_c - m_new)
        decay = jnp.exp(b_last + m_prev - m_new)
        wk = wa * kh.astype(F32)
        c_ref[h] = decay * c_prev + _dot_tn(vh, wk.astype(BF))
        n_ref[h:h + 1, :] = decay * n_prev + jnp.sum(wk, axis=0, keepdims=True)
        m_ref[h:h + 1, :] = jnp.broadcast_to(m_new, (1, m_ref.shape[1]))
        x = hh * og_ref[:, h * dv:(h + 1) * dv]
        x = x * lax.rsqrt(jnp.mean(x * x, axis=-1, keepdims=True) + EPS)
        hm_ref[:, h * dv:(h + 1) * dv] = (x * nrm_ref[:, h * dv:(h + 1) * dv]).astype(hm_ref.dtype)


def _mlstm_call(q, k, v, og, gates_c, gates_r, c0, n0, m0, norm, layer, row0, n_seq, n_chunks, n_heads):
    m, hdk = q.shape
    hdv = v.shape[1]
    dv, dk = c0.shape[-2:]
    g2 = gates_r.shape[1]

    def tok(n):
        return pl.BlockSpec((CHUNK, n), lambda b, c: (row0 + b * n_chunks + c, 0))

    def state(shape):
        return pl.BlockSpec((None,) + shape, lambda b, c: (b,) + (0,) * len(shape))

    kern = functools.partial(_mlstm_kernel, n_heads=n_heads)
    return pl.pallas_call(
        kern,
        grid=(n_seq, n_chunks),
        in_specs=[
            tok(hdk), tok(hdk), tok(hdv), tok(hdv), tok(LANES),
            pl.BlockSpec((None, g2, CHUNK), lambda b, c: (row0 + b * n_chunks + c, 0, 0)),
            state((n_heads, dv, dk)), state((n_heads, dk)), state((n_heads, LANES)),
            pl.BlockSpec((None, 1, hdv), lambda b, c: (layer, 0, 0)),
        ],
        out_specs=[
            pl.BlockSpec((CHUNK, hdv), lambda b, c: (b * n_chunks + c, 0)),
            state((n_heads, dv, dk)), state((n_heads, dk)), state((n_heads, LANES)),
        ],
        out_shape=[
            jax.ShapeDtypeStruct((n_seq * n_chunks * CHUNK, hdv), BF),
            jax.ShapeDtypeStruct((n_seq, n_heads, dv, dk), F32),
            jax.ShapeDtypeStruct((n_seq, n_heads, dk), F32),
            jax.ShapeDtypeStruct((n_seq, n_heads, LANES), F32),
        ],
        compiler_params=_cparams("parallel", "arbitrary"),
        name="mlstm",
    )(q, k, v, og, gates_c, gates_r, c0, n0, m0, norm)


def _build_queries(q_sc, cq_ref, wuq_ref, wuk_ref, cos_ref, sin_ref, n_heads, nope, rope, kvl):
    tq = cq_ref.shape[0]
    qa = _dot(cq_ref[...], wuq_ref[...])
    cos = cos_ref[...]
    sin = sin_ref[...]
    for h in range(n_heads):
        qn = qa[:, h * nope:(h + 1) * nope].astype(BF)
        q_sc[h * tq:(h + 1) * tq, 0:kvl] = _dot(qn, wuk_ref[h]).astype(BF)
    per = LANES // rope
    for p in range(n_heads // per):
        base = n_heads * nope + p * LANES
        pe = _rope128(qa[:, base:base + LANES], cos, sin)
        for u in range(per):
            h = p * per + u
            q_sc[h * tq:(h + 1) * tq, kvl:kvl + rope] = pe[:, u * rope:(u + 1) * rope].astype(BF)


def _lane_tile(x, width):
    if width <= LANES:
        return x[:, :width]
    return jnp.concatenate([x] * (width // LANES), axis=1)


def _flash_step(q_sc, keys_next, s_next, s_cur, vals_cur, m_sc, l_sc, acc_sc, sm_scale, rs, mask_chunks=None):
    n_groups = q_sc.shape[0] // rs

    def group(r, carry):
        r0 = pl.multiple_of(r * rs, rs)
        rows = pl.ds(r0, rs)
        if s_next is not None:
            s_next[rows, :] = _dot_nt(q_sc[rows, :], keys_next)
        if s_cur is not None:
            kb, kvl = vals_cur.shape
            s = s_cur[rows, :] * sm_scale
            if mask_chunks is not None:
                q_chunks, q_chunk0, k_chunk0 = mask_chunks
                chunk_bits = CHUNK.bit_length() - 1
                row = lax.broadcasted_iota(jnp.int32, (rs, kb), 0) + r0
                q_chunk = jnp.bitwise_and(jnp.right_shift(row, chunk_bits), q_chunks - 1) + q_chunk0
                k_chunk = jnp.right_shift(lax.broadcasted_iota(jnp.int32, (rs, kb), 1), chunk_bits) + k_chunk0
                s = jnp.where(k_chunk <= q_chunk, s, -jnp.inf)
            m_prev = m_sc[rows, :]
            m_new = jnp.maximum(m_prev, jnp.max(s, axis=1, keepdims=True))
            alpha = jnp.exp(m_prev - m_new)
            p = jnp.exp(s - _lane_tile(m_new, kb))
            l_sc[rows, :] = alpha * l_sc[rows, :] + jnp.sum(p, axis=1, keepdims=True)
            acc_sc[rows, :] = _lane_tile(alpha, kvl) * acc_sc[rows, :] + _dot(p.astype(BF), vals_cur)
            m_sc[rows, :] = m_new
        return carry

    lax.fori_loop(0, n_groups, group, 0, unroll=2 if n_groups % 2 == 0 else 1)


def _flash_init(m_sc, l_sc, acc_sc):
    m_sc[...] = jnp.full(m_sc.shape, -jnp.inf, F32)
    l_sc[...] = jnp.zeros(l_sc.shape, F32)
    acc_sc[...] = jnp.zeros(acc_sc.shape, F32)


def _flash_finish(o_ref, l_sc, acc_sc, wuv_ref, n_heads, tq):
    vd = wuv_ref.shape[2]
    kvl = acc_sc.shape[1]
    for h in range(n_heads):
        rows = pl.ds(h * tq, tq)
        o_lat = (acc_sc[rows, :] / _lane_tile(l_sc[rows, :], kvl)).astype(BF)
        o_ref[:, h * vd:(h + 1) * vd] = _dot(o_lat, wuv_ref[h]).astype(o_ref.dtype)


def _attn_prompt_kernel(cq_ref, ckv_ref, kpe_ref, wuq_ref, wuk_ref, wuv_ref, cos_ref, sin_ref, o_ref,
                        kcat, q_sc, m_sc, l_sc, acc_sc, s_a, s_b, *, n_heads, nope, rope, sm_scale, rs):
    i = pl.program_id(1)
    tq = cq_ref.shape[0]
    kvl = ckv_ref.shape[1]
    kb = s_a.shape[1]

    @pl.when(i == 0)
    def _():
        kcat[:, 0:kvl] = ckv_ref[...].astype(BF)
        kcat[:, kvl:kvl + rope] = kpe_ref[...].astype(BF)

    _build_queries(q_sc, cq_ref, wuq_ref, wuk_ref, cos_ref, sin_ref, n_heads, nope, rope, kvl)
    _flash_init(m_sc, l_sc, acc_sc)

    def step(j_next, s_next, j_cur, s_cur, masked=False):
        keys = None if s_next is None else kcat[pl.ds(pl.multiple_of(j_next * kb, kb), kb), :]
        vals = None if s_cur is None else kcat[pl.ds(pl.multiple_of(j_cur * kb, kb), kb), 0:kvl]
        mask = (tq // CHUNK, i * (tq // CHUNK), j_cur * (kb // CHUNK)) if masked else None
        _flash_step(q_sc, keys, s_next, s_cur, vals, m_sc, l_sc, acc_sc, sm_scale, rs, mask)

    last = (i * tq) // kb
    step(0, s_a, None, None)

    def pair(t, carry):
        j = 2 * t
        step(j + 1, s_b, j, s_a)
        step(j + 2, s_a, j + 1, s_b)
        return carry

    lax.fori_loop(0, last // 2, pair, 0)

    @pl.when(last % 2 == 0)
    def _():
        step(None, None, last, s_a, masked=True)

    @pl.when(last % 2 == 1)
    def _():
        step(last, s_b, last - 1, s_a)
        step(None, None, last, s_b, masked=True)

    _flash_finish(o_ref, l_sc, acc_sc, wuv_ref, n_heads, tq)


def _attn_prompt_call(cq, ckv, kpe, wuq, wuk, wuv, cos, sin, layer, n_seq, seq, tq, dims):
    ah, nope, rope, kvl, vd = dims["ah"], dims["nope"], dims["rope"], dims["kvl"], dims["vd"]
    ql = cq.shape[1]
    nq = seq // tq
    kb = _pick_tile(seq, 512)
    assert tq % CHUNK == 0 and (tq // CHUNK) & (tq // CHUNK - 1) == 0 and kb % tq == 0
    rs = _pick_tile(ah * tq, 512)
    kern = functools.partial(_attn_prompt_kernel, n_heads=ah, nope=nope, rope=rope,
                             sm_scale=(nope + rope) ** -0.5, rs=rs)

    def wspec(w):
        return pl.BlockSpec((None,) + w.shape[1:], lambda b, i: (layer,) + (0,) * (w.ndim - 1))

    return pl.pallas_call(
        kern,
        grid=(n_seq, nq),
        in_specs=[
            pl.BlockSpec((tq, ql), lambda b, i: (b * nq + i, 0)),
            pl.BlockSpec((seq, kvl), lambda b, i: (b, 0)),
            pl.BlockSpec((seq, rope), lambda b, i: (b, 0)),
            wspec(wuq), wspec(wuk), wspec(wuv),
            pl.BlockSpec((tq, LANES), lambda b, i: (b * nq + i, 0)),
            pl.BlockSpec((tq, LANES), lambda b, i: (b * nq + i, 0)),
        ],
        out_specs=pl.BlockSpec((tq, ah * vd), lambda b, i: (b * nq + i, 0)),
        out_shape=jax.ShapeDtypeStruct((n_seq * seq, ah * vd), BF),
        scratch_shapes=[
            pltpu.VMEM((seq, kvl + rope), BF),
            pltpu.VMEM((ah * tq, kvl + rope), BF),
            pltpu.VMEM((ah * tq, LANES), F32),
            pltpu.VMEM((ah * tq, LANES), F32),
            pltpu.VMEM((ah * tq, kvl), F32),
            pltpu.VMEM((ah * tq, kb), F32),
            pltpu.VMEM((ah * tq, kb), F32),
        ],
        compiler_params=_cparams("parallel", "arbitrary"),
        name="mla_prompt",
    )(cq, ckv, kpe, wuq, wuk, wuv, cos, sin)


def _attn_sample_kernel(cq_ref, ckv_ref, kpe_ref, pckv_ref, pkpe_ref, wuq_ref, wuk_ref, wuv_ref, cos_ref, sin_ref,
                        o_ref, kcat, q_sc, m_sc, l_sc, acc_sc, s_a, s_b, s_n, *, n_heads, nope, rope, sm_scale):
    tq = cq_ref.shape[0]
    kvl = ckv_ref.shape[1]
    past = pckv_ref.shape[0]
    kb = s_a.shape[1]
    kcat[0:past, 0:kvl] = pckv_ref[...].astype(BF)
    kcat[0:past, kvl:kvl + rope] = pkpe_ref[...].astype(BF)
    kcat[past:past + tq, 0:kvl] = ckv_ref[...].astype(BF)
    kcat[past:past + tq, kvl:kvl + rope] = kpe_ref[...].astype(BF)
    _build_queries(q_sc, cq_ref, wuq_ref, wuk_ref, cos_ref, sin_ref, n_heads, nope, rope, kvl)
    _flash_init(m_sc, l_sc, acc_sc)
    rs = q_sc.shape[0]

    blocks = [(j * kb, kb, (s_a, s_b)[j % 2]) for j in range(past // kb)] + [(past, tq, s_n)]
    for cur, nxt in zip([None] + blocks, blocks + [None]):
        keys, s_next = (None, None) if nxt is None else (kcat[nxt[0]:nxt[0] + nxt[1], :], nxt[2])
        vals, s_cur = (None, None) if cur is None else (kcat[cur[0]:cur[0] + cur[1], 0:kvl], cur[2])
        _flash_step(q_sc, keys, s_next, s_cur, vals, m_sc, l_sc, acc_sc, sm_scale, rs)
    _flash_finish(o_ref, l_sc, acc_sc, wuv_ref, n_heads, tq)


def _attn_sample_call(cq, ckv, kpe, cache_ckv, cache_kpe, wuq, wuk, wuv, cos, sin, layer, row0, n_seq, tq, dims):
    ah, nope, rope, kvl, vd = dims["ah"], dims["nope"], dims["rope"], dims["kvl"], dims["vd"]
    ql = cq.shape[1]
    past = cache_ckv.shape[2]
    kb = math.gcd(past, 256)
    kern = functools.partial(_attn_sample_kernel, n_heads=ah, nope=nope, rope=rope,
                             sm_scale=(nope + rope) ** -0.5)

    def wspec(w):
        return pl.BlockSpec((None,) + w.shape[1:], lambda b: (layer,) + (0,) * (w.ndim - 1))

    def tok(n):
        return pl.BlockSpec((tq, n), lambda b: (row0 + b, 0))

    return pl.pallas_call(
        kern,
        grid=(n_seq,),
        in_specs=[
            tok(ql), tok(kvl), tok(rope),
            pl.BlockSpec((None, None, past, kvl), lambda b: (layer, b, 0, 0)),
            pl.BlockSpec((None, None, past, rope), lambda b: (layer, b, 0, 0)),
            wspec(wuq), wspec(wuk), wspec(wuv), tok(LANES), tok(LANES),
        ],
        out_specs=pl.BlockSpec((tq, ah * vd), lambda b: (b, 0)),
        out_shape=jax.ShapeDtypeStruct((n_seq * tq, ah * vd), BF),
        scratch_shapes=[
            pltpu.VMEM((past + tq, kvl + rope), BF),
            pltpu.VMEM((ah * tq, kvl + rope), BF),
            pltpu.VMEM((ah * tq, LANES), F32),
            pltpu.VMEM((ah * tq, LANES), F32),
            pltpu.VMEM((ah * tq, kvl), F32),
            pltpu.VMEM((ah * tq, kb), F32),
            pltpu.VMEM((ah * tq, kb), F32),
            pltpu.VMEM((ah * tq, tq), F32),
        ],
        compiler_params=_cparams("parallel"),
        name="mla_sample",
    )(cq, ckv, kpe, cache_ckv, cache_kpe, wuq, wuk, wuv, cos, sin)


def _outproj_kernel(x_ref, hm_ref, oa_ref, gt_ref, wm_ref, wa_ref, o_ref):
    y = _dot(hm_ref[...], wm_ref[...]) + _dot(oa_ref[...], wa_ref[...])
    for g in range(x_ref.shape[0] // CHUNK):
        rows = pl.ds(g * CHUNK, CHUNK)
        o_ref[rows, :] = x_ref[rows, :] + gt_ref[g:g + 1, :] * y[g * CHUNK:(g + 1) * CHUNK, :]


def _outproj_call(x, hm, oa, mod, w_out, layer, tm):
    m, d = x.shape
    wm = hm.shape[1]
    wa = oa.shape[1]
    assert wm == wa
    ng = tm // CHUNK
    return pl.pallas_call(
        _outproj_kernel,
        grid=(m // tm,),
        in_specs=[
            pl.BlockSpec((tm, d), lambda i: (i, 0)),
            pl.BlockSpec((tm, wm), lambda i: (i, 0)),
            pl.BlockSpec((tm, wa), lambda i: (i, 0)),
            pl.BlockSpec((None, ng, d), lambda i: (layer, i, 5)),
            pl.BlockSpec((None, wm, d), lambda i: (layer, 0, 0)),
            pl.BlockSpec((None, wa, d), lambda i: (layer, 1, 0)),
        ],
        out_specs=pl.BlockSpec((tm, d), lambda i: (i, 0)),
        out_shape=jax.ShapeDtypeStruct((m, d), F32),
        compiler_params=_cparams("parallel"),
        name="out_proj",
    )(x, hm, oa, mod, w_out, w_out)


def _final_norm_kernel(x_ref, g_ref, o_ref):
    o_ref[...] = _rms(x_ref[...], g_ref[...])


def _final_norm_call(x, g, row0, rows, tm):
    d = x.shape[1]
    return pl.pallas_call(
        _final_norm_kernel,
        grid=(rows // tm,),
        in_specs=[pl.BlockSpec((tm, d), lambda i: (row0 + i, 0)), pl.BlockSpec((1, d), lambda i: (0, 0))],
        out_specs=pl.BlockSpec((tm, d), lambda i: (i, 0)),
        out_shape=jax.ShapeDtypeStruct((rows, d), F32),
        compiler_params=_cparams("parallel"),
        name="final_norm",
    )(x, g)


def _rope_tables(pos, rope):
    half = rope // 2
    freqs = ROPE_THETA ** (-jnp.arange(half, dtype=F32) / half)
    ang = pos.astype(F32)[:, None] * freqs[None, :]
    cos = jnp.cos(ang)
    sin = jnp.sin(ang)
    reps = LANES // rope
    return jnp.tile(jnp.concatenate([cos, cos], axis=1), (1, reps)), jnp.tile(jnp.concatenate([-sin, sin], axis=1), (1, reps))


def _pick_tile(m, cap):
    t = cap
    while m % t:
        t //= 2
    return t


def kernel(x_prompt, x_sample, c_prompt, c_sample, cache_ckv, cache_kpe, state_C, state_n, state_m, mod_w, mod_b, ln_ffn1, ffn1_w_in, ffn1_w_out, ln_mix, w_in, mlstm_b_i, mlstm_b_f, mlstm_norm, q_norm, w_uq, kv_norm, w_uk, w_uv, w_out, ln_ffn2, ffn2_w_in, ffn2_w_out, final_norm):
    bp, sp, d = x_prompt.shape
    bs, ss, _ = x_sample.shape
    depth = mod_w.shape[0]
    past = cache_ckv.shape[2]
    mh, dv, dk = state_C.shape[2:]
    kvl, ah, nope = w_uk.shape[1:]
    vd = w_uv.shape[3]
    rope = cache_kpe.shape[3]
    ql = q_norm.shape[1]
    hdk, hdv = mh * dk, mh * dv
    dff = ffn1_w_out.shape[1]
    dims = dict(mh=mh, dk=dk, dv=dv, hdk=hdk, hdv=hdv, ql=ql, kvl=kvl, rope=rope, ah=ah, nope=nope, vd=vd)
    assert sp % CHUNK == 0 and ss == CHUNK and LANES % rope == 0 and ah % (LANES // rope) == 0
    mp, ms = bp * sp, bs * ss
    m = mp + ms
    tm = _pick_tile(math.gcd(mp, ms), 512)
    tf = _pick_tile(dff, 512)
    tq = _pick_tile(sp, 256)

    x = jnp.concatenate([x_prompt.reshape(mp, d), x_sample.reshape(ms, d)], axis=0)
    cg = jnp.concatenate([jnp.repeat(c_prompt, sp // CHUNK, axis=0), jnp.repeat(c_sample, ss // CHUNK, axis=0)], axis=0)
    pos = jnp.concatenate([jnp.tile(jnp.arange(sp), bp), jnp.tile(past + jnp.arange(ss), bs)])
    cos, sin = _rope_tables(pos, rope)

    o_mq, o_mk, o_mv, o_mo = 0, hdk, 2 * hdk, 2 * hdk + hdv
    o_mi = o_mo + hdv
    o_mf, o_qa = o_mi + mh, o_mi + 2 * mh
    o_kva, o_pe = o_qa + ql, o_qa + ql + kvl
    wqk = w_in[:, :, o_mq:o_mv].astype(BF)
    wv = w_in[:, :, o_mv:o_mo].astype(BF)
    wo = w_in[:, :, o_mo:o_mi].astype(BF)
    zpad = lambda n: jnp.zeros((depth, d, n), w_in.dtype)
    wa = jnp.concatenate([w_in[:, :, o_qa:o_pe + rope], zpad(LANES - rope), w_in[:, :, o_mi:o_qa], zpad(LANES - 2 * mh)], axis=2).astype(BF)
    gate_bias = jnp.concatenate([mlstm_b_i, mlstm_b_f, jnp.zeros((depth, LANES - 2 * mh), F32)], axis=1).reshape(depth, 1, LANES)
    wuq4 = w_uq.reshape(depth, ql, ah, nope + rope)
    wuq = jnp.concatenate([wuq4[..., :nope].reshape(depth, ql, ah * nope), wuq4[..., nope:].reshape(depth, ql, ah * rope)], axis=2).astype(BF)
    wuk = jnp.transpose(w_uk, (0, 2, 3, 1)).astype(BF)
    wuv = jnp.transpose(w_uv, (0, 2, 1, 3)).astype(BF)
    w_out_b = w_out.astype(BF)
    f1_in, f1_out = ffn1_w_in.astype(BF), ffn1_w_out.astype(BF)
    f2_in, f2_out = ffn2_w_in.astype(BF), ffn2_w_out.astype(BF)
    r3 = lambda a: a.reshape(depth, 1, a.shape[1])

    mod = _mod_call(cg, mod_w, mod_b)

    zeros_c = jnp.zeros((bp, mh, dv, dk), F32)
    zeros_n = jnp.zeros((bp, mh, dk), F32)
    zeros_m = jnp.zeros((bp, mh, LANES), F32)
    m0_s = jnp.broadcast_to(state_m[..., None], state_m.shape + (LANES,))

    outs = {k: [] for k in ("p_ckv", "p_kpe", "p_C", "p_n", "p_m", "s_ckv", "s_kpe", "s_C", "s_n", "s_m")}
    for l in range(depth):
        x = _ffn_call(x, mod, r3(ln_ffn1), f1_in, f1_out, l, 0, tm, tf)
        q, k, v, og, cq, ckv, kpe, gates = _inproj_call(
            x, mod, r3(ln_mix), wqk, wv, wo, wa, r3(q_norm), r3(kv_norm), gate_bias, cos, sin, l, tm, dims)
        gates_r = jnp.transpose(gates[:, :2 * mh].reshape(m // CHUNK, CHUNK, 2 * mh), (0, 2, 1))
        nrm = r3(mlstm_norm)
        hm_p, c_p, n_p, m_p = _mlstm_call(q, k, v, og, gates, gates_r, zeros_c, zeros_n, zeros_m, nrm,
                                          l, 0, bp, sp // CHUNK, mh)
        hm_s, c_s, n_s, m_s = _mlstm_call(q, k, v, og, gates, gates_r, state_C[l], state_n[l], m0_s[l], nrm,
                                          l, mp // CHUNK, bs, ss // CHUNK, mh)
        oa_p = _attn_prompt_call(cq, ckv, kpe, wuq, wuk, wuv, cos, sin, l, bp, sp, tq, dims)
        oa_s = _attn_sample_call(cq, ckv, kpe, cache_ckv, cache_kpe, wuq, wuk, wuv, cos, sin, l, mp // ss, bs, ss, dims)
        hm = jnp.concatenate([hm_p, hm_s], axis=0)
        oa = jnp.concatenate([oa_p, oa_s], axis=0)
        x = _outproj_call(x, hm, oa, mod, w_out_b, l, tm)
        x = _ffn_call(x, mod, r3(ln_ffn2), f2_in, f2_out, l, 6, tm, tf)
        outs["p_ckv"].append(ckv[:mp].reshape(bp, sp, kvl))
        outs["p_kpe"].append(kpe[:mp].reshape(bp, sp, rope))
        outs["s_ckv"].append(ckv[mp:].reshape(bs, ss, kvl))
        outs["s_kpe"].append(kpe[mp:].reshape(bs, ss, rope))
        outs["p_C"].append(c_p)
        outs["p_n"].append(n_p)
        outs["p_m"].append(m_p[..., 0])
        outs["s_C"].append(c_s)
        outs["s_n"].append(n_s)
        outs["s_m"].append(m_s[..., 0])

    fn = final_norm.reshape(1, d)
    y_prompt = _final_norm_call(x, fn, 0, mp, tm).reshape(bp, sp, d)
    y_sample = _final_norm_call(x, fn, mp // tm, ms, tm).reshape(bs, ss, d)
    st = {k: jnp.stack(v) for k, v in outs.items()}
    return (y_prompt, y_sample, st["p_ckv"], st["p_kpe"], st["p_C"], st["p_n"], st["p_m"],
            st["s_ckv"], st["s_kpe"], st["s_C"], st["s_n"], st["s_m"])
```

```python
import functools
import math

import jax
import jax.numpy as jnp
from jax import lax
from jax.experimental import pallas as pl
from jax.experimental.pallas import tpu as pltpu

CHUNK = 64
EPS = 1e-6
ROPE_THETA = 10000.0
N_MOD = 9
LANES = 128
VMEM_LIMIT = 56 * 1024 * 1024

BF = jnp.bfloat16
F32 = jnp.float32


def _cparams(*sem):
    return pltpu.CompilerParams(dimension_semantics=sem, vmem_limit_bytes=VMEM_LIMIT)


def _dot(a, b):
    return jnp.dot(a, b, preferred_element_type=F32)


def _dot_nt(a, b):
    return lax.dot_general(a, b, (((1,), (1,)), ((), ())), preferred_element_type=F32)


def _dot_tn(a, b):
    return lax.dot_general(a, b, (((0,), (0,)), ((), ())), preferred_element_type=F32)


def _rms(x, g):
    ms = jnp.mean(x * x, axis=-1, keepdims=True)
    return x * lax.rsqrt(ms + EPS) * g


def _silu(x):
    return x * jax.nn.sigmoid(x)


def _rope128(pe, cos, sin):
    lane = lax.broadcasted_iota(jnp.int32, pe.shape, 1)
    first_half = jnp.bitwise_and(lane, 63) < 32
    swapped = jnp.where(first_half, pltpu.roll(pe, 96, 1), pltpu.roll(pe, 32, 1))
    return pe * cos + swapped * sin


def _mod_kernel(c_ref, w_ref, b_ref, o_ref):
    a = _silu(c_ref[...]).astype(BF)
    o_ref[...] = _dot(a, w_ref[...].astype(BF)) + b_ref[...]


def _mod_call(cg, mod_w, mod_b):
    depth, d, nd = mod_w.shape
    g = cg.shape[0]
    tn = 1024
    return pl.pallas_call(
        _mod_kernel,
        grid=(depth, nd // tn),
        in_specs=[
            pl.BlockSpec((g, d), lambda l, j: (0, 0)),
            pl.BlockSpec((None, d, tn), lambda l, j: (l, 0, j)),
            pl.BlockSpec((None, 1, tn), lambda l, j: (l, 0, j)),
        ],
        out_specs=pl.BlockSpec((None, g, tn), lambda l, j: (l, 0, j)),
        out_shape=jax.ShapeDtypeStruct((depth, g, nd), F32),
        compiler_params=_cparams("parallel", "parallel"),
        name="adaln_mod",
    )(cg, mod_w, mod_b.reshape(depth, 1, nd))


def _norm_mod_to(h_ref, x_ref, ln_ref, sh_ref, sc_ref):
    ln = ln_ref[...]
    for g in range(x_ref.shape[0] // CHUNK):
        rows = pl.ds(g * CHUNK, CHUNK)
        y = _rms(x_ref[rows, :], ln)
        h_ref[rows, :] = (y * (1.0 + sc_ref[g:g + 1, :]) + sh_ref[g:g + 1, :]).astype(h_ref.dtype)


def _ffn_kernel(x_ref, sh_ref, sc_ref, gt_ref, ln_ref, wg_ref, wu_ref, wo_ref, o_ref, h_sc):
    j = pl.program_id(1)
    nj = pl.num_programs(1)

    @pl.when(j == 0)
    def _():
        _norm_mod_to(h_sc, x_ref, ln_ref, sh_ref, sc_ref)
        o_ref[...] = jnp.zeros(o_ref.shape, o_ref.dtype)

    h = h_sc[...]
    a = (_silu(_dot(h, wg_ref[...])) * _dot(h, wu_ref[...])).astype(BF)
    o_ref[...] += _dot(a, wo_ref[...])

    @pl.when(j == nj - 1)
    def _():
        for g in range(x_ref.shape[0] // CHUNK):
            rows = pl.ds(g * CHUNK, CHUNK)
            o_ref[rows, :] = x_ref[rows, :] + (0.5 * gt_ref[g:g + 1, :]) * o_ref[rows, :]


def _ffn_call(x, mod, ln, w_in, w_out, layer, k0, tm, tf):
    m, d = x.shape
    f = w_out.shape[1]
    ng = tm // CHUNK
    nf = f // tf

    def mod_spec(k):
        return pl.BlockSpec((None, ng, d), lambda i, j: (layer, i, k))

    return pl.pallas_call(
        _ffn_kernel,
        grid=(m // tm, nf),
        in_specs=[
            pl.BlockSpec((tm, d), lambda i, j: (i, 0)),
            mod_spec(k0), mod_spec(k0 + 1), mod_spec(k0 + 2),
            pl.BlockSpec((None, 1, d), lambda i, j: (layer, 0, 0)),
            pl.BlockSpec((None, d, tf), lambda i, j: (layer, 0, j)),
            pl.BlockSpec((None, d, tf), lambda i, j: (layer, 0, j + nf)),
            pl.BlockSpec((None, tf, d), lambda i, j: (layer, j, 0)),
        ],
        out_specs=pl.BlockSpec((tm, d), lambda i, j: (i, 0)),
        out_shape=jax.ShapeDtypeStruct((m, d), F32),
        scratch_shapes=[pltpu.VMEM((tm, d), BF)],
        compiler_params=_cparams("parallel", "arbitrary"),
        name="ffn",
    )(x, mod, mod, mod, ln, w_in, w_in, w_out)


def _inproj_kernel(x_ref, sh_ref, sc_ref, ln_ref, wqk_ref, wv_ref, wo_ref, wa_ref, qn_ref, kvn_ref, gb_ref,
                   cos_ref, sin_ref, q_ref, k_ref, v_ref, og_ref, cq_ref, ckv_ref, kpe_ref, gt_ref, h_sc,
                   *, n_heads, k_scale):
    _norm_mod_to(h_sc, x_ref, ln_ref, sh_ref, sc_ref)
    h = h_sc[...]
    hdk = q_ref.shape[1]
    zqk = _dot(h, wqk_ref[...])
    q_ref[...] = zqk[:, :hdk].astype(BF)
    k_ref[...] = (zqk[:, hdk:] * k_scale).astype(BF)
    v_ref[...] = _dot(h, wv_ref[...]).astype(BF)
    og_ref[...] = jax.nn.sigmoid(_dot(h, wo_ref[...]))
    za = _dot(h, wa_ref[...])
    ql = cq_ref.shape[1]
    kvl = ckv_ref.shape[1]
    cq_ref[...] = _rms(za[:, :ql], qn_ref[...]).astype(BF)
    ckv_ref[...] = _rms(za[:, ql:ql + kvl], kvn_ref[...])
    pe = _rope128(za[:, ql + kvl:ql + kvl + LANES], cos_ref[...], sin_ref[...])
    kpe_ref[...] = pe[:, :kpe_ref.shape[1]]
    zg = za[:, ql + kvl + LANES:] + gb_ref[...]
    lane = lax.broadcasted_iota(jnp.int32, zg.shape, 1)
    log_sig = jnp.minimum(zg, 0.0) - jnp.log1p(jnp.exp(-jnp.abs(zg)))
    gt_ref[...] = jnp.where(lane < n_heads, zg, log_sig)


def _inproj_call(x, mod, ln, wqk, wv, wo, wa, q_norm, kv_norm, gate_bias, cos, sin, layer, tm, dims):
    m, d = x.shape
    ng = tm // CHUNK
    hdk, hdv, ql, kvl, rope = dims["hdk"], dims["hdv"], dims["ql"], dims["kvl"], dims["rope"]

    def mod_spec(k):
        return pl.BlockSpec((None, ng, d), lambda i: (layer, i, k))

    def wspec(w):
        return pl.BlockSpec((None,) + w.shape[1:], lambda i: (layer, 0, 0))

    def row(n):
        return pl.BlockSpec((tm, n), lambda i: (i, 0))

    kern = functools.partial(_inproj_kernel, n_heads=dims["mh"], k_scale=dims["dk"] ** -0.5)
    return pl.pallas_call(
        kern,
        grid=(m // tm,),
        in_specs=[
            row(d), mod_spec(3), mod_spec(4), wspec(ln), wspec(wqk), wspec(wv), wspec(wo), wspec(wa),
            wspec(q_norm), wspec(kv_norm), wspec(gate_bias), row(LANES), row(LANES),
        ],
        out_specs=[row(hdk), row(hdk), row(hdv), row(hdv), row(ql), row(kvl), row(rope), row(LANES)],
        out_shape=[
            jax.ShapeDtypeStruct((m, hdk), BF), jax.ShapeDtypeStruct((m, hdk), BF),
            jax.ShapeDtypeStruct((m, hdv), BF), jax.ShapeDtypeStruct((m, hdv), F32),
            jax.ShapeDtypeStruct((m, ql), BF), jax.ShapeDtypeStruct((m, kvl), F32),
            jax.ShapeDtypeStruct((m, rope), F32), jax.ShapeDtypeStruct((m, LANES), F32),
        ],
        scratch_shapes=[pltpu.VMEM((tm, d), BF)],
        compiler_params=_cparams("parallel"),
        name="in_proj",
    )(x, mod, mod, ln, wqk, wv, wo, wa, q_norm, kv_norm, gate_bias, cos, sin)


def _mlstm_kernel(*refs, n_heads, bt):
    tok = [refs[6 * u:6 * u + 6] for u in range(bt)]
    c0_ref, n0_ref, m0_ref, nrm_ref, hm_ref, c_ref, n_ref, m_ref = refs[6 * bt:]
    c_idx = pl.program_id(1)

    @pl.when(c_idx == 0)
    def _():
        c_ref[...] = c0_ref[...]
        n_ref[...] = n0_ref[...]
        m_ref[...] = m0_ref[...]

    L = tok[0][0].shape[0]
    dk = tok[0][0].shape[1] // n_heads
    dv = tok[0][2].shape[1] // n_heads
    chains = [(u, h) for u in range(bt) for h in range(n_heads)]
    nc = len(chains)

    def stack(fn):
        return jnp.concatenate([fn(u, h) for u, h in chains], axis=0)

    def per_chain_last(col):
        last = col.reshape(nc, L, 1)[:, L - 1:L, :]
        return jnp.broadcast_to(last, (nc, L, 1)).reshape(nc * L, 1)

    t_idx = jnp.bitwise_and(lax.broadcasted_iota(jnp.int32, (nc * L, L), 0), L - 1)
    s_idx = lax.broadcasted_iota(jnp.int32, (nc * L, L), 1)
    causal = s_idx <= t_idx
    ig_c = stack(lambda u, h: tok[u][4][:, h:h + 1])
    lf_c = stack(lambda u, h: tok[u][4][:, n_heads + h:n_heads + h + 1])
    ig_r = stack(lambda u, h: jnp.broadcast_to(tok[u][5][h:h + 1, :], (L, L)))
    lf_r = stack(lambda u, h: jnp.broadcast_to(tok[u][5][n_heads + h:n_heads + h + 1, :], (L, L)))
    m_prev = stack(lambda u, h: jnp.broadcast_to(m_ref[u, h:h + 1, 0:1], (L, 1)))
    b_c = jnp.sum(jnp.where(causal, lf_r, 0.0), axis=1, keepdims=True)
    b_r = jnp.sum(jnp.where(causal, 0.0, lf_c).reshape(nc, L, L), axis=1, keepdims=True)
    b_r = jnp.broadcast_to(b_r, (nc, L, L)).reshape(nc * L, L) + lf_r
    b_last = per_chain_last(b_c)
    dmat = jnp.where(causal, b_c - b_r + ig_r, -jnp.inf)
    g_c = b_c + m_prev
    m_t = jnp.maximum(g_c, jnp.max(dmat, axis=1, keepdims=True))
    w_intra = jnp.exp(dmat - m_t)
    w_inter = jnp.exp(g_c - m_t)
    m_new = per_chain_last(m_t)
    wa = jnp.exp(b_last - b_c + ig_c - m_new)
    decay = jnp.exp(b_last + m_prev - m_new)

    qs = {c: tok[c[0]][0][:, c[1] * dk:(c[1] + 1) * dk] for c in chains}
    ks = {c: tok[c[0]][1][:, c[1] * dk:(c[1] + 1) * dk] for c in chains}
    vs = {c: tok[c[0]][2][:, c[1] * dv:(c[1] + 1) * dv] for c in chains}
    c_prev = {c: c_ref[c[0], c[1]] for c in chains}
    n_prev = {c: n_ref[c[0], c[1]:c[1] + 1, :] for c in chains}
    s = stack(lambda u, h: _dot_nt(qs[u, h], ks[u, h])) * w_intra
    qc = stack(lambda u, h: _dot_nt(qs[u, h], c_prev[u, h].astype(BF)))
    s_bf = s.astype(BF)
    sv = jnp.concatenate([_dot(s_bf[i * L:(i + 1) * L, :], vs[c]) for i, c in enumerate(chains)], axis=0)
    qn = jnp.sum(stack(lambda u, h: qs[u, h].astype(F32) * n_prev[u, h]), axis=1, keepdims=True)
    num = sv + w_inter * qc
    nq = jnp.sum(s, axis=1, keepdims=True) + w_inter * qn
    x = num / jnp.maximum(jnp.abs(nq), jnp.exp(-m_t))
    x = x * stack(lambda u, h: tok[u][3][:, h * dv:(h + 1) * dv])
    x = x * lax.rsqrt(jnp.mean(x * x, axis=-1, keepdims=True) + EPS)
    wk = wa * stack(lambda u, h: ks[u, h].astype(F32))
    wk_bf = wk.astype(BF)
    for i, (u, h) in enumerate(chains):
        rows = slice(i * L, (i + 1) * L)
        hm_ref[u, :, h * dv:(h + 1) * dv] = (x[rows, :] * nrm_ref[:, h * dv:(h + 1) * dv]).astype(hm_ref.dtype)
        dec = decay[i * L:i * L + 1, :]
        c_ref[u, h] = dec * c_prev[u, h] + _dot_tn(vs[u, h], wk_bf[rows, :])
        n_ref[u, h:h + 1, :] = dec * n_prev[u, h] + jnp.sum(wk[rows, :], axis=0, keepdims=True)
        m_ref[u, h:h + 1, :] = jnp.broadcast_to(m_new[i * L:i * L + 1, :], (1, m_ref.shape[2]))


def _mlstm_call(q, k, v, og, gates_c, gates_r, c0, n0, m0, norm, layer, row0, n_seq, n_chunks, n_heads):
    hdk = q.shape[1]
    hdv = v.shape[1]
    dv, dk = c0.shape[-2:]
    g2 = gates_r.shape[1]
    bt = _pick_tile(n_seq, 4)

    def chunked(a):
        return a.reshape(a.shape[0] // CHUNK, CHUNK, a.shape[1])

    arrays = [chunked(q), chunked(k), chunked(v), chunked(og), chunked(gates_c), gates_r]
    shapes = [(CHUNK, hdk), (CHUNK, hdk), (CHUNK, hdv), (CHUNK, hdv), (CHUNK, LANES), (g2, CHUNK)]

    def tok(u, shape):
        return pl.BlockSpec((None,) + shape, lambda t, c: (row0 + (t * bt + u) * n_chunks + c, 0, 0))

    def state(shape):
        return pl.BlockSpec((bt,) + shape, lambda t, c: (t,) + (0,) * len(shape))

    in_specs = [tok(u, shape) for u in range(bt) for shape in shapes]
    in_specs += [state((n_heads, dv, dk)), state((n_heads, dk)), state((n_heads, LANES)),
                 pl.BlockSpec((None, 1, hdv), lambda t, c: (layer, 0, 0))]
    kern = functools.partial(_mlstm_kernel, n_heads=n_heads, bt=bt)
    return pl.pallas_call(
        kern,
        grid=(n_seq // bt, n_chunks),
        in_specs=in_specs,
        out_specs=[
            pl.BlockSpec((None, bt, CHUNK, hdv), lambda t, c: (c, t, 0, 0)),
            state((n_heads, dv, dk)), state((n_heads, dk)), state((n_heads, LANES)),
        ],
        out_shape=[
            jax.ShapeDtypeStruct((n_chunks, n_seq, CHUNK, hdv), BF),
            jax.ShapeDtypeStruct((n_seq, n_heads, dv, dk), F32),
            jax.ShapeDtypeStruct((n_seq, n_heads, dk), F32),
            jax.ShapeDtypeStruct((n_seq, n_heads, LANES), F32),
        ],
        compiler_params=_cparams("parallel", "arbitrary"),
        name="mlstm",
    )(*(arrays * bt), c0, n0, m0, norm)


def _build_queries(q_sc, cq_ref, wuq_ref, wuk_ref, cos_ref, sin_ref, n_heads, nope, rope, kvl):
    tq = cq_ref.shape[0]
    qa = _dot(cq_ref[...], wuq_ref[...])
    cos = cos_ref[...]
    sin = sin_ref[...]
    for h in range(n_heads):
        qn = qa[:, h * nope:(h + 1) * nope].astype(BF)
        q_sc[h * tq:(h + 1) * tq, 0:kvl] = _dot(qn, wuk_ref[h]).astype(BF)
    per = LANES // rope
    for p in range(n_heads // per):
        base = n_heads * nope + p * LANES
        pe = _rope128(qa[:, base:base + LANES], cos, sin)
        for u in range(per):
            h = p * per + u
            q_sc[h * tq:(h + 1) * tq, kvl:kvl + rope] = pe[:, u * rope:(u + 1) * rope].astype(BF)


def _lane_tile(x, width):
    if width <= LANES:
        return x[:, :width]
    return jnp.concatenate([x] * (width // LANES), axis=1)


def _flash_step(q_sc, keys_next, s_next, s_cur, vals_cur, m_sc, l_sc, acc_sc, sm_scale, rs, mask_chunks=None):
    n_groups = q_sc.shape[0] // rs

    def group(r, carry):
        r0 = pl.multiple_of(r * rs, rs)
        rows = pl.ds(r0, rs)
        if s_next is not None:
            s_next[rows, :] = _dot_nt(q_sc[rows, :], keys_next)
        if s_cur is not None:
            kb, kvl = vals_cur.shape
            s = s_cur[rows, :] * sm_scale
            if mask_chunks is not None:
                q_chunks, q_chunk0, k_chunk0 = mask_chunks
                chunk_bits = CHUNK.bit_length() - 1
                row = lax.broadcasted_iota(jnp.int32, (rs, kb), 0) + r0
                q_chunk = jnp.bitwise_and(jnp.right_shift(row, chunk_bits), q_chunks - 1) + q_chunk0
                k_chunk = jnp.right_shift(lax.broadcasted_iota(jnp.int32, (rs, kb), 1), chunk_bits) + k_chunk0
                s = jnp.where(k_chunk <= q_chunk, s, -jnp.inf)
            m_prev = m_sc[rows, :]
            m_new = jnp.maximum(m_prev, jnp.max(s, axis=1, keepdims=True))
            alpha = jnp.exp(m_prev - m_new)
            p = jnp.exp(s - _lane_tile(m_new, kb))
            l_sc[rows, :] = alpha * l_sc[rows, :] + jnp.sum(p, axis=1, keepdims=True)
            acc_sc[rows, :] = _lane_tile(alpha, kvl) * acc_sc[rows, :] + _dot(p.astype(BF), vals_cur)
            m_sc[rows, :] = m_new
        return carry

    lax.fori_loop(0, n_groups, group, 0, unroll=2 if n_groups % 2 == 0 else 1)


def _flash_init(m_sc, l_sc, acc_sc):
    m_sc[...] = jnp.full(m_sc.shape, -jnp.inf, F32)
    l_sc[...] = jnp.zeros(l_sc.shape, F32)
    acc_sc[...] = jnp.zeros(acc_sc.shape, F32)


def _flash_finish(o_ref, l_sc, acc_sc, wuv_ref, n_heads, tq):
    vd = wuv_ref.shape[2]
    kvl = acc_sc.shape[1]
    for h in range(n_heads):
        rows = pl.ds(h * tq, tq)
        o_lat = (acc_sc[rows, :] / _lane_tile(l_sc[rows, :], kvl)).astype(BF)
        o_ref[:, h * vd:(h + 1) * vd] = _dot(o_lat, wuv_ref[h]).astype(o_ref.dtype)


def _attn_prompt_kernel(cq_ref, ckv_ref, kpe_ref, wuq_ref, wuk_ref, wuv_ref, cos_ref, sin_ref, o_ref,
                        kcat, q_sc, m_sc, l_sc, acc_sc, s_a, s_b, *, n_heads, nope, rope, sm_scale, rs):
    i = pl.program_id(1)
    tq = cq_ref.shape[0]
    kvl = ckv_ref.shape[1]
    kb = s_a.shape[1]

    @pl.when(i == 0)
    def _():
        kcat[:, 0:kvl] = ckv_ref[...].astype(BF)
        kcat[:, kvl:kvl + rope] = kpe_ref[...].astype(BF)

    _build_queries(q_sc, cq_ref, wuq_ref, wuk_ref, cos_ref, sin_ref, n_heads, nope, rope, kvl)
    _flash_init(m_sc, l_sc, acc_sc)

    def step(j_next, s_next, j_cur, s_cur, masked=False):
        keys = None if s_next is None else kcat[pl.ds(pl.multiple_of(j_next * kb, kb), kb), :]
        vals = None if s_cur is None else kcat[pl.ds(pl.multiple_of(j_cur * kb, kb), kb), 0:kvl]
        mask = (tq // CHUNK, i * (tq // CHUNK), j_cur * (kb // CHUNK)) if masked else None
        _flash_step(q_sc, keys, s_next, s_cur, vals, m_sc, l_sc, acc_sc, sm_scale, rs, mask)

    last = (i * tq) // kb
    step(0, s_a, None, None)

    def pair(t, carry):
        j = 2 * t
        step(j + 1, s_b, j, s_a)
        step(j + 2, s_a, j + 1, s_b)
        return carry

    lax.fori_loop(0, last // 2, pair, 0)

    @pl.when(last % 2 == 0)
    def _():
        step(None, None, last, s_a, masked=True)

    @pl.when(last % 2 == 1)
    def _():
        step(last, s_b, last - 1, s_a)
        step(None, None, last, s_b, masked=True)

    _flash_finish(o_ref, l_sc, acc_sc, wuv_ref, n_heads, tq)


def _attn_prompt_call(cq, ckv, kpe, wuq, wuk, wuv, cos, sin, layer, n_seq, seq, tq, dims):
    ah, nope, rope, kvl, vd = dims["ah"], dims["nope"], dims["rope"], dims["kvl"], dims["vd"]
    ql = cq.shape[1]
    nq = seq // tq
    kb = _pick_tile(seq, 512)
    assert tq % CHUNK == 0 and (tq // CHUNK) & (tq // CHUNK - 1) == 0 and kb % tq == 0
    rs = _pick_tile(ah * tq, 512)
    kern = functools.partial(_attn_prompt_kernel, n_heads=ah, nope=nope, rope=rope,
                             sm_scale=(nope + rope) ** -0.5, rs=rs)

    def wspec(w):
        return pl.BlockSpec((None,) + w.shape[1:], lambda b, i: (layer,) + (0,) * (w.ndim - 1))

    return pl.pallas_call(
        kern,
        grid=(n_seq, nq),
        in_specs=[
            pl.BlockSpec((tq, ql), lambda b, i: (b * nq + i, 0)),
            pl.BlockSpec((seq, kvl), lambda b, i: (b, 0)),
            pl.BlockSpec((seq, rope), lambda b, i: (b, 0)),
            wspec(wuq), wspec(wuk), wspec(wuv),
            pl.BlockSpec((tq, LANES), lambda b, i: (b * nq + i, 0)),
            pl.BlockSpec((tq, LANES), lambda b, i: (b * nq + i, 0)),
        ],
        out_specs=pl.BlockSpec((tq, ah * vd), lambda b, i: (b * nq + i, 0)),
        out_shape=jax.ShapeDtypeStruct((n_seq * seq, ah * vd), BF),
        scratch_shapes=[
            pltpu.VMEM((seq, kvl + rope), BF),
            pltpu.VMEM((ah * tq, kvl + rope), BF),
            pltpu.VMEM((ah * tq, LANES), F32),
            pltpu.VMEM((ah * tq, LANES), F32),
            pltpu.VMEM((ah * tq, kvl), F32),
            pltpu.VMEM((ah * tq, kb), F32),
            pltpu.VMEM((ah * tq, kb), F32),
        ],
        compiler_params=_cparams("parallel", "arbitrary"),
        name="mla_prompt",
    )(cq, ckv, kpe, wuq, wuk, wuv, cos, sin)


def _attn_sample_kernel(cq_ref, ckv_ref, kpe_ref, pckv_ref, pkpe_ref, wuq_ref, wuk_ref, wuv_ref, cos_ref, sin_ref,
                        o_ref, kcat, q_sc, m_sc, l_sc, acc_sc, s_a, s_b, s_n, *, n_heads, nope, rope, sm_scale):
    tq = cq_ref.shape[0]
    kvl = ckv_ref.shape[1]
    past = pckv_ref.shape[0]
    kb = s_a.shape[1]
    kcat[0:past, 0:kvl] = pckv_ref[...].astype(BF)
    kcat[0:past, kvl:kvl + rope] = pkpe_ref[...].astype(BF)
    kcat[past:past + tq, 0:kvl] = ckv_ref[...].astype(BF)
    kcat[past:past + tq, kvl:kvl + rope] = kpe_ref[...].astype(BF)
    _build_queries(q_sc, cq_ref, wuq_ref, wuk_ref, cos_ref, sin_ref, n_heads, nope, rope, kvl)
    _flash_init(m_sc, l_sc, acc_sc)
    rs = q_sc.shape[0]

    blocks = [(j * kb, kb, (s_a, s_b)[j % 2]) for j in range(past // kb)] + [(past, tq, s_n)]
    for cur, nxt in zip([None] + blocks, blocks + [None]):
        keys, s_next = (None, None) if nxt is None else (kcat[nxt[0]:nxt[0] + nxt[1], :], nxt[2])
        vals, s_cur = (None, None) if cur is None else (kcat[cur[0]:cur[0] + cur[1], 0:kvl], cur[2])
        _flash_step(q_sc, keys, s_next, s_cur, vals, m_sc, l_sc, acc_sc, sm_scale, rs)
    _flash_finish(o_ref, l_sc, acc_sc, wuv_ref, n_heads, tq)


def _attn_sample_call(cq, ckv, kpe, cache_ckv, cache_kpe, wuq, wuk, wuv, cos, sin, layer, row0, n_seq, tq, dims):
    ah, nope, rope, kvl, vd = dims["ah"], dims["nope"], dims["rope"], dims["kvl"], dims["vd"]
    ql = cq.shape[1]
    past = cache_ckv.shape[2]
    kb = math.gcd(past, 256)
    kern = functools.partial(_attn_sample_kernel, n_heads=ah, nope=nope, rope=rope,
                             sm_scale=(nope + rope) ** -0.5)

    def wspec(w):
        return pl.BlockSpec((None,) + w.shape[1:], lambda b: (layer,) + (0,) * (w.ndim - 1))

    def tok(n):
        return pl.BlockSpec((tq, n), lambda b: (row0 + b, 0))

    return pl.pallas_call(
        kern,
        grid=(n_seq,),
        in_specs=[
            tok(ql), tok(kvl), tok(rope),
            pl.BlockSpec((None, None, past, kvl), lambda b: (layer, b, 0, 0)),
            pl.BlockSpec((None, None, past, rope), lambda b: (layer, b, 0, 0)),
            wspec(wuq), wspec(wuk), wspec(wuv), tok(LANES), tok(LANES),
        ],
        out_specs=pl.BlockSpec((tq, ah * vd), lambda b: (b, 0)),
        out_shape=jax.ShapeDtypeStruct((n_seq * tq, ah * vd), BF),
        scratch_shapes=[
            pltpu.VMEM((past + tq, kvl + rope), BF),
            pltpu.VMEM((ah * tq, kvl + rope), BF),
            pltpu.VMEM((ah * tq, LANES), F32),
            pltpu.VMEM((ah * tq, LANES), F32),
            pltpu.VMEM((ah * tq, kvl), F32),
            pltpu.VMEM((ah * tq, kb), F32),
            pltpu.VMEM((ah * tq, kb), F32),
            pltpu.VMEM((ah * tq, tq), F32),
        ],
        compiler_params=_cparams("parallel"),
        name="mla_sample",
    )(cq, ckv, kpe, cache_ckv, cache_kpe, wuq, wuk, wuv, cos, sin)


def _outproj_kernel(x_ref, hm_ref, oa_ref, gt_ref, wm_ref, wa_ref, o_ref):
    y = _dot(hm_ref[...], wm_ref[...]) + _dot(oa_ref[...], wa_ref[...])
    for g in range(x_ref.shape[0] // CHUNK):
        rows = pl.ds(g * CHUNK, CHUNK)
        o_ref[rows, :] = x_ref[rows, :] + gt_ref[g:g + 1, :] * y[g * CHUNK:(g + 1) * CHUNK, :]


def _outproj_call(x, hm, oa, mod, w_out, layer, tm):
    m, d = x.shape
    wm = hm.shape[1]
    wa = oa.shape[1]
    assert wm == wa
    ng = tm // CHUNK
    return pl.pallas_call(
        _outproj_kernel,
        grid=(m // tm,),
        in_specs=[
            pl.BlockSpec((tm, d), lambda i: (i, 0)),
            pl.BlockSpec((tm, wm), lambda i: (i, 0)),
            pl.BlockSpec((tm, wa), lambda i: (i, 0)),
            pl.BlockSpec((None, ng, d), lambda i: (layer, i, 5)),
            pl.BlockSpec((None, wm, d), lambda i: (layer, 0, 0)),
            pl.BlockSpec((None, wa, d), lambda i: (layer, 1, 0)),
        ],
        out_specs=pl.BlockSpec((tm, d), lambda i: (i, 0)),
        out_shape=jax.ShapeDtypeStruct((m, d), F32),
        compiler_params=_cparams("parallel"),
        name="out_proj",
    )(x, hm, oa, mod, w_out, w_out)


def _final_norm_kernel(x_ref, g_ref, o_ref):
    o_ref[...] = _rms(x_ref[...], g_ref[...])


def _final_norm_call(x, g, row0, rows, tm):
    d = x.shape[1]
    return pl.pallas_call(
        _final_norm_kernel,
        grid=(rows // tm,),
        in_specs=[pl.BlockSpec((tm, d), lambda i: (row0 + i, 0)), pl.BlockSpec((1, d), lambda i: (0, 0))],
        out_specs=pl.BlockSpec((tm, d), lambda i: (i, 0)),
        out_shape=jax.ShapeDtypeStruct((rows, d), F32),
        compiler_params=_cparams("parallel"),
        name="final_norm",
    )(x, g)


def _rope_tables(pos, rope):
    half = rope // 2
    freqs = ROPE_THETA ** (-jnp.arange(half, dtype=F32) / half)
    ang = pos.astype(F32)[:, None] * freqs[None, :]
    cos = jnp.cos(ang)
    sin = jnp.sin(ang)
    reps = LANES // rope
    return jnp.tile(jnp.concatenate([cos, cos], axis=1), (1, reps)), jnp.tile(jnp.concatenate([-sin, sin], axis=1), (1, reps))


def _pick_tile(m, cap):
    t = cap
    while m % t:
        t //= 2
    return t


def kernel(x_prompt, x_sample, c_prompt, c_sample, cache_ckv, cache_kpe, state_C, state_n, state_m, mod_w, mod_b, ln_ffn1, ffn1_w_in, ffn1_w_out, ln_mix, w_in, mlstm_b_i, mlstm_b_f, mlstm_norm, q_norm, w_uq, kv_norm, w_uk, w_uv, w_out, ln_ffn2, ffn2_w_in, ffn2_w_out, final_norm):
    bp, sp, d = x_prompt.shape
    bs, ss, _ = x_sample.shape
    depth = mod_w.shape[0]
    past = cache_ckv.shape[2]
    mh, dv, dk = state_C.shape[2:]
    kvl, ah, nope = w_uk.shape[1:]
    vd = w_uv.shape[3]
    rope = cache_kpe.shape[3]
    ql = q_norm.shape[1]
    hdk, hdv = mh * dk, mh * dv
    dff = ffn1_w_out.shape[1]
    dims = dict(mh=mh, dk=dk, dv=dv, hdk=hdk, hdv=hdv, ql=ql, kvl=kvl, rope=rope, ah=ah, nope=nope, vd=vd)
    assert sp % CHUNK == 0 and ss == CHUNK and LANES % rope == 0 and ah % (LANES // rope) == 0
    mp, ms = bp * sp, bs * ss
    m = mp + ms
    tm = _pick_tile(math.gcd(mp, ms), 512)
    tm_ffn = _pick_tile(m, 1024)
    tf = _pick_tile(dff, 512)
    tq = _pick_tile(sp, 256)

    x = jnp.concatenate([x_prompt.reshape(mp, d), x_sample.reshape(ms, d)], axis=0)
    cg = jnp.concatenate([jnp.repeat(c_prompt, sp // CHUNK, axis=0), jnp.repeat(c_sample, ss // CHUNK, axis=0)], axis=0)
    pos = jnp.concatenate([jnp.tile(jnp.arange(sp), bp), jnp.tile(past + jnp.arange(ss), bs)])
    cos, sin = _rope_tables(pos, rope)

    o_mq, o_mk, o_mv, o_mo = 0, hdk, 2 * hdk, 2 * hdk + hdv
    o_mi = o_mo + hdv
    o_mf, o_qa = o_mi + mh, o_mi + 2 * mh
    o_kva, o_pe = o_qa + ql, o_qa + ql + kvl
    wqk = w_in[:, :, o_mq:o_mv].astype(BF)
    wv = w_in[:, :, o_mv:o_mo].astype(BF)
    wo = w_in[:, :, o_mo:o_mi].astype(BF)
    zpad = lambda n: jnp.zeros((depth, d, n), w_in.dtype)
    wa = jnp.concatenate([w_in[:, :, o_qa:o_pe + rope], zpad(LANES - rope), w_in[:, :, o_mi:o_qa], zpad(LANES - 2 * mh)], axis=2).astype(BF)
    gate_bias = jnp.concatenate([mlstm_b_i, mlstm_b_f, jnp.zeros((depth, LANES - 2 * mh), F32)], axis=1).reshape(depth, 1, LANES)
    wuq4 = w_uq.reshape(depth, ql, ah, nope + rope)
    wuq = jnp.concatenate([wuq4[..., :nope].reshape(depth, ql, ah * nope), wuq4[..., nope:].reshape(depth, ql, ah * rope)], axis=2).astype(BF)
    wuk = jnp.transpose(w_uk, (0, 2, 3, 1)).astype(BF)
    wuv = jnp.transpose(w_uv, (0, 2, 1, 3)).astype(BF)
    w_out_b = w_out.astype(BF)
    f1_in, f1_out = ffn1_w_in.astype(BF), ffn1_w_out.astype(BF)
    f2_in, f2_out = ffn2_w_in.astype(BF), ffn2_w_out.astype(BF)
    r3 = lambda a: a.reshape(depth, 1, a.shape[1])

    mod = _mod_call(cg, mod_w, mod_b)

    zeros_c = jnp.zeros((bp, mh, dv, dk), F32)
    zeros_n = jnp.zeros((bp, mh, dk), F32)
    zeros_m = jnp.zeros((bp, mh, LANES), F32)
    m0_s = jnp.broadcast_to(state_m[..., None], state_m.shape + (LANES,))

    outs = {k: [] for k in ("p_ckv", "p_kpe", "p_C", "p_n", "p_m", "s_ckv", "s_kpe", "s_C", "s_n", "s_m")}
    for l in range(depth):
        x = _ffn_call(x, mod, r3(ln_ffn1), f1_in, f1_out, l, 0, tm_ffn, tf)
        q, k, v, og, cq, ckv, kpe, gates = _inproj_call(
            x, mod, r3(ln_mix), wqk, wv, wo, wa, r3(q_norm), r3(kv_norm), gate_bias, cos, sin, l, tm, dims)
        gates_r = jnp.transpose(gates[:, :2 * mh].reshape(m // CHUNK, CHUNK, 2 * mh), (0, 2, 1))
        nrm = r3(mlstm_norm)
        hm_p, c_p, n_p, m_p = _mlstm_call(q, k, v, og, gates, gates_r, zeros_c, zeros_n, zeros_m, nrm,
                                          l, 0, bp, sp // CHUNK, mh)
        hm_s, c_s, n_s, m_s = _mlstm_call(q, k, v, og, gates, gates_r, state_C[l], state_n[l], m0_s[l], nrm,
                                          l, mp // CHUNK, bs, ss // CHUNK, mh)
        oa_p = _attn_prompt_call(cq, ckv, kpe, wuq, wuk, wuv, cos, sin, l, bp, sp, tq, dims)
        oa_s = _attn_sample_call(cq, ckv, kpe, cache_ckv, cache_kpe, wuq, wuk, wuv, cos, sin, l, mp // ss, bs, ss, dims)
        hm = jnp.concatenate([jnp.swapaxes(hm_p, 0, 1).reshape(mp, hdv), jnp.swapaxes(hm_s, 0, 1).reshape(ms, hdv)], axis=0)
        oa = jnp.concatenate([oa_p, oa_s], axis=0)
        x = _outproj_call(x, hm, oa, mod, w_out_b, l, tm)
        x = _ffn_call(x, mod, r3(ln_ffn2), f2_in, f2_out, l, 6, tm_ffn, tf)
        outs["p_ckv"].append(ckv[:mp].reshape(bp, sp, kvl))
        outs["p_kpe"].append(kpe[:mp].reshape(bp, sp, rope))
        outs["s_ckv"].append(ckv[mp:].reshape(bs, ss, kvl))
        outs["s_kpe"].append(kpe[mp:].reshape(bs, ss, rope))
        outs["p_C"].append(c_p)
        outs["p_n"].append(n_p)
        outs["p_m"].append(m_p[..., 0])
        outs["s_C"].append(c_s)
        outs["s_n"].append(n_s)
        outs["s_m"].append(m_s[..., 0])

    fn = final_norm.reshape(1, d)
    y_prompt = _final_norm_call(x, fn, 0, mp, tm).reshape(bp, sp, d)
    y_sample = _final_norm_call(x, fn, mp // tm, ms, tm).reshape(bs, ss, d)
    st = {k: jnp.stack(v) for k, v in outs.items()}
    return (y_prompt, y_sample, st["p_ckv"], st["p_kpe"], st["p_C"], st["p_n"], st["p_m"],
            st["s_ckv"], st["s_kpe"], st["s_C"], st["s_n"], st["s_m"])
```

```python
import functools
import math

import jax
import jax.numpy as jnp
from jax import lax
from jax.experimental import pallas as pl
from jax.experimental.pallas import tpu as pltpu

CHUNK = 64
EPS = 1e-6
ROPE_THETA = 10000.0
N_MOD = 9
LANES = 128
VMEM_LIMIT = 56 * 1024 * 1024

BF = jnp.bfloat16
F32 = jnp.float32


def _cparams(*sem):
    return pltpu.CompilerParams(dimension_semantics=sem, vmem_limit_bytes=VMEM_LIMIT)


def _dot(a, b):
    return jnp.dot(a, b, preferred_element_type=F32)


def _dot_nt(a, b):
    return lax.dot_general(a, b, (((1,), (1,)), ((), ())), preferred_element_type=F32)


def _dot_tn(a, b):
    return lax.dot_general(a, b, (((0,), (0,)), ((), ())), preferred_element_type=F32)


def _rms(x, g):
    ms = jnp.mean(x * x, axis=-1, keepdims=True)
    return x * lax.rsqrt(ms + EPS) * g


def _silu(x):
    return x * jax.nn.sigmoid(x)


def _rope128(pe, cos, sin):
    lane = lax.broadcasted_iota(jnp.int32, pe.shape, 1)
    first_half = jnp.bitwise_and(lane, 63) < 32
    swapped = jnp.where(first_half, pltpu.roll(pe, 96, 1), pltpu.roll(pe, 32, 1))
    return pe * cos + swapped * sin


def _mod_kernel(c_ref, w_ref, b_ref, o_ref):
    a = _silu(c_ref[...]).astype(BF)
    o_ref[...] = _dot(a, w_ref[...].astype(BF)) + b_ref[...]


def _mod_call(cg, mod_w, mod_b):
    depth, d, nd = mod_w.shape
    g = cg.shape[0]
    tn = 1024
    return pl.pallas_call(
        _mod_kernel,
        grid=(depth, nd // tn),
        in_specs=[
            pl.BlockSpec((g, d), lambda l, j: (0, 0)),
            pl.BlockSpec((None, d, tn), lambda l, j: (l, 0, j)),
            pl.BlockSpec((None, 1, tn), lambda l, j: (l, 0, j)),
        ],
        out_specs=pl.BlockSpec((None, g, tn), lambda l, j: (l, 0, j)),
        out_shape=jax.ShapeDtypeStruct((depth, g, nd), F32),
        compiler_params=_cparams("parallel", "parallel"),
        name="adaln_mod",
    )(cg, mod_w, mod_b.reshape(depth, 1, nd))


def _norm_mod_to(h_ref, x_ref, ln_ref, sh_ref, sc_ref):
    ln = ln_ref[...]
    for g in range(x_ref.shape[0] // CHUNK):
        rows = pl.ds(g * CHUNK, CHUNK)
        y = _rms(x_ref[rows, :], ln)
        h_ref[rows, :] = (y * (1.0 + sc_ref[g:g + 1, :]) + sh_ref[g:g + 1, :]).astype(h_ref.dtype)


def _ffn_kernel(x_ref, sh_ref, sc_ref, gt_ref, ln_ref, wg_ref, wu_ref, wo_ref, o_ref, h_sc):
    j = pl.program_id(1)
    nj = pl.num_programs(1)

    def up_down(h):
        a = (_silu(_dot(h, wg_ref[...])) * _dot(h, wu_ref[...])).astype(BF)
        return _dot(a, wo_ref[...])

    @pl.when(j == 0)
    def _():
        _norm_mod_to(h_sc, x_ref, ln_ref, sh_ref, sc_ref)
        o_ref[...] = up_down(h_sc[...])

    @pl.when(j > 0)
    def _():
        o_ref[...] += up_down(h_sc[...])

    @pl.when(j == nj - 1)
    def _():
        for g in range(x_ref.shape[0] // CHUNK):
            rows = pl.ds(g * CHUNK, CHUNK)
            o_ref[rows, :] = x_ref[rows, :] + (0.5 * gt_ref[g:g + 1, :]) * o_ref[rows, :]


def _ffn_call(x, mod, ln, w_in, w_out, layer, k0, tm, tf):
    m, d = x.shape
    f = w_out.shape[1]
    ng = tm // CHUNK
    nf = f // tf

    def mod_spec(k):
        return pl.BlockSpec((None, ng, d), lambda i, j: (layer, i, k))

    return pl.pallas_call(
        _ffn_kernel,
        grid=(m // tm, nf),
        in_specs=[
            pl.BlockSpec((tm, d), lambda i, j: (i, 0)),
            mod_spec(k0), mod_spec(k0 + 1), mod_spec(k0 + 2),
            pl.BlockSpec((None, 1, d), lambda i, j: (layer, 0, 0)),
            pl.BlockSpec((None, d, tf), lambda i, j: (layer, 0, j)),
            pl.BlockSpec((None, d, tf), lambda i, j: (layer, 0, j + nf)),
            pl.BlockSpec((None, tf, d), lambda i, j: (layer, j, 0)),
        ],
        out_specs=pl.BlockSpec((tm, d), lambda i, j: (i, 0)),
        out_shape=jax.ShapeDtypeStruct((m, d), F32),
        scratch_shapes=[pltpu.VMEM((tm, d), BF)],
        compiler_params=_cparams("parallel", "arbitrary"),
        name="ffn",
    )(x, mod, mod, mod, ln, w_in, w_in, w_out)


def _inproj_kernel(x_ref, sh_ref, sc_ref, ln_ref, wqk_ref, wv_ref, wo_ref, wa_ref, qn_ref, kvn_ref, gb_ref,
                   cos_ref, sin_ref, q_ref, k_ref, v_ref, og_ref, cq_ref, ckv_ref, kpe_ref, gt_ref, h_sc,
                   *, n_heads, k_scale):
    _norm_mod_to(h_sc, x_ref, ln_ref, sh_ref, sc_ref)
    h = h_sc[...]
    hdk = q_ref.shape[1]
    zqk = _dot(h, wqk_ref[...])
    q_ref[...] = zqk[:, :hdk].astype(BF)
    k_ref[...] = (zqk[:, hdk:] * k_scale).astype(BF)
    v_ref[...] = _dot(h, wv_ref[...]).astype(BF)
    og_ref[...] = jax.nn.sigmoid(_dot(h, wo_ref[...]))
    za = _dot(h, wa_ref[...])
    ql = cq_ref.shape[1]
    kvl = ckv_ref.shape[1]
    cq_ref[...] = _rms(za[:, :ql], qn_ref[...]).astype(BF)
    ckv_ref[...] = _rms(za[:, ql:ql + kvl], kvn_ref[...])
    pe = _rope128(za[:, ql + kvl:ql + kvl + LANES], cos_ref[...], sin_ref[...])
    kpe_ref[...] = pe[:, :kpe_ref.shape[1]]
    zg = za[:, ql + kvl + LANES:] + gb_ref[...]
    lane = lax.broadcasted_iota(jnp.int32, zg.shape, 1)
    log_sig = jnp.minimum(zg, 0.0) - jnp.log1p(jnp.exp(-jnp.abs(zg)))
    gt_ref[...] = jnp.where(lane < n_heads, zg, log_sig)


def _inproj_call(x, mod, ln, wqk, wv, wo, wa, q_norm, kv_norm, gate_bias, cos, sin, layer, tm, dims):
    m, d = x.shape
    ng = tm // CHUNK
    hdk, hdv, ql, kvl, rope = dims["hdk"], dims["hdv"], dims["ql"], dims["kvl"], dims["rope"]

    def mod_spec(k):
        return pl.BlockSpec((None, ng, d), lambda i: (layer, i, k))

    def wspec(w):
        return pl.BlockSpec((None,) + w.shape[1:], lambda i: (layer, 0, 0))

    def row(n):
        return pl.BlockSpec((tm, n), lambda i: (i, 0))

    kern = functools.partial(_inproj_kernel, n_heads=dims["mh"], k_scale=dims["dk"] ** -0.5)
    return pl.pallas_call(
        kern,
        grid=(m // tm,),
        in_specs=[
            row(d), mod_spec(3), mod_spec(4), wspec(ln), wspec(wqk), wspec(wv), wspec(wo), wspec(wa),
            wspec(q_norm), wspec(kv_norm), wspec(gate_bias), row(LANES), row(LANES),
        ],
        out_specs=[row(hdk), row(hdk), row(hdv), row(hdv), row(ql), row(kvl), row(rope), row(LANES)],
        out_shape=[
            jax.ShapeDtypeStruct((m, hdk), BF), jax.ShapeDtypeStruct((m, hdk), BF),
            jax.ShapeDtypeStruct((m, hdv), BF), jax.ShapeDtypeStruct((m, hdv), F32),
            jax.ShapeDtypeStruct((m, ql), BF), jax.ShapeDtypeStruct((m, kvl), F32),
            jax.ShapeDtypeStruct((m, rope), F32), jax.ShapeDtypeStruct((m, LANES), F32),
        ],
        scratch_shapes=[pltpu.VMEM((tm, d), BF)],
        compiler_params=_cparams("parallel"),
        name="in_proj",
    )(x, mod, mod, ln, wqk, wv, wo, wa, q_norm, kv_norm, gate_bias, cos, sin)


def _mlstm_kernel(*refs, n_heads, bt):
    tok = [refs[6 * u:6 * u + 6] for u in range(bt)]
    c0_ref, n0_ref, m0_ref, nrm_ref, hm_ref, c_ref, n_ref, m_ref = refs[6 * bt:]
    c_idx = pl.program_id(1)

    @pl.when(c_idx == 0)
    def _():
        c_ref[...] = c0_ref[...]
        n_ref[...] = n0_ref[...]
        m_ref[...] = m0_ref[...]

    L = tok[0][0].shape[0]
    dk = tok[0][0].shape[1] // n_heads
    dv = tok[0][2].shape[1] // n_heads
    chains = [(u, h) for u in range(bt) for h in range(n_heads)]
    nc = len(chains)

    def stack(fn):
        return jnp.concatenate([fn(u, h) for u, h in chains], axis=0)

    def per_chain_last(col):
        last = col.reshape(nc, L, 1)[:, L - 1:L, :]
        return jnp.broadcast_to(last, (nc, L, 1)).reshape(nc * L, 1)

    t_idx = jnp.bitwise_and(lax.broadcasted_iota(jnp.int32, (nc * L, L), 0), L - 1)
    s_idx = lax.broadcasted_iota(jnp.int32, (nc * L, L), 1)
    causal = s_idx <= t_idx
    ig_c = stack(lambda u, h: tok[u][4][:, h:h + 1])
    lf_c = stack(lambda u, h: tok[u][4][:, n_heads + h:n_heads + h + 1])
    ig_r = stack(lambda u, h: jnp.broadcast_to(tok[u][5][h:h + 1, :], (L, L)))
    lf_r = stack(lambda u, h: jnp.broadcast_to(tok[u][5][n_heads + h:n_heads + h + 1, :], (L, L)))
    m_prev = stack(lambda u, h: jnp.broadcast_to(m_ref[u, h:h + 1, 0:1], (L, 1)))
    b_c = jnp.sum(jnp.where(causal, lf_r, 0.0), axis=1, keepdims=True)
    b_r = jnp.sum(jnp.where(causal, 0.0, lf_c).reshape(nc, L, L), axis=1, keepdims=True)
    b_r = jnp.broadcast_to(b_r, (nc, L, L)).reshape(nc * L, L) + lf_r
    b_last = per_chain_last(b_c)
    dmat = jnp.where(causal, b_c - b_r + ig_r, -jnp.inf)
    g_c = b_c + m_prev
    m_t = jnp.maximum(g_c, jnp.max(dmat, axis=1, keepdims=True))
    w_intra = jnp.exp(dmat - m_t)
    w_inter = jnp.exp(g_c - m_t)
    m_new = per_chain_last(m_t)
    wa = jnp.exp(b_last - b_c + ig_c - m_new)
    decay = jnp.exp(b_last + m_prev - m_new)

    qs = {c: tok[c[0]][0][:, c[1] * dk:(c[1] + 1) * dk] for c in chains}
    ks = {c: tok[c[0]][1][:, c[1] * dk:(c[1] + 1) * dk] for c in chains}
    vs = {c: tok[c[0]][2][:, c[1] * dv:(c[1] + 1) * dv] for c in chains}
    c_prev = {c: c_ref[c[0], c[1]] for c in chains}
    n_prev = {c: n_ref[c[0], c[1]:c[1] + 1, :] for c in chains}
    s = stack(lambda u, h: _dot_nt(qs[u, h], ks[u, h])) * w_intra
    qc = stack(lambda u, h: _dot_nt(qs[u, h], c_prev[u, h].astype(BF)))
    s_bf = s.astype(BF)
    sv = jnp.concatenate([_dot(s_bf[i * L:(i + 1) * L, :], vs[c]) for i, c in enumerate(chains)], axis=0)
    qn = jnp.sum(stack(lambda u, h: qs[u, h].astype(F32) * n_prev[u, h]), axis=1, keepdims=True)
    num = sv + w_inter * qc
    nq = jnp.sum(s, axis=1, keepdims=True) + w_inter * qn
    x = num / jnp.maximum(jnp.abs(nq), jnp.exp(-m_t))
    x = x * stack(lambda u, h: tok[u][3][:, h * dv:(h + 1) * dv])
    x = x * lax.rsqrt(jnp.mean(x * x, axis=-1, keepdims=True) + EPS)
    wk = wa * stack(lambda u, h: ks[u, h].astype(F32))
    wk_bf = wk.astype(BF)
    for i, (u, h) in enumerate(chains):
        rows = slice(i * L, (i + 1) * L)
        hm_ref[u, :, h * dv:(h + 1) * dv] = (x[rows, :] * nrm_ref[:, h * dv:(h + 1) * dv]).astype(hm_ref.dtype)
        dec = decay[i * L:i * L + 1, :]
        c_ref[u, h] = dec * c_prev[u, h] + _dot_tn(vs[u, h], wk_bf[rows, :])
        n_ref[u, h:h + 1, :] = dec * n_prev[u, h] + jnp.sum(wk[rows, :], axis=0, keepdims=True)
        m_ref[u, h:h + 1, :] = jnp.broadcast_to(m_new[i * L:i * L + 1, :], (1, m_ref.shape[2]))


def _mlstm_call(q, k, v, og, gates_c, gates_r, c0, n0, m0, norm, layer, row0, n_seq, n_chunks, n_heads):
    hdk = q.shape[1]
    hdv = v.shape[1]
    dv, dk = c0.shape[-2:]
    g2 = gates_r.shape[1]
    bt = _pick_tile(n_seq, 4)

    def chunked(a):
        return a.reshape(a.shape[0] // CHUNK, CHUNK, a.shape[1])

    arrays = [chunked(q), chunked(k), chunked(v), chunked(og), chunked(gates_c), gates_r]
    shapes = [(CHUNK, hdk), (CHUNK, hdk), (CHUNK, hdv), (CHUNK, hdv), (CHUNK, LANES), (g2, CHUNK)]

    def tok(u, shape):
        return pl.BlockSpec((None,) + shape, lambda t, c: (row0 + (t * bt + u) * n_chunks + c, 0, 0))

    def state(shape):
        return pl.BlockSpec((bt,) + shape, lambda t, c: (t,) + (0,) * len(shape))

    in_specs = [tok(u, shape) for u in range(bt) for shape in shapes]
    in_specs += [state((n_heads, dv, dk)), state((n_heads, dk)), state((n_heads, LANES)),
                 pl.BlockSpec((None, 1, hdv), lambda t, c: (layer, 0, 0))]
    kern = functools.partial(_mlstm_kernel, n_heads=n_heads, bt=bt)
    return pl.pallas_call(
        kern,
        grid=(n_seq // bt, n_chunks),
        in_specs=in_specs,
        out_specs=[
            pl.BlockSpec((None, bt, CHUNK, hdv), lambda t, c: (c, t, 0, 0)),
            state((n_heads, dv, dk)), state((n_heads, dk)), state((n_heads, LANES)),
        ],
        out_shape=[
            jax.ShapeDtypeStruct((n_chunks, n_seq, CHUNK, hdv), BF),
            jax.ShapeDtypeStruct((n_seq, n_heads, dv, dk), F32),
            jax.ShapeDtypeStruct((n_seq, n_heads, dk), F32),
            jax.ShapeDtypeStruct((n_seq, n_heads, LANES), F32),
        ],
        compiler_params=_cparams("parallel", "arbitrary"),
        name="mlstm",
    )(*(arrays * bt), c0, n0, m0, norm)


def _build_queries(q_sc, cq_ref, wuq_ref, wuk_ref, cos_ref, sin_ref, n_heads, nope, rope, kvl):
    tq = cq_ref.shape[0]
    qa = _dot(cq_ref[...], wuq_ref[...])
    cos = cos_ref[...]
    sin = sin_ref[...]
    for h in range(n_heads):
        qn = qa[:, h * nope:(h + 1) * nope].astype(BF)
        q_sc[h * tq:(h + 1) * tq, 0:kvl] = _dot(qn, wuk_ref[h]).astype(BF)
    per = LANES // rope
    for p in range(n_heads // per):
        base = n_heads * nope + p * LANES
        pe = _rope128(qa[:, base:base + LANES], cos, sin)
        for u in range(per):
            h = p * per + u
            q_sc[h * tq:(h + 1) * tq, kvl:kvl + rope] = pe[:, u * rope:(u + 1) * rope].astype(BF)


def _lane_tile(x, width):
    if width <= LANES:
        return x[:, :width]
    return jnp.concatenate([x] * (width // LANES), axis=1)


def _flash_step(q_sc, keys_next, s_next, s_cur, vals_cur, m_sc, l_sc, acc_sc, sm_scale, rs, mask_chunks=None):
    n_groups = q_sc.shape[0] // rs

    def group(r, carry):
        r0 = pl.multiple_of(r * rs, rs)
        rows = pl.ds(r0, rs)
        if s_next is not None:
            s_next[rows, :] = _dot_nt(q_sc[rows, :], keys_next)
        if s_cur is not None:
            kb, kvl = vals_cur.shape
            s = s_cur[rows, :] * sm_scale
            if mask_chunks is not None:
                q_chunks, q_chunk0, k_chunk0 = mask_chunks
                chunk_bits = CHUNK.bit_length() - 1
                row = lax.broadcasted_iota(jnp.int32, (rs, kb), 0) + r0
                q_chunk = jnp.bitwise_and(jnp.right_shift(row, chunk_bits), q_chunks - 1) + q_chunk0
                k_chunk = jnp.right_shift(lax.broadcasted_iota(jnp.int32, (rs, kb), 1), chunk_bits) + k_chunk0
                s = jnp.where(k_chunk <= q_chunk, s, -jnp.inf)
            m_prev = m_sc[rows, :]
            m_new = jnp.maximum(m_prev, jnp.max(s, axis=1, keepdims=True))
            alpha = jnp.exp(m_prev - m_new)
            p = jnp.exp(s - _lane_tile(m_new, kb))
            l_sc[rows, :] = alpha * l_sc[rows, :] + jnp.sum(p, axis=1, keepdims=True)
            acc_sc[rows, :] = _lane_tile(alpha, kvl) * acc_sc[rows, :] + _dot(p.astype(BF), vals_cur)
            m_sc[rows, :] = m_new
        return carry

    lax.fori_loop(0, n_groups, group, 0, unroll=2 if n_groups % 2 == 0 else 1)


def _flash_init(m_sc, l_sc, acc_sc):
    m_sc[...] = jnp.full(m_sc.shape, -jnp.inf, F32)
    l_sc[...] = jnp.zeros(l_sc.shape, F32)
    acc_sc[...] = jnp.zeros(acc_sc.shape, F32)


def _flash_finish(o_ref, l_sc, acc_sc, wuv_ref, n_heads, tq):
    vd = wuv_ref.shape[2]
    kvl = acc_sc.shape[1]
    for h in range(n_heads):
        rows = pl.ds(h * tq, tq)
        o_lat = (acc_sc[rows, :] / _lane_tile(l_sc[rows, :], kvl)).astype(BF)
        o_ref[:, h * vd:(h + 1) * vd] = _dot(o_lat, wuv_ref[h]).astype(o_ref.dtype)


def _attn_prompt_kernel(cq_ref, ckv_ref, kpe_ref, wuq_ref, wuk_ref, wuv_ref, cos_ref, sin_ref, o_ref,
                        kcat, q_sc, m_sc, l_sc, acc_sc, s_a, s_b, *, n_heads, nope, rope, sm_scale, rs):
    i = pl.program_id(1)
    tq = cq_ref.shape[0]
    kvl = ckv_ref.shape[1]
    kb = s_a.shape[1]

    @pl.when(i == 0)
    def _():
        kcat[:, 0:kvl] = ckv_ref[...].astype(BF)
        kcat[:, kvl:kvl + rope] = kpe_ref[...].astype(BF)

    _build_queries(q_sc, cq_ref, wuq_ref, wuk_ref, cos_ref, sin_ref, n_heads, nope, rope, kvl)
    _flash_init(m_sc, l_sc, acc_sc)

    def step(j_next, s_next, j_cur, s_cur, masked=False):
        keys = None if s_next is None else kcat[pl.ds(pl.multiple_of(j_next * kb, kb), kb), :]
        vals = None if s_cur is None else kcat[pl.ds(pl.multiple_of(j_cur * kb, kb), kb), 0:kvl]
        mask = (tq // CHUNK, i * (tq // CHUNK), j_cur * (kb // CHUNK)) if masked else None
        _flash_step(q_sc, keys, s_next, s_cur, vals, m_sc, l_sc, acc_sc, sm_scale, rs, mask)

    last = (i * tq) // kb
    step(0, s_a, None, None)

    def pair(t, carry):
        j = 2 * t
        step(j + 1, s_b, j, s_a)
        step(j + 2, s_a, j + 1, s_b)
        return carry

    lax.fori_loop(0, last // 2, pair, 0)

    @pl.when(last % 2 == 0)
    def _():
        step(None, None, last, s_a, masked=True)

    @pl.when(last % 2 == 1)
    def _():
        step(last, s_b, last - 1, s_a)
        step(None, None, last, s_b, masked=True)

    _flash_finish(o_ref, l_sc, acc_sc, wuv_ref, n_heads, tq)


def _attn_prompt_call(cq, ckv, kpe, wuq, wuk, wuv, cos, sin, layer, n_seq, seq, tq, dims):
    ah, nope, rope, kvl, vd = dims["ah"], dims["nope"], dims["rope"], dims["kvl"], dims["vd"]
    ql = cq.shape[1]
    nq = seq // tq
    kb = _pick_tile(seq, 512)
    assert tq % CHUNK == 0 and (tq // CHUNK) & (tq // CHUNK - 1) == 0 and kb % tq == 0
    rs = _pick_tile(ah * tq, 512)
    kern = functools.partial(_attn_prompt_kernel, n_heads=ah, nope=nope, rope=rope,
                             sm_scale=(nope + rope) ** -0.5, rs=rs)

    def wspec(w):
        return pl.BlockSpec((None,) + w.shape[1:], lambda b, i: (layer,) + (0,) * (w.ndim - 1))

    return pl.pallas_call(
        kern,
        grid=(n_seq, nq),
        in_specs=[
            pl.BlockSpec((tq, ql), lambda b, i: (b * nq + i, 0)),
            pl.BlockSpec((seq, kvl), lambda b, i: (b, 0)),
            pl.BlockSpec((seq, rope), lambda b, i: (b, 0)),
            wspec(wuq), wspec(wuk), wspec(wuv),
            pl.BlockSpec((tq, LANES), lambda b, i: (b * nq + i, 0)),
            pl.BlockSpec((tq, LANES), lambda b, i: (b * nq + i, 0)),
        ],
        out_specs=pl.BlockSpec((tq, ah * vd), lambda b, i: (b * nq + i, 0)),
        out_shape=jax.ShapeDtypeStruct((n_seq * seq, ah * vd), BF),
        scratch_shapes=[
            pltpu.VMEM((seq, kvl + rope), BF),
            pltpu.VMEM((ah * tq, kvl + rope), BF),
            pltpu.VMEM((ah * tq, LANES), F32),
            pltpu.VMEM((ah * tq, LANES), F32),
            pltpu.VMEM((ah * tq, kvl), F32),
            pltpu.VMEM((ah * tq, kb), F32),
            pltpu.VMEM((ah * tq, kb), F32),
        ],
        compiler_params=_cparams("parallel", "arbitrary"),
        name="mla_prompt",
    )(cq, ckv, kpe, wuq, wuk, wuv, cos, sin)


def _attn_sample_kernel(cq_ref, ckv_ref, kpe_ref, pckv_ref, pkpe_ref, wuq_ref, wuk_ref, wuv_ref, cos_ref, sin_ref,
                        o_ref, kcat, q_sc, m_sc, l_sc, acc_sc, s_a, s_b, s_n, *, n_heads, nope, rope, sm_scale):
    tq = cq_ref.shape[0]
    kvl = ckv_ref.shape[1]
    past = pckv_ref.shape[0]
    kb = s_a.shape[1]
    kcat[0:past, 0:kvl] = pckv_ref[...].astype(BF)
    kcat[0:past, kvl:kvl + rope] = pkpe_ref[...].astype(BF)
    kcat[past:past + tq, 0:kvl] = ckv_ref[...].astype(BF)
    kcat[past:past + tq, kvl:kvl + rope] = kpe_ref[...].astype(BF)
    _build_queries(q_sc, cq_ref, wuq_ref, wuk_ref, cos_ref, sin_ref, n_heads, nope, rope, kvl)
    _flash_init(m_sc, l_sc, acc_sc)
    rs = q_sc.shape[0]

    blocks = [(j * kb, kb, (s_a, s_b)[j % 2]) for j in range(past // kb)] + [(past, tq, s_n)]
    for cur, nxt in zip([None] + blocks, blocks + [None]):
        keys, s_next = (None, None) if nxt is None else (kcat[nxt[0]:nxt[0] + nxt[1], :], nxt[2])
        vals, s_cur = (None, None) if cur is None else (kcat[cur[0]:cur[0] + cur[1], 0:kvl], cur[2])
        _flash_step(q_sc, keys, s_next, s_cur, vals, m_sc, l_sc, acc_sc, sm_scale, rs)
    _flash_finish(o_ref, l_sc, acc_sc, wuv_ref, n_heads, tq)


def _attn_sample_call(cq, ckv, kpe, cache_ckv, cache_kpe, wuq, wuk, wuv, cos, sin, layer, row0, n_seq, tq, dims):
    ah, nope, rope, kvl, vd = dims["ah"], dims["nope"], dims["rope"], dims["kvl"], dims["vd"]
    ql = cq.shape[1]
    past = cache_ckv.shape[2]
    kb = math.gcd(past, 256)
    kern = functools.partial(_attn_sample_kernel, n_heads=ah, nope=nope, rope=rope,
                             sm_scale=(nope + rope) ** -0.5)

    def wspec(w):
        return pl.BlockSpec((None,) + w.shape[1:], lambda b: (layer,) + (0,) * (w.ndim - 1))

    def tok(n):
        return pl.BlockSpec((tq, n), lambda b: (row0 + b, 0))

    return pl.pallas_call(
        kern,
        grid=(n_seq,),
        in_specs=[
            tok(ql), tok(kvl), tok(rope),
            pl.BlockSpec((None, None, past, kvl), lambda b: (layer, b, 0, 0)),
            pl.BlockSpec((None, None, past, rope), lambda b: (layer, b, 0, 0)),
            wspec(wuq), wspec(wuk), wspec(wuv), tok(LANES), tok(LANES),
        ],
        out_specs=pl.BlockSpec((tq, ah * vd), lambda b: (b, 0)),
        out_shape=jax.ShapeDtypeStruct((n_seq * tq, ah * vd), BF),
        scratch_shapes=[
            pltpu.VMEM((past + tq, kvl + rope), BF),
            pltpu.VMEM((ah * tq, kvl + rope), BF),
            pltpu.VMEM((ah * tq, LANES), F32),
            pltpu.VMEM((ah * tq, LANES), F32),
            pltpu.VMEM((ah * tq, kvl), F32),
            pltpu.VMEM((ah * tq, kb), F32),
            pltpu.VMEM((ah * tq, kb), F32),
            pltpu.VMEM((ah * tq, tq), F32),
        ],
        compiler_params=_cparams("parallel"),
        name="mla_sample",
    )(cq, ckv, kpe, cache_ckv, cache_kpe, wuq, wuk, wuv, cos, sin)


def _outproj_kernel(x_ref, hmp_ref, hms_ref, oap_ref, oas_ref, gt_ref, wm_ref, wa_ref, o_ref, *, n_prompt_tiles):
    i = pl.program_id(0)
    tm = x_ref.shape[0]

    def project(hm, oa):
        y = _dot(hm, wm_ref[...]) + _dot(oa, wa_ref[...])
        for g in range(tm // CHUNK):
            rows = pl.ds(g * CHUNK, CHUNK)
            o_ref[rows, :] = x_ref[rows, :] + gt_ref[g:g + 1, :] * y[g * CHUNK:(g + 1) * CHUNK, :]

    @pl.when(i < n_prompt_tiles)
    def _():
        project(hmp_ref[...].reshape(tm, hmp_ref.shape[-1]), oap_ref[...])

    @pl.when(i >= n_prompt_tiles)
    def _():
        project(hms_ref[...], oas_ref[...])


def _outproj_call(x, hm_p, hm_s, oa_p, oa_s, mod, w_out, layer, tm, seq):
    m, d = x.shape
    wm = hm_s.shape[1]
    wa = oa_s.shape[1]
    assert wm == wa and seq % tm == 0
    ng = tm // CHUNK
    npt = oa_p.shape[0] // tm
    tps = seq // tm

    def prompt_tile(i):
        return jnp.minimum(i, npt - 1)

    def sample_tile(i):
        return jnp.maximum(i - npt, 0)

    return pl.pallas_call(
        functools.partial(_outproj_kernel, n_prompt_tiles=npt),
        grid=(m // tm,),
        in_specs=[
            pl.BlockSpec((tm, d), lambda i: (i, 0)),
            pl.BlockSpec((ng, None, CHUNK, wm), lambda i: (prompt_tile(i) % tps, prompt_tile(i) // tps, 0, 0)),
            pl.BlockSpec((tm, wm), lambda i: (sample_tile(i), 0)),
            pl.BlockSpec((tm, wa), lambda i: (prompt_tile(i), 0)),
            pl.BlockSpec((tm, wa), lambda i: (sample_tile(i), 0)),
            pl.BlockSpec((None, ng, d), lambda i: (layer, i, 5)),
            pl.BlockSpec((None, wm, d), lambda i: (layer, 0, 0)),
            pl.BlockSpec((None, wa, d), lambda i: (layer, 1, 0)),
        ],
        out_specs=pl.BlockSpec((tm, d), lambda i: (i, 0)),
        out_shape=jax.ShapeDtypeStruct((m, d), F32),
        compiler_params=_cparams("parallel"),
        name="out_proj",
    )(x, hm_p, hm_s, oa_p, oa_s, mod, w_out, w_out)


def _final_norm_kernel(x_ref, g_ref, o_ref):
    o_ref[...] = _rms(x_ref[...], g_ref[...])


def _final_norm_call(x, g, row0, rows, tm):
    d = x.shape[1]
    return pl.pallas_call(
        _final_norm_kernel,
        grid=(rows // tm,),
        in_specs=[pl.BlockSpec((tm, d), lambda i: (row0 + i, 0)), pl.BlockSpec((1, d), lambda i: (0, 0))],
        out_specs=pl.BlockSpec((tm, d), lambda i: (i, 0)),
        out_shape=jax.ShapeDtypeStruct((rows, d), F32),
        compiler_params=_cparams("parallel"),
        name="final_norm",
    )(x, g)


def _rope_tables(pos, rope):
    half = rope // 2
    freqs = ROPE_THETA ** (-jnp.arange(half, dtype=F32) / half)
    ang = pos.astype(F32)[:, None] * freqs[None, :]
    cos = jnp.cos(ang)
    sin = jnp.sin(ang)
    reps = LANES // rope
    return jnp.tile(jnp.concatenate([cos, cos], axis=1), (1, reps)), jnp.tile(jnp.concatenate([-sin, sin], axis=1), (1, reps))


def _pick_tile(m, cap):
    t = cap
    while m % t:
        t //= 2
    return t


def kernel(x_prompt, x_sample, c_prompt, c_sample, cache_ckv, cache_kpe, state_C, state_n, state_m, mod_w, mod_b, ln_ffn1, ffn1_w_in, ffn1_w_out, ln_mix, w_in, mlstm_b_i, mlstm_b_f, mlstm_norm, q_norm, w_uq, kv_norm, w_uk, w_uv, w_out, ln_ffn2, ffn2_w_in, ffn2_w_out, final_norm):
    bp, sp, d = x_prompt.shape
    bs, ss, _ = x_sample.shape
    depth = mod_w.shape[0]
    past = cache_ckv.shape[2]
    mh, dv, dk = state_C.shape[2:]
    kvl, ah, nope = w_uk.shape[1:]
    vd = w_uv.shape[3]
    rope = cache_kpe.shape[3]
    ql = q_norm.shape[1]
    hdk, hdv = mh * dk, mh * dv
    dff = ffn1_w_out.shape[1]
    dims = dict(mh=mh, dk=dk, dv=dv, hdk=hdk, hdv=hdv, ql=ql, kvl=kvl, rope=rope, ah=ah, nope=nope, vd=vd)
    assert sp % CHUNK == 0 and ss == CHUNK and LANES % rope == 0 and ah % (LANES // rope) == 0
    mp, ms = bp * sp, bs * ss
    m = mp + ms
    tm = _pick_tile(math.gcd(mp, ms), 512)
    tm_ffn = _pick_tile(m, 1024)
    tf = _pick_tile(dff, 512)
    tq = _pick_tile(sp, 256)

    x = jnp.concatenate([x_prompt.reshape(mp, d), x_sample.reshape(ms, d)], axis=0)
    cg = jnp.concatenate([jnp.repeat(c_prompt, sp // CHUNK, axis=0), jnp.repeat(c_sample, ss // CHUNK, axis=0)], axis=0)
    pos = jnp.concatenate([jnp.tile(jnp.arange(sp), bp), jnp.tile(past + jnp.arange(ss), bs)])
    cos, sin = _rope_tables(pos, rope)

    o_mq, o_mk, o_mv, o_mo = 0, hdk, 2 * hdk, 2 * hdk + hdv
    o_mi = o_mo + hdv
    o_mf, o_qa = o_mi + mh, o_mi + 2 * mh
    o_kva, o_pe = o_qa + ql, o_qa + ql + kvl
    wqk = w_in[:, :, o_mq:o_mv].astype(BF)
    wv = w_in[:, :, o_mv:o_mo].astype(BF)
    wo = w_in[:, :, o_mo:o_mi].astype(BF)
    zpad = lambda n: jnp.zeros((depth, d, n), w_in.dtype)
    wa = jnp.concatenate([w_in[:, :, o_qa:o_pe + rope], zpad(LANES - rope), w_in[:, :, o_mi:o_qa], zpad(LANES - 2 * mh)], axis=2).astype(BF)
    gate_bias = jnp.concatenate([mlstm_b_i, mlstm_b_f, jnp.zeros((depth, LANES - 2 * mh), F32)], axis=1).reshape(depth, 1, LANES)
    wuq4 = w_uq.reshape(depth, ql, ah, nope + rope)
    wuq = jnp.concatenate([wuq4[..., :nope].reshape(depth, ql, ah * nope), wuq4[..., nope:].reshape(depth, ql, ah * rope)], axis=2).astype(BF)
    wuk = jnp.transpose(w_uk, (0, 2, 3, 1)).astype(BF)
    wuv = jnp.transpose(w_uv, (0, 2, 1, 3)).astype(BF)
    w_out_b = w_out.astype(BF)
    f1_in, f1_out = ffn1_w_in.astype(BF), ffn1_w_out.astype(BF)
    f2_in, f2_out = ffn2_w_in.astype(BF), ffn2_w_out.astype(BF)
    r3 = lambda a: a.reshape(depth, 1, a.shape[1])

    mod = _mod_call(cg, mod_w, mod_b)

    zeros_c = jnp.zeros((bp, mh, dv, dk), F32)
    zeros_n = jnp.zeros((bp, mh, dk), F32)
    zeros_m = jnp.zeros((bp, mh, LANES), F32)
    m0_s = jnp.broadcast_to(state_m[..., None], state_m.shape + (LANES,))

    outs = {k: [] for k in ("p_ckv", "p_kpe", "p_C", "p_n", "p_m", "s_ckv", "s_kpe", "s_C", "s_n", "s_m")}
    for l in range(depth):
        x = _ffn_call(x, mod, r3(ln_ffn1), f1_in, f1_out, l, 0, tm_ffn, tf)
        q, k, v, og, cq, ckv, kpe, gates = _inproj_call(
            x, mod, r3(ln_mix), wqk, wv, wo, wa, r3(q_norm), r3(kv_norm), gate_bias, cos, sin, l, tm, dims)
        gates_r = jnp.transpose(gates[:, :2 * mh].reshape(m // CHUNK, CHUNK, 2 * mh), (0, 2, 1))
        nrm = r3(mlstm_norm)
        hm_p, c_p, n_p, m_p = _mlstm_call(q, k, v, og, gates, gates_r, zeros_c, zeros_n, zeros_m, nrm,
                                          l, 0, bp, sp // CHUNK, mh)
        hm_s, c_s, n_s, m_s = _mlstm_call(q, k, v, og, gates, gates_r, state_C[l], state_n[l], m0_s[l], nrm,
                                          l, mp // CHUNK, bs, ss // CHUNK, mh)
        oa_p = _attn_prompt_call(cq, ckv, kpe, wuq, wuk, wuv, cos, sin, l, bp, sp, tq, dims)
        oa_s = _attn_sample_call(cq, ckv, kpe, cache_ckv, cache_kpe, wuq, wuk, wuv, cos, sin, l, mp // ss, bs, ss, dims)
        x = _outproj_call(x, hm_p, hm_s.reshape(ms, hdv), oa_p, oa_s, mod, w_out_b, l, tm, sp)
        x = _ffn_call(x, mod, r3(ln_ffn2), f2_in, f2_out, l, 6, tm_ffn, tf)
        outs["p_ckv"].append(ckv[:mp].reshape(bp, sp, kvl))
        outs["p_kpe"].append(kpe[:mp].reshape(bp, sp, rope))
        outs["s_ckv"].append(ckv[mp:].reshape(bs, ss, kvl))
        outs["s_kpe"].append(kpe[mp:].reshape(bs, ss, rope))
        outs["p_C"].append(c_p)
        outs["p_n"].append(n_p)
        outs["p_m"].append(m_p[..., 0])
        outs["s_C"].append(c_s)
        outs["s_n"].append(n_s)
        outs["s_m"].append(m_s[..., 0])

    fn = final_norm.reshape(1, d)
    y_prompt = _final_norm_call(x, fn, 0, mp, tm).reshape(bp, sp, d)
    y_sample = _final_norm_call(x, fn, mp // tm, ms, tm).reshape(bs, ss, d)
    st = {k: jnp.stack(v) for k, v in outs.items()}
    return (y_prompt, y_sample, st["p_ckv"], st["p_kpe"], st["p_C"], st["p_n"], st["p_m"],
            st["s_ckv"], st["s_kpe"], st["s_C"], st["s_n"], st["s_m"])
```

```python
import functools
import math

import jax
import jax.numpy as jnp
from jax import lax
from jax.experimental import pallas as pl
from jax.experimental.pallas import tpu as pltpu

CHUNK = 64
EPS = 1e-6
ROPE_THETA = 10000.0
N_MOD = 9
LANES = 128
VMEM_LIMIT = 56 * 1024 * 1024

BF = jnp.bfloat16
F32 = jnp.float32


def _cparams(*sem):
    return pltpu.CompilerParams(dimension_semantics=sem, vmem_limit_bytes=VMEM_LIMIT)


def _dot(a, b):
    return jnp.dot(a, b, preferred_element_type=F32)


def _dot_nt(a, b):
    return lax.dot_general(a, b, (((1,), (1,)), ((), ())), preferred_element_type=F32)


def _dot_tn(a, b):
    return lax.dot_general(a, b, (((0,), (0,)), ((), ())), preferred_element_type=F32)


def _rms(x, g):
    ms = jnp.mean(x * x, axis=-1, keepdims=True)
    return x * lax.rsqrt(ms + EPS) * g


def _silu(x):
    return x * jax.nn.sigmoid(x)


def _rope128(pe, cos, sin):
    lane = lax.broadcasted_iota(jnp.int32, pe.shape, 1)
    first_half = jnp.bitwise_and(lane, 63) < 32
    swapped = jnp.where(first_half, pltpu.roll(pe, 96, 1), pltpu.roll(pe, 32, 1))
    return pe * cos + swapped * sin


def _mod_kernel(c_ref, w_ref, b_ref, o_ref):
    a = _silu(c_ref[...]).astype(BF)
    o_ref[...] = _dot(a, w_ref[...].astype(BF)) + b_ref[...]


def _mod_call(cg, mod_w, mod_b):
    depth, d, nd = mod_w.shape
    g = cg.shape[0]
    tn = 1024
    return pl.pallas_call(
        _mod_kernel,
        grid=(depth, nd // tn),
        in_specs=[
            pl.BlockSpec((g, d), lambda l, j: (0, 0)),
            pl.BlockSpec((None, d, tn), lambda l, j: (l, 0, j)),
            pl.BlockSpec((None, 1, tn), lambda l, j: (l, 0, j)),
        ],
        out_specs=pl.BlockSpec((None, g, tn), lambda l, j: (l, 0, j)),
        out_shape=jax.ShapeDtypeStruct((depth, g, nd), F32),
        compiler_params=_cparams("parallel", "parallel"),
        name="adaln_mod",
    )(cg, mod_w, mod_b.reshape(depth, 1, nd))


def _norm_mod_to(h_ref, x_ref, ln_ref, sh_ref, sc_ref):
    ln = ln_ref[...]
    for g in range(x_ref.shape[0] // CHUNK):
        rows = pl.ds(g * CHUNK, CHUNK)
        y = _rms(x_ref[rows, :], ln)
        h_ref[rows, :] = (y * (1.0 + sc_ref[g:g + 1, :]) + sh_ref[g:g + 1, :]).astype(h_ref.dtype)


def _ffn_kernel(x_ref, sh_ref, sc_ref, gt_ref, ln_ref, wg_ref, wu_ref, wo_ref, o_ref, h_sc):
    j = pl.program_id(1)
    nj = pl.num_programs(1)

    def up_down(h):
        a = (_silu(_dot(h, wg_ref[...])) * _dot(h, wu_ref[...])).astype(BF)
        return _dot(a, wo_ref[...])

    def residual(acc):
        for g in range(x_ref.shape[0] // CHUNK):
            rows = pl.ds(g * CHUNK, CHUNK)
            o_ref[rows, :] = x_ref[rows, :] + (0.5 * gt_ref[g:g + 1, :]) * acc[g * CHUNK:(g + 1) * CHUNK, :]

    @pl.when(j == 0)
    def _():
        _norm_mod_to(h_sc, x_ref, ln_ref, sh_ref, sc_ref)
        o_ref[...] = up_down(h_sc[...])

    @pl.when(jnp.logical_and(j > 0, j < nj - 1))
    def _():
        o_ref[...] += up_down(h_sc[...])

    @pl.when(jnp.logical_and(j > 0, j == nj - 1))
    def _():
        residual(o_ref[...] + up_down(h_sc[...]))

    @pl.when(nj == 1)
    def _():
        residual(o_ref[...])


def _ffn_call(x, mod, ln, w_in, w_out, layer, k0, tm, tf):
    m, d = x.shape
    f = w_out.shape[0]
    ng = tm // CHUNK
    nf = f // tf

    def mod_spec(k):
        return pl.BlockSpec((None, ng, d), lambda i, j: (layer, i, k))

    return pl.pallas_call(
        _ffn_kernel,
        grid=(m // tm, nf),
        in_specs=[
            pl.BlockSpec((tm, d), lambda i, j: (i, 0)),
            mod_spec(k0), mod_spec(k0 + 1), mod_spec(k0 + 2),
            pl.BlockSpec((None, 1, d), lambda i, j: (layer, 0, 0)),
            pl.BlockSpec((d, tf), lambda i, j: (0, j)),
            pl.BlockSpec((d, tf), lambda i, j: (0, j + nf)),
            pl.BlockSpec((tf, d), lambda i, j: (j, 0)),
        ],
        out_specs=pl.BlockSpec((tm, d), lambda i, j: (i, 0)),
        out_shape=jax.ShapeDtypeStruct((m, d), F32),
        scratch_shapes=[pltpu.VMEM((tm, d), BF)],
        compiler_params=_cparams("parallel", "arbitrary"),
        name="ffn",
    )(x, mod, mod, mod, ln, w_in, w_in, w_out)


def _inproj_kernel(x_ref, sh_ref, sc_ref, ln_ref, wqk_ref, wv_ref, wo_ref, wa_ref, qn_ref, kvn_ref, gb_ref,
                   cos_ref, sin_ref, q_ref, k_ref, v_ref, og_ref, cq_ref, ckv_ref, kpe_ref, gt_ref, h_sc,
                   *, n_heads, k_scale):
    _norm_mod_to(h_sc, x_ref, ln_ref, sh_ref, sc_ref)
    h = h_sc[...]
    hdk = q_ref.shape[1]
    zqk = _dot(h, wqk_ref[...])
    q_ref[...] = zqk[:, :hdk].astype(BF)
    k_ref[...] = (zqk[:, hdk:] * k_scale).astype(BF)
    v_ref[...] = _dot(h, wv_ref[...]).astype(BF)
    og_ref[...] = jax.nn.sigmoid(_dot(h, wo_ref[...]))
    za = _dot(h, wa_ref[...])
    ql = cq_ref.shape[1]
    kvl = ckv_ref.shape[1]
    cq_ref[...] = _rms(za[:, :ql], qn_ref[...]).astype(BF)
    ckv_ref[...] = _rms(za[:, ql:ql + kvl], kvn_ref[...])
    pe = _rope128(za[:, ql + kvl:ql + kvl + LANES], cos_ref[...], sin_ref[...])
    kpe_ref[...] = pe[:, :kpe_ref.shape[1]]
    zg = za[:, ql + kvl + LANES:] + gb_ref[...]
    lane = lax.broadcasted_iota(jnp.int32, zg.shape, 1)
    log_sig = jnp.minimum(zg, 0.0) - jnp.log1p(jnp.exp(-jnp.abs(zg)))
    gt_ref[...] = jnp.where(lane < n_heads, zg, log_sig)


def _inproj_call(x, mod, ln, wqk, wv, wo, wa, q_norm, kv_norm, gate_bias, cos, sin, layer, tm, dims):
    m, d = x.shape
    ng = tm // CHUNK
    hdk, hdv, ql, kvl, rope = dims["hdk"], dims["hdv"], dims["ql"], dims["kvl"], dims["rope"]

    def mod_spec(k):
        return pl.BlockSpec((None, ng, d), lambda i: (layer, i, k))

    def wspec(w):
        return pl.BlockSpec((None,) + w.shape[1:], lambda i: (layer, 0, 0))

    def row(n):
        return pl.BlockSpec((tm, n), lambda i: (i, 0))

    kern = functools.partial(_inproj_kernel, n_heads=dims["mh"], k_scale=dims["dk"] ** -0.5)
    return pl.pallas_call(
        kern,
        grid=(m // tm,),
        in_specs=[
            row(d), mod_spec(3), mod_spec(4), wspec(ln), wspec(wqk), wspec(wv), wspec(wo), wspec(wa),
            wspec(q_norm), wspec(kv_norm), wspec(gate_bias), row(LANES), row(LANES),
        ],
        out_specs=[row(hdk), row(hdk), row(hdv), row(hdv), row(ql), row(kvl), row(rope), row(LANES)],
        out_shape=[
            jax.ShapeDtypeStruct((m, hdk), BF), jax.ShapeDtypeStruct((m, hdk), BF),
            jax.ShapeDtypeStruct((m, hdv), BF), jax.ShapeDtypeStruct((m, hdv), F32),
            jax.ShapeDtypeStruct((m, ql), BF), jax.ShapeDtypeStruct((m, kvl), F32),
            jax.ShapeDtypeStruct((m, rope), F32), jax.ShapeDtypeStruct((m, LANES), F32),
        ],
        scratch_shapes=[pltpu.VMEM((tm, d), BF)],
        compiler_params=_cparams("parallel"),
        name="in_proj",
    )(x, mod, mod, ln, wqk, wv, wo, wa, q_norm, kv_norm, gate_bias, cos, sin)


def _mlstm_kernel(*refs, n_heads, bt):
    tok = [refs[6 * u:6 * u + 6] for u in range(bt)]
    c0_ref, n0_ref, m0_ref, nrm_ref, hm_ref, c_ref, n_ref, m_ref = refs[6 * bt:]
    c_idx = pl.program_id(1)

    @pl.when(c_idx == 0)
    def _():
        c_ref[...] = c0_ref[...]
        n_ref[...] = n0_ref[...]
        m_ref[...] = m0_ref[...]

    L = tok[0][0].shape[0]
    dk = tok[0][0].shape[1] // n_heads
    dv = tok[0][2].shape[1] // n_heads
    chains = [(u, h) for u in range(bt) for h in range(n_heads)]
    nc = len(chains)

    def stack(fn):
        return jnp.concatenate([fn(u, h) for u, h in chains], axis=0)

    def per_chain_last(col):
        last = col.reshape(nc, L, 1)[:, L - 1:L, :]
        return jnp.broadcast_to(last, (nc, L, 1)).reshape(nc * L, 1)

    t_idx = jnp.bitwise_and(lax.broadcasted_iota(jnp.int32, (nc * L, L), 0), L - 1)
    s_idx = lax.broadcasted_iota(jnp.int32, (nc * L, L), 1)
    causal = s_idx <= t_idx
    ig_c = stack(lambda u, h: tok[u][4][:, h:h + 1])
    lf_c = stack(lambda u, h: tok[u][4][:, n_heads + h:n_heads + h + 1])
    ig_r = stack(lambda u, h: jnp.broadcast_to(tok[u][5][h:h + 1, :], (L, L)))
    lf_r = stack(lambda u, h: jnp.broadcast_to(tok[u][5][n_heads + h:n_heads + h + 1, :], (L, L)))
    m_prev = stack(lambda u, h: jnp.broadcast_to(m_ref[u, h:h + 1, 0:1], (L, 1)))
    b_c = jnp.sum(jnp.where(causal, lf_r, 0.0), axis=1, keepdims=True)
    b_r = jnp.sum(jnp.where(causal, 0.0, lf_c).reshape(nc, L, L), axis=1, keepdims=True)
    b_r = jnp.broadcast_to(b_r, (nc, L, L)).reshape(nc * L, L) + lf_r
    b_last = per_chain_last(b_c)
    dmat = jnp.where(causal, b_c - b_r + ig_r, -jnp.inf)
    g_c = b_c + m_prev
    m_t = jnp.maximum(g_c, jnp.max(dmat, axis=1, keepdims=True))
    w_intra = jnp.exp(dmat - m_t)
    w_inter = jnp.exp(g_c - m_t)
    m_new = per_chain_last(m_t)
    wa = jnp.exp(b_last - b_c + ig_c - m_new)
    decay = jnp.exp(b_last + m_prev - m_new)

    qs = {c: tok[c[0]][0][:, c[1] * dk:(c[1] + 1) * dk] for c in chains}
    ks = {c: tok[c[0]][1][:, c[1] * dk:(c[1] + 1) * dk] for c in chains}
    vs = {c: tok[c[0]][2][:, c[1] * dv:(c[1] + 1) * dv] for c in chains}
    c_prev = {c: c_ref[c[0], c[1]] for c in chains}
    n_prev = {c: n_ref[c[0], c[1]:c[1] + 1, :] for c in chains}
    s = stack(lambda u, h: _dot_nt(qs[u, h], ks[u, h])) * w_intra
    qc = stack(lambda u, h: _dot_nt(qs[u, h], c_prev[u, h].astype(BF)))
    s_bf = s.astype(BF)
    sv = jnp.concatenate([_dot(s_bf[i * L:(i + 1) * L, :], vs[c]) for i, c in enumerate(chains)], axis=0)
    qn = jnp.sum(stack(lambda u, h: qs[u, h].astype(F32) * n_prev[u, h]), axis=1, keepdims=True)
    num = sv + w_inter * qc
    nq = jnp.sum(s, axis=1, keepdims=True) + w_inter * qn
    x = num / jnp.maximum(jnp.abs(nq), jnp.exp(-m_t))
    x = x * stack(lambda u, h: tok[u][3][:, h * dv:(h + 1) * dv])
    x = x * lax.rsqrt(jnp.mean(x * x, axis=-1, keepdims=True) + EPS)
    wk = wa * stack(lambda u, h: ks[u, h].astype(F32))
    wk_bf = wk.astype(BF)
    for i, (u, h) in enumerate(chains):
        rows = slice(i * L, (i + 1) * L)
        hm_ref[u, :, h * dv:(h + 1) * dv] = (x[rows, :] * nrm_ref[:, h * dv:(h + 1) * dv]).astype(hm_ref.dtype)
        dec = decay[i * L:i * L + 1, :]
        c_ref[u, h] = dec * c_prev[u, h] + _dot_tn(vs[u, h], wk_bf[rows, :])
        n_ref[u, h:h + 1, :] = dec * n_prev[u, h] + jnp.sum(wk[rows, :], axis=0, keepdims=True)
        m_ref[u, h:h + 1, :] = jnp.broadcast_to(m_new[i * L:i * L + 1, :], (1, m_ref.shape[2]))


def _mlstm_call(q, k, v, og, gates_c, gates_r, c0, n0, m0, norm, layer, row0, n_seq, n_chunks, n_heads):
    hdk = q.shape[1]
    hdv = v.shape[1]
    dv, dk = c0.shape[-2:]
    g2 = gates_r.shape[1]
    bt = _pick_tile(n_seq, 4)

    def chunked(a):
        return a.reshape(a.shape[0] // CHUNK, CHUNK, a.shape[1])

    arrays = [chunked(q), chunked(k), chunked(v), chunked(og), chunked(gates_c), gates_r]
    shapes = [(CHUNK, hdk), (CHUNK, hdk), (CHUNK, hdv), (CHUNK, hdv), (CHUNK, LANES), (g2, CHUNK)]

    def tok(u, shape):
        return pl.BlockSpec((None,) + shape, lambda t, c: (row0 + (t * bt + u) * n_chunks + c, 0, 0))

    def state(shape):
        return pl.BlockSpec((bt,) + shape, lambda t, c: (t,) + (0,) * len(shape))

    in_specs = [tok(u, shape) for u in range(bt) for shape in shapes]
    in_specs += [state((n_heads, dv, dk)), state((n_heads, dk)), state((n_heads, LANES)),
                 pl.BlockSpec((None, 1, hdv), lambda t, c: (layer, 0, 0))]
    kern = functools.partial(_mlstm_kernel, n_heads=n_heads, bt=bt)
    return pl.pallas_call(
        kern,
        grid=(n_seq // bt, n_chunks),
        in_specs=in_specs,
        out_specs=[
            pl.BlockSpec((None, bt, CHUNK, hdv), lambda t, c: (c, t, 0, 0)),
            state((n_heads, dv, dk)), state((n_heads, dk)), state((n_heads, LANES)),
        ],
        out_shape=[
            jax.ShapeDtypeStruct((n_chunks, n_seq, CHUNK, hdv), BF),
            jax.ShapeDtypeStruct((n_seq, n_heads, dv, dk), F32),
            jax.ShapeDtypeStruct((n_seq, n_heads, dk), F32),
            jax.ShapeDtypeStruct((n_seq, n_heads, LANES), F32),
        ],
        compiler_params=_cparams("parallel", "arbitrary"),
        name="mlstm",
    )(*(arrays * bt), c0, n0, m0, norm)


def _build_queries(q_sc, cq_ref, wuq_ref, wuk_ref, cos_ref, sin_ref, n_heads, nope, rope, kvl):
    tq = cq_ref.shape[0]
    qa = _dot(cq_ref[...], wuq_ref[...])
    cos = cos_ref[...]
    sin = sin_ref[...]
    for h in range(n_heads):
        qn = qa[:, h * nope:(h + 1) * nope].astype(BF)
        q_sc[h * tq:(h + 1) * tq, 0:kvl] = _dot(qn, wuk_ref[h]).astype(BF)
    per = LANES // rope
    for p in range(n_heads // per):
        base = n_heads * nope + p * LANES
        pe = _rope128(qa[:, base:base + LANES], cos, sin)
        for u in range(per):
            h = p * per + u
            q_sc[h * tq:(h + 1) * tq, kvl:kvl + rope] = pe[:, u * rope:(u + 1) * rope].astype(BF)


def _lane_tile(x, width):
    if width <= LANES:
        return x[:, :width]
    return jnp.concatenate([x] * (width // LANES), axis=1)


def _flash_step(q_sc, keys_next, s_next, s_cur, vals_cur, m_sc, l_sc, acc_sc, sm_scale, rs, mask_chunks=None):
    n_groups = q_sc.shape[0] // rs

    def group(r, carry):
        r0 = pl.multiple_of(r * rs, rs)
        rows = pl.ds(r0, rs)
        if s_next is not None:
            s_next[rows, :] = _dot_nt(q_sc[rows, :], keys_next)
        if s_cur is not None:
            kb, kvl = vals_cur.shape
            s = s_cur[rows, :] * sm_scale
            if mask_chunks is not None:
                q_chunks, q_chunk0, k_chunk0 = mask_chunks
                chunk_bits = CHUNK.bit_length() - 1
                row = lax.broadcasted_iota(jnp.int32, (rs, kb), 0) + r0
                q_chunk = jnp.bitwise_and(jnp.right_shift(row, chunk_bits), q_chunks - 1) + q_chunk0
                k_chunk = jnp.right_shift(lax.broadcasted_iota(jnp.int32, (rs, kb), 1), chunk_bits) + k_chunk0
                s = jnp.where(k_chunk <= q_chunk, s, -jnp.inf)
            m_prev = m_sc[rows, :]
            m_new = jnp.maximum(m_prev, jnp.max(s, axis=1, keepdims=True))
            alpha = jnp.exp(m_prev - m_new)
            p = jnp.exp(s - _lane_tile(m_new, kb))
            l_sc[rows, :] = alpha * l_sc[rows, :] + jnp.sum(p, axis=1, keepdims=True)
            acc_sc[rows, :] = _lane_tile(alpha, kvl) * acc_sc[rows, :] + _dot(p.astype(BF), vals_cur)
            m_sc[rows, :] = m_new
        return carry

    lax.fori_loop(0, n_groups, group, 0, unroll=2 if n_groups % 2 == 0 else 1)


def _flash_init(m_sc, l_sc, acc_sc):
    m_sc[...] = jnp.full(m_sc.shape, -jnp.inf, F32)
    l_sc[...] = jnp.zeros(l_sc.shape, F32)
    acc_sc[...] = jnp.zeros(acc_sc.shape, F32)


def _flash_finish(o_ref, l_sc, acc_sc, wuv_ref, n_heads, tq):
    vd = wuv_ref.shape[2]
    kvl = acc_sc.shape[1]
    for h in range(n_heads):
        rows = pl.ds(h * tq, tq)
        o_lat = (acc_sc[rows, :] / _lane_tile(l_sc[rows, :], kvl)).astype(BF)
        o_ref[:, h * vd:(h + 1) * vd] = _dot(o_lat, wuv_ref[h]).astype(o_ref.dtype)


def _attn_prompt_kernel(*refs, n_heads, nope, rope, sm_scale, rs, n_riders):
    cq_ref, ckv_ref, kpe_ref, wuq_ref, wuk_ref, wuv_ref, cos_ref, sin_ref = refs[:8]
    rider_in = refs[8:8 + n_riders]
    o_ref = refs[8 + n_riders]
    rider_out = refs[9 + n_riders:9 + 2 * n_riders]
    kcat, q_sc, m_sc, l_sc, acc_sc, s_a, s_b = refs[9 + 2 * n_riders:]
    i = pl.program_id(1)
    tq = cq_ref.shape[0]
    kvl = ckv_ref.shape[1]
    kb = s_a.shape[1]

    for src, dst in zip(rider_in, rider_out):
        dst[...] = src[...].astype(dst.dtype)

    @pl.when(i == 0)
    def _():
        kcat[:, 0:kvl] = ckv_ref[...].astype(BF)
        kcat[:, kvl:kvl + rope] = kpe_ref[...].astype(BF)

    _build_queries(q_sc, cq_ref, wuq_ref, wuk_ref, cos_ref, sin_ref, n_heads, nope, rope, kvl)
    _flash_init(m_sc, l_sc, acc_sc)

    def step(j_next, s_next, j_cur, s_cur, masked=False):
        keys = None if s_next is None else kcat[pl.ds(pl.multiple_of(j_next * kb, kb), kb), :]
        vals = None if s_cur is None else kcat[pl.ds(pl.multiple_of(j_cur * kb, kb), kb), 0:kvl]
        mask = (tq // CHUNK, i * (tq // CHUNK), j_cur * (kb // CHUNK)) if masked else None
        _flash_step(q_sc, keys, s_next, s_cur, vals, m_sc, l_sc, acc_sc, sm_scale, rs, mask)

    last = (i * tq) // kb
    step(0, s_a, None, None)

    def pair(t, carry):
        j = 2 * t
        step(j + 1, s_b, j, s_a)
        step(j + 2, s_a, j + 1, s_b)
        return carry

    lax.fori_loop(0, last // 2, pair, 0)

    @pl.when(last % 2 == 0)
    def _():
        step(None, None, last, s_a, masked=True)

    @pl.when(last % 2 == 1)
    def _():
        step(last, s_b, last - 1, s_a)
        step(None, None, last, s_b, masked=True)

    _flash_finish(o_ref, l_sc, acc_sc, wuv_ref, n_heads, tq)


def _rider_block(rows, cols, n_steps):
    for n_row_blocks in range(n_steps, 0, -1):
        if n_steps % n_row_blocks or rows % n_row_blocks or cols % (n_steps // n_row_blocks):
            continue
        br, bc = rows // n_row_blocks, cols // (n_steps // n_row_blocks)
        if br % 16 == 0 and bc % LANES == 0:
            return br, bc
    raise ValueError(f"no aligned {n_steps}-block tiling of ({rows}, {cols})")


def _attn_prompt_call(cq, ckv, kpe, wuq, wuk, wuv, cos, sin, layer, n_seq, seq, tq, dims, riders=()):
    ah, nope, rope, kvl, vd = dims["ah"], dims["nope"], dims["rope"], dims["kvl"], dims["vd"]
    ql = cq.shape[1]
    nq = seq // tq
    kb = _pick_tile(seq, 512)
    assert tq % CHUNK == 0 and (tq // CHUNK) & (tq // CHUNK - 1) == 0 and kb % tq == 0
    rs = _pick_tile(ah * tq, 512)
    kern = functools.partial(_attn_prompt_kernel, n_heads=ah, nope=nope, rope=rope,
                             sm_scale=(nope + rope) ** -0.5, rs=rs, n_riders=len(riders))

    def wspec(w):
        return pl.BlockSpec((None,) + w.shape[1:], lambda b, i: (layer,) + (0,) * (w.ndim - 1))

    rider_in, rider_out, rider_shapes = [], [], []
    for arr, arr_layer in riders:
        _, rows, cols = arr.shape
        br, bc = _rider_block(rows, cols, n_seq * nq)
        ncb = cols // bc
        rider_in.append(pl.BlockSpec((None, br, bc), functools.partial(
            lambda b, i, ncb, arr_layer: (arr_layer, (b * nq + i) // ncb, (b * nq + i) % ncb), ncb=ncb, arr_layer=arr_layer)))
        rider_out.append(pl.BlockSpec((br, bc), functools.partial(
            lambda b, i, ncb: ((b * nq + i) // ncb, (b * nq + i) % ncb), ncb=ncb)))
        rider_shapes.append(jax.ShapeDtypeStruct((rows, cols), BF))

    return pl.pallas_call(
        kern,
        grid=(n_seq, nq),
        in_specs=[
            pl.BlockSpec((tq, ql), lambda b, i: (b * nq + i, 0)),
            pl.BlockSpec((seq, kvl), lambda b, i: (b, 0)),
            pl.BlockSpec((seq, rope), lambda b, i: (b, 0)),
            wspec(wuq), wspec(wuk), wspec(wuv),
            pl.BlockSpec((tq, LANES), lambda b, i: (b * nq + i, 0)),
            pl.BlockSpec((tq, LANES), lambda b, i: (b * nq + i, 0)),
        ] + rider_in,
        out_specs=[pl.BlockSpec((tq, ah * vd), lambda b, i: (b * nq + i, 0))] + rider_out,
        out_shape=[jax.ShapeDtypeStruct((n_seq * seq, ah * vd), BF)] + rider_shapes,
        scratch_shapes=[
            pltpu.VMEM((seq, kvl + rope), BF),
            pltpu.VMEM((ah * tq, kvl + rope), BF),
            pltpu.VMEM((ah * tq, LANES), F32),
            pltpu.VMEM((ah * tq, LANES), F32),
            pltpu.VMEM((ah * tq, kvl), F32),
            pltpu.VMEM((ah * tq, kb), F32),
            pltpu.VMEM((ah * tq, kb), F32),
        ],
        compiler_params=_cparams("parallel", "arbitrary"),
        name="mla_prompt",
    )(cq, ckv, kpe, wuq, wuk, wuv, cos, sin, *[arr for arr, _ in riders])


def _attn_sample_kernel(cq_ref, ckv_ref, kpe_ref, pckv_ref, pkpe_ref, wuq_ref, wuk_ref, wuv_ref, cos_ref, sin_ref,
                        o_ref, kcat, q_sc, m_sc, l_sc, acc_sc, s_a, s_b, s_n, *, n_heads, nope, rope, sm_scale):
    tq = cq_ref.shape[0]
    kvl = ckv_ref.shape[1]
    past = pckv_ref.shape[0]
    kb = s_a.shape[1]
    kcat[0:past, 0:kvl] = pckv_ref[...].astype(BF)
    kcat[0:past, kvl:kvl + rope] = pkpe_ref[...].astype(BF)
    kcat[past:past + tq, 0:kvl] = ckv_ref[...].astype(BF)
    kcat[past:past + tq, kvl:kvl + rope] = kpe_ref[...].astype(BF)
    _build_queries(q_sc, cq_ref, wuq_ref, wuk_ref, cos_ref, sin_ref, n_heads, nope, rope, kvl)
    _flash_init(m_sc, l_sc, acc_sc)
    rs = q_sc.shape[0]

    blocks = [(j * kb, kb, (s_a, s_b)[j % 2]) for j in range(past // kb)] + [(past, tq, s_n)]
    for cur, nxt in zip([None] + blocks, blocks + [None]):
        keys, s_next = (None, None) if nxt is None else (kcat[nxt[0]:nxt[0] + nxt[1], :], nxt[2])
        vals, s_cur = (None, None) if cur is None else (kcat[cur[0]:cur[0] + cur[1], 0:kvl], cur[2])
        _flash_step(q_sc, keys, s_next, s_cur, vals, m_sc, l_sc, acc_sc, sm_scale, rs)
    _flash_finish(o_ref, l_sc, acc_sc, wuv_ref, n_heads, tq)


def _attn_sample_call(cq, ckv, kpe, cache_ckv, cache_kpe, wuq, wuk, wuv, cos, sin, layer, row0, n_seq, tq, dims):
    ah, nope, rope, kvl, vd = dims["ah"], dims["nope"], dims["rope"], dims["kvl"], dims["vd"]
    ql = cq.shape[1]
    past = cache_ckv.shape[2]
    kb = math.gcd(past, 256)
    kern = functools.partial(_attn_sample_kernel, n_heads=ah, nope=nope, rope=rope,
                             sm_scale=(nope + rope) ** -0.5)

    def wspec(w):
        return pl.BlockSpec((None,) + w.shape[1:], lambda b: (layer,) + (0,) * (w.ndim - 1))

    def tok(n):
        return pl.BlockSpec((tq, n), lambda b: (row0 + b, 0))

    return pl.pallas_call(
        kern,
        grid=(n_seq,),
        in_specs=[
            tok(ql), tok(kvl), tok(rope),
            pl.BlockSpec((None, None, past, kvl), lambda b: (layer, b, 0, 0)),
            pl.BlockSpec((None, None, past, rope), lambda b: (layer, b, 0, 0)),
            wspec(wuq), wspec(wuk), wspec(wuv), tok(LANES), tok(LANES),
        ],
        out_specs=pl.BlockSpec((tq, ah * vd), lambda b: (b, 0)),
        out_shape=jax.ShapeDtypeStruct((n_seq * tq, ah * vd), BF),
        scratch_shapes=[
            pltpu.VMEM((past + tq, kvl + rope), BF),
            pltpu.VMEM((ah * tq, kvl + rope), BF),
            pltpu.VMEM((ah * tq, LANES), F32),
            pltpu.VMEM((ah * tq, LANES), F32),
            pltpu.VMEM((ah * tq, kvl), F32),
            pltpu.VMEM((ah * tq, kb), F32),
            pltpu.VMEM((ah * tq, kb), F32),
            pltpu.VMEM((ah * tq, tq), F32),
        ],
        compiler_params=_cparams("parallel"),
        name="mla_sample",
    )(cq, ckv, kpe, cache_ckv, cache_kpe, wuq, wuk, wuv, cos, sin)


def _outproj_kernel(x_ref, hmp_ref, hms_ref, oap_ref, oas_ref, gt_ref, wm_ref, wa_ref, o_ref, *, n_prompt_tiles):
    i = pl.program_id(0)
    tm = x_ref.shape[0]

    def project(hm, oa):
        y = _dot(hm, wm_ref[...]) + _dot(oa, wa_ref[...])
        for g in range(tm // CHUNK):
            rows = pl.ds(g * CHUNK, CHUNK)
            o_ref[rows, :] = x_ref[rows, :] + gt_ref[g:g + 1, :] * y[g * CHUNK:(g + 1) * CHUNK, :]

    @pl.when(i < n_prompt_tiles)
    def _():
        project(hmp_ref[...].reshape(tm, hmp_ref.shape[-1]), oap_ref[...])

    @pl.when(i >= n_prompt_tiles)
    def _():
        project(hms_ref[...], oas_ref[...])


def _outproj_call(x, hm_p, hm_s, oa_p, oa_s, mod, w_out, layer, tm, seq):
    m, d = x.shape
    wm = hm_s.shape[1]
    wa = oa_s.shape[1]
    assert wm == wa and seq % tm == 0
    ng = tm // CHUNK
    npt = oa_p.shape[0] // tm
    tps = seq // tm

    def prompt_tile(i):
        return jnp.minimum(i, npt - 1)

    def sample_tile(i):
        return jnp.maximum(i - npt, 0)

    return pl.pallas_call(
        functools.partial(_outproj_kernel, n_prompt_tiles=npt),
        grid=(m // tm,),
        in_specs=[
            pl.BlockSpec((tm, d), lambda i: (i, 0)),
            pl.BlockSpec((ng, None, CHUNK, wm), lambda i: (prompt_tile(i) % tps, prompt_tile(i) // tps, 0, 0)),
            pl.BlockSpec((tm, wm), lambda i: (sample_tile(i), 0)),
            pl.BlockSpec((tm, wa), lambda i: (prompt_tile(i), 0)),
            pl.BlockSpec((tm, wa), lambda i: (sample_tile(i), 0)),
            pl.BlockSpec((None, ng, d), lambda i: (layer, i, 5)),
            pl.BlockSpec((None, wm, d), lambda i: (layer, 0, 0)),
            pl.BlockSpec((None, wa, d), lambda i: (layer, 1, 0)),
        ],
        out_specs=pl.BlockSpec((tm, d), lambda i: (i, 0)),
        out_shape=jax.ShapeDtypeStruct((m, d), F32),
        compiler_params=_cparams("parallel"),
        name="out_proj",
    )(x, hm_p, hm_s, oa_p, oa_s, mod, w_out, w_out)


def _final_norm_kernel(x_ref, g_ref, o_ref):
    o_ref[...] = _rms(x_ref[...], g_ref[...])


def _final_norm_call(x, g, row0, rows, tm):
    d = x.shape[1]
    return pl.pallas_call(
        _final_norm_kernel,
        grid=(rows // tm,),
        in_specs=[pl.BlockSpec((tm, d), lambda i: (row0 + i, 0)), pl.BlockSpec((1, d), lambda i: (0, 0))],
        out_specs=pl.BlockSpec((tm, d), lambda i: (i, 0)),
        out_shape=jax.ShapeDtypeStruct((rows, d), F32),
        compiler_params=_cparams("parallel"),
        name="final_norm",
    )(x, g)


def _rope_tables(pos, rope):
    half = rope // 2
    freqs = ROPE_THETA ** (-jnp.arange(half, dtype=F32) / half)
    ang = pos.astype(F32)[:, None] * freqs[None, :]
    cos = jnp.cos(ang)
    sin = jnp.sin(ang)
    reps = LANES // rope
    return jnp.tile(jnp.concatenate([cos, cos], axis=1), (1, reps)), jnp.tile(jnp.concatenate([-sin, sin], axis=1), (1, reps))


def _pick_tile(m, cap):
    t = cap
    while m % t:
        t //= 2
    return t


def kernel(x_prompt, x_sample, c_prompt, c_sample, cache_ckv, cache_kpe, state_C, state_n, state_m, mod_w, mod_b, ln_ffn1, ffn1_w_in, ffn1_w_out, ln_mix, w_in, mlstm_b_i, mlstm_b_f, mlstm_norm, q_norm, w_uq, kv_norm, w_uk, w_uv, w_out, ln_ffn2, ffn2_w_in, ffn2_w_out, final_norm):
    bp, sp, d = x_prompt.shape
    bs, ss, _ = x_sample.shape
    depth = mod_w.shape[0]
    past = cache_ckv.shape[2]
    mh, dv, dk = state_C.shape[2:]
    kvl, ah, nope = w_uk.shape[1:]
    vd = w_uv.shape[3]
    rope = cache_kpe.shape[3]
    ql = q_norm.shape[1]
    hdk, hdv = mh * dk, mh * dv
    dff = ffn1_w_out.shape[1]
    dims = dict(mh=mh, dk=dk, dv=dv, hdk=hdk, hdv=hdv, ql=ql, kvl=kvl, rope=rope, ah=ah, nope=nope, vd=vd)
    assert sp % CHUNK == 0 and ss == CHUNK and LANES % rope == 0 and ah % (LANES // rope) == 0
    mp, ms = bp * sp, bs * ss
    m = mp + ms
    tm = _pick_tile(math.gcd(mp, ms), 512)
    tm_ffn = _pick_tile(m, 1024)
    tf = _pick_tile(dff, 512)
    tq = _pick_tile(sp, 256)

    x = jnp.concatenate([x_prompt.reshape(mp, d), x_sample.reshape(ms, d)], axis=0)
    cg = jnp.concatenate([jnp.repeat(c_prompt, sp // CHUNK, axis=0), jnp.repeat(c_sample, ss // CHUNK, axis=0)], axis=0)
    pos = jnp.concatenate([jnp.tile(jnp.arange(sp), bp), jnp.tile(past + jnp.arange(ss), bs)])
    cos, sin = _rope_tables(pos, rope)

    o_mq, o_mk, o_mv, o_mo = 0, hdk, 2 * hdk, 2 * hdk + hdv
    o_mi = o_mo + hdv
    o_mf, o_qa = o_mi + mh, o_mi + 2 * mh
    o_kva, o_pe = o_qa + ql, o_qa + ql + kvl
    wqk = w_in[:, :, o_mq:o_mv].astype(BF)
    wv = w_in[:, :, o_mv:o_mo].astype(BF)
    wo = w_in[:, :, o_mo:o_mi].astype(BF)
    zpad = lambda n: jnp.zeros((depth, d, n), w_in.dtype)
    wa = jnp.concatenate([w_in[:, :, o_qa:o_pe + rope], zpad(LANES - rope), w_in[:, :, o_mi:o_qa], zpad(LANES - 2 * mh)], axis=2).astype(BF)
    gate_bias = jnp.concatenate([mlstm_b_i, mlstm_b_f, jnp.zeros((depth, LANES - 2 * mh), F32)], axis=1).reshape(depth, 1, LANES)
    wuq4 = w_uq.reshape(depth, ql, ah, nope + rope)
    wuq = jnp.concatenate([wuq4[..., :nope].reshape(depth, ql, ah * nope), wuq4[..., nope:].reshape(depth, ql, ah * rope)], axis=2).astype(BF)
    wuk = jnp.transpose(w_uk, (0, 2, 3, 1)).astype(BF)
    wuv = jnp.transpose(w_uv, (0, 2, 1, 3)).astype(BF)
    w_out_b = w_out.astype(BF)
    f1_in, f1_out = ffn1_w_in[0].astype(BF), ffn1_w_out[0].astype(BF)
    r3 = lambda a: a.reshape(depth, 1, a.shape[1])

    mod = _mod_call(cg, mod_w, mod_b)

    zeros_c = jnp.zeros((bp, mh, dv, dk), F32)
    zeros_n = jnp.zeros((bp, mh, dk), F32)
    zeros_m = jnp.zeros((bp, mh, LANES), F32)
    m0_s = jnp.broadcast_to(state_m[..., None], state_m.shape + (LANES,))

    outs = {k: [] for k in ("p_ckv", "p_kpe", "p_C", "p_n", "p_m", "s_ckv", "s_kpe", "s_C", "s_n", "s_m")}
    for l in range(depth):
        x = _ffn_call(x, mod, r3(ln_ffn1), f1_in, f1_out, l, 0, tm_ffn, tf)
        q, k, v, og, cq, ckv, kpe, gates = _inproj_call(
            x, mod, r3(ln_mix), wqk, wv, wo, wa, r3(q_norm), r3(kv_norm), gate_bias, cos, sin, l, tm, dims)
        gates_r = jnp.transpose(gates[:, :2 * mh].reshape(m // CHUNK, CHUNK, 2 * mh), (0, 2, 1))
        nrm = r3(mlstm_norm)
        hm_p, c_p, n_p, m_p = _mlstm_call(q, k, v, og, gates, gates_r, zeros_c, zeros_n, zeros_m, nrm,
                                          l, 0, bp, sp // CHUNK, mh)
        hm_s, c_s, n_s, m_s = _mlstm_call(q, k, v, og, gates, gates_r, state_C[l], state_n[l], m0_s[l], nrm,
                                          l, mp // CHUNK, bs, ss // CHUNK, mh)
        riders = [(ffn2_w_in, l), (ffn2_w_out, l)]
        if l + 1 < depth:
            riders += [(ffn1_w_in, l + 1), (ffn1_w_out, l + 1)]
        oa_p, f2_in, f2_out, *next_f1 = _attn_prompt_call(cq, ckv, kpe, wuq, wuk, wuv, cos, sin, l, bp, sp, tq, dims, riders)
        if next_f1:
            f1_in, f1_out = next_f1
        oa_s = _attn_sample_call(cq, ckv, kpe, cache_ckv, cache_kpe, wuq, wuk, wuv, cos, sin, l, mp // ss, bs, ss, dims)
        x = _outproj_call(x, hm_p, hm_s.reshape(ms, hdv), oa_p, oa_s, mod, w_out_b, l, tm, sp)
        x = _ffn_call(x, mod, r3(ln_ffn2), f2_in, f2_out, l, 6, tm_ffn, tf)
        outs["p_ckv"].append(ckv[:mp].reshape(bp, sp, kvl))
        outs["p_kpe"].append(kpe[:mp].reshape(bp, sp, rope))
        outs["s_ckv"].append(ckv[mp:].reshape(bs, ss, kvl))
        outs["s_kpe"].append(kpe[mp:].reshape(bs, ss, rope))
        outs["p_C"].append(c_p)
        outs["p_n"].append(n_p)
        outs["p_m"].append(m_p[..., 0])
        outs["s_C"].append(c_s)
        outs["s_n"].append(n_s)
        outs["s_m"].append(m_s[..., 0])

    fn = final_norm.reshape(1, d)
    y_prompt = _final_norm_call(x, fn, 0, mp, tm).reshape(bp, sp, d)
    y_sample = _final_norm_call(x, fn, mp // tm, ms, tm).reshape(bs, ss, d)
    st = {k: jnp.stack(v) for k, v in outs.items()}
    return (y_prompt, y_sample, st["p_ckv"], st["p_kpe"], st["p_C"], st["p_n"], st["p_m"],
            st["s_ckv"], st["s_kpe"], st["s_C"], st["s_n"], st["s_m"])
```

```python
import functools
import math

import jax
import jax.numpy as jnp
from jax import lax
from jax.experimental import pallas as pl
from jax.experimental.pallas import tpu as pltpu

CHUNK = 64
EPS = 1e-6
ROPE_THETA = 10000.0
N_MOD = 9
LANES = 128
VMEM_LIMIT = 58 * 1024 * 1024

BF = jnp.bfloat16
F32 = jnp.float32


def _cparams(*sem):
    return pltpu.CompilerParams(dimension_semantics=sem, vmem_limit_bytes=VMEM_LIMIT)


def _dot(a, b):
    return jnp.dot(a, b, preferred_element_type=F32)


def _dot_nt(a, b):
    return lax.dot_general(a, b, (((1,), (1,)), ((), ())), preferred_element_type=F32)


def _dot_tn(a, b):
    return lax.dot_general(a, b, (((0,), (0,)), ((), ())), preferred_element_type=F32)


def _rms(x, g):
    ms = jnp.mean(x * x, axis=-1, keepdims=True)
    return x * lax.rsqrt(ms + EPS) * g


def _silu(x):
    return x * jax.nn.sigmoid(x)


def _rope128(pe, cos, sin):
    lane = lax.broadcasted_iota(jnp.int32, pe.shape, 1)
    first_half = jnp.bitwise_and(lane, 63) < 32
    swapped = jnp.where(first_half, pltpu.roll(pe, 96, 1), pltpu.roll(pe, 32, 1))
    return pe * cos + swapped * sin


def _mod_kernel(c_ref, w_ref, b_ref, o_ref):
    a = _silu(c_ref[...]).astype(BF)
    o_ref[...] = _dot(a, w_ref[...].astype(BF)) + b_ref[...]


def _mod_call(cg, mod_w, mod_b):
    depth, d, nd = mod_w.shape
    g = cg.shape[0]
    tn = 1024
    return pl.pallas_call(
        _mod_kernel,
        grid=(depth, nd // tn),
        in_specs=[
            pl.BlockSpec((g, d), lambda l, j: (0, 0)),
            pl.BlockSpec((None, d, tn), lambda l, j: (l, 0, j)),
            pl.BlockSpec((None, 1, tn), lambda l, j: (l, 0, j)),
        ],
        out_specs=pl.BlockSpec((None, g, tn), lambda l, j: (l, 0, j)),
        out_shape=jax.ShapeDtypeStruct((depth, g, nd), F32),
        compiler_params=_cparams("parallel", "parallel"),
        name="adaln_mod",
    )(cg, mod_w, mod_b.reshape(depth, 1, nd))


def _norm_mod_to(h_ref, x_ref, ln_ref, sh_ref, sc_ref):
    ln = ln_ref[...]
    for g in range(x_ref.shape[0] // CHUNK):
        rows = pl.ds(g * CHUNK, CHUNK)
        y = _rms(x_ref[rows, :], ln)
        h_ref[rows, :] = (y * (1.0 + sc_ref[g:g + 1, :]) + sh_ref[g:g + 1, :]).astype(h_ref.dtype)


def _ffn_kernel(*refs, final_norm, aliased):
    x_ref, sh_ref, sc_ref, gt_ref, ln_ref, wg_ref, wu_ref, wo_ref = refs[:8]
    fg_ref = refs[8] if final_norm else None
    o_ref, h_sc = refs[8 + final_norm + aliased:]
    j = pl.program_id(1)
    nj = pl.num_programs(1)

    def up_down(h):
        a = (_silu(_dot(h, wg_ref[...])) * _dot(h, wu_ref[...])).astype(BF)
        return _dot(a, wo_ref[...])

    def residual(acc):
        for g in range(x_ref.shape[0] // CHUNK):
            rows = pl.ds(g * CHUNK, CHUNK)
            y = x_ref[rows, :] + (0.5 * gt_ref[g:g + 1, :]) * acc[g * CHUNK:(g + 1) * CHUNK, :]
            o_ref[rows, :] = _rms(y, fg_ref[...]) if final_norm else y

    @pl.when(j == 0)
    def _():
        _norm_mod_to(h_sc, x_ref, ln_ref, sh_ref, sc_ref)
        o_ref[...] = up_down(h_sc[...])

    @pl.when(jnp.logical_and(j > 0, j < nj - 1))
    def _():
        o_ref[...] += up_down(h_sc[...])

    @pl.when(jnp.logical_and(j > 0, j == nj - 1))
    def _():
        residual(o_ref[...] + up_down(h_sc[...]))

    @pl.when(nj == 1)
    def _():
        residual(o_ref[...])


def _ffn_call(x, mod, ln, w_in, w_out, layer, k0, tm, tf, *, src_tile0=0, n_tiles=None, stream_tile0=0,
              out_rows=None, out_tile0=0, into=None, final_gain=None):
    d = x.shape[1]
    f = w_out.shape[0]
    ng = tm // CHUNK
    nf = f // tf
    n_tiles = x.shape[0] // tm if n_tiles is None else n_tiles
    out_rows = (into.shape[0] if into is not None else n_tiles * tm) if out_rows is None else out_rows

    def mod_spec(k):
        return pl.BlockSpec((None, ng, d), lambda i, j: (layer, i + stream_tile0, k))

    in_specs = [
        pl.BlockSpec((tm, d), lambda i, j: (i + src_tile0, 0)),
        mod_spec(k0), mod_spec(k0 + 1), mod_spec(k0 + 2),
        pl.BlockSpec((None, 1, d), lambda i, j: (layer, 0, 0)),
        pl.BlockSpec((d, tf), lambda i, j: (0, j)),
        pl.BlockSpec((d, tf), lambda i, j: (0, j + nf)),
        pl.BlockSpec((tf, d), lambda i, j: (j, 0)),
    ]
    operands = [x, mod, mod, mod, ln, w_in, w_in, w_out]
    if final_gain is not None:
        in_specs.append(pl.BlockSpec((1, d), lambda i, j: (0, 0)))
        operands.append(final_gain)
    aliases = {}
    if into is not None:
        aliases = {len(operands): 0}
        in_specs.append(pl.BlockSpec(memory_space=pl.ANY))
        operands.append(into)
    kern = functools.partial(_ffn_kernel, final_norm=final_gain is not None, aliased=into is not None)
    return pl.pallas_call(
        kern,
        grid=(n_tiles, nf),
        in_specs=in_specs,
        out_specs=pl.BlockSpec((tm, d), lambda i, j: (i + out_tile0, 0)),
        out_shape=jax.ShapeDtypeStruct((out_rows, d), F32),
        scratch_shapes=[pltpu.VMEM((tm, d), BF)],
        input_output_aliases=aliases,
        compiler_params=_cparams("parallel", "arbitrary"),
        name="ffn",
    )(*operands)


def _inproj_kernel(x_ref, sh_ref, sc_ref, ln_ref, wqk_ref, wv_ref, wo_ref, wa_ref, qn_ref, kvn_ref, gb_ref,
                   cos_ref, sin_ref, q_ref, k_ref, v_ref, og_ref, cq_ref, ckv_ref, kpe_ref, gt_ref, h_sc,
                   *, n_heads, k_scale):
    _norm_mod_to(h_sc, x_ref, ln_ref, sh_ref, sc_ref)
    h = h_sc[...]
    hdk = q_ref.shape[1]
    zqk = _dot(h, wqk_ref[...])
    q_ref[...] = zqk[:, :hdk].astype(BF)
    k_ref[...] = (zqk[:, hdk:] * k_scale).astype(BF)
    v_ref[...] = _dot(h, wv_ref[...]).astype(BF)
    og_ref[...] = jax.nn.sigmoid(_dot(h, wo_ref[...]))
    za = _dot(h, wa_ref[...])
    ql = cq_ref.shape[1]
    kvl = ckv_ref.shape[1]
    cq_ref[...] = _rms(za[:, :ql], qn_ref[...]).astype(BF)
    ckv_ref[...] = _rms(za[:, ql:ql + kvl], kvn_ref[...])
    pe = _rope128(za[:, ql + kvl:ql + kvl + LANES], cos_ref[...], sin_ref[...])
    kpe_ref[...] = pe[:, :kpe_ref.shape[1]]
    zg = za[:, ql + kvl + LANES:] + gb_ref[...]
    lane = lax.broadcasted_iota(jnp.int32, zg.shape, 1)
    log_sig = jnp.minimum(zg, 0.0) - jnp.log1p(jnp.exp(-jnp.abs(zg)))
    gt_ref[...] = jnp.where(lane < n_heads, zg, log_sig)


def _inproj_call(x, mod, ln, wqk, wv, wo, wa, q_norm, kv_norm, gate_bias, cos, sin, layer, tm, dims):
    m, d = x.shape
    ng = tm // CHUNK
    hdk, hdv, ql, kvl, rope = dims["hdk"], dims["hdv"], dims["ql"], dims["kvl"], dims["rope"]

    def mod_spec(k):
        return pl.BlockSpec((None, ng, d), lambda i: (layer, i, k))

    def wspec(w):
        return pl.BlockSpec((None,) + w.shape[1:], lambda i: (layer, 0, 0))

    def row(n):
        return pl.BlockSpec((tm, n), lambda i: (i, 0))

    kern = functools.partial(_inproj_kernel, n_heads=dims["mh"], k_scale=dims["dk"] ** -0.5)
    return pl.pallas_call(
        kern,
        grid=(m // tm,),
        in_specs=[
            row(d), mod_spec(3), mod_spec(4), wspec(ln), wspec(wqk), wspec(wv), wspec(wo), wspec(wa),
            wspec(q_norm), wspec(kv_norm), wspec(gate_bias), row(LANES), row(LANES),
        ],
        out_specs=[row(hdk), row(hdk), row(hdv), row(hdv), row(ql), row(kvl), row(rope), row(LANES)],
        out_shape=[
            jax.ShapeDtypeStruct((m, hdk), BF), jax.ShapeDtypeStruct((m, hdk), BF),
            jax.ShapeDtypeStruct((m, hdv), BF), jax.ShapeDtypeStruct((m, hdv), F32),
            jax.ShapeDtypeStruct((m, ql), BF), jax.ShapeDtypeStruct((m, kvl), F32),
            jax.ShapeDtypeStruct((m, rope), F32), jax.ShapeDtypeStruct((m, LANES), F32),
        ],
        scratch_shapes=[pltpu.VMEM((tm, d), BF)],
        compiler_params=_cparams("parallel"),
        name="in_proj",
    )(x, mod, mod, ln, wqk, wv, wo, wa, q_norm, kv_norm, gate_bias, cos, sin)


def _mlstm_kernel(*refs, n_heads, bt):
    tok = [refs[6 * u:6 * u + 6] for u in range(bt)]
    c0_ref, n0_ref, m0_ref, nrm_ref, hm_ref, c_ref, n_ref, m_ref = refs[6 * bt:]
    c_idx = pl.program_id(1)

    @pl.when(c_idx == 0)
    def _():
        c_ref[...] = c0_ref[...]
        n_ref[...] = n0_ref[...]
        m_ref[...] = m0_ref[...]

    L = tok[0][0].shape[0]
    dk = tok[0][0].shape[1] // n_heads
    dv = tok[0][2].shape[1] // n_heads
    chains = [(u, h) for u in range(bt) for h in range(n_heads)]
    nc = len(chains)

    def stack(fn):
        return jnp.concatenate([fn(u, h) for u, h in chains], axis=0)

    def per_chain_last(col):
        last = col.reshape(nc, L, 1)[:, L - 1:L, :]
        return jnp.broadcast_to(last, (nc, L, 1)).reshape(nc * L, 1)

    t_idx = jnp.bitwise_and(lax.broadcasted_iota(jnp.int32, (nc * L, L), 0), L - 1)
    s_idx = lax.broadcasted_iota(jnp.int32, (nc * L, L), 1)
    causal = s_idx <= t_idx
    ig_c = stack(lambda u, h: tok[u][4][:, h:h + 1])
    lf_c = stack(lambda u, h: tok[u][4][:, n_heads + h:n_heads + h + 1])
    ig_r = stack(lambda u, h: jnp.broadcast_to(tok[u][5][h:h + 1, :], (L, L)))
    lf_r = stack(lambda u, h: jnp.broadcast_to(tok[u][5][n_heads + h:n_heads + h + 1, :], (L, L)))
    m_prev = stack(lambda u, h: jnp.broadcast_to(m_ref[u, h:h + 1, 0:1], (L, 1)))
    b_c = jnp.sum(jnp.where(causal, lf_r, 0.0), axis=1, keepdims=True)
    b_r = jnp.sum(jnp.where(causal, 0.0, lf_c).reshape(nc, L, L), axis=1, keepdims=True)
    b_r = jnp.broadcast_to(b_r, (nc, L, L)).reshape(nc * L, L) + lf_r
    b_last = per_chain_last(b_c)
    dmat = jnp.where(causal, b_c - b_r + ig_r, -jnp.inf)
    g_c = b_c + m_prev
    m_t = jnp.maximum(g_c, jnp.max(dmat, axis=1, keepdims=True))
    w_intra = jnp.exp(dmat - m_t)
    w_inter = jnp.exp(g_c - m_t)
    m_new = per_chain_last(m_t)
    wa = jnp.exp(b_last - b_c + ig_c - m_new)
    decay = jnp.exp(b_last + m_prev - m_new)

    qs = {c: tok[c[0]][0][:, c[1] * dk:(c[1] + 1) * dk] for c in chains}
    ks = {c: tok[c[0]][1][:, c[1] * dk:(c[1] + 1) * dk] for c in chains}
    vs = {c: tok[c[0]][2][:, c[1] * dv:(c[1] + 1) * dv] for c in chains}
    c_prev = {c: c_ref[c[0], c[1]] for c in chains}
    n_prev = {c: n_ref[c[0], c[1]:c[1] + 1, :] for c in chains}
    s = stack(lambda u, h: _dot_nt(qs[u, h], ks[u, h])) * w_intra
    qc = stack(lambda u, h: _dot_nt(qs[u, h], c_prev[u, h].astype(BF)))
    s_bf = s.astype(BF)
    sv = jnp.concatenate([_dot(s_bf[i * L:(i + 1) * L, :], vs[c]) for i, c in enumerate(chains)], axis=0)
    qn = jnp.sum(stack(lambda u, h: qs[u, h].astype(F32) * n_prev[u, h]), axis=1, keepdims=True)
    num = sv + w_inter * qc
    nq = jnp.sum(s, axis=1, keepdims=True) + w_inter * qn
    x = num / jnp.maximum(jnp.abs(nq), jnp.exp(-m_t))
    x = x * stack(lambda u, h: tok[u][3][:, h * dv:(h + 1) * dv])
    x = x * lax.rsqrt(jnp.mean(x * x, axis=-1, keepdims=True) + EPS)
    wk = wa * stack(lambda u, h: ks[u, h].astype(F32))
    wk_bf = wk.astype(BF)
    for i, (u, h) in enumerate(chains):
        rows = slice(i * L, (i + 1) * L)
        hm_ref[u, :, h * dv:(h + 1) * dv] = (x[rows, :] * nrm_ref[:, h * dv:(h + 1) * dv]).astype(hm_ref.dtype)
        dec = decay[i * L:i * L + 1, :]
        c_ref[u, h] = dec * c_prev[u, h] + _dot_tn(vs[u, h], wk_bf[rows, :])
        n_ref[u, h:h + 1, :] = dec * n_prev[u, h] + jnp.sum(wk[rows, :], axis=0, keepdims=True)
        m_ref[u, h:h + 1, :] = jnp.broadcast_to(m_new[i * L:i * L + 1, :], (1, m_ref.shape[2]))


def _mlstm_call(q, k, v, og, gates_c, gates_r, c0, n0, m0, norm, layer, row0, n_seq, n_chunks, n_heads):
    hdk = q.shape[1]
    hdv = v.shape[1]
    dv, dk = c0.shape[-2:]
    g2 = gates_r.shape[1]
    bt = _pick_tile(n_seq, 4)

    def chunked(a):
        return a.reshape(a.shape[0] // CHUNK, CHUNK, a.shape[1])

    arrays = [chunked(q), chunked(k), chunked(v), chunked(og), chunked(gates_c), gates_r]
    shapes = [(CHUNK, hdk), (CHUNK, hdk), (CHUNK, hdv), (CHUNK, hdv), (CHUNK, LANES), (g2, CHUNK)]

    def tok(u, shape):
        return pl.BlockSpec((None,) + shape, lambda t, c: (row0 + (t * bt + u) * n_chunks + c, 0, 0))

    def state(shape):
        return pl.BlockSpec((bt,) + shape, lambda t, c: (t,) + (0,) * len(shape))

    in_specs = [tok(u, shape) for u in range(bt) for shape in shapes]
    in_specs += [state((n_heads, dv, dk)), state((n_heads, dk)), state((n_heads, LANES)),
                 pl.BlockSpec((None, 1, hdv), lambda t, c: (layer, 0, 0))]
    kern = functools.partial(_mlstm_kernel, n_heads=n_heads, bt=bt)
    return pl.pallas_call(
        kern,
        grid=(n_seq // bt, n_chunks),
        in_specs=in_specs,
        out_specs=[
            pl.BlockSpec((None, bt, CHUNK, hdv), lambda t, c: (c, t, 0, 0)),
            state((n_heads, dv, dk)), state((n_heads, dk)), state((n_heads, LANES)),
        ],
        out_shape=[
            jax.ShapeDtypeStruct((n_chunks, n_seq, CHUNK, hdv), BF),
            jax.ShapeDtypeStruct((n_seq, n_heads, dv, dk), F32),
            jax.ShapeDtypeStruct((n_seq, n_heads, dk), F32),
            jax.ShapeDtypeStruct((n_seq, n_heads, LANES), F32),
        ],
        compiler_params=_cparams("parallel", "arbitrary"),
        name="mlstm",
    )(*(arrays * bt), c0, n0, m0, norm)


def _build_queries(q_sc, cq_ref, wuq_ref, wuk_ref, cos_ref, sin_ref, n_heads, nope, rope, kvl):
    tq = cq_ref.shape[0]
    qa = _dot(cq_ref[...], wuq_ref[...])
    cos = cos_ref[...]
    sin = sin_ref[...]
    for h in range(n_heads):
        qn = qa[:, h * nope:(h + 1) * nope].astype(BF)
        q_sc[h * tq:(h + 1) * tq, 0:kvl] = _dot(qn, wuk_ref[h]).astype(BF)
    per = LANES // rope
    for p in range(n_heads // per):
        base = n_heads * nope + p * LANES
        pe = _rope128(qa[:, base:base + LANES], cos, sin)
        for u in range(per):
            h = p * per + u
            q_sc[h * tq:(h + 1) * tq, kvl:kvl + rope] = pe[:, u * rope:(u + 1) * rope].astype(BF)


def _lane_tile(x, width):
    if width <= LANES:
        return x[:, :width]
    return jnp.concatenate([x] * (width // LANES), axis=1)


def _flash_step(q_sc, keys_next, s_next, s_cur, vals_cur, m_sc, l_sc, acc_sc, sm_scale, rs, mask_chunks=None):
    n_groups = q_sc.shape[0] // rs

    def group(r, carry):
        r0 = pl.multiple_of(r * rs, rs)
        rows = pl.ds(r0, rs)
        if s_next is not None:
            s_next[rows, :] = _dot_nt(q_sc[rows, :], keys_next)
        if s_cur is not None:
            kb, kvl = vals_cur.shape
            s = s_cur[rows, :] * sm_scale
            if mask_chunks is not None:
                q_chunks, q_chunk0, k_chunk0 = mask_chunks
                chunk_bits = CHUNK.bit_length() - 1
                row = lax.broadcasted_iota(jnp.int32, (rs, kb), 0) + r0
                q_chunk = jnp.bitwise_and(jnp.right_shift(row, chunk_bits), q_chunks - 1) + q_chunk0
                k_chunk = jnp.right_shift(lax.broadcasted_iota(jnp.int32, (rs, kb), 1), chunk_bits) + k_chunk0
                s = jnp.where(k_chunk <= q_chunk, s, -jnp.inf)
            m_prev = m_sc[rows, :]
            m_new = jnp.maximum(m_prev, jnp.max(s, axis=1, keepdims=True))
            alpha = jnp.exp(m_prev - m_new)
            p = jnp.exp(s - _lane_tile(m_new, kb))
            l_sc[rows, :] = alpha * l_sc[rows, :] + jnp.sum(p, axis=1, keepdims=True)
            acc_sc[rows, :] = _lane_tile(alpha, kvl) * acc_sc[rows, :] + _dot(p.astype(BF), vals_cur)
            m_sc[rows, :] = m_new
        return carry

    lax.fori_loop(0, n_groups, group, 0, unroll=2 if n_groups % 2 == 0 else 1)


def _flash_init(m_sc, l_sc, acc_sc):
    m_sc[...] = jnp.full(m_sc.shape, -jnp.inf, F32)
    l_sc[...] = jnp.zeros(l_sc.shape, F32)
    acc_sc[...] = jnp.zeros(acc_sc.shape, F32)


def _flash_finish(o_ref, l_sc, acc_sc, wuv_ref, n_heads, tq):
    vd = wuv_ref.shape[2]
    kvl = acc_sc.shape[1]
    for h in range(n_heads):
        rows = pl.ds(h * tq, tq)
        o_lat = (acc_sc[rows, :] / _lane_tile(l_sc[rows, :], kvl)).astype(BF)
        o_ref[:, h * vd:(h + 1) * vd] = _dot(o_lat, wuv_ref[h]).astype(o_ref.dtype)


def _attn_prompt_kernel(*refs, n_heads, nope, rope, sm_scale, rs, n_riders):
    cq_ref, ckv_ref, kpe_ref, wuq_ref, wuk_ref, wuv_ref, cos_ref, sin_ref = refs[:8]
    rider_in = refs[8:8 + n_riders]
    o_ref = refs[8 + n_riders]
    rider_out = refs[9 + n_riders:9 + 2 * n_riders]
    kcat, q_sc, m_sc, l_sc, acc_sc, s_a, s_b = refs[9 + 2 * n_riders:]
    i = pl.program_id(1)
    tq = cq_ref.shape[0]
    kvl = ckv_ref.shape[1]
    kb = s_a.shape[1]

    for src, dst in zip(rider_in, rider_out):
        dst[...] = src[...].astype(dst.dtype)

    @pl.when(i == 0)
    def _():
        kcat[:, 0:kvl] = ckv_ref[...].astype(BF)
        kcat[:, kvl:kvl + rope] = kpe_ref[...].astype(BF)

    _build_queries(q_sc, cq_ref, wuq_ref, wuk_ref, cos_ref, sin_ref, n_heads, nope, rope, kvl)
    _flash_init(m_sc, l_sc, acc_sc)

    def step(j_next, s_next, j_cur, s_cur, masked=False):
        keys = None if s_next is None else kcat[pl.ds(pl.multiple_of(j_next * kb, kb), kb), :]
        vals = None if s_cur is None else kcat[pl.ds(pl.multiple_of(j_cur * kb, kb), kb), 0:kvl]
        mask = (tq // CHUNK, i * (tq // CHUNK), j_cur * (kb // CHUNK)) if masked else None
        _flash_step(q_sc, keys, s_next, s_cur, vals, m_sc, l_sc, acc_sc, sm_scale, rs, mask)

    last = (i * tq) // kb
    step(0, s_a, None, None)

    def pair(t, carry):
        j = 2 * t
        step(j + 1, s_b, j, s_a)
        step(j + 2, s_a, j + 1, s_b)
        return carry

    lax.fori_loop(0, last // 2, pair, 0)

    @pl.when(last % 2 == 0)
    def _():
        step(None, None, last, s_a, masked=True)

    @pl.when(last % 2 == 1)
    def _():
        step(last, s_b, last - 1, s_a)
        step(None, None, last, s_b, masked=True)

    _flash_finish(o_ref, l_sc, acc_sc, wuv_ref, n_heads, tq)


def _rider_block(rows, cols, n_steps):
    for n_row_blocks in range(n_steps, 0, -1):
        if n_steps % n_row_blocks or rows % n_row_blocks or cols % (n_steps // n_row_blocks):
            continue
        br, bc = rows // n_row_blocks, cols // (n_steps // n_row_blocks)
        if br % 16 == 0 and bc % LANES == 0:
            return br, bc
    raise ValueError(f"no aligned {n_steps}-block tiling of ({rows}, {cols})")


def _attn_prompt_call(cq, ckv, kpe, wuq, wuk, wuv, cos, sin, layer, n_seq, seq, tq, dims, riders=()):
    ah, nope, rope, kvl, vd = dims["ah"], dims["nope"], dims["rope"], dims["kvl"], dims["vd"]
    ql = cq.shape[1]
    nq = seq // tq
    kb = _pick_tile(seq, 512)
    assert tq % CHUNK == 0 and (tq // CHUNK) & (tq // CHUNK - 1) == 0 and kb % tq == 0
    rs = _pick_tile(ah * tq, 512)
    kern = functools.partial(_attn_prompt_kernel, n_heads=ah, nope=nope, rope=rope,
                             sm_scale=(nope + rope) ** -0.5, rs=rs, n_riders=len(riders))

    def wspec(w):
        return pl.BlockSpec((None,) + w.shape[1:], lambda b, i: (layer,) + (0,) * (w.ndim - 1))

    rider_in, rider_out, rider_shapes = [], [], []
    for arr, arr_layer in riders:
        _, rows, cols = arr.shape
        br, bc = _rider_block(rows, cols, n_seq * nq)
        ncb = cols // bc
        rider_in.append(pl.BlockSpec((None, br, bc), functools.partial(
            lambda b, i, ncb, arr_layer: (arr_layer, (b * nq + i) // ncb, (b * nq + i) % ncb), ncb=ncb, arr_layer=arr_layer)))
        rider_out.append(pl.BlockSpec((br, bc), functools.partial(
            lambda b, i, ncb: ((b * nq + i) // ncb, (b * nq + i) % ncb), ncb=ncb)))
        rider_shapes.append(jax.ShapeDtypeStruct((rows, cols), BF))

    return pl.pallas_call(
        kern,
        grid=(n_seq, nq),
        in_specs=[
            pl.BlockSpec((tq, ql), lambda b, i: (b * nq + i, 0)),
            pl.BlockSpec((seq, kvl), lambda b, i: (b, 0)),
            pl.BlockSpec((seq, rope), lambda b, i: (b, 0)),
            wspec(wuq), wspec(wuk), wspec(wuv),
            pl.BlockSpec((tq, LANES), lambda b, i: (b * nq + i, 0)),
            pl.BlockSpec((tq, LANES), lambda b, i: (b * nq + i, 0)),
        ] + rider_in,
        out_specs=[pl.BlockSpec((tq, ah * vd), lambda b, i: (b * nq + i, 0))] + rider_out,
        out_shape=[jax.ShapeDtypeStruct((n_seq * seq, ah * vd), BF)] + rider_shapes,
        scratch_shapes=[
            pltpu.VMEM((seq, kvl + rope), BF),
            pltpu.VMEM((ah * tq, kvl + rope), BF),
            pltpu.VMEM((ah * tq, LANES), F32),
            pltpu.VMEM((ah * tq, LANES), F32),
            pltpu.VMEM((ah * tq, kvl), F32),
            pltpu.VMEM((ah * tq, kb), F32),
            pltpu.VMEM((ah * tq, kb), F32),
        ],
        compiler_params=_cparams("parallel", "arbitrary"),
        name="mla_prompt",
    )(cq, ckv, kpe, wuq, wuk, wuv, cos, sin, *[arr for arr, _ in riders])


def _attn_sample_kernel(cq_ref, ckv_ref, kpe_ref, pckv_ref, pkpe_ref, wuq_ref, wuk_ref, wuv_ref, cos_ref, sin_ref,
                        o_ref, kcat, q_sc, m_sc, l_sc, acc_sc, s_a, s_b, s_n, *, n_heads, nope, rope, sm_scale):
    tq = cq_ref.shape[0]
    kvl = ckv_ref.shape[1]
    past = pckv_ref.shape[0]
    kb = s_a.shape[1]
    kcat[0:past, 0:kvl] = pckv_ref[...].astype(BF)
    kcat[0:past, kvl:kvl + rope] = pkpe_ref[...].astype(BF)
    kcat[past:past + tq, 0:kvl] = ckv_ref[...].astype(BF)
    kcat[past:past + tq, kvl:kvl + rope] = kpe_ref[...].astype(BF)
    _build_queries(q_sc, cq_ref, wuq_ref, wuk_ref, cos_ref, sin_ref, n_heads, nope, rope, kvl)
    _flash_init(m_sc, l_sc, acc_sc)
    rs = q_sc.shape[0]

    blocks = [(j * kb, kb, (s_a, s_b)[j % 2]) for j in range(past // kb)] + [(past, tq, s_n)]
    for cur, nxt in zip([None] + blocks, blocks + [None]):
        keys, s_next = (None, None) if nxt is None else (kcat[nxt[0]:nxt[0] + nxt[1], :], nxt[2])
        vals, s_cur = (None, None) if cur is None else (kcat[cur[0]:cur[0] + cur[1], 0:kvl], cur[2])
        _flash_step(q_sc, keys, s_next, s_cur, vals, m_sc, l_sc, acc_sc, sm_scale, rs)
    _flash_finish(o_ref, l_sc, acc_sc, wuv_ref, n_heads, tq)


def _attn_sample_call(cq, ckv, kpe, cache_ckv, cache_kpe, wuq, wuk, wuv, cos, sin, layer, row0, n_seq, tq, dims):
    ah, nope, rope, kvl, vd = dims["ah"], dims["nope"], dims["rope"], dims["kvl"], dims["vd"]
    ql = cq.shape[1]
    past = cache_ckv.shape[2]
    kb = math.gcd(past, 256)
    kern = functools.partial(_attn_sample_kernel, n_heads=ah, nope=nope, rope=rope,
                             sm_scale=(nope + rope) ** -0.5)

    def wspec(w):
        return pl.BlockSpec((None,) + w.shape[1:], lambda b: (layer,) + (0,) * (w.ndim - 1))

    def tok(n):
        return pl.BlockSpec((tq, n), lambda b: (row0 + b, 0))

    return pl.pallas_call(
        kern,
        grid=(n_seq,),
        in_specs=[
            tok(ql), tok(kvl), tok(rope),
            pl.BlockSpec((None, None, past, kvl), lambda b: (layer, b, 0, 0)),
            pl.BlockSpec((None, None, past, rope), lambda b: (layer, b, 0, 0)),
            wspec(wuq), wspec(wuk), wspec(wuv), tok(LANES), tok(LANES),
        ],
        out_specs=pl.BlockSpec((tq, ah * vd), lambda b: (b, 0)),
        out_shape=jax.ShapeDtypeStruct((n_seq * tq, ah * vd), BF),
        scratch_shapes=[
            pltpu.VMEM((past + tq, kvl + rope), BF),
            pltpu.VMEM((ah * tq, kvl + rope), BF),
            pltpu.VMEM((ah * tq, LANES), F32),
            pltpu.VMEM((ah * tq, LANES), F32),
            pltpu.VMEM((ah * tq, kvl), F32),
            pltpu.VMEM((ah * tq, kb), F32),
            pltpu.VMEM((ah * tq, kb), F32),
            pltpu.VMEM((ah * tq, tq), F32),
        ],
        compiler_params=_cparams("parallel"),
        name="mla_sample",
    )(cq, ckv, kpe, cache_ckv, cache_kpe, wuq, wuk, wuv, cos, sin)


def _outproj_kernel(x_ref, hmp_ref, hms_ref, oap_ref, oas_ref, gt_ref, wm_ref, wa_ref, o_ref, *, n_prompt_tiles):
    i = pl.program_id(0)
    tm = x_ref.shape[0]

    def project(hm, oa):
        y = _dot(hm, wm_ref[...]) + _dot(oa, wa_ref[...])
        for g in range(tm // CHUNK):
            rows = pl.ds(g * CHUNK, CHUNK)
            o_ref[rows, :] = x_ref[rows, :] + gt_ref[g:g + 1, :] * y[g * CHUNK:(g + 1) * CHUNK, :]

    @pl.when(i < n_prompt_tiles)
    def _():
        project(hmp_ref[...].reshape(tm, hmp_ref.shape[-1]), oap_ref[...])

    @pl.when(i >= n_prompt_tiles)
    def _():
        project(hms_ref[...], oas_ref[...])


def _outproj_call(x, hm_p, hm_s, oa_p, oa_s, mod, w_out, layer, tm, seq):
    m, d = x.shape
    wm = hm_s.shape[1]
    wa = oa_s.shape[1]
    assert wm == wa and seq % tm == 0
    ng = tm // CHUNK
    npt = oa_p.shape[0] // tm
    tps = seq // tm

    def prompt_tile(i):
        return jnp.minimum(i, npt - 1)

    def sample_tile(i):
        return jnp.maximum(i - npt, 0)

    return pl.pallas_call(
        functools.partial(_outproj_kernel, n_prompt_tiles=npt),
        grid=(m // tm,),
        in_specs=[
            pl.BlockSpec((tm, d), lambda i: (i, 0)),
            pl.BlockSpec((ng, None, CHUNK, wm), lambda i: (prompt_tile(i) % tps, prompt_tile(i) // tps, 0, 0)),
            pl.BlockSpec((tm, wm), lambda i: (sample_tile(i), 0)),
            pl.BlockSpec((tm, wa), lambda i: (prompt_tile(i), 0)),
            pl.BlockSpec((tm, wa), lambda i: (sample_tile(i), 0)),
            pl.BlockSpec((None, ng, d), lambda i: (layer, i, 5)),
            pl.BlockSpec((None, wm, d), lambda i: (layer, 0, 0)),
            pl.BlockSpec((None, wa, d), lambda i: (layer, 1, 0)),
        ],
        out_specs=pl.BlockSpec((tm, d), lambda i: (i, 0)),
        out_shape=jax.ShapeDtypeStruct((m, d), F32),
        compiler_params=_cparams("parallel"),
        name="out_proj",
    )(x, hm_p, hm_s, oa_p, oa_s, mod, w_out, w_out)


def _rope_tables(pos, rope):
    half = rope // 2
    freqs = ROPE_THETA ** (-jnp.arange(half, dtype=F32) / half)
    ang = pos.astype(F32)[:, None] * freqs[None, :]
    cos = jnp.cos(ang)
    sin = jnp.sin(ang)
    reps = LANES // rope
    return jnp.tile(jnp.concatenate([cos, cos], axis=1), (1, reps)), jnp.tile(jnp.concatenate([-sin, sin], axis=1), (1, reps))


def _pick_tile(m, cap):
    t = cap
    while m % t:
        t //= 2
    return t


def kernel(x_prompt, x_sample, c_prompt, c_sample, cache_ckv, cache_kpe, state_C, state_n, state_m, mod_w, mod_b, ln_ffn1, ffn1_w_in, ffn1_w_out, ln_mix, w_in, mlstm_b_i, mlstm_b_f, mlstm_norm, q_norm, w_uq, kv_norm, w_uk, w_uv, w_out, ln_ffn2, ffn2_w_in, ffn2_w_out, final_norm):
    bp, sp, d = x_prompt.shape
    bs, ss, _ = x_sample.shape
    depth = mod_w.shape[0]
    past = cache_ckv.shape[2]
    mh, dv, dk = state_C.shape[2:]
    kvl, ah, nope = w_uk.shape[1:]
    vd = w_uv.shape[3]
    rope = cache_kpe.shape[3]
    ql = q_norm.shape[1]
    hdk, hdv = mh * dk, mh * dv
    dff = ffn1_w_out.shape[1]
    dims = dict(mh=mh, dk=dk, dv=dv, hdk=hdk, hdv=hdv, ql=ql, kvl=kvl, rope=rope, ah=ah, nope=nope, vd=vd)
    assert sp % CHUNK == 0 and ss == CHUNK and LANES % rope == 0 and ah % (LANES // rope) == 0
    mp, ms = bp * sp, bs * ss
    m = mp + ms
    tm = _pick_tile(math.gcd(mp, ms), 512)
    tm_ffn = _pick_tile(math.gcd(mp, ms), 1024)
    ffn_tiles_p, ffn_tiles_s = mp // tm_ffn, ms // tm_ffn
    tf = _pick_tile(dff, 512)
    tq = _pick_tile(sp, 256)

    cg = jnp.concatenate([jnp.repeat(c_prompt, sp // CHUNK, axis=0), jnp.repeat(c_sample, ss // CHUNK, axis=0)], axis=0)
    pos = jnp.concatenate([jnp.tile(jnp.arange(sp), bp), jnp.tile(past + jnp.arange(ss), bs)])
    cos, sin = _rope_tables(pos, rope)

    o_mq, o_mk, o_mv, o_mo = 0, hdk, 2 * hdk, 2 * hdk + hdv
    o_mi = o_mo + hdv
    o_mf, o_qa = o_mi + mh, o_mi + 2 * mh
    o_kva, o_pe = o_qa + ql, o_qa + ql + kvl
    wqk = w_in[:, :, o_mq:o_mv].astype(BF)
    wv = w_in[:, :, o_mv:o_mo].astype(BF)
    wo = w_in[:, :, o_mo:o_mi].astype(BF)
    zpad = lambda n: jnp.zeros((depth, d, n), w_in.dtype)
    wa = jnp.concatenate([w_in[:, :, o_qa:o_pe + rope], zpad(LANES - rope), w_in[:, :, o_mi:o_qa], zpad(LANES - 2 * mh)], axis=2).astype(BF)
    gate_bias = jnp.concatenate([mlstm_b_i, mlstm_b_f, jnp.zeros((depth, LANES - 2 * mh), F32)], axis=1).reshape(depth, 1, LANES)
    wuq4 = w_uq.reshape(depth, ql, ah, nope + rope)
    wuq = jnp.concatenate([wuq4[..., :nope].reshape(depth, ql, ah * nope), wuq4[..., nope:].reshape(depth, ql, ah * rope)], axis=2).astype(BF)
    wuk = jnp.transpose(w_uk, (0, 2, 3, 1)).astype(BF)
    wuv = jnp.transpose(w_uv, (0, 2, 1, 3)).astype(BF)
    w_out_b = w_out.astype(BF)
    f1_in, f1_out = ffn1_w_in[0].astype(BF), ffn1_w_out[0].astype(BF)
    r3 = lambda a: a.reshape(depth, 1, a.shape[1])

    mod = _mod_call(cg, mod_w, mod_b)

    zeros_c = jnp.zeros((bp, mh, dv, dk), F32)
    zeros_n = jnp.zeros((bp, mh, dk), F32)
    zeros_m = jnp.zeros((bp, mh, LANES), F32)
    m0_s = jnp.broadcast_to(state_m[..., None], state_m.shape + (LANES,))

    outs = {k: [] for k in ("p_ckv", "p_kpe", "p_C", "p_n", "p_m", "s_ckv", "s_kpe", "s_C", "s_n", "s_m")}
    for l in range(depth):
        if l == 0:
            x = _ffn_call(x_prompt.reshape(mp, d), mod, r3(ln_ffn1), f1_in, f1_out, l, 0, tm_ffn, tf, out_rows=m)
            x = _ffn_call(x_sample.reshape(ms, d), mod, r3(ln_ffn1), f1_in, f1_out, l, 0, tm_ffn, tf,
                          stream_tile0=ffn_tiles_p, out_tile0=ffn_tiles_p, into=x)
        else:
            x = _ffn_call(x, mod, r3(ln_ffn1), f1_in, f1_out, l, 0, tm_ffn, tf)
        q, k, v, og, cq, ckv, kpe, gates = _inproj_call(
            x, mod, r3(ln_mix), wqk, wv, wo, wa, r3(q_norm), r3(kv_norm), gate_bias, cos, sin, l, tm, dims)
        gates_r = jnp.transpose(gates[:, :2 * mh].reshape(m // CHUNK, CHUNK, 2 * mh), (0, 2, 1))
        nrm = r3(mlstm_norm)
        hm_p, c_p, n_p, m_p = _mlstm_call(q, k, v, og, gates, gates_r, zeros_c, zeros_n, zeros_m, nrm,
                                          l, 0, bp, sp // CHUNK, mh)
        hm_s, c_s, n_s, m_s = _mlstm_call(q, k, v, og, gates, gates_r, state_C[l], state_n[l], m0_s[l], nrm,
                                          l, mp // CHUNK, bs, ss // CHUNK, mh)
        riders = [(ffn2_w_in, l), (ffn2_w_out, l)]
        if l + 1 < depth:
            riders += [(ffn1_w_in, l + 1), (ffn1_w_out, l + 1)]
        oa_p, f2_in, f2_out, *next_f1 = _attn_prompt_call(cq, ckv, kpe, wuq, wuk, wuv, cos, sin, l, bp, sp, tq, dims, riders)
        if next_f1:
            f1_in, f1_out = next_f1
        oa_s = _attn_sample_call(cq, ckv, kpe, cache_ckv, cache_kpe, wuq, wuk, wuv, cos, sin, l, mp // ss, bs, ss, dims)
        x = _outproj_call(x, hm_p, hm_s.reshape(ms, hdv), oa_p, oa_s, mod, w_out_b, l, tm, sp)
        if l + 1 < depth:
            x = _ffn_call(x, mod, r3(ln_ffn2), f2_in, f2_out, l, 6, tm_ffn, tf)
        else:
            fn = final_norm.reshape(1, d)
            y_prompt = _ffn_call(x, mod, r3(ln_ffn2), f2_in, f2_out, l, 6, tm_ffn, tf,
                                 n_tiles=ffn_tiles_p, final_gain=fn).reshape(bp, sp, d)
            y_sample = _ffn_call(x, mod, r3(ln_ffn2), f2_in, f2_out, l, 6, tm_ffn, tf, src_tile0=ffn_tiles_p,
                                 n_tiles=ffn_tiles_s, stream_tile0=ffn_tiles_p, final_gain=fn).reshape(bs, ss, d)
        outs["p_ckv"].append(ckv[:mp].reshape(bp, sp, kvl))
        outs["p_kpe"].append(kpe[:mp].reshape(bp, sp, rope))
        outs["s_ckv"].append(ckv[mp:].reshape(bs, ss, kvl))
        outs["s_kpe"].append(kpe[mp:].reshape(bs, ss, rope))
        outs["p_C"].append(c_p)
        outs["p_n"].append(n_p)
        outs["p_m"].append(m_p[..., 0])
        outs["s_C"].append(c_s)
        outs["s_n"].append(n_s)
        outs["s_m"].append(m_s[..., 0])

    st ={k: jnp.stack(v) for k, v in outs.items()}
    return (y_prompt, y_sample, st["p_ckv"], st["p_kpe"], st["p_C"], st["p_n"], st["p_m"],
            st["s_ckv"], st["s_kpe"], st["s_C"], st["s_n"], st["s_m"])
```

```python
import functools
import math

import jax
import jax.numpy as jnp
from jax import lax
from jax.experimental import pallas as pl
from jax.experimental.pallas import tpu as pltpu

CHUNK = 64
EPS = 1e-6
ROPE_THETA = 10000.0
N_MOD = 9
LANES = 128
VMEM_LIMIT = 58 * 1024 * 1024

BF = jnp.bfloat16
F32 = jnp.float32


def _cparams(*sem):
    return pltpu.CompilerParams(dimension_semantics=sem, vmem_limit_bytes=VMEM_LIMIT)


def _dot(a, b):
    return jnp.dot(a, b, preferred_element_type=F32)


def _dot_nt(a, b):
    return lax.dot_general(a, b, (((1,), (1,)), ((), ())), preferred_element_type=F32)


def _dot_tn(a, b):
    return lax.dot_general(a, b, (((0,), (0,)), ((), ())), preferred_element_type=F32)


def _rms(x, g):
    ms = jnp.mean(x * x, axis=-1, keepdims=True)
    return x * lax.rsqrt(ms + EPS) * g


def _silu(x):
    return x * jax.nn.sigmoid(x)


def _rope128(pe, cos, sin):
    lane = lax.broadcasted_iota(jnp.int32, pe.shape, 1)
    first_half = jnp.bitwise_and(lane, 63) < 32
    swapped = jnp.where(first_half, pltpu.roll(pe, 96, 1), pltpu.roll(pe, 32, 1))
    return pe * cos + swapped * sin


def _mod_kernel(c_ref, w_ref, b_ref, o_ref):
    a = _silu(c_ref[...]).astype(BF)
    o_ref[...] = _dot(a, w_ref[...].astype(BF)) + b_ref[...]


def _mod_call(cg, mod_w, mod_b):
    depth, d, nd = mod_w.shape
    g = cg.shape[0]
    tn = 1024
    return pl.pallas_call(
        _mod_kernel,
        grid=(depth, nd // tn),
        in_specs=[
            pl.BlockSpec((g, d), lambda l, j: (0, 0)),
            pl.BlockSpec((None, d, tn), lambda l, j: (l, 0, j)),
            pl.BlockSpec((None, 1, tn), lambda l, j: (l, 0, j)),
        ],
        out_specs=pl.BlockSpec((None, g, tn), lambda l, j: (l, 0, j)),
        out_shape=jax.ShapeDtypeStruct((depth, g, nd), F32),
        compiler_params=_cparams("parallel", "parallel"),
        name="adaln_mod",
    )(cg, mod_w, mod_b.reshape(depth, 1, nd))


def _norm_mod_to(h_ref, x_ref, ln_ref, sh_ref, sc_ref):
    ln = ln_ref[...]
    for g in range(x_ref.shape[0] // CHUNK):
        rows = pl.ds(g * CHUNK, CHUNK)
        y = _rms(x_ref[rows, :], ln)
        h_ref[rows, :] = (y * (1.0 + sc_ref[g:g + 1, :]) + sh_ref[g:g + 1, :]).astype(h_ref.dtype)


def _ffn_kernel(*refs, final_norm, aliased):
    x_ref, sh_ref, sc_ref, gt_ref, ln_ref, wg_ref, wu_ref, wo_ref = refs[:8]
    fg_ref = refs[8] if final_norm else None
    o_ref, h_sc = refs[8 + final_norm + aliased:]
    j = pl.program_id(1)
    nj = pl.num_programs(1)

    def up_down(h):
        a = (_silu(_dot(h, wg_ref[...])) * _dot(h, wu_ref[...])).astype(BF)
        return _dot(a, wo_ref[...])

    def residual(acc):
        for g in range(x_ref.shape[0] // CHUNK):
            rows = pl.ds(g * CHUNK, CHUNK)
            y = x_ref[rows, :] + (0.5 * gt_ref[g:g + 1, :]) * acc[g * CHUNK:(g + 1) * CHUNK, :]
            o_ref[rows, :] = _rms(y, fg_ref[...]) if final_norm else y

    @pl.when(j == 0)
    def _():
        _norm_mod_to(h_sc, x_ref, ln_ref, sh_ref, sc_ref)
        o_ref[...] = up_down(h_sc[...])

    @pl.when(jnp.logical_and(j > 0, j < nj - 1))
    def _():
        o_ref[...] += up_down(h_sc[...])

    @pl.when(jnp.logical_and(j > 0, j == nj - 1))
    def _():
        residual(o_ref[...] + up_down(h_sc[...]))

    @pl.when(nj == 1)
    def _():
        residual(o_ref[...])


def _ffn_call(x, mod, ln, w_in, w_out, layer, k0, tm, tf, *, src_tile0=0, n_tiles=None, stream_tile0=0,
              out_rows=None, out_tile0=0, into=None, final_gain=None):
    d = x.shape[1]
    f = w_out.shape[0]
    ng = tm // CHUNK
    nf = f // tf
    n_tiles = x.shape[0] // tm if n_tiles is None else n_tiles
    out_rows = (into.shape[0] if into is not None else n_tiles * tm) if out_rows is None else out_rows

    def mod_spec(k):
        return pl.BlockSpec((None, ng, d), lambda i, j: (layer, i + stream_tile0, k))

    in_specs = [
        pl.BlockSpec((tm, d), lambda i, j: (i + src_tile0, 0)),
        mod_spec(k0), mod_spec(k0 + 1), mod_spec(k0 + 2),
        pl.BlockSpec((None, 1, d), lambda i, j: (layer, 0, 0)),
        pl.BlockSpec((d, tf), lambda i, j: (0, j)),
        pl.BlockSpec((d, tf), lambda i, j: (0, j + nf)),
        pl.BlockSpec((tf, d), lambda i, j: (j, 0)),
    ]
    operands = [x, mod, mod, mod, ln, w_in, w_in, w_out]
    if final_gain is not None:
        in_specs.append(pl.BlockSpec((1, d), lambda i, j: (0, 0)))
        operands.append(final_gain)
    aliases = {}
    if into is not None:
        aliases = {len(operands): 0}
        in_specs.append(pl.BlockSpec(memory_space=pl.ANY))
        operands.append(into)
    kern = functools.partial(_ffn_kernel, final_norm=final_gain is not None, aliased=into is not None)
    return pl.pallas_call(
        kern,
        grid=(n_tiles, nf),
        in_specs=in_specs,
        out_specs=pl.BlockSpec((tm, d), lambda i, j: (i + out_tile0, 0)),
        out_shape=jax.ShapeDtypeStruct((out_rows, d), F32),
        scratch_shapes=[pltpu.VMEM((tm, d), BF)],
        input_output_aliases=aliases,
        compiler_params=_cparams("parallel", "arbitrary"),
        name="ffn",
    )(*operands)


def _inproj_kernel(*refs, n_heads, k_scale, n_aliased, n_prompt_tiles):
    (x_ref, sh_ref, sc_ref, ln_ref, wqk_ref, wv_ref, wo_ref, wa_ref, qn_ref, kvn_ref, gb_ref,
     cos_ref, sin_ref) = refs[:13]
    (q_ref, k_ref, v_ref, og_ref, cq_ref, ckv_ref, kpe_ref, gt_ref, ckvp_ref, ckvs_ref,
     h_sc) = refs[13 + n_aliased:]
    _norm_mod_to(h_sc, x_ref, ln_ref, sh_ref, sc_ref)
    h = h_sc[...]
    hdk = q_ref.shape[1]
    zqk = _dot(h, wqk_ref[...])
    q_ref[...] = zqk[:, :hdk].astype(BF)
    k_ref[...] = (zqk[:, hdk:] * k_scale).astype(BF)
    v_ref[...] = _dot(h, wv_ref[...]).astype(BF)
    og_ref[...] = jax.nn.sigmoid(_dot(h, wo_ref[...]))
    za = _dot(h, wa_ref[...])
    ql = cq_ref.shape[1]
    kvl = ckv_ref.shape[1]
    cq_ref[...] = _rms(za[:, :ql], qn_ref[...]).astype(BF)
    ckv = _rms(za[:, ql:ql + kvl], kvn_ref[...])
    ckv_ref[...] = ckv
    pe = _rope128(za[:, ql + kvl:ql + kvl + LANES], cos_ref[...], sin_ref[...])
    kpe_ref[...] = pe[:, :kpe_ref.shape[1]]
    zg = za[:, ql + kvl + LANES:] + gb_ref[...]
    lane = lax.broadcasted_iota(jnp.int32, zg.shape, 1)
    log_sig = jnp.minimum(zg, 0.0) - jnp.log1p(jnp.exp(-jnp.abs(zg)))
    gt_ref[...] = jnp.where(lane < n_heads, zg, log_sig)

    i = pl.program_id(0)

    @pl.when(i < n_prompt_tiles)
    def _():
        ckvp_ref[...] = ckv_ref[...]

    @pl.when(i >= n_prompt_tiles)
    def _():
        ckvs_ref[...] = ckv_ref[...]


def _inproj_call(x, mod, ln, wqk, wv, wo, wa, q_norm, kv_norm, gate_bias, cos, sin, layer, tm, dims,
                 n_layers, rows_prompt, ckv_stacks=None):
    m, d = x.shape
    ng = tm // CHUNK
    hdk, hdv, ql, kvl, rope = dims["hdk"], dims["hdv"], dims["ql"], dims["kvl"], dims["rope"]
    npt = rows_prompt // tm
    aliased = [] if ckv_stacks is None else list(ckv_stacks)

    def mod_spec(k):
        return pl.BlockSpec((None, ng, d), lambda i: (layer, i, k))

    def wspec(w):
        return pl.BlockSpec((None,) + w.shape[1:], lambda i: (layer, 0, 0))

    def row(n):
        return pl.BlockSpec((tm, n), lambda i: (i, 0))

    kern = functools.partial(_inproj_kernel, n_heads=dims["mh"], k_scale=dims["dk"] ** -0.5,
                             n_aliased=len(aliased), n_prompt_tiles=npt)
    return pl.pallas_call(
        kern,
        grid=(m // tm,),
        in_specs=[
            row(d), mod_spec(3), mod_spec(4), wspec(ln), wspec(wqk), wspec(wv), wspec(wo), wspec(wa),
            wspec(q_norm), wspec(kv_norm), wspec(gate_bias), row(LANES), row(LANES),
        ] + [pl.BlockSpec(memory_space=pl.ANY)] * len(aliased),
        out_specs=[
            row(hdk), row(hdk), row(hdv), row(hdv), row(ql), row(kvl), row(rope), row(LANES),
            pl.BlockSpec((None, tm, kvl), lambda i: (layer, jnp.minimum(i, npt - 1), 0)),
            pl.BlockSpec((None, tm, kvl), lambda i: (layer, jnp.maximum(i - npt, 0), 0)),
        ],
        out_shape=[
            jax.ShapeDtypeStruct((m, hdk), BF), jax.ShapeDtypeStruct((m, hdk), BF),
            jax.ShapeDtypeStruct((m, hdv), BF), jax.ShapeDtypeStruct((m, hdv), F32),
            jax.ShapeDtypeStruct((m, ql), BF), jax.ShapeDtypeStruct((m, kvl), F32),
            jax.ShapeDtypeStruct((m, rope), F32), jax.ShapeDtypeStruct((m, LANES), F32),
            jax.ShapeDtypeStruct((n_layers, rows_prompt, kvl), F32),
            jax.ShapeDtypeStruct((n_layers, m - rows_prompt, kvl), F32),
        ],
        scratch_shapes=[pltpu.VMEM((tm, d), BF)],
        input_output_aliases={13 + a: 8 + a for a in range(len(aliased))},
        compiler_params=_cparams("arbitrary"),
        name="in_proj",
    )(x, mod, mod, ln, wqk, wv, wo, wa, q_norm, kv_norm, gate_bias, cos, sin, *aliased)


def _mlstm_kernel(*refs, n_heads, bt):
    tok = [refs[6 * u:6 * u + 6] for u in range(bt)]
    c0_ref, n0_ref, m0_ref, nrm_ref = refs[6 * bt:6 * bt + 4]
    hm_ref, c_ref, n_ref, m_ref = refs[-4:]
    c_idx = pl.program_id(1)

    @pl.when(c_idx == 0)
    def _():
        c_ref[...] = c0_ref[...]
        n_ref[...] = n0_ref[...]
        m_ref[...] = m0_ref[...]

    L = tok[0][0].shape[0]
    dk = tok[0][0].shape[1] // n_heads
    dv = tok[0][2].shape[1] // n_heads
    chains = [(u, h) for u in range(bt) for h in range(n_heads)]
    nc = len(chains)

    def stack(fn):
        return jnp.concatenate([fn(u, h) for u, h in chains], axis=0)

    def per_chain_last(col):
        last = col.reshape(nc, L, 1)[:, L - 1:L, :]
        return jnp.broadcast_to(last, (nc, L, 1)).reshape(nc * L, 1)

    t_idx = jnp.bitwise_and(lax.broadcasted_iota(jnp.int32, (nc * L, L), 0), L - 1)
    s_idx = lax.broadcasted_iota(jnp.int32, (nc * L, L), 1)
    causal = s_idx <= t_idx
    ig_c = stack(lambda u, h: tok[u][4][:, h:h + 1])
    lf_c = stack(lambda u, h: tok[u][4][:, n_heads + h:n_heads + h + 1])
    ig_r = stack(lambda u, h: jnp.broadcast_to(tok[u][5][h:h + 1, :], (L, L)))
    lf_r = stack(lambda u, h: jnp.broadcast_to(tok[u][5][n_heads + h:n_heads + h + 1, :], (L, L)))
    m_prev = stack(lambda u, h: jnp.broadcast_to(m_ref[u, h:h + 1, 0:1], (L, 1)))
    b_c = jnp.sum(jnp.where(causal, lf_r, 0.0), axis=1, keepdims=True)
    b_r = jnp.sum(jnp.where(causal, 0.0, lf_c).reshape(nc, L, L), axis=1, keepdims=True)
    b_r = jnp.broadcast_to(b_r, (nc, L, L)).reshape(nc * L, L) + lf_r
    b_last = per_chain_last(b_c)
    dmat = jnp.where(causal, b_c - b_r + ig_r, -jnp.inf)
    g_c = b_c + m_prev
    m_t = jnp.maximum(g_c, jnp.max(dmat, axis=1, keepdims=True))
    w_intra = jnp.exp(dmat - m_t)
    w_inter = jnp.exp(g_c - m_t)
    m_new = per_chain_last(m_t)
    wa = jnp.exp(b_last - b_c + ig_c - m_new)
    decay = jnp.exp(b_last + m_prev - m_new)

    qs = {c: tok[c[0]][0][:, c[1] * dk:(c[1] + 1) * dk] for c in chains}
    ks = {c: tok[c[0]][1][:, c[1] * dk:(c[1] + 1) * dk] for c in chains}
    vs = {c: tok[c[0]][2][:, c[1] * dv:(c[1] + 1) * dv] for c in chains}
    c_prev = {c: c_ref[c[0], c[1]] for c in chains}
    n_prev = {c: n_ref[c[0], c[1]:c[1] + 1, :] for c in chains}
    s = stack(lambda u, h: _dot_nt(qs[u, h], ks[u, h])) * w_intra
    qc = stack(lambda u, h: _dot_nt(qs[u, h], c_prev[u, h].astype(BF)))
    s_bf = s.astype(BF)
    sv = jnp.concatenate([_dot(s_bf[i * L:(i + 1) * L, :], vs[c]) for i, c in enumerate(chains)], axis=0)
    qn = jnp.sum(stack(lambda u, h: qs[u, h].astype(F32) * n_prev[u, h]), axis=1, keepdims=True)
    num = sv + w_inter * qc
    nq = jnp.sum(s, axis=1, keepdims=True) + w_inter * qn
    x = num / jnp.maximum(jnp.abs(nq), jnp.exp(-m_t))
    x = x * stack(lambda u, h: tok[u][3][:, h * dv:(h + 1) * dv])
    x = x * lax.rsqrt(jnp.mean(x * x, axis=-1, keepdims=True) + EPS)
    wk = wa * stack(lambda u, h: ks[u, h].astype(F32))
    wk_bf = wk.astype(BF)
    for i, (u, h) in enumerate(chains):
        rows = slice(i * L, (i + 1) * L)
        hm_ref[u, :, h * dv:(h + 1) * dv] = (x[rows, :] * nrm_ref[:, h * dv:(h + 1) * dv]).astype(hm_ref.dtype)
        dec = decay[i * L:i * L + 1, :]
        c_ref[u, h] = dec * c_prev[u, h] + _dot_tn(vs[u, h], wk_bf[rows, :])
        n_ref[u, h:h + 1, :] = dec * n_prev[u, h] + jnp.sum(wk[rows, :], axis=0, keepdims=True)
        m_ref[u, h:h + 1, :] = jnp.broadcast_to(m_new[i * L:i * L + 1, :], (1, m_ref.shape[2]))


def _mlstm_call(q, k, v, og, gates_c, gates_r, c0, n0, m0, norm, layer, row0, n_seq, n_chunks, n_heads,
                state_layer, n_layers, c_stack=None):
    hdk = q.shape[1]
    hdv = v.shape[1]
    dv, dk = c0.shape[-2:]
    g2 = gates_r.shape[1]
    bt = _pick_tile(n_seq, 4)
    aliased = [] if c_stack is None else [c_stack]

    def chunked(a):
        return a.reshape(a.shape[0] // CHUNK, CHUNK, a.shape[1])

    arrays = [chunked(q), chunked(k), chunked(v), chunked(og), chunked(gates_c), gates_r]
    shapes = [(CHUNK, hdk), (CHUNK, hdk), (CHUNK, hdv), (CHUNK, hdv), (CHUNK, LANES), (g2, CHUNK)]

    def tok(u, shape):
        return pl.BlockSpec((None,) + shape, lambda t, c: (row0 + (t * bt + u) * n_chunks + c, 0, 0))

    def state(shape):
        return pl.BlockSpec((bt,) + shape, lambda t, c: (t,) + (0,) * len(shape))

    def stacked(shape, which):
        return pl.BlockSpec((None, bt) + shape, lambda t, c: (which, t) + (0,) * len(shape))

    in_specs = [tok(u, shape) for u in range(bt) for shape in shapes]
    in_specs += [stacked((n_heads, dv, dk), state_layer), stacked((n_heads, dk), state_layer),
                 stacked((n_heads, LANES), state_layer), pl.BlockSpec((None, 1, hdv), lambda t, c: (layer, 0, 0))]
    in_specs += [pl.BlockSpec(memory_space=pl.ANY)] * len(aliased)
    kern = functools.partial(_mlstm_kernel, n_heads=n_heads, bt=bt)
    return pl.pallas_call(
        kern,
        grid=(n_seq // bt, n_chunks),
        in_specs=in_specs,
        out_specs=[
            pl.BlockSpec((None, bt, CHUNK, hdv), lambda t, c: (c, t, 0, 0)),
            stacked((n_heads, dv, dk), layer), state((n_heads, dk)), state((n_heads, LANES)),
        ],
        out_shape=[
            jax.ShapeDtypeStruct((n_chunks, n_seq, CHUNK, hdv), BF),
            jax.ShapeDtypeStruct((n_layers, n_seq, n_heads, dv, dk), F32),
            jax.ShapeDtypeStruct((n_seq, n_heads, dk), F32),
            jax.ShapeDtypeStruct((n_seq, n_heads, LANES), F32),
        ],
        input_output_aliases={6 * bt + 4: 1} if aliased else {},
        compiler_params=_cparams("parallel", "arbitrary"),
        name="mlstm",
    )(*(arrays * bt), c0, n0, m0, norm, *aliased)


def _build_queries(q_sc, cq_ref, wuq_ref, wuk_ref, cos_ref, sin_ref, n_heads, nope, rope, kvl):
    tq = cq_ref.shape[0]
    qa = _dot(cq_ref[...], wuq_ref[...])
    cos = cos_ref[...]
    sin = sin_ref[...]
    for h in range(n_heads):
        qn = qa[:, h * nope:(h + 1) * nope].astype(BF)
        q_sc[h * tq:(h + 1) * tq, 0:kvl] = _dot(qn, wuk_ref[h]).astype(BF)
    per = LANES // rope
    for p in range(n_heads // per):
        base = n_heads * nope + p * LANES
        pe = _rope128(qa[:, base:base + LANES], cos, sin)
        for u in range(per):
            h = p * per + u
            q_sc[h * tq:(h + 1) * tq, kvl:kvl + rope] = pe[:, u * rope:(u + 1) * rope].astype(BF)


def _lane_tile(x, width):
    if width <= LANES:
        return x[:, :width]
    return jnp.concatenate([x] * (width // LANES), axis=1)


def _flash_step(q_sc, keys_next, s_next, s_cur, vals_cur, m_sc, l_sc, acc_sc, sm_scale, rs, mask_chunks=None):
    n_groups = q_sc.shape[0] // rs

    def group(r, carry):
        r0 = pl.multiple_of(r * rs, rs)
        rows = pl.ds(r0, rs)
        if s_next is not None:
            s_next[rows, :] = _dot_nt(q_sc[rows, :], keys_next)
        if s_cur is not None:
            kb, kvl = vals_cur.shape
            s = s_cur[rows, :] * sm_scale
            if mask_chunks is not None:
                q_chunks, q_chunk0, k_chunk0 = mask_chunks
                chunk_bits = CHUNK.bit_length() - 1
                row = lax.broadcasted_iota(jnp.int32, (rs, kb), 0) + r0
                q_chunk = jnp.bitwise_and(jnp.right_shift(row, chunk_bits), q_chunks - 1) + q_chunk0
                k_chunk = jnp.right_shift(lax.broadcasted_iota(jnp.int32, (rs, kb), 1), chunk_bits) + k_chunk0
                s = jnp.where(k_chunk <= q_chunk, s, -jnp.inf)
            m_prev = m_sc[rows, :]
            m_new = jnp.maximum(m_prev, jnp.max(s, axis=1, keepdims=True))
            alpha = jnp.exp(m_prev - m_new)
            p = jnp.exp(s - _lane_tile(m_new, kb))
            l_sc[rows, :] = alpha * l_sc[rows, :] + jnp.sum(p, axis=1, keepdims=True)
            acc_sc[rows, :] = _lane_tile(alpha, kvl) * acc_sc[rows, :] + _dot(p.astype(BF), vals_cur)
            m_sc[rows, :] = m_new
        return carry

    lax.fori_loop(0, n_groups, group, 0, unroll=2 if n_groups % 2 == 0 else 1)


def _flash_init(m_sc, l_sc, acc_sc):
    m_sc[...] = jnp.full(m_sc.shape, -jnp.inf, F32)
    l_sc[...] = jnp.zeros(l_sc.shape, F32)
    acc_sc[...] = jnp.zeros(acc_sc.shape, F32)


def _flash_finish(o_ref, l_sc, acc_sc, wuv_ref, n_heads, tq):
    vd = wuv_ref.shape[2]
    kvl = acc_sc.shape[1]
    for h in range(n_heads):
        rows = pl.ds(h * tq, tq)
        o_lat = (acc_sc[rows, :] / _lane_tile(l_sc[rows, :], kvl)).astype(BF)
        o_ref[:, h * vd:(h + 1) * vd] = _dot(o_lat, wuv_ref[h]).astype(o_ref.dtype)


def _attn_prompt_kernel(*refs, n_heads, nope, rope, sm_scale, rs, n_riders):
    cq_ref, ckv_ref, kpe_ref, wuq_ref, wuk_ref, wuv_ref, cos_ref, sin_ref = refs[:8]
    rider_in = refs[8:8 + n_riders]
    o_ref = refs[8 + n_riders]
    rider_out = refs[9 + n_riders:9 + 2 * n_riders]
    kcat, q_sc, m_sc, l_sc, acc_sc, s_a, s_b = refs[9 + 2 * n_riders:]
    i = pl.program_id(1)
    tq = cq_ref.shape[0]
    kvl = ckv_ref.shape[1]
    kb = s_a.shape[1]

    for src, dst in zip(rider_in, rider_out):
        dst[...] = src[...].astype(dst.dtype)

    @pl.when(i == 0)
    def _():
        kcat[:, 0:kvl] = ckv_ref[...].astype(BF)
        kcat[:, kvl:kvl + rope] = kpe_ref[...].astype(BF)

    _build_queries(q_sc, cq_ref, wuq_ref, wuk_ref, cos_ref, sin_ref, n_heads, nope, rope, kvl)
    _flash_init(m_sc, l_sc, acc_sc)

    def step(j_next, s_next, j_cur, s_cur, masked=False):
        keys = None if s_next is None else kcat[pl.ds(pl.multiple_of(j_next * kb, kb), kb), :]
        vals = None if s_cur is None else kcat[pl.ds(pl.multiple_of(j_cur * kb, kb), kb), 0:kvl]
        mask = (tq // CHUNK, i * (tq // CHUNK), j_cur * (kb // CHUNK)) if masked else None
        _flash_step(q_sc, keys, s_next, s_cur, vals, m_sc, l_sc, acc_sc, sm_scale, rs, mask)

    last = (i * tq) // kb
    step(0, s_a, None, None)

    def pair(t, carry):
        j = 2 * t
        step(j + 1, s_b, j, s_a)
        step(j + 2, s_a, j + 1, s_b)
        return carry

    lax.fori_loop(0, last // 2, pair, 0)

    @pl.when(last % 2 == 0)
    def _():
        step(None, None, last, s_a, masked=True)

    @pl.when(last % 2 == 1)
    def _():
        step(last, s_b, last - 1, s_a)
        step(None, None, last, s_b, masked=True)

    _flash_finish(o_ref, l_sc, acc_sc, wuv_ref, n_heads, tq)


def _rider_block(rows, cols, n_steps):
    for n_row_blocks in range(n_steps, 0, -1):
        if n_steps % n_row_blocks or rows % n_row_blocks or cols % (n_steps // n_row_blocks):
            continue
        br, bc = rows // n_row_blocks, cols // (n_steps // n_row_blocks)
        if br % 16 == 0 and bc % LANES == 0:
            return br, bc
    raise ValueError(f"no aligned {n_steps}-block tiling of ({rows}, {cols})")


def _attn_prompt_call(cq, ckv, kpe, wuq, wuk, wuv, cos, sin, layer, n_seq, seq, tq, dims, riders=()):
    ah, nope, rope, kvl, vd = dims["ah"], dims["nope"], dims["rope"], dims["kvl"], dims["vd"]
    ql = cq.shape[1]
    nq = seq // tq
    kb = _pick_tile(seq, 512)
    assert tq % CHUNK == 0 and (tq // CHUNK) & (tq // CHUNK - 1) == 0 and kb % tq == 0
    rs = _pick_tile(ah * tq, 512)
    kern = functools.partial(_attn_prompt_kernel, n_heads=ah, nope=nope, rope=rope,
                             sm_scale=(nope + rope) ** -0.5, rs=rs, n_riders=len(riders))

    def wspec(w):
        return pl.BlockSpec((None,) + w.shape[1:], lambda b, i: (layer,) + (0,) * (w.ndim - 1))

    rider_in, rider_out, rider_shapes = [], [], []
    for arr, arr_layer in riders:
        _, rows, cols = arr.shape
        br, bc = _rider_block(rows, cols, n_seq * nq)
        ncb = cols // bc
        rider_in.append(pl.BlockSpec((None, br, bc), functools.partial(
            lambda b, i, ncb, arr_layer: (arr_layer, (b * nq + i) // ncb, (b * nq + i) % ncb), ncb=ncb, arr_layer=arr_layer)))
        rider_out.append(pl.BlockSpec((br, bc), functools.partial(
            lambda b, i, ncb: ((b * nq + i) // ncb, (b * nq + i) % ncb), ncb=ncb)))
        rider_shapes.append(jax.ShapeDtypeStruct((rows, cols), BF))

    return pl.pallas_call(
        kern,
        grid=(n_seq, nq),
        in_specs=[
            pl.BlockSpec((tq, ql), lambda b, i: (b * nq + i, 0)),
            pl.BlockSpec((seq, kvl), lambda b, i: (b, 0)),
            pl.BlockSpec((seq, rope), lambda b, i: (b, 0)),
            wspec(wuq), wspec(wuk), wspec(wuv),
            pl.BlockSpec((tq, LANES), lambda b, i: (b * nq + i, 0)),
            pl.BlockSpec((tq, LANES), lambda b, i: (b * nq + i, 0)),
        ] + rider_in,
        out_specs=[pl.BlockSpec((tq, ah * vd), lambda b, i: (b * nq + i, 0))] + rider_out,
        out_shape=[jax.ShapeDtypeStruct((n_seq * seq, ah * vd), BF)] + rider_shapes,
        scratch_shapes=[
            pltpu.VMEM((seq, kvl + rope), BF),
            pltpu.VMEM((ah * tq, kvl + rope), BF),
            pltpu.VMEM((ah * tq, LANES), F32),
            pltpu.VMEM((ah * tq, LANES), F32),
            pltpu.VMEM((ah * tq, kvl), F32),
            pltpu.VMEM((ah * tq, kb), F32),
            pltpu.VMEM((ah * tq, kb), F32),
        ],
        compiler_params=_cparams("parallel", "arbitrary"),
        name="mla_prompt",
    )(cq, ckv, kpe, wuq, wuk, wuv, cos, sin, *[arr for arr, _ in riders])


def _attn_sample_kernel(cq_ref, ckv_ref, kpe_ref, pckv_ref, pkpe_ref, wuq_ref, wuk_ref, wuv_ref, cos_ref, sin_ref,
                        o_ref, kcat, q_sc, m_sc, l_sc, acc_sc, s_a, s_b, s_n, *, n_heads, nope, rope, sm_scale):
    tq = cq_ref.shape[0]
    kvl = ckv_ref.shape[1]
    past = pckv_ref.shape[0]
    kb = s_a.shape[1]
    kcat[0:past, 0:kvl] = pckv_ref[...].astype(BF)
    kcat[0:past, kvl:kvl + rope] = pkpe_ref[...].astype(BF)
    kcat[past:past + tq, 0:kvl] = ckv_ref[...].astype(BF)
    kcat[past:past + tq, kvl:kvl + rope] = kpe_ref[...].astype(BF)
    _build_queries(q_sc, cq_ref, wuq_ref, wuk_ref, cos_ref, sin_ref, n_heads, nope, rope, kvl)
    _flash_init(m_sc, l_sc, acc_sc)
    rs = q_sc.shape[0]

    blocks = [(j * kb, kb, (s_a, s_b)[j % 2]) for j in range(past // kb)] + [(past, tq, s_n)]
    for cur, nxt in zip([None] + blocks, blocks + [None]):
        keys, s_next = (None, None) if nxt is None else (kcat[nxt[0]:nxt[0] + nxt[1], :], nxt[2])
        vals, s_cur = (None, None) if cur is None else (kcat[cur[0]:cur[0] + cur[1], 0:kvl], cur[2])
        _flash_step(q_sc, keys, s_next, s_cur, vals, m_sc, l_sc, acc_sc, sm_scale, rs)
    _flash_finish(o_ref, l_sc, acc_sc, wuv_ref, n_heads, tq)


def _attn_sample_call(cq, ckv, kpe, cache_ckv, cache_kpe, wuq, wuk, wuv, cos, sin, layer, row0, n_seq, tq, dims):
    ah, nope, rope, kvl, vd = dims["ah"], dims["nope"], dims["rope"], dims["kvl"], dims["vd"]
    ql = cq.shape[1]
    past = cache_ckv.shape[2]
    kb = math.gcd(past, 256)
    kern = functools.partial(_attn_sample_kernel, n_heads=ah, nope=nope, rope=rope,
                             sm_scale=(nope + rope) ** -0.5)

    def wspec(w):
        return pl.BlockSpec((None,) + w.shape[1:], lambda b: (layer,) + (0,) * (w.ndim - 1))

    def tok(n):
        return pl.BlockSpec((tq, n), lambda b: (row0 + b, 0))

    return pl.pallas_call(
        kern,
        grid=(n_seq,),
        in_specs=[
            tok(ql), tok(kvl), tok(rope),
            pl.BlockSpec((None, None, past, kvl), lambda b: (layer, b, 0, 0)),
            pl.BlockSpec((None, None, past, rope), lambda b: (layer, b, 0, 0)),
            wspec(wuq), wspec(wuk), wspec(wuv), tok(LANES), tok(LANES),
        ],
        out_specs=pl.BlockSpec((tq, ah * vd), lambda b: (b, 0)),
        out_shape=jax.ShapeDtypeStruct((n_seq * tq, ah * vd), BF),
        scratch_shapes=[
            pltpu.VMEM((past + tq, kvl + rope), BF),
            pltpu.VMEM((ah * tq, kvl + rope), BF),
            pltpu.VMEM((ah * tq, LANES), F32),
            pltpu.VMEM((ah * tq, LANES), F32),
            pltpu.VMEM((ah * tq, kvl), F32),
            pltpu.VMEM((ah * tq, kb), F32),
            pltpu.VMEM((ah * tq, kb), F32),
            pltpu.VMEM((ah * tq, tq), F32),
        ],
        compiler_params=_cparams("parallel"),
        name="mla_sample",
    )(cq, ckv, kpe, cache_ckv, cache_kpe, wuq, wuk, wuv, cos, sin)


def _outproj_kernel(x_ref, hmp_ref, hms_ref, oap_ref, oas_ref, gt_ref, wm_ref, wa_ref, o_ref, *, n_prompt_tiles):
    i = pl.program_id(0)
    tm = x_ref.shape[0]

    def project(hm, oa):
        y = _dot(hm, wm_ref[...]) + _dot(oa, wa_ref[...])
        for g in range(tm // CHUNK):
            rows = pl.ds(g * CHUNK, CHUNK)
            o_ref[rows, :] = x_ref[rows, :] + gt_ref[g:g + 1, :] * y[g * CHUNK:(g + 1) * CHUNK, :]

    @pl.when(i < n_prompt_tiles)
    def _():
        project(hmp_ref[...].reshape(tm, hmp_ref.shape[-1]), oap_ref[...])

    @pl.when(i >= n_prompt_tiles)
    def _():
        project(hms_ref[...], oas_ref[...])


def _outproj_call(x, hm_p, hm_s, oa_p, oa_s, mod, w_out, layer, tm, seq):
    m, d = x.shape
    wm = hm_s.shape[1]
    wa = oa_s.shape[1]
    assert wm == wa and seq % tm == 0
    ng = tm // CHUNK
    npt = oa_p.shape[0] // tm
    tps = seq // tm

    def prompt_tile(i):
        return jnp.minimum(i, npt - 1)

    def sample_tile(i):
        return jnp.maximum(i - npt, 0)

    return pl.pallas_call(
        functools.partial(_outproj_kernel, n_prompt_tiles=npt),
        grid=(m // tm,),
        in_specs=[
            pl.BlockSpec((tm, d), lambda i: (i, 0)),
            pl.BlockSpec((ng, None, CHUNK, wm), lambda i: (prompt_tile(i) % tps, prompt_tile(i) // tps, 0, 0)),
            pl.BlockSpec((tm, wm), lambda i: (sample_tile(i), 0)),
            pl.BlockSpec((tm, wa), lambda i: (prompt_tile(i), 0)),
            pl.BlockSpec((tm, wa), lambda i: (sample_tile(i), 0)),
            pl.BlockSpec((None, ng, d), lambda i: (layer, i, 5)),
            pl.BlockSpec((None, wm, d), lambda i: (layer, 0, 0)),
            pl.BlockSpec((None, wa, d), lambda i: (layer, 1, 0)),
        ],
        out_specs=pl.BlockSpec((tm, d), lambda i: (i, 0)),
        out_shape=jax.ShapeDtypeStruct((m, d), F32),
        compiler_params=_cparams("parallel"),
        name="out_proj",
    )(x, hm_p, hm_s, oa_p, oa_s, mod, w_out, w_out)


def _rope_tables(pos, rope):
    half = rope // 2
    freqs = ROPE_THETA ** (-jnp.arange(half, dtype=F32) / half)
    ang = pos.astype(F32)[:, None] * freqs[None, :]
    cos = jnp.cos(ang)
    sin = jnp.sin(ang)
    reps = LANES // rope
    return jnp.tile(jnp.concatenate([cos, cos], axis=1), (1, reps)), jnp.tile(jnp.concatenate([-sin, sin], axis=1), (1, reps))


def _pick_tile(m, cap):
    t = cap
    while m % t:
        t //= 2
    return t


def kernel(x_prompt, x_sample, c_prompt, c_sample, cache_ckv, cache_kpe, state_C, state_n, state_m, mod_w, mod_b, ln_ffn1, ffn1_w_in, ffn1_w_out, ln_mix, w_in, mlstm_b_i, mlstm_b_f, mlstm_norm, q_norm, w_uq, kv_norm, w_uk, w_uv, w_out, ln_ffn2, ffn2_w_in, ffn2_w_out, final_norm):
    bp, sp, d = x_prompt.shape
    bs, ss, _ = x_sample.shape
    depth = mod_w.shape[0]
    past = cache_ckv.shape[2]
    mh, dv, dk = state_C.shape[2:]
    kvl, ah, nope = w_uk.shape[1:]
    vd = w_uv.shape[3]
    rope = cache_kpe.shape[3]
    ql = q_norm.shape[1]
    hdk, hdv = mh * dk, mh * dv
    dff = ffn1_w_out.shape[1]
    dims = dict(mh=mh, dk=dk, dv=dv, hdk=hdk, hdv=hdv, ql=ql, kvl=kvl, rope=rope, ah=ah, nope=nope, vd=vd)
    assert sp % CHUNK == 0 and ss == CHUNK and LANES % rope == 0 and ah % (LANES // rope) == 0
    mp, ms = bp * sp, bs * ss
    m = mp + ms
    tm = _pick_tile(math.gcd(mp, ms), 512)
    tm_ffn = _pick_tile(math.gcd(mp, ms), 1024)
    ffn_tiles_p, ffn_tiles_s = mp // tm_ffn, ms // tm_ffn
    tf = _pick_tile(dff, 512)
    tq = _pick_tile(sp, 256)

    cg = jnp.concatenate([jnp.repeat(c_prompt, sp // CHUNK, axis=0), jnp.repeat(c_sample, ss // CHUNK, axis=0)], axis=0)
    pos = jnp.concatenate([jnp.tile(jnp.arange(sp), bp), jnp.tile(past + jnp.arange(ss), bs)])
    cos, sin = _rope_tables(pos, rope)

    o_mq, o_mk, o_mv, o_mo = 0, hdk, 2 * hdk, 2 * hdk + hdv
    o_mi = o_mo + hdv
    o_mf, o_qa = o_mi + mh, o_mi + 2 * mh
    o_kva, o_pe = o_qa + ql, o_qa + ql + kvl
    wqk = w_in[:, :, o_mq:o_mv].astype(BF)
    wv = w_in[:, :, o_mv:o_mo].astype(BF)
    wo = w_in[:, :, o_mo:o_mi].astype(BF)
    zpad = lambda n: jnp.zeros((depth, d, n), w_in.dtype)
    wa = jnp.concatenate([w_in[:, :, o_qa:o_pe + rope], zpad(LANES - rope), w_in[:, :, o_mi:o_qa], zpad(LANES - 2 * mh)], axis=2).astype(BF)
    gate_bias = jnp.concatenate([mlstm_b_i, mlstm_b_f, jnp.zeros((depth, LANES - 2 * mh), F32)], axis=1).reshape(depth, 1, LANES)
    wuq4 = w_uq.reshape(depth, ql, ah, nope + rope)
    wuq = jnp.concatenate([wuq4[..., :nope].reshape(depth, ql, ah * nope), wuq4[..., nope:].reshape(depth, ql, ah * rope)], axis=2).astype(BF)
    wuk = jnp.transpose(w_uk, (0, 2, 3, 1)).astype(BF)
    wuv = jnp.transpose(w_uv, (0, 2, 1, 3)).astype(BF)
    w_out_b = w_out.astype(BF)
    f1_in, f1_out = ffn1_w_in[0].astype(BF), ffn1_w_out[0].astype(BF)
    r3 = lambda a: a.reshape(depth, 1, a.shape[1])

    mod = _mod_call(cg, mod_w, mod_b)

    zeros_c = jnp.zeros((1, bp, mh, dv, dk), F32)
    zeros_n = jnp.zeros((1, bp, mh, dk), F32)
    zeros_m = jnp.zeros((1, bp, mh, LANES), F32)
    m0_s = jnp.broadcast_to(state_m[..., None], state_m.shape + (LANES,))

    outs = {k: [] for k in ("p_kpe", "p_n", "p_m", "s_kpe", "s_n", "s_m")}
    ckv_stacks = p_c = s_c = None
    for l in range(depth):
        if l == 0:
            x = _ffn_call(x_prompt.reshape(mp, d), mod, r3(ln_ffn1), f1_in, f1_out, l, 0, tm_ffn, tf, out_rows=m)
            x = _ffn_call(x_sample.reshape(ms, d), mod, r3(ln_ffn1), f1_in, f1_out, l, 0, tm_ffn, tf,
                          stream_tile0=ffn_tiles_p, out_tile0=ffn_tiles_p, into=x)
        else:
            x = _ffn_call(x, mod, r3(ln_ffn1), f1_in, f1_out, l, 0, tm_ffn, tf)
        q, k, v, og, cq, ckv, kpe, gates, *ckv_stacks = _inproj_call(
            x, mod, r3(ln_mix), wqk, wv, wo, wa, r3(q_norm), r3(kv_norm), gate_bias, cos, sin, l, tm, dims,
            depth, mp, ckv_stacks)
        gates_r = jnp.transpose(gates[:, :2 * mh].reshape(m // CHUNK, CHUNK, 2 * mh), (0, 2, 1))
        nrm = r3(mlstm_norm)
        hm_p, p_c, n_p, m_p = _mlstm_call(q, k, v, og, gates, gates_r, zeros_c, zeros_n, zeros_m, nrm,
                                          l, 0, bp, sp // CHUNK, mh, 0, depth, p_c)
        hm_s, s_c, n_s, m_s = _mlstm_call(q, k, v, og, gates, gates_r, state_C, state_n, m0_s, nrm,
                                          l, mp // CHUNK, bs, ss // CHUNK, mh, l, depth, s_c)
        riders = [(ffn2_w_in, l), (ffn2_w_out, l)]
        if l + 1 < depth:
            riders += [(ffn1_w_in, l + 1), (ffn1_w_out, l + 1)]
        oa_p, f2_in, f2_out, *next_f1 = _attn_prompt_call(cq, ckv, kpe, wuq, wuk, wuv, cos, sin, l, bp, sp, tq, dims, riders)
        if next_f1:
            f1_in, f1_out = next_f1
        oa_s = _attn_sample_call(cq, ckv, kpe, cache_ckv, cache_kpe, wuq, wuk, wuv, cos, sin, l, mp // ss, bs, ss, dims)
        x = _outproj_call(x, hm_p, hm_s.reshape(ms, hdv), oa_p, oa_s, mod, w_out_b, l, tm, sp)
        if l + 1 < depth:
            x = _ffn_call(x, mod, r3(ln_ffn2), f2_in, f2_out, l, 6, tm_ffn, tf)
        else:
            fn = final_norm.reshape(1, d)
            y_prompt = _ffn_call(x, mod, r3(ln_ffn2), f2_in, f2_out, l, 6, tm_ffn, tf,
                                 n_tiles=ffn_tiles_p, final_gain=fn).reshape(bp, sp, d)
            y_sample = _ffn_call(x, mod, r3(ln_ffn2), f2_in, f2_out, l, 6, tm_ffn, tf, src_tile0=ffn_tiles_p,
                                 n_tiles=ffn_tiles_s, stream_tile0=ffn_tiles_p, final_gain=fn).reshape(bs, ss, d)
        outs["p_kpe"].append(kpe[:mp].reshape(bp, sp, rope))
        outs["s_kpe"].append(kpe[mp:].reshape(bs, ss, rope))
        outs["p_n"].append(n_p)
        outs["p_m"].append(m_p[..., 0])
        outs["s_n"].append(n_s)
        outs["s_m"].append(m_s[..., 0])

    st = {k: jnp.stack(v) for k, v in outs.items()}
    p_ckv = ckv_stacks[0].reshape(depth, bp, sp, kvl)
    s_ckv = ckv_stacks[1].reshape(depth, bs, ss, kvl)
    return (y_prompt, y_sample, p_ckv, st["p_kpe"], p_c, st["p_n"], st["p_m"],
            s_ckv, st["s_kpe"], s_c, st["s_n"], st["s_m"])
```

```python
import functools
import math

import jax
import jax.numpy as jnp
from jax import lax
from jax.experimental import pallas as pl
from jax.experimental.pallas import tpu as pltpu

CHUNK = 64
EPS = 1e-6
ROPE_THETA = 10000.0
N_MOD = 9
LANES = 128
VMEM_LIMIT = 58 * 1024 * 1024

BF = jnp.bfloat16
F32 = jnp.float32


def _cparams(*sem):
    return pltpu.CompilerParams(dimension_semantics=sem, vmem_limit_bytes=VMEM_LIMIT)


def _dot(a, b):
    return jnp.dot(a, b, preferred_element_type=F32)


def _dot_nt(a, b):
    return lax.dot_general(a, b, (((1,), (1,)), ((), ())), preferred_element_type=F32)


def _dot_tn(a, b):
    return lax.dot_general(a, b, (((0,), (0,)), ((), ())), preferred_element_type=F32)


def _rms(x, g):
    ms = jnp.mean(x * x, axis=-1, keepdims=True)
    return x * lax.rsqrt(ms + EPS) * g


def _silu(x):
    return x * jax.nn.sigmoid(x)


def _rope128(pe, cos, sin):
    lane = lax.broadcasted_iota(jnp.int32, pe.shape, 1)
    first_half = jnp.bitwise_and(lane, 63) < 32
    swapped = jnp.where(first_half, pltpu.roll(pe, 96, 1), pltpu.roll(pe, 32, 1))
    return pe * cos + swapped * sin


def _mod_kernel(c_ref, w_ref, b_ref, o_ref):
    a = _silu(c_ref[...]).astype(BF)
    o_ref[...] = _dot(a, w_ref[...].astype(BF)) + b_ref[...]


def _mod_call(cg, mod_w, mod_b):
    depth, d, nd = mod_w.shape
    g = cg.shape[0]
    tn = 1024
    return pl.pallas_call(
        _mod_kernel,
        grid=(depth, nd // tn),
        in_specs=[
            pl.BlockSpec((g, d), lambda l, j: (0, 0)),
            pl.BlockSpec((None, d, tn), lambda l, j: (l, 0, j)),
            pl.BlockSpec((None, 1, tn), lambda l, j: (l, 0, j)),
        ],
        out_specs=pl.BlockSpec((None, g, tn), lambda l, j: (l, 0, j)),
        out_shape=jax.ShapeDtypeStruct((depth, g, nd), F32),
        compiler_params=_cparams("parallel", "parallel"),
        name="adaln_mod",
    )(cg, mod_w, mod_b.reshape(depth, 1, nd))


def _norm_mod_to(h_ref, x_ref, ln_ref, sh_ref, sc_ref):
    ln = ln_ref[...]
    for g in range(x_ref.shape[0] // CHUNK):
        rows = pl.ds(g * CHUNK, CHUNK)
        y = _rms(x_ref[rows, :], ln)
        h_ref[rows, :] = (y * (1.0 + sc_ref[g:g + 1, :]) + sh_ref[g:g + 1, :]).astype(h_ref.dtype)


def _ffn_kernel(*refs, final_norm, aliased):
    x_ref, sh_ref, sc_ref, gt_ref, ln_ref, wg_ref, wu_ref, wo_ref = refs[:8]
    fg_ref = refs[8] if final_norm else None
    o_ref, h_sc = refs[8 + final_norm + aliased:]
    j = pl.program_id(1)
    nj = pl.num_programs(1)

    def up_down(h):
        a = (_silu(_dot(h, wg_ref[...])) * _dot(h, wu_ref[...])).astype(BF)
        return _dot(a, wo_ref[...])

    def residual(acc):
        for g in range(x_ref.shape[0] // CHUNK):
            rows = pl.ds(g * CHUNK, CHUNK)
            y = x_ref[rows, :] + (0.5 * gt_ref[g:g + 1, :]) * acc[g * CHUNK:(g + 1) * CHUNK, :]
            o_ref[rows, :] = _rms(y, fg_ref[...]) if final_norm else y

    @pl.when(j == 0)
    def _():
        _norm_mod_to(h_sc, x_ref, ln_ref, sh_ref, sc_ref)
        o_ref[...] = up_down(h_sc[...])

    @pl.when(jnp.logical_and(j > 0, j < nj - 1))
    def _():
        o_ref[...] += up_down(h_sc[...])

    @pl.when(jnp.logical_and(j > 0, j == nj - 1))
    def _():
        residual(o_ref[...] + up_down(h_sc[...]))

    @pl.when(nj == 1)
    def _():
        residual(o_ref[...])


def _ffn_call(x, mod, ln, w_in, w_out, layer, k0, tm, tf, *, src_tile0=0, n_tiles=None, stream_tile0=0,
              out_rows=None, out_tile0=0, into=None, final_gain=None):
    d = x.shape[1]
    f = w_out.shape[0]
    ng = tm // CHUNK
    nf = f // tf
    n_tiles = x.shape[0] // tm if n_tiles is None else n_tiles
    out_rows = (into.shape[0] if into is not None else n_tiles * tm) if out_rows is None else out_rows

    def mod_spec(k):
        return pl.BlockSpec((None, ng, d), lambda i, j: (layer, i + stream_tile0, k))

    in_specs = [
        pl.BlockSpec((tm, d), lambda i, j: (i + src_tile0, 0)),
        mod_spec(k0), mod_spec(k0 + 1), mod_spec(k0 + 2),
        pl.BlockSpec((None, 1, d), lambda i, j: (layer, 0, 0)),
        pl.BlockSpec((d, tf), lambda i, j: (0, j)),
        pl.BlockSpec((d, tf), lambda i, j: (0, j + nf)),
        pl.BlockSpec((tf, d), lambda i, j: (j, 0)),
    ]
    operands = [x, mod, mod, mod, ln, w_in, w_in, w_out]
    if final_gain is not None:
        in_specs.append(pl.BlockSpec((1, d), lambda i, j: (0, 0)))
        operands.append(final_gain)
    aliases = {}
    if into is not None:
        aliases = {len(operands): 0}
        in_specs.append(pl.BlockSpec(memory_space=pl.ANY))
        operands.append(into)
    kern = functools.partial(_ffn_kernel, final_norm=final_gain is not None, aliased=into is not None)
    return pl.pallas_call(
        kern,
        grid=(n_tiles, nf),
        in_specs=in_specs,
        out_specs=pl.BlockSpec((tm, d), lambda i, j: (i + out_tile0, 0)),
        out_shape=jax.ShapeDtypeStruct((out_rows, d), F32),
        scratch_shapes=[pltpu.VMEM((tm, d), BF)],
        input_output_aliases=aliases,
        compiler_params=_cparams("parallel", "arbitrary"),
        name="ffn",
    )(*operands)


def _inproj_kernel(*refs, n_heads, k_scale, n_aliased, n_prompt_tiles):
    (x_ref, sh_ref, sc_ref, ln_ref, wqk_ref, wv_ref, wo_ref, wa_ref, qn_ref, kvn_ref, gb_ref,
     cos_ref, sin_ref) = refs[:13]
    (q_ref, k_ref, v_ref, og_ref, cq_ref, ckv_ref, kpe_ref, gt_ref, ckvp_ref, ckvs_ref,
     h_sc) = refs[13 + n_aliased:]
    _norm_mod_to(h_sc, x_ref, ln_ref, sh_ref, sc_ref)
    h = h_sc[...]
    hdk = q_ref.shape[1]
    zqk = _dot(h, wqk_ref[...])
    q_ref[...] = zqk[:, :hdk].astype(BF)
    k_ref[...] = (zqk[:, hdk:] * k_scale).astype(BF)
    v_ref[...] = _dot(h, wv_ref[...]).astype(BF)
    og_ref[...] = jax.nn.sigmoid(_dot(h, wo_ref[...]))
    za = _dot(h, wa_ref[...])
    ql = cq_ref.shape[1]
    kvl = ckv_ref.shape[1]
    cq_ref[...] = _rms(za[:, :ql], qn_ref[...]).astype(BF)
    ckv = _rms(za[:, ql:ql + kvl], kvn_ref[...])
    ckv_ref[...] = ckv
    pe = _rope128(za[:, ql + kvl:ql + kvl + LANES], cos_ref[...], sin_ref[...])
    kpe_ref[...] = pe[:, :kpe_ref.shape[1]]
    zg = za[:, ql + kvl + LANES:] + gb_ref[...]
    lane = lax.broadcasted_iota(jnp.int32, zg.shape, 1)
    log_sig = jnp.minimum(zg, 0.0) - jnp.log1p(jnp.exp(-jnp.abs(zg)))
    gt_ref[...] = jnp.where(lane < n_heads, zg, log_sig)

    i = pl.program_id(0)

    @pl.when(i < n_prompt_tiles)
    def _():
        ckvp_ref[...] = ckv_ref[...]

    @pl.when(i >= n_prompt_tiles)
    def _():
        ckvs_ref[...] = ckv_ref[...]


def _inproj_call(x, mod, ln, wqk, wv, wo, wa, q_norm, kv_norm, gate_bias, cos, sin, layer, tm, dims,
                 n_layers, rows_prompt, ckv_stacks=None):
    m, d = x.shape
    ng = tm // CHUNK
    hdk, hdv, ql, kvl, rope = dims["hdk"], dims["hdv"], dims["ql"], dims["kvl"], dims["rope"]
    npt = rows_prompt // tm
    aliased = [] if ckv_stacks is None else list(ckv_stacks)

    def mod_spec(k):
        return pl.BlockSpec((None, ng, d), lambda i: (layer, i, k))

    def wspec(w):
        return pl.BlockSpec((None,) + w.shape[1:], lambda i: (layer, 0, 0))

    def row(n):
        return pl.BlockSpec((tm, n), lambda i: (i, 0))

    kern = functools.partial(_inproj_kernel, n_heads=dims["mh"], k_scale=dims["dk"] ** -0.5,
                             n_aliased=len(aliased), n_prompt_tiles=npt)
    return pl.pallas_call(
        kern,
        grid=(m // tm,),
        in_specs=[
            row(d), mod_spec(3), mod_spec(4), wspec(ln), wspec(wqk), wspec(wv), wspec(wo), wspec(wa),
            wspec(q_norm), wspec(kv_norm), wspec(gate_bias), row(LANES), row(LANES),
        ] + [pl.BlockSpec(memory_space=pl.ANY)] * len(aliased),
        out_specs=[
            row(hdk), row(hdk), row(hdv), row(hdv), row(ql), row(kvl), row(rope), row(LANES),
            pl.BlockSpec((None, tm, kvl), lambda i: (layer, jnp.minimum(i, npt - 1), 0)),
            pl.BlockSpec((None, tm, kvl), lambda i: (layer, jnp.maximum(i - npt, 0), 0)),
        ],
        out_shape=[
            jax.ShapeDtypeStruct((m, hdk), BF), jax.ShapeDtypeStruct((m, hdk), BF),
            jax.ShapeDtypeStruct((m, hdv), BF), jax.ShapeDtypeStruct((m, hdv), F32),
            jax.ShapeDtypeStruct((m, ql), BF), jax.ShapeDtypeStruct((m, kvl), F32),
            jax.ShapeDtypeStruct((m, rope), F32), jax.ShapeDtypeStruct((m, LANES), F32),
            jax.ShapeDtypeStruct((n_layers, rows_prompt, kvl), F32),
            jax.ShapeDtypeStruct((n_layers, m - rows_prompt, kvl), F32),
        ],
        scratch_shapes=[pltpu.VMEM((tm, d), BF)],
        input_output_aliases={13 + a: 8 + a for a in range(len(aliased))},
        compiler_params=_cparams("arbitrary"),
        name="in_proj",
    )(x, mod, mod, ln, wqk, wv, wo, wa, q_norm, kv_norm, gate_bias, cos, sin, *aliased)


def _mlstm_kernel(*refs, n_heads, bt):
    tok = [refs[6 * u:6 * u + 6] for u in range(bt)]
    c0_ref, n0_ref, m0_ref, nrm_ref = refs[6 * bt:6 * bt + 4]
    hm_ref, c_ref, n_ref, m_ref = refs[-4:]
    c_idx = pl.program_id(1)

    @pl.when(c_idx == 0)
    def _():
        c_ref[...] = c0_ref[...]
        n_ref[...] = n0_ref[...]
        m_ref[...] = m0_ref[...]

    L = tok[0][0].shape[0]
    dk = tok[0][0].shape[1] // n_heads
    dv = tok[0][2].shape[1] // n_heads
    chains = [(u, h) for u in range(bt) for h in range(n_heads)]
    nc = len(chains)

    def stack(fn):
        return jnp.concatenate([fn(u, h) for u, h in chains], axis=0)

    def per_chain_last(col):
        last = col.reshape(nc, L, 1)[:, L - 1:L, :]
        return jnp.broadcast_to(last, (nc, L, 1)).reshape(nc * L, 1)

    t_idx = jnp.bitwise_and(lax.broadcasted_iota(jnp.int32, (nc * L, L), 0), L - 1)
    s_idx = lax.broadcasted_iota(jnp.int32, (nc * L, L), 1)
    causal = s_idx <= t_idx
    ig_c = stack(lambda u, h: tok[u][4][:, h:h + 1])
    lf_c = stack(lambda u, h: tok[u][4][:, n_heads + h:n_heads + h + 1])
    ig_r = stack(lambda u, h: jnp.broadcast_to(tok[u][5][h:h + 1, :], (L, L)))
    lf_r = stack(lambda u, h: jnp.broadcast_to(tok[u][5][n_heads + h:n_heads + h + 1, :], (L, L)))
    m_prev = stack(lambda u, h: jnp.broadcast_to(m_ref[u, h:h + 1, 0:1], (L, 1)))
    b_c = jnp.sum(jnp.where(causal, lf_r, 0.0), axis=1, keepdims=True)
    b_r = jnp.sum(jnp.where(causal, 0.0, lf_c).reshape(nc, L, L), axis=1, keepdims=True)
    b_r = jnp.broadcast_to(b_r, (nc, L, L)).reshape(nc * L, L) + lf_r
    b_last = per_chain_last(b_c)
    dmat = jnp.where(causal, b_c - b_r + ig_r, -jnp.inf)
    g_c = b_c + m_prev
    m_t = jnp.maximum(g_c, jnp.max(dmat, axis=1, keepdims=True))
    w_intra = jnp.exp(dmat - m_t)
    w_inter = jnp.exp(g_c - m_t)
    m_new = per_chain_last(m_t)
    wa = jnp.exp(b_last - b_c + ig_c - m_new)
    decay = jnp.exp(b_last + m_prev - m_new)

    qs = {c: tok[c[0]][0][:, c[1] * dk:(c[1] + 1) * dk] for c in chains}
    ks = {c: tok[c[0]][1][:, c[1] * dk:(c[1] + 1) * dk] for c in chains}
    vs = {c: tok[c[0]][2][:, c[1] * dv:(c[1] + 1) * dv] for c in chains}
    c_prev = {c: c_ref[c[0], c[1]] for c in chains}
    n_prev = {c: n_ref[c[0], c[1]:c[1] + 1, :] for c in chains}
    s = stack(lambda u, h: _dot_nt(qs[u, h], ks[u, h])) * w_intra
    qc = stack(lambda u, h: _dot_nt(qs[u, h], c_prev[u, h].astype(BF)))
    s_bf = s.astype(BF)
    sv = jnp.concatenate([_dot(s_bf[i * L:(i + 1) * L, :], vs[c]) for i, c in enumerate(chains)], axis=0)
    qn = jnp.sum(stack(lambda u, h: qs[u, h].astype(F32) * n_prev[u, h]), axis=1, keepdims=True)
    num = sv + w_inter * qc
    nq = jnp.sum(s, axis=1, keepdims=True) + w_inter * qn
    x = num / jnp.maximum(jnp.abs(nq), jnp.exp(-m_t))
    x = x * stack(lambda u, h: tok[u][3][:, h * dv:(h + 1) * dv])
    x = x * lax.rsqrt(jnp.mean(x * x, axis=-1, keepdims=True) + EPS)
    wk = wa * stack(lambda u, h: ks[u, h].astype(F32))
    wk_bf = wk.astype(BF)
    for i, (u, h) in enumerate(chains):
        rows = slice(i * L, (i + 1) * L)
        hm_ref[u, :, h * dv:(h + 1) * dv] = (x[rows, :] * nrm_ref[:, h * dv:(h + 1) * dv]).astype(hm_ref.dtype)
        dec = decay[i * L:i * L + 1, :]
        c_ref[u, h] = dec * c_prev[u, h] + _dot_tn(vs[u, h], wk_bf[rows, :])
        n_ref[u, h:h + 1, :] = dec * n_prev[u, h] + jnp.sum(wk[rows, :], axis=0, keepdims=True)
        m_ref[u, h:h + 1, :] = jnp.broadcast_to(m_new[i * L:i * L + 1, :], (1, m_ref.shape[2]))


def _mlstm_call(q, k, v, og, gates_c, gates_r, c0, n0, m0, norm, layer, row0, n_seq, n_chunks, n_heads,
                state_layer, n_layers, c_stack=None):
    hdk = q.shape[1]
    hdv = v.shape[1]
    dv, dk = c0.shape[-2:]
    g2 = gates_r.shape[1]
    bt = _pick_tile(n_seq, 4)
    aliased = [] if c_stack is None else [c_stack]

    def chunked(a):
        return a.reshape(a.shape[0] // CHUNK, CHUNK, a.shape[1])

    arrays = [chunked(q), chunked(k), chunked(v), chunked(og), chunked(gates_c), gates_r]
    shapes = [(CHUNK, hdk), (CHUNK, hdk), (CHUNK, hdv), (CHUNK, hdv), (CHUNK, LANES), (g2, CHUNK)]

    def tok(u, shape):
        return pl.BlockSpec((None,) + shape, lambda t, c: (row0 + (t * bt + u) * n_chunks + c, 0, 0))

    def state(shape):
        return pl.BlockSpec((bt,) + shape, lambda t, c: (t,) + (0,) * len(shape))

    def stacked(shape, which):
        return pl.BlockSpec((None, bt) + shape, lambda t, c: (which, t) + (0,) * len(shape))

    in_specs = [tok(u, shape) for u in range(bt) for shape in shapes]
    in_specs += [stacked((n_heads, dv, dk), state_layer), stacked((n_heads, dk), state_layer),
                 stacked((n_heads, LANES), state_layer), pl.BlockSpec((None, 1, hdv), lambda t, c: (layer, 0, 0))]
    in_specs += [pl.BlockSpec(memory_space=pl.ANY)] * len(aliased)
    kern = functools.partial(_mlstm_kernel, n_heads=n_heads, bt=bt)
    return pl.pallas_call(
        kern,
        grid=(n_seq // bt, n_chunks),
        in_specs=in_specs,
        out_specs=[
            pl.BlockSpec((None, bt, CHUNK, hdv), lambda t, c: (c, t, 0, 0)),
            stacked((n_heads, dv, dk), layer), state((n_heads, dk)), state((n_heads, LANES)),
        ],
        out_shape=[
            jax.ShapeDtypeStruct((n_chunks, n_seq, CHUNK, hdv), BF),
            jax.ShapeDtypeStruct((n_layers, n_seq, n_heads, dv, dk), F32),
            jax.ShapeDtypeStruct((n_seq, n_heads, dk), F32),
            jax.ShapeDtypeStruct((n_seq, n_heads, LANES), F32),
        ],
        input_output_aliases={6 * bt + 4: 1} if aliased else {},
        compiler_params=_cparams("parallel", "arbitrary"),
        name="mlstm",
    )(*(arrays * bt), c0, n0, m0, norm, *aliased)


def _build_queries(q_sc, cq_ref, wuq_ref, wuk_ref, cos_ref, sin_ref, n_heads, nope, rope, kvl):
    tq = cq_ref.shape[0]
    qa = _dot(cq_ref[...], wuq_ref[...])
    cos = cos_ref[...]
    sin = sin_ref[...]
    for h in range(n_heads):
        qn = qa[:, h * nope:(h + 1) * nope].astype(BF)
        q_sc[h * tq:(h + 1) * tq, 0:kvl] = _dot(qn, wuk_ref[h]).astype(BF)
    per = LANES // rope
    for p in range(n_heads // per):
        base = n_heads * nope + p * LANES
        pe = _rope128(qa[:, base:base + LANES], cos, sin)
        for u in range(per):
            h = p * per + u
            q_sc[h * tq:(h + 1) * tq, kvl:kvl + rope] = pe[:, u * rope:(u + 1) * rope].astype(BF)


def _lane_tile(x, width):
    if width <= LANES:
        return x[:, :width]
    return jnp.concatenate([x] * (width // LANES), axis=1)


def _flash_step(q_sc, keys_next, s_next, s_cur, vals_cur, m_sc, l_sc, acc_sc, sm_scale, rs, mask_chunks=None,
                inline=False):
    n_groups = q_sc.shape[0] // rs

    def group(r, carry):
        r0 = pl.multiple_of(r * rs, rs)
        rows = pl.ds(r0, rs)
        if s_next is not None:
            s_next[rows, :] = _dot_nt(q_sc[rows, :], keys_next)
        if s_cur is not None:
            kb, kvl = vals_cur.shape
            s = s_cur[rows, :] * sm_scale
            if mask_chunks is not None:
                q_chunks, q_chunk0, k_chunk0 = mask_chunks
                chunk_bits = CHUNK.bit_length() - 1
                row = lax.broadcasted_iota(jnp.int32, (rs, kb), 0) + r0
                q_chunk = jnp.bitwise_and(jnp.right_shift(row, chunk_bits), q_chunks - 1) + q_chunk0
                k_chunk = jnp.right_shift(lax.broadcasted_iota(jnp.int32, (rs, kb), 1), chunk_bits) + k_chunk0
                s = jnp.where(k_chunk <= q_chunk, s, -jnp.inf)
            m_prev = m_sc[rows, :]
            m_new = jnp.maximum(m_prev, jnp.max(s, axis=1, keepdims=True))
            alpha = jnp.exp(m_prev - m_new)
            p = jnp.exp(s - _lane_tile(m_new, kb))
            l_sc[rows, :] = alpha * l_sc[rows, :] + jnp.sum(p, axis=1, keepdims=True)
            acc_sc[rows, :] = _lane_tile(alpha, kvl) * acc_sc[rows, :] + _dot(p.astype(BF), vals_cur)
            m_sc[rows, :] = m_new
        return carry

    lax.fori_loop(0, n_groups, group, 0, unroll=True if inline else (2 if n_groups % 2 == 0 else 1))


def _flash_init(m_sc, l_sc, acc_sc):
    m_sc[...] = jnp.full(m_sc.shape, -jnp.inf, F32)
    l_sc[...] = jnp.zeros(l_sc.shape, F32)
    acc_sc[...] = jnp.zeros(acc_sc.shape, F32)


def _flash_finish(o_ref, l_sc, acc_sc, wuv_ref, n_heads, tq):
    vd = wuv_ref.shape[2]
    kvl = acc_sc.shape[1]
    for h in range(n_heads):
        rows = pl.ds(h * tq, tq)
        o_lat = (acc_sc[rows, :] / _lane_tile(l_sc[rows, :], kvl)).astype(BF)
        o_ref[:, h * vd:(h + 1) * vd] = _dot(o_lat, wuv_ref[h]).astype(o_ref.dtype)


def _attn_prompt_kernel(*refs, n_heads, nope, rope, sm_scale, rs, n_riders):
    cq_ref, ckv_ref, kpe_ref, wuq_ref, wuk_ref, wuv_ref, cos_ref, sin_ref = refs[:8]
    rider_in = refs[8:8 + n_riders]
    o_ref = refs[8 + n_riders]
    rider_out = refs[9 + n_riders:9 + 2 * n_riders]
    kcat, q_sc, m_sc, l_sc, acc_sc, s_a, s_b = refs[9 + 2 * n_riders:]
    i = pl.program_id(1)
    tq = cq_ref.shape[0]
    kvl = ckv_ref.shape[1]
    kb = s_a.shape[1]

    for src, dst in zip(rider_in, rider_out):
        dst[...] = src[...].astype(dst.dtype)

    @pl.when(i == 0)
    def _():
        kcat[:, 0:kvl] = ckv_ref[...].astype(BF)
        kcat[:, kvl:kvl + rope] = kpe_ref[...].astype(BF)

    _build_queries(q_sc, cq_ref, wuq_ref, wuk_ref, cos_ref, sin_ref, n_heads, nope, rope, kvl)
    _flash_init(m_sc, l_sc, acc_sc)

    def step(j_next, s_next, j_cur, s_cur, masked=False):
        keys = None if s_next is None else kcat[pl.ds(pl.multiple_of(j_next * kb, kb), kb), :]
        vals = None if s_cur is None else kcat[pl.ds(pl.multiple_of(j_cur * kb, kb), kb), 0:kvl]
        mask = (tq // CHUNK, i * (tq // CHUNK), j_cur * (kb // CHUNK)) if masked else None
        _flash_step(q_sc, keys, s_next, s_cur, vals, m_sc, l_sc, acc_sc, sm_scale, rs, mask,
                    inline=(s_cur is None) or masked)

    last = (i * tq) // kb
    step(0, s_a, None, None)

    def pair(t, carry):
        j = 2 * t
        step(j + 1, s_b, j, s_a)
        step(j + 2, s_a, j + 1, s_b)
        return carry

    lax.fori_loop(0, last // 2, pair, 0)

    @pl.when(last % 2 == 0)
    def _():
        step(None, None, last, s_a, masked=True)
        _flash_finish(o_ref, l_sc, acc_sc, wuv_ref, n_heads, tq)

    @pl.when(last % 2 == 1)
    def _():
        step(last, s_b, last - 1, s_a)
        step(None, None, last, s_b, masked=True)
        _flash_finish(o_ref, l_sc, acc_sc, wuv_ref, n_heads, tq)


def _rider_block(rows, cols, n_steps):
    for n_row_blocks in range(n_steps, 0, -1):
        if n_steps % n_row_blocks or rows % n_row_blocks or cols % (n_steps // n_row_blocks):
            continue
        br, bc = rows // n_row_blocks, cols // (n_steps // n_row_blocks)
        if br % 16 == 0 and bc % LANES == 0:
            return br, bc
    raise ValueError(f"no aligned {n_steps}-block tiling of ({rows}, {cols})")


def _attn_prompt_call(cq, ckv, kpe, wuq, wuk, wuv, cos, sin, layer, n_seq, seq, tq, dims, riders=()):
    ah, nope, rope, kvl, vd = dims["ah"], dims["nope"], dims["rope"], dims["kvl"], dims["vd"]
    ql = cq.shape[1]
    nq = seq // tq
    kb = _pick_tile(seq, 512)
    assert tq % CHUNK == 0 and (tq // CHUNK) & (tq // CHUNK - 1) == 0 and kb % tq == 0
    rs = _pick_tile(ah * tq, 512)
    kern = functools.partial(_attn_prompt_kernel, n_heads=ah, nope=nope, rope=rope,
                             sm_scale=(nope + rope) ** -0.5, rs=rs, n_riders=len(riders))

    def wspec(w):
        return pl.BlockSpec((None,) + w.shape[1:], lambda b, i: (layer,) + (0,) * (w.ndim - 1))

    rider_in, rider_out, rider_shapes = [], [], []
    for arr, arr_layer in riders:
        _, rows, cols = arr.shape
        br, bc = _rider_block(rows, cols, n_seq * nq)
        ncb = cols // bc
        rider_in.append(pl.BlockSpec((None, br, bc), functools.partial(
            lambda b, i, ncb, arr_layer: (arr_layer, (b * nq + i) // ncb, (b * nq + i) % ncb), ncb=ncb, arr_layer=arr_layer)))
        rider_out.append(pl.BlockSpec((br, bc), functools.partial(
            lambda b, i, ncb: ((b * nq + i) // ncb, (b * nq + i) % ncb), ncb=ncb)))
        rider_shapes.append(jax.ShapeDtypeStruct((rows, cols), BF))

    return pl.pallas_call(
        kern,
        grid=(n_seq, nq),
        in_specs=[
            pl.BlockSpec((tq, ql), lambda b, i: (b * nq + i, 0)),
            pl.BlockSpec((seq, kvl), lambda b, i: (b, 0)),
            pl.BlockSpec((seq, rope), lambda b, i: (b, 0)),
            wspec(wuq), wspec(wuk), wspec(wuv),
            pl.BlockSpec((tq, LANES), lambda b, i: (b * nq + i, 0)),
            pl.BlockSpec((tq, LANES), lambda b, i: (b * nq + i, 0)),
        ] + rider_in,
        out_specs=[pl.BlockSpec((tq, ah * vd), lambda b, i: (b * nq + i, 0))] + rider_out,
        out_shape=[jax.ShapeDtypeStruct((n_seq * seq, ah * vd), BF)] + rider_shapes,
        scratch_shapes=[
            pltpu.VMEM((seq, kvl + rope), BF),
            pltpu.VMEM((ah * tq, kvl + rope), BF),
            pltpu.VMEM((ah * tq, LANES), F32),
            pltpu.VMEM((ah * tq, LANES), F32),
            pltpu.VMEM((ah * tq, kvl), F32),
            pltpu.VMEM((ah * tq, kb), F32),
            pltpu.VMEM((ah * tq, kb), F32),
        ],
        compiler_params=_cparams("parallel", "arbitrary"),
        name="mla_prompt",
    )(cq, ckv, kpe, wuq, wuk, wuv, cos, sin, *[arr for arr, _ in riders])


def _attn_sample_kernel(cq_ref, ckv_ref, kpe_ref, pckv_ref, pkpe_ref, wuq_ref, wuk_ref, wuv_ref, cos_ref, sin_ref,
                        o_ref, kcat, q_sc, m_sc, l_sc, acc_sc, s_a, s_b, s_n, *, n_heads, nope, rope, sm_scale):
    tq = cq_ref.shape[0]
    kvl = ckv_ref.shape[1]
    past = pckv_ref.shape[0]
    kb = s_a.shape[1]
    kcat[0:past, 0:kvl] = pckv_ref[...].astype(BF)
    kcat[0:past, kvl:kvl + rope] = pkpe_ref[...].T.astype(BF)
    kcat[past:past + tq, 0:kvl] = ckv_ref[...].astype(BF)
    kcat[past:past + tq, kvl:kvl + rope] = kpe_ref[...].astype(BF)
    _build_queries(q_sc, cq_ref, wuq_ref, wuk_ref, cos_ref, sin_ref, n_heads, nope, rope, kvl)
    _flash_init(m_sc, l_sc, acc_sc)
    rs = q_sc.shape[0]

    blocks = [(j * kb, kb, (s_a, s_b)[j % 2]) for j in range(past // kb)] + [(past, tq, s_n)]
    for cur, nxt in zip([None] + blocks, blocks + [None]):
        keys, s_next = (None, None) if nxt is None else (kcat[nxt[0]:nxt[0] + nxt[1], :], nxt[2])
        vals, s_cur = (None, None) if cur is None else (kcat[cur[0]:cur[0] + cur[1], 0:kvl], cur[2])
        _flash_step(q_sc, keys, s_next, s_cur, vals, m_sc, l_sc, acc_sc, sm_scale, rs)
    _flash_finish(o_ref, l_sc, acc_sc, wuv_ref, n_heads, tq)


def _attn_sample_call(cq, ckv, kpe, cache_ckv, cache_kpe, wuq, wuk, wuv, cos, sin, layer, row0, n_seq, tq, dims):
    ah, nope, rope, kvl, vd = dims["ah"], dims["nope"], dims["rope"], dims["kvl"], dims["vd"]
    ql = cq.shape[1]
    past = cache_ckv.shape[2]
    kb = math.gcd(past, 256)
    kern = functools.partial(_attn_sample_kernel, n_heads=ah, nope=nope, rope=rope,
                             sm_scale=(nope + rope) ** -0.5)

    def wspec(w):
        return pl.BlockSpec((None,) + w.shape[1:], lambda b: (layer,) + (0,) * (w.ndim - 1))

    def tok(n):
        return pl.BlockSpec((tq, n), lambda b: (row0 + b, 0))

    return pl.pallas_call(
        kern,
        grid=(n_seq,),
        in_specs=[
            tok(ql), tok(kvl), tok(rope),
            pl.BlockSpec((None, None, past, kvl), lambda b: (layer, b, 0, 0)),
            pl.BlockSpec((None, None, rope, past), lambda b: (layer, b, 0, 0)),
            wspec(wuq), wspec(wuk), wspec(wuv), tok(LANES), tok(LANES),
        ],
        out_specs=pl.BlockSpec((tq, ah * vd), lambda b: (b, 0)),
        out_shape=jax.ShapeDtypeStruct((n_seq * tq, ah * vd), BF),
        scratch_shapes=[
            pltpu.VMEM((past + tq, kvl + rope), BF),
            pltpu.VMEM((ah * tq, kvl + rope), BF),
            pltpu.VMEM((ah * tq, LANES), F32),
            pltpu.VMEM((ah * tq, LANES), F32),
            pltpu.VMEM((ah * tq, kvl), F32),
            pltpu.VMEM((ah * tq, kb), F32),
            pltpu.VMEM((ah * tq, kb), F32),
            pltpu.VMEM((ah * tq, tq), F32),
        ],
        compiler_params=_cparams("parallel"),
        name="mla_sample",
    )(cq, ckv, kpe, cache_ckv, cache_kpe, wuq, wuk, wuv, cos, sin)


def _outproj_kernel(x_ref, hmp_ref, hms_ref, oap_ref, oas_ref, gt_ref, wm_ref, wa_ref, o_ref, *, n_prompt_tiles):
    i = pl.program_id(0)
    tm = x_ref.shape[0]

    def project(hm, oa):
        y = _dot(hm, wm_ref[...]) + _dot(oa, wa_ref[...])
        for g in range(tm // CHUNK):
            rows = pl.ds(g * CHUNK, CHUNK)
            o_ref[rows, :] = x_ref[rows, :] + gt_ref[g:g + 1, :] * y[g * CHUNK:(g + 1) * CHUNK, :]

    @pl.when(i < n_prompt_tiles)
    def _():
        project(hmp_ref[...].reshape(tm, hmp_ref.shape[-1]), oap_ref[...])

    @pl.when(i >= n_prompt_tiles)
    def _():
        project(hms_ref[...], oas_ref[...])


def _outproj_call(x, hm_p, hm_s, oa_p, oa_s, mod, w_out, layer, tm, seq):
    m, d = x.shape
    wm = hm_s.shape[1]
    wa = oa_s.shape[1]
    assert wm == wa and seq % tm == 0
    ng = tm // CHUNK
    npt = oa_p.shape[0] // tm
    tps = seq // tm

    def prompt_tile(i):
        return jnp.minimum(i, npt - 1)

    def sample_tile(i):
        return jnp.maximum(i - npt, 0)

    return pl.pallas_call(
        functools.partial(_outproj_kernel, n_prompt_tiles=npt),
        grid=(m // tm,),
        in_specs=[
            pl.BlockSpec((tm, d), lambda i: (i, 0)),
            pl.BlockSpec((ng, None, CHUNK, wm), lambda i: (prompt_tile(i) % tps, prompt_tile(i) // tps, 0, 0)),
            pl.BlockSpec((tm, wm), lambda i: (sample_tile(i), 0)),
            pl.BlockSpec((tm, wa), lambda i: (prompt_tile(i), 0)),
            pl.BlockSpec((tm, wa), lambda i: (sample_tile(i), 0)),
            pl.BlockSpec((None, ng, d), lambda i: (layer, i, 5)),
            pl.BlockSpec((None, wm, d), lambda i: (layer, 0, 0)),
            pl.BlockSpec((None, wa, d), lambda i: (layer, 1, 0)),
        ],
        out_specs=pl.BlockSpec((tm, d), lambda i: (i, 0)),
        out_shape=jax.ShapeDtypeStruct((m, d), F32),
        compiler_params=_cparams("parallel"),
        name="out_proj",
    )(x, hm_p, hm_s, oa_p, oa_s, mod, w_out, w_out)


def _rope_tables(pos, rope):
    half = rope // 2
    freqs = ROPE_THETA ** (-jnp.arange(half, dtype=F32) / half)
    ang = pos.astype(F32)[:, None] * freqs[None, :]
    cos = jnp.cos(ang)
    sin = jnp.sin(ang)
    reps = LANES // rope
    return jnp.tile(jnp.concatenate([cos, cos], axis=1), (1, reps)), jnp.tile(jnp.concatenate([-sin, sin], axis=1), (1, reps))


def _pick_tile(m, cap):
    t = cap
    while m % t:
        t //= 2
    return t


def kernel(x_prompt, x_sample, c_prompt, c_sample, cache_ckv, cache_kpe, state_C, state_n, state_m, mod_w, mod_b, ln_ffn1, ffn1_w_in, ffn1_w_out, ln_mix, w_in, mlstm_b_i, mlstm_b_f, mlstm_norm, q_norm, w_uq, kv_norm, w_uk, w_uv, w_out, ln_ffn2, ffn2_w_in, ffn2_w_out, final_norm):
    bp, sp, d = x_prompt.shape
    bs, ss, _ = x_sample.shape
    depth = mod_w.shape[0]
    past = cache_ckv.shape[2]
    mh, dv, dk = state_C.shape[2:]
    kvl, ah, nope = w_uk.shape[1:]
    vd = w_uv.shape[3]
    rope = cache_kpe.shape[3]
    ql = q_norm.shape[1]
    hdk, hdv = mh * dk, mh * dv
    dff = ffn1_w_out.shape[1]
    dims = dict(mh=mh, dk=dk, dv=dv, hdk=hdk, hdv=hdv, ql=ql, kvl=kvl, rope=rope, ah=ah, nope=nope, vd=vd)
    assert sp % CHUNK == 0 and ss == CHUNK and LANES % rope == 0 and ah % (LANES // rope) == 0
    mp, ms = bp * sp, bs * ss
    m = mp + ms
    tm = _pick_tile(math.gcd(mp, ms), 512)
    tm_ffn = _pick_tile(math.gcd(mp, ms), 1024)
    ffn_tiles_p, ffn_tiles_s = mp // tm_ffn, ms // tm_ffn
    tf = _pick_tile(dff, 512)
    tq = _pick_tile(sp, 256)

    cg = jnp.concatenate([jnp.repeat(c_prompt, sp // CHUNK, axis=0), jnp.repeat(c_sample, ss // CHUNK, axis=0)], axis=0)
    pos = jnp.concatenate([jnp.tile(jnp.arange(sp), bp), jnp.tile(past + jnp.arange(ss), bs)])
    cos, sin = _rope_tables(pos, rope)

    o_mq, o_mk, o_mv, o_mo = 0, hdk, 2 * hdk, 2 * hdk + hdv
    o_mi = o_mo + hdv
    o_mf, o_qa = o_mi + mh, o_mi + 2 * mh
    o_kva, o_pe = o_qa + ql, o_qa + ql + kvl
    wqk = w_in[:, :, o_mq:o_mv].astype(BF)
    wv = w_in[:, :, o_mv:o_mo].astype(BF)
    wo = w_in[:, :, o_mo:o_mi].astype(BF)
    zpad = lambda n: jnp.zeros((depth, d, n), w_in.dtype)
    wa = jnp.concatenate([w_in[:, :, o_qa:o_pe + rope], zpad(LANES - rope), w_in[:, :, o_mi:o_qa], zpad(LANES - 2 * mh)], axis=2).astype(BF)
    gate_bias = jnp.concatenate([mlstm_b_i, mlstm_b_f, jnp.zeros((depth, LANES - 2 * mh), F32)], axis=1).reshape(depth, 1, LANES)
    wuq4 = w_uq.reshape(depth, ql, ah, nope + rope)
    wuq = jnp.concatenate([wuq4[..., :nope].reshape(depth, ql, ah * nope), wuq4[..., nope:].reshape(depth, ql, ah * rope)], axis=2).astype(BF)
    wuk = jnp.transpose(w_uk, (0, 2, 3, 1)).astype(BF)
    wuv = jnp.transpose(w_uv, (0, 2, 1, 3)).astype(BF)
    w_out_b = w_out.astype(BF)
    f1_in, f1_out = ffn1_w_in[0].astype(BF), ffn1_w_out[0].astype(BF)
    r3 = lambda a: a.reshape(depth, 1, a.shape[1])

    mod = _mod_call(cg, mod_w, mod_b)

    zeros_c = jnp.zeros((1, bp, mh, dv, dk), F32)
    zeros_n = jnp.zeros((1, bp, mh, dk), F32)
    zeros_m = jnp.zeros((1, bp, mh, LANES), F32)
    m0_s = jnp.broadcast_to(state_m[..., None], state_m.shape + (LANES,))
    cache_kpe_t = jnp.swapaxes(cache_kpe, 2, 3)

    outs = {k: [] for k in ("p_kpe", "p_n", "p_m", "s_kpe", "s_n", "s_m")}
    ckv_stacks = p_c = s_c = None
    for l in range(depth):
        if l == 0:
            x = _ffn_call(x_prompt.reshape(mp, d), mod, r3(ln_ffn1), f1_in, f1_out, l, 0, tm_ffn, tf, out_rows=m)
            x = _ffn_call(x_sample.reshape(ms, d), mod, r3(ln_ffn1), f1_in, f1_out, l, 0, tm_ffn, tf,
                          stream_tile0=ffn_tiles_p, out_tile0=ffn_tiles_p, into=x)
        else:
            x = _ffn_call(x, mod, r3(ln_ffn1), f1_in, f1_out, l, 0, tm_ffn, tf)
        q, k, v, og, cq, ckv, kpe, gates, *ckv_stacks = _inproj_call(
            x, mod, r3(ln_mix), wqk, wv, wo, wa, r3(q_norm), r3(kv_norm), gate_bias, cos, sin, l, tm, dims,
            depth, mp, ckv_stacks)
        gates_r = jnp.transpose(gates[:, :2 * mh].reshape(m // CHUNK, CHUNK, 2 * mh), (0, 2, 1))
        nrm = r3(mlstm_norm)
        hm_p, p_c, n_p, m_p = _mlstm_call(q, k, v, og, gates, gates_r, zeros_c, zeros_n, zeros_m, nrm,
                                          l, 0, bp, sp // CHUNK, mh, 0, depth, p_c)
        hm_s, s_c, n_s, m_s = _mlstm_call(q, k, v, og, gates, gates_r, state_C, state_n, m0_s, nrm,
                                          l, mp // CHUNK, bs, ss // CHUNK, mh, l, depth, s_c)
        riders = [(ffn2_w_in, l), (ffn2_w_out, l)]
        if l + 1 < depth:
            riders += [(ffn1_w_in, l + 1), (ffn1_w_out, l + 1)]
        oa_p, f2_in, f2_out, *next_f1 = _attn_prompt_call(cq, ckv, kpe, wuq, wuk, wuv, cos, sin, l, bp, sp, tq, dims, riders)
        if next_f1:
            f1_in, f1_out = next_f1
        oa_s = _attn_sample_call(cq, ckv, kpe, cache_ckv, cache_kpe_t, wuq, wuk, wuv, cos, sin, l, mp // ss, bs, ss, dims)
        x = _outproj_call(x, hm_p, hm_s.reshape(ms, hdv), oa_p, oa_s, mod, w_out_b, l, tm, sp)
        if l + 1 < depth:
            x = _ffn_call(x, mod, r3(ln_ffn2), f2_in, f2_out, l, 6, tm_ffn, tf)
        else:
            fn = final_norm.reshape(1, d)
            y_prompt = _ffn_call(x, mod, r3(ln_ffn2), f2_in, f2_out, l, 6, tm_ffn, tf,
                                 n_tiles=ffn_tiles_p, final_gain=fn).reshape(bp, sp, d)
            y_sample = _ffn_call(x, mod, r3(ln_ffn2), f2_in, f2_out, l, 6, tm_ffn, tf, src_tile0=ffn_tiles_p,
                                 n_tiles=ffn_tiles_s, stream_tile0=ffn_tiles_p, final_gain=fn).reshape(bs, ss, d)
        outs["p_kpe"].append(kpe[:mp].reshape(bp, sp, rope))
        outs["s_kpe"].append(kpe[mp:].reshape(bs, ss, rope))
        outs["p_n"].append(n_p)
        outs["p_m"].append(m_p[..., 0])
        outs["s_n"].append(n_s)
        outs["s_m"].append(m_s[..., 0])

    st = {k: jnp.stack(v) for k, v in outs.items()}
    p_ckv = ckv_stacks[0].reshape(depth, bp, sp, kvl)
    s_ckv = ckv_stacks[1].reshape(depth, bs, ss, kvl)
    return (y_prompt, y_sample, p_ckv, st["p_kpe"], p_c, st["p_n"], st["p_m"],
            s_ckv, st["s_kpe"], s_c, st["s_n"], st["s_m"])
```

```python
import functools
import math

import jax
import jax.numpy as jnp
from jax import lax
from jax.experimental import pallas as pl
from jax.experimental.pallas import tpu as pltpu

CHUNK = 64
EPS = 1e-6
ROPE_THETA = 10000.0
N_MOD = 9
LANES = 128
VMEM_LIMIT = 58 * 1024 * 1024

BF = jnp.bfloat16
F32 = jnp.float32


def _cparams(*sem):
    return pltpu.CompilerParams(dimension_semantics=sem, vmem_limit_bytes=VMEM_LIMIT)


def _dot(a, b):
    return jnp.dot(a, b, preferred_element_type=F32)


def _dot_nt(a, b):
    return lax.dot_general(a, b, (((1,), (1,)), ((), ())), preferred_element_type=F32)


def _dot_tn(a, b):
    return lax.dot_general(a, b, (((0,), (0,)), ((), ())), preferred_element_type=F32)


def _rms(x, g):
    ms = jnp.mean(x * x, axis=-1, keepdims=True)
    return x * lax.rsqrt(ms + EPS) * g


def _silu(x):
    return x * jax.nn.sigmoid(x)


def _rope128(pe, cos, sin):
    lane = lax.broadcasted_iota(jnp.int32, pe.shape, 1)
    first_half = jnp.bitwise_and(lane, 63) < 32
    swapped = jnp.where(first_half, pltpu.roll(pe, 96, 1), pltpu.roll(pe, 32, 1))
    return pe * cos + swapped * sin


def _mod_kernel(c_ref, w_ref, b_ref, o_ref):
    a = _silu(c_ref[...]).astype(BF)
    o_ref[...] = _dot(a, w_ref[...].astype(BF)) + b_ref[...]


def _mod_call(cg, mod_w, mod_b):
    depth, d, nd = mod_w.shape
    g = cg.shape[0]
    tn = 1024
    return pl.pallas_call(
        _mod_kernel,
        grid=(depth, nd // tn),
        in_specs=[
            pl.BlockSpec((g, d), lambda l, j: (0, 0)),
            pl.BlockSpec((None, d, tn), lambda l, j: (l, 0, j)),
            pl.BlockSpec((None, 1, tn), lambda l, j: (l, 0, j)),
        ],
        out_specs=pl.BlockSpec((None, g, tn), lambda l, j: (l, 0, j)),
        out_shape=jax.ShapeDtypeStruct((depth, g, nd), F32),
        compiler_params=_cparams("parallel", "parallel"),
        name="adaln_mod",
    )(cg, mod_w, mod_b.reshape(depth, 1, nd))


def _norm_mod_to(h_ref, x_ref, ln_ref, sh_ref, sc_ref):
    ln = ln_ref[...]
    for g in range(x_ref.shape[0] // CHUNK):
        rows = pl.ds(g * CHUNK, CHUNK)
        y = _rms(x_ref[rows, :], ln)
        h_ref[rows, :] = (y * (1.0 + sc_ref[g:g + 1, :]) + sh_ref[g:g + 1, :]).astype(h_ref.dtype)


def _ffn_kernel(*refs, final_norm, aliased):
    x_ref, sh_ref, sc_ref, gt_ref, ln_ref, wg_ref, wu_ref, wo_ref = refs[:8]
    fg_ref = refs[8] if final_norm else None
    o_ref, h_sc = refs[8 + final_norm + aliased:]
    j = pl.program_id(1)
    nj = pl.num_programs(1)

    def up_down(h):
        a = (_silu(_dot(h, wg_ref[...])) * _dot(h, wu_ref[...])).astype(BF)
        return _dot(a, wo_ref[...])

    def residual(acc):
        for g in range(x_ref.shape[0] // CHUNK):
            rows = pl.ds(g * CHUNK, CHUNK)
            y = x_ref[rows, :] + (0.5 * gt_ref[g:g + 1, :]) * acc[g * CHUNK:(g + 1) * CHUNK, :]
            o_ref[rows, :] = _rms(y, fg_ref[...]) if final_norm else y

    @pl.when(j == 0)
    def _():
        _norm_mod_to(h_sc, x_ref, ln_ref, sh_ref, sc_ref)
        o_ref[...] = up_down(h_sc[...])

    @pl.when(jnp.logical_and(j > 0, j < nj - 1))
    def _():
        o_ref[...] += up_down(h_sc[...])

    @pl.when(jnp.logical_and(j > 0, j == nj - 1))
    def _():
        residual(o_ref[...] + up_down(h_sc[...]))

    @pl.when(nj == 1)
    def _():
        residual(o_ref[...])


def _ffn_call(x, mod, ln, w_in, w_out, layer, k0, tm, tf, *, src_tile0=0, n_tiles=None, stream_tile0=0,
              out_rows=None, out_tile0=0, into=None, final_gain=None):
    d = x.shape[1]
    f = w_out.shape[0]
    ng = tm // CHUNK
    nf = f // tf
    n_tiles = x.shape[0] // tm if n_tiles is None else n_tiles
    out_rows = (into.shape[0] if into is not None else n_tiles * tm) if out_rows is None else out_rows

    def mod_spec(k):
        return pl.BlockSpec((None, ng, d), lambda i, j: (layer, i + stream_tile0, k))

    in_specs = [
        pl.BlockSpec((tm, d), lambda i, j: (i + src_tile0, 0)),
        mod_spec(k0), mod_spec(k0 + 1), mod_spec(k0 + 2),
        pl.BlockSpec((None, 1, d), lambda i, j: (layer, 0, 0)),
        pl.BlockSpec((d, tf), lambda i, j: (0, j)),
        pl.BlockSpec((d, tf), lambda i, j: (0, j + nf)),
        pl.BlockSpec((tf, d), lambda i, j: (j, 0)),
    ]
    operands = [x, mod, mod, mod, ln, w_in, w_in, w_out]
    if final_gain is not None:
        in_specs.append(pl.BlockSpec((1, d), lambda i, j: (0, 0)))
        operands.append(final_gain)
    aliases = {}
    if into is not None:
        aliases = {len(operands): 0}
        in_specs.append(pl.BlockSpec(memory_space=pl.ANY))
        operands.append(into)
    kern = functools.partial(_ffn_kernel, final_norm=final_gain is not None, aliased=into is not None)
    return pl.pallas_call(
        kern,
        grid=(n_tiles, nf),
        in_specs=in_specs,
        out_specs=pl.BlockSpec((tm, d), lambda i, j: (i + out_tile0, 0)),
        out_shape=jax.ShapeDtypeStruct((out_rows, d), F32),
        scratch_shapes=[pltpu.VMEM((tm, d), BF)],
        input_output_aliases=aliases,
        compiler_params=_cparams("parallel", "arbitrary"),
        name="ffn",
    )(*operands)


def _inproj_kernel(*refs, n_heads, k_scale, n_aliased, n_prompt_tiles):
    (x_ref, sh_ref, sc_ref, ln_ref, wqk_ref, wv_ref, wo_ref, wa_ref, qn_ref, kvn_ref, gb_ref,
     cos_ref, sin_ref) = refs[:13]
    (q_ref, k_ref, v_ref, og_ref, cq_ref, ckv_ref, kpe_ref, gt_ref, ckvp_ref, ckvs_ref,
     h_sc) = refs[13 + n_aliased:]
    _norm_mod_to(h_sc, x_ref, ln_ref, sh_ref, sc_ref)
    h = h_sc[...]
    hdk = q_ref.shape[1]
    za = _dot(h, wa_ref[...])
    og_ref[...] = jax.nn.sigmoid(_dot(h, wo_ref[...]))
    zqk = _dot(h, wqk_ref[...])
    q_ref[...] = zqk[:, :hdk].astype(BF)
    k_ref[...] = (zqk[:, hdk:] * k_scale).astype(BF)
    v_ref[...] = _dot(h, wv_ref[...]).astype(BF)
    ql = cq_ref.shape[1]
    kvl = ckv_ref.shape[1]
    cq_ref[...] = _rms(za[:, :ql], qn_ref[...]).astype(BF)
    ckv = _rms(za[:, ql:ql + kvl], kvn_ref[...])
    ckv_ref[...] = ckv
    pe = _rope128(za[:, ql + kvl:ql + kvl + LANES], cos_ref[...], sin_ref[...])
    kpe_ref[...] = pe[:, :kpe_ref.shape[1]]
    zg = za[:, ql + kvl + LANES:] + gb_ref[...]
    lane = lax.broadcasted_iota(jnp.int32, zg.shape, 1)
    log_sig = jnp.minimum(zg, 0.0) - jnp.log1p(jnp.exp(-jnp.abs(zg)))
    gt_ref[...] = jnp.where(lane < n_heads, zg, log_sig)

    i = pl.program_id(0)

    @pl.when(i < n_prompt_tiles)
    def _():
        ckvp_ref[...] = ckv_ref[...]

    @pl.when(i >= n_prompt_tiles)
    def _():
        ckvs_ref[...] = ckv_ref[...]


def _inproj_call(x, mod, ln, wqk, wv, wo, wa, q_norm, kv_norm, gate_bias, cos, sin, layer, tm, dims,
                 n_layers, rows_prompt, ckv_stacks=None):
    m, d = x.shape
    ng = tm // CHUNK
    hdk, hdv, ql, kvl, rope = dims["hdk"], dims["hdv"], dims["ql"], dims["kvl"], dims["rope"]
    npt = rows_prompt // tm
    aliased = [] if ckv_stacks is None else list(ckv_stacks)

    def mod_spec(k):
        return pl.BlockSpec((None, ng, d), lambda i: (layer, i, k))

    def wspec(w):
        return pl.BlockSpec((None,) + w.shape[1:], lambda i: (layer, 0, 0))

    def row(n):
        return pl.BlockSpec((tm, n), lambda i: (i, 0))

    kern = functools.partial(_inproj_kernel, n_heads=dims["mh"], k_scale=dims["dk"] ** -0.5,
                             n_aliased=len(aliased), n_prompt_tiles=npt)
    return pl.pallas_call(
        kern,
        grid=(m // tm,),
        in_specs=[
            row(d), mod_spec(3), mod_spec(4), wspec(ln), wspec(wqk), wspec(wv), wspec(wo), wspec(wa),
            wspec(q_norm), wspec(kv_norm), wspec(gate_bias), row(LANES), row(LANES),
        ] + [pl.BlockSpec(memory_space=pl.ANY)] * len(aliased),
        out_specs=[
            row(hdk), row(hdk), row(hdv), row(hdv), row(ql), row(kvl), row(rope), row(LANES),
            pl.BlockSpec((None, tm, kvl), lambda i: (layer, jnp.minimum(i, npt - 1), 0)),
            pl.BlockSpec((None, tm, kvl), lambda i: (layer, jnp.maximum(i - npt, 0), 0)),
        ],
        out_shape=[
            jax.ShapeDtypeStruct((m, hdk), BF), jax.ShapeDtypeStruct((m, hdk), BF),
            jax.ShapeDtypeStruct((m, hdv), BF), jax.ShapeDtypeStruct((m, hdv), F32),
            jax.ShapeDtypeStruct((m, ql), BF), jax.ShapeDtypeStruct((m, kvl), F32),
            jax.ShapeDtypeStruct((m, rope), F32), jax.ShapeDtypeStruct((m, LANES), F32),
            jax.ShapeDtypeStruct((n_layers, rows_prompt, kvl), F32),
            jax.ShapeDtypeStruct((n_layers, m - rows_prompt, kvl), F32),
        ],
        scratch_shapes=[pltpu.VMEM((tm, d), BF)],
        input_output_aliases={13 + a: 8 + a for a in range(len(aliased))},
        compiler_params=_cparams("arbitrary"),
        name="in_proj",
    )(x, mod, mod, ln, wqk, wv, wo, wa, q_norm, kv_norm, gate_bias, cos, sin, *aliased)


def _mlstm_kernel(*refs, n_heads, bt):
    tok = [refs[6 * u:6 * u + 6] for u in range(bt)]
    c0_ref, n0_ref, m0_ref, nrm_ref = refs[6 * bt:6 * bt + 4]
    hm_ref, c_ref, n_ref, m_ref = refs[-4:]
    c_idx = pl.program_id(1)

    @pl.when(c_idx == 0)
    def _():
        c_ref[...] = c0_ref[...]
        n_ref[...] = n0_ref[...]
        m_ref[...] = m0_ref[...]

    L = tok[0][0].shape[0]
    dk = tok[0][0].shape[1] // n_heads
    dv = tok[0][2].shape[1] // n_heads
    chains = [(u, h) for u in range(bt) for h in range(n_heads)]
    nc = len(chains)

    def stack(fn):
        return jnp.concatenate([fn(u, h) for u, h in chains], axis=0)

    def per_chain_last(col):
        last = col.reshape(nc, L, 1)[:, L - 1:L, :]
        return jnp.broadcast_to(last, (nc, L, 1)).reshape(nc * L, 1)

    t_idx = jnp.bitwise_and(lax.broadcasted_iota(jnp.int32, (nc * L, L), 0), L - 1)
    s_idx = lax.broadcasted_iota(jnp.int32, (nc * L, L), 1)
    causal = s_idx <= t_idx
    ig_c = stack(lambda u, h: tok[u][4][:, h:h + 1])
    lf_c = stack(lambda u, h: tok[u][4][:, n_heads + h:n_heads + h + 1])
    ig_r = stack(lambda u, h: jnp.broadcast_to(tok[u][5][h:h + 1, :], (L, L)))
    lf_r = stack(lambda u, h: jnp.broadcast_to(tok[u][5][n_heads + h:n_heads + h + 1, :], (L, L)))
    m_prev = stack(lambda u, h: jnp.broadcast_to(m_ref[u, h:h + 1, 0:1], (L, 1)))
    b_c = jnp.sum(jnp.where(causal, lf_r, 0.0), axis=1, keepdims=True)
    b_r = jnp.sum(jnp.where(causal, 0.0, lf_c).reshape(nc, L, L), axis=1, keepdims=True)
    b_r = jnp.broadcast_to(b_r, (nc, L, L)).reshape(nc * L, L) + lf_r
    b_last = per_chain_last(b_c)
    dmat = jnp.where(causal, b_c - b_r + ig_r, -jnp.inf)
    g_c = b_c + m_prev
    m_t = jnp.maximum(g_c, jnp.max(dmat, axis=1, keepdims=True))
    w_intra = jnp.exp(dmat - m_t)
    w_inter = jnp.exp(g_c - m_t)
    m_new = per_chain_last(m_t)
    wa = jnp.exp(b_last - b_c + ig_c - m_new)
    decay = jnp.exp(b_last + m_prev - m_new)

    qs = {c: tok[c[0]][0][:, c[1] * dk:(c[1] + 1) * dk] for c in chains}
    ks = {c: tok[c[0]][1][:, c[1] * dk:(c[1] + 1) * dk] for c in chains}
    vs = {c: tok[c[0]][2][:, c[1] * dv:(c[1] + 1) * dv] for c in chains}
    c_prev = {c: c_ref[c[0], c[1]] for c in chains}
    n_prev = {c: n_ref[c[0], c[1]:c[1] + 1, :] for c in chains}
    s = stack(lambda u, h: _dot_nt(qs[u, h], ks[u, h])) * w_intra
    qc = stack(lambda u, h: _dot_nt(qs[u, h], c_prev[u, h].astype(BF)))
    s_bf = s.astype(BF)
    sv = jnp.concatenate([_dot(s_bf[i * L:(i + 1) * L, :], vs[c]) for i, c in enumerate(chains)], axis=0)
    qn = jnp.sum(stack(lambda u, h: qs[u, h].astype(F32) * n_prev[u, h]), axis=1, keepdims=True)
    num = sv + w_inter * qc
    nq = jnp.sum(s, axis=1, keepdims=True) + w_inter * qn
    r = 1.0 / jnp.maximum(jnp.abs(nq), jnp.exp(-m_t))
    y = num * stack(lambda u, h: tok[u][3][:, h * dv:(h + 1) * dv])
    x = y * (r * lax.rsqrt(jnp.mean(y * y, axis=-1, keepdims=True) * (r * r) + EPS))
    wk = wa * stack(lambda u, h: ks[u, h].astype(F32))
    wk_bf = wk.astype(BF)
    for i, (u, h) in enumerate(chains):
        rows = slice(i * L, (i + 1) * L)
        hm_ref[u, :, h * dv:(h + 1) * dv] = (x[rows, :] * nrm_ref[:, h * dv:(h + 1) * dv]).astype(hm_ref.dtype)
        dec = decay[i * L:i * L + 1, :]
        c_ref[u, h] = dec * c_prev[u, h] + _dot_tn(vs[u, h], wk_bf[rows, :])
        n_ref[u, h:h + 1, :] = dec * n_prev[u, h] + jnp.sum(wk[rows, :], axis=0, keepdims=True)
        m_ref[u, h:h + 1, :] = jnp.broadcast_to(m_new[i * L:i * L + 1, :], (1, m_ref.shape[2]))


def _mlstm_call(q, k, v, og, gates_c, gates_r, c0, n0, m0, norm, layer, row0, n_seq, n_chunks, n_heads,
                state_layer, n_layers, c_stack=None):
    hdk = q.shape[1]
    hdv = v.shape[1]
    dv, dk = c0.shape[-2:]
    g2 = gates_r.shape[1]
    bt = _pick_tile(n_seq, 4)
    aliased = [] if c_stack is None else [c_stack]

    def chunked(a):
        return a.reshape(a.shape[0] // CHUNK, CHUNK, a.shape[1])

    arrays = [chunked(q), chunked(k), chunked(v), chunked(og), chunked(gates_c), gates_r]
    shapes = [(CHUNK, hdk), (CHUNK, hdk), (CHUNK, hdv), (CHUNK, hdv), (CHUNK, LANES), (g2, CHUNK)]

    def tok(u, shape):
        return pl.BlockSpec((None,) + shape, lambda t, c: (row0 + (t * bt + u) * n_chunks + c, 0, 0))

    def state(shape):
        return pl.BlockSpec((bt,) + shape, lambda t, c: (t,) + (0,) * len(shape))

    def stacked(shape, which):
        return pl.BlockSpec((None, bt) + shape, lambda t, c: (which, t) + (0,) * len(shape))

    in_specs = [tok(u, shape) for u in range(bt) for shape in shapes]
    in_specs += [stacked((n_heads, dv, dk), state_layer), stacked((n_heads, dk), state_layer),
                 stacked((n_heads, LANES), state_layer), pl.BlockSpec((None, 1, hdv), lambda t, c: (layer, 0, 0))]
    in_specs += [pl.BlockSpec(memory_space=pl.ANY)] * len(aliased)
    kern = functools.partial(_mlstm_kernel, n_heads=n_heads, bt=bt)
    return pl.pallas_call(
        kern,
        grid=(n_seq // bt, n_chunks),
        in_specs=in_specs,
        out_specs=[
            pl.BlockSpec((None, bt, CHUNK, hdv), lambda t, c: (c, t, 0, 0)),
            stacked((n_heads, dv, dk), layer), state((n_heads, dk)), state((n_heads, LANES)),
        ],
        out_shape=[
            jax.ShapeDtypeStruct((n_chunks, n_seq, CHUNK, hdv), BF),
            jax.ShapeDtypeStruct((n_layers, n_seq, n_heads, dv, dk), F32),
            jax.ShapeDtypeStruct((n_seq, n_heads, dk), F32),
            jax.ShapeDtypeStruct((n_seq, n_heads, LANES), F32),
        ],
        input_output_aliases={6 * bt + 4: 1} if aliased else {},
        compiler_params=_cparams("parallel", "arbitrary"),
        name="mlstm",
    )(*(arrays * bt), c0, n0, m0, norm, *aliased)


def _build_queries(q_sc, cq_ref, wuq_ref, wuk_ref, cos_ref, sin_ref, n_heads, nope, rope, kvl):
    tq = cq_ref.shape[0]
    qa = _dot(cq_ref[...], wuq_ref[...])
    cos = cos_ref[...]
    sin = sin_ref[...]
    for h in range(n_heads):
        qn = qa[:, h * nope:(h + 1) * nope].astype(BF)
        q_sc[h * tq:(h + 1) * tq, 0:kvl] = _dot(qn, wuk_ref[h]).astype(BF)
    per = LANES // rope
    for p in range(n_heads // per):
        base = n_heads * nope + p * LANES
        pe = _rope128(qa[:, base:base + LANES], cos, sin)
        for u in range(per):
            h = p * per + u
            q_sc[h * tq:(h + 1) * tq, kvl:kvl + rope] = pe[:, u * rope:(u + 1) * rope].astype(BF)


def _lane_tile(x, width):
    if width <= LANES:
        return x[:, :width]
    return jnp.concatenate([x] * (width // LANES), axis=1)


def _flash_step(q_sc, keys_next, s_next, s_cur, vals_cur, m_sc, l_sc, acc_sc, sm_scale, rs, mask_chunks=None,
                inline=False):
    n_groups = q_sc.shape[0] // rs

    def group(r, carry):
        r0 = pl.multiple_of(r * rs, rs)
        rows = pl.ds(r0, rs)
        if s_next is not None:
            s_next[rows, :] = _dot_nt(q_sc[rows, :], keys_next)
        if s_cur is not None:
            kb, kvl = vals_cur.shape
            s = s_cur[rows, :] * sm_scale
            if mask_chunks is not None:
                q_chunks, q_chunk0, k_chunk0 = mask_chunks
                chunk_bits = CHUNK.bit_length() - 1
                row = lax.broadcasted_iota(jnp.int32, (rs, kb), 0) + r0
                q_chunk = jnp.bitwise_and(jnp.right_shift(row, chunk_bits), q_chunks - 1) + q_chunk0
                k_chunk = jnp.right_shift(lax.broadcasted_iota(jnp.int32, (rs, kb), 1), chunk_bits) + k_chunk0
                s = jnp.where(k_chunk <= q_chunk, s, -jnp.inf)
            m_prev = m_sc[rows, :]
            m_new = jnp.maximum(m_prev, jnp.max(s, axis=1, keepdims=True))
            alpha = jnp.exp(m_prev - m_new)
            p = jnp.exp(s - _lane_tile(m_new, kb))
            l_sc[rows, :] = alpha * l_sc[rows, :] + jnp.sum(p, axis=1, keepdims=True)
            acc_sc[rows, :] = _lane_tile(alpha, kvl) * acc_sc[rows, :] + _dot(p.astype(BF), vals_cur)
            m_sc[rows, :] = m_new
        return carry

    lax.fori_loop(0, n_groups, group, 0, unroll=True if inline else (2 if n_groups % 2 == 0 else 1))


def _flash_init(m_sc, l_sc, acc_sc):
    m_sc[...] = jnp.full(m_sc.shape, -jnp.inf, F32)
    l_sc[...] = jnp.zeros(l_sc.shape, F32)
    acc_sc[...] = jnp.zeros(acc_sc.shape, F32)


def _flash_finish(o_ref, l_sc, acc_sc, wuv_ref, n_heads, tq):
    vd = wuv_ref.shape[2]
    kvl = acc_sc.shape[1]
    for h in range(n_heads):
        rows = pl.ds(h * tq, tq)
        o_lat = (acc_sc[rows, :] / _lane_tile(l_sc[rows, :], kvl)).astype(BF)
        o_ref[:, h * vd:(h + 1) * vd] = _dot(o_lat, wuv_ref[h]).astype(o_ref.dtype)


def _attn_prompt_kernel(*refs, n_heads, nope, rope, sm_scale, rs, n_riders):
    cq_ref, ckv_ref, kpe_ref, wuq_ref, wuk_ref, wuv_ref, cos_ref, sin_ref = refs[:8]
    rider_in = refs[8:8 + n_riders]
    o_ref = refs[8 + n_riders]
    rider_out = refs[9 + n_riders:9 + 2 * n_riders]
    kcat, q_sc, m_sc, l_sc, acc_sc, s_a, s_b = refs[9 + 2 * n_riders:]
    i = pl.program_id(1)
    tq = cq_ref.shape[0]
    kvl = ckv_ref.shape[1]
    kb = s_a.shape[1]

    for src, dst in zip(rider_in, rider_out):
        dst[...] = src[...].astype(dst.dtype)

    @pl.when(i == 0)
    def _():
        kcat[:, 0:kvl] = ckv_ref[...].astype(BF)
        kcat[:, kvl:kvl + rope] = kpe_ref[...].astype(BF)

    _build_queries(q_sc, cq_ref, wuq_ref, wuk_ref, cos_ref, sin_ref, n_heads, nope, rope, kvl)
    _flash_init(m_sc, l_sc, acc_sc)

    def step(j_next, s_next, j_cur, s_cur, masked=False):
        keys = None if s_next is None else kcat[pl.ds(pl.multiple_of(j_next * kb, kb), kb), :]
        vals = None if s_cur is None else kcat[pl.ds(pl.multiple_of(j_cur * kb, kb), kb), 0:kvl]
        mask = (tq // CHUNK, i * (tq // CHUNK), j_cur * (kb // CHUNK)) if masked else None
        _flash_step(q_sc, keys, s_next, s_cur, vals, m_sc, l_sc, acc_sc, sm_scale, rs, mask,
                    inline=(s_cur is None) or masked)

    last = (i * tq) // kb
    step(0, s_a, None, None)

    def pair(t, carry):
        j = 2 * t
        step(j + 1, s_b, j, s_a)
        step(j + 2, s_a, j + 1, s_b)
        return carry

    lax.fori_loop(0, last // 2, pair, 0)

    @pl.when(last % 2 == 0)
    def _():
        step(None, None, last, s_a, masked=True)
        _flash_finish(o_ref, l_sc, acc_sc, wuv_ref, n_heads, tq)

    @pl.when(last % 2 == 1)
    def _():
        step(last, s_b, last - 1, s_a)
        step(None, None, last, s_b, masked=True)
        _flash_finish(o_ref, l_sc, acc_sc, wuv_ref, n_heads, tq)


def _rider_block(rows, cols, n_steps):
    for n_row_blocks in range(n_steps, 0, -1):
        if n_steps % n_row_blocks or rows % n_row_blocks or cols % (n_steps // n_row_blocks):
            continue
        br, bc = rows // n_row_blocks, cols // (n_steps // n_row_blocks)
        if br % 16 == 0 and bc % LANES == 0:
            return br, bc
    raise ValueError(f"no aligned {n_steps}-block tiling of ({rows}, {cols})")


def _attn_prompt_call(cq, ckv, kpe, wuq, wuk, wuv, cos, sin, layer, n_seq, seq, tq, dims, riders=()):
    ah, nope, rope, kvl, vd = dims["ah"], dims["nope"], dims["rope"], dims["kvl"], dims["vd"]
    ql = cq.shape[1]
    nq = seq // tq
    kb = _pick_tile(seq, 512)
    assert tq % CHUNK == 0 and (tq // CHUNK) & (tq // CHUNK - 1) == 0 and kb % tq == 0
    rs = _pick_tile(ah * tq, 512)
    kern = functools.partial(_attn_prompt_kernel, n_heads=ah, nope=nope, rope=rope,
                             sm_scale=(nope + rope) ** -0.5, rs=rs, n_riders=len(riders))

    def wspec(w):
        return pl.BlockSpec((None,) + w.shape[1:], lambda b, i: (layer,) + (0,) * (w.ndim - 1))

    rider_in, rider_out, rider_shapes = [], [], []
    for arr, arr_layer in riders:
        _, rows, cols = arr.shape
        br, bc = _rider_block(rows, cols, n_seq * nq)
        ncb = cols // bc
        rider_in.append(pl.BlockSpec((None, br, bc), functools.partial(
            lambda b, i, ncb, arr_layer: (arr_layer, (b * nq + i) // ncb, (b * nq + i) % ncb), ncb=ncb, arr_layer=arr_layer)))
        rider_out.append(pl.BlockSpec((br, bc), functools.partial(
            lambda b, i, ncb: ((b * nq + i) // ncb, (b * nq + i) % ncb), ncb=ncb)))
        rider_shapes.append(jax.ShapeDtypeStruct((rows, cols), BF))

    return pl.pallas_call(
        kern,
        grid=(n_seq, nq),
        in_specs=[
            pl.BlockSpec((tq, ql), lambda b, i: (b * nq + i, 0)),
            pl.BlockSpec((seq, kvl), lambda b, i: (b, 0)),
            pl.BlockSpec((seq, rope), lambda b, i: (b, 0)),
            wspec(wuq), wspec(wuk), wspec(wuv),
            pl.BlockSpec((tq, LANES), lambda b, i: (b * nq + i, 0)),
            pl.BlockSpec((tq, LANES), lambda b, i: (b * nq + i, 0)),
        ] + rider_in,
        out_specs=[pl.BlockSpec((tq, ah * vd), lambda b, i: (b * nq + i, 0))] + rider_out,
        out_shape=[jax.ShapeDtypeStruct((n_seq * seq, ah * vd), BF)] + rider_shapes,
        scratch_shapes=[
            pltpu.VMEM((seq, kvl + rope), BF),
            pltpu.VMEM((ah * tq, kvl + rope), BF),
            pltpu.VMEM((ah * tq, LANES), F32),
            pltpu.VMEM((ah * tq, LANES), F32),
            pltpu.VMEM((ah * tq, kvl), F32),
            pltpu.VMEM((ah * tq, kb), F32),
            pltpu.VMEM((ah * tq, kb), F32),
        ],
        compiler_params=_cparams("parallel", "arbitrary"),
        name="mla_prompt",
    )(cq, ckv, kpe, wuq, wuk, wuv, cos, sin, *[arr for arr, _ in riders])


def _attn_sample_kernel(cq_ref, ckv_ref, kpe_ref, pckv_ref, pkpe_ref, wuq_ref, wuk_ref, wuv_ref, cos_ref, sin_ref,
                        o_ref, kcat, q_sc, m_sc, l_sc, acc_sc, s_a, s_b, s_n, *, n_heads, nope, rope, sm_scale):
    tq = cq_ref.shape[0]
    kvl = ckv_ref.shape[1]
    past = pckv_ref.shape[0]
    kb = s_a.shape[1]
    kcat[0:past, 0:kvl] = pckv_ref[...].astype(BF)
    kcat[0:past, kvl:kvl + rope] = pkpe_ref[...].T.astype(BF)
    kcat[past:past + tq, 0:kvl] = ckv_ref[...].astype(BF)
    kcat[past:past + tq, kvl:kvl + rope] = kpe_ref[...].astype(BF)
    _build_queries(q_sc, cq_ref, wuq_ref, wuk_ref, cos_ref, sin_ref, n_heads, nope, rope, kvl)
    _flash_init(m_sc, l_sc, acc_sc)
    rs = q_sc.shape[0]

    blocks = [(j * kb, kb, (s_a, s_b)[j % 2]) for j in range(past // kb)] + [(past, tq, s_n)]
    for cur, nxt in zip([None] + blocks, blocks + [None]):
        keys, s_next = (None, None) if nxt is None else (kcat[nxt[0]:nxt[0] + nxt[1], :], nxt[2])
        vals, s_cur = (None, None) if cur is None else (kcat[cur[0]:cur[0] + cur[1], 0:kvl], cur[2])
        _flash_step(q_sc, keys, s_next, s_cur, vals, m_sc, l_sc, acc_sc, sm_scale, rs)
    _flash_finish(o_ref, l_sc, acc_sc, wuv_ref, n_heads, tq)


def _attn_sample_call(cq, ckv, kpe, cache_ckv, cache_kpe, wuq, wuk, wuv, cos, sin, layer, row0, n_seq, tq, dims):
    ah, nope, rope, kvl, vd = dims["ah"], dims["nope"], dims["rope"], dims["kvl"], dims["vd"]
    ql = cq.shape[1]
    past = cache_ckv.shape[2]
    kb = math.gcd(past, 256)
    kern = functools.partial(_attn_sample_kernel, n_heads=ah, nope=nope, rope=rope,
                             sm_scale=(nope + rope) ** -0.5)

    def wspec(w):
        return pl.BlockSpec((None,) + w.shape[1:], lambda b: (layer,) + (0,) * (w.ndim - 1))

    def tok(n):
        return pl.BlockSpec((tq, n), lambda b: (row0 + b, 0))

    return pl.pallas_call(
        kern,
        grid=(n_seq,),
        in_specs=[
            tok(ql), tok(kvl), tok(rope),
            pl.BlockSpec((None, None, past, kvl), lambda b: (layer, b, 0, 0)),
            pl.BlockSpec((None, None, rope, past), lambda b: (layer, b, 0, 0)),
            wspec(wuq), wspec(wuk), wspec(wuv), tok(LANES), tok(LANES),
        ],
        out_specs=pl.BlockSpec((tq, ah * vd), lambda b: (b, 0)),
        out_shape=jax.ShapeDtypeStruct((n_seq * tq, ah * vd), BF),
        scratch_shapes=[
            pltpu.VMEM((past + tq, kvl + rope), BF),
            pltpu.VMEM((ah * tq, kvl + rope), BF),
            pltpu.VMEM((ah * tq, LANES), F32),
            pltpu.VMEM((ah * tq, LANES), F32),
            pltpu.VMEM((ah * tq, kvl), F32),
            pltpu.VMEM((ah * tq, kb), F32),
            pltpu.VMEM((ah * tq, kb), F32),
            pltpu.VMEM((ah * tq, tq), F32),
        ],
        compiler_params=_cparams("parallel"),
        name="mla_sample",
    )(cq, ckv, kpe, cache_ckv, cache_kpe, wuq, wuk, wuv, cos, sin)


def _outproj_kernel(x_ref, hmp_ref, hms_ref, oap_ref, oas_ref, gt_ref, wm_ref, wa_ref, o_ref, *, n_prompt_tiles):
    i = pl.program_id(0)
    tm = x_ref.shape[0]

    def project(hm, oa):
        y = _dot(hm, wm_ref[...]) + _dot(oa, wa_ref[...])
        for g in range(tm // CHUNK):
            rows = pl.ds(g * CHUNK, CHUNK)
            o_ref[rows, :] = x_ref[rows, :] + gt_ref[g:g + 1, :] * y[g * CHUNK:(g + 1) * CHUNK, :]

    @pl.when(i < n_prompt_tiles)
    def _():
        project(hmp_ref[...].reshape(tm, hmp_ref.shape[-1]), oap_ref[...])

    @pl.when(i >= n_prompt_tiles)
    def _():
        project(hms_ref[...], oas_ref[...])


def _outproj_call(x, hm_p, hm_s, oa_p, oa_s, mod, w_out, layer, tm, seq):
    m, d = x.shape
    wm = hm_s.shape[1]
    wa = oa_s.shape[1]
    assert wm == wa and seq % tm == 0
    ng = tm // CHUNK
    npt = oa_p.shape[0] // tm
    tps = seq // tm

    def prompt_tile(i):
        return jnp.minimum(i, npt - 1)

    def sample_tile(i):
        return jnp.maximum(i - npt, 0)

    return pl.pallas_call(
        functools.partial(_outproj_kernel, n_prompt_tiles=npt),
        grid=(m // tm,),
        in_specs=[
            pl.BlockSpec((tm, d), lambda i: (i, 0)),
            pl.BlockSpec((ng, None, CHUNK, wm), lambda i: (prompt_tile(i) % tps, prompt_tile(i) // tps, 0, 0)),
            pl.BlockSpec((tm, wm), lambda i: (sample_tile(i), 0)),
            pl.BlockSpec((tm, wa), lambda i: (prompt_tile(i), 0)),
            pl.BlockSpec((tm, wa), lambda i: (sample_tile(i), 0)),
            pl.BlockSpec((None, ng, d), lambda i: (layer, i, 5)),
            pl.BlockSpec((None, wm, d), lambda i: (layer, 0, 0)),
            pl.BlockSpec((None, wa, d), lambda i: (layer, 1, 0)),
        ],
        out_specs=pl.BlockSpec((tm, d), lambda i: (i, 0)),
        out_shape=jax.ShapeDtypeStruct((m, d), F32),
        compiler_params=_cparams("parallel"),
        name="out_proj",
    )(x, hm_p, hm_s, oa_p, oa_s, mod, w_out, w_out)


def _rope_tables(pos, rope):
    half = rope // 2
    freqs = ROPE_THETA ** (-jnp.arange(half, dtype=F32) / half)
    ang = pos.astype(F32)[:, None] * freqs[None, :]
    cos = jnp.cos(ang)
    sin = jnp.sin(ang)
    reps = LANES // rope
    return jnp.tile(jnp.concatenate([cos, cos], axis=1), (1, reps)), jnp.tile(jnp.concatenate([-sin, sin], axis=1), (1, reps))


def _pick_tile(m, cap):
    t = cap
    while m % t:
        t //= 2
    return t


def kernel(x_prompt, x_sample, c_prompt, c_sample, cache_ckv, cache_kpe, state_C, state_n, state_m, mod_w, mod_b, ln_ffn1, ffn1_w_in, ffn1_w_out, ln_mix, w_in, mlstm_b_i, mlstm_b_f, mlstm_norm, q_norm, w_uq, kv_norm, w_uk, w_uv, w_out, ln_ffn2, ffn2_w_in, ffn2_w_out, final_norm):
    bp, sp, d = x_prompt.shape
    bs, ss, _ = x_sample.shape
    depth = mod_w.shape[0]
    past = cache_ckv.shape[2]
    mh, dv, dk = state_C.shape[2:]
    kvl, ah, nope = w_uk.shape[1:]
    vd = w_uv.shape[3]
    rope = cache_kpe.shape[3]
    ql = q_norm.shape[1]
    hdk, hdv = mh * dk, mh * dv
    dff = ffn1_w_out.shape[1]
    dims = dict(mh=mh, dk=dk, dv=dv, hdk=hdk, hdv=hdv, ql=ql, kvl=kvl, rope=rope, ah=ah, nope=nope, vd=vd)
    assert sp % CHUNK == 0 and ss == CHUNK and LANES % rope == 0 and ah % (LANES // rope) == 0
    mp, ms = bp * sp, bs * ss
    m = mp + ms
    tm = _pick_tile(math.gcd(mp, ms), 512)
    tm_ffn = _pick_tile(math.gcd(mp, ms), 1024)
    ffn_tiles_p, ffn_tiles_s = mp // tm_ffn, ms // tm_ffn
    tf = _pick_tile(dff, 512)
    tq = _pick_tile(sp, 256)

    cg = jnp.concatenate([jnp.repeat(c_prompt, sp // CHUNK, axis=0), jnp.repeat(c_sample, ss // CHUNK, axis=0)], axis=0)
    pos = jnp.concatenate([jnp.tile(jnp.arange(sp), bp), jnp.tile(past + jnp.arange(ss), bs)])
    cos, sin = _rope_tables(pos, rope)

    o_mq, o_mk, o_mv, o_mo = 0, hdk, 2 * hdk, 2 * hdk + hdv
    o_mi = o_mo + hdv
    o_mf, o_qa = o_mi + mh, o_mi + 2 * mh
    o_kva, o_pe = o_qa + ql, o_qa + ql + kvl
    wqk = w_in[:, :, o_mq:o_mv].astype(BF)
    wv = w_in[:, :, o_mv:o_mo].astype(BF)
    wo = w_in[:, :, o_mo:o_mi].astype(BF)
    zpad = lambda n: jnp.zeros((depth, d, n), w_in.dtype)
    wa = jnp.concatenate([w_in[:, :, o_qa:o_pe + rope], zpad(LANES - rope), w_in[:, :, o_mi:o_qa], zpad(LANES - 2 * mh)], axis=2).astype(BF)
    gate_bias = jnp.concatenate([mlstm_b_i, mlstm_b_f, jnp.zeros((depth, LANES - 2 * mh), F32)], axis=1).reshape(depth, 1, LANES)
    wuq4 = w_uq.reshape(depth, ql, ah, nope + rope)
    wuq = jnp.concatenate([wuq4[..., :nope].reshape(depth, ql, ah * nope), wuq4[..., nope:].reshape(depth, ql, ah * rope)], axis=2).astype(BF)
    wuk = jnp.transpose(w_uk, (0, 2, 3, 1)).astype(BF)
    wuv = jnp.transpose(w_uv, (0, 2, 1, 3)).astype(BF)
    w_out_b = w_out.astype(BF)
    f1_in, f1_out = ffn1_w_in[0].astype(BF), ffn1_w_out[0].astype(BF)
    r3 = lambda a: a.reshape(depth, 1, a.shape[1])

    mod = _mod_call(cg, mod_w, mod_b)

    zeros_c = jnp.zeros((1, bp, mh, dv, dk), F32)
    zeros_n = jnp.zeros((1, bp, mh, dk), F32)
    zeros_m = jnp.zeros((1, bp, mh, LANES), F32)
    m0_s = jnp.broadcast_to(state_m[..., None], state_m.shape + (LANES,))
    cache_kpe_t = jnp.swapaxes(cache_kpe, 2, 3)

    outs = {k: [] for k in ("p_kpe", "p_n", "p_m", "s_kpe", "s_n", "s_m")}
    ckv_stacks = p_c = s_c = None
    for l in range(depth):
        if l == 0:
            x = _ffn_call(x_prompt.reshape(mp, d), mod, r3(ln_ffn1), f1_in, f1_out, l, 0, tm_ffn, tf, out_rows=m)
            x = _ffn_call(x_sample.reshape(ms, d), mod, r3(ln_ffn1), f1_in, f1_out, l, 0, tm_ffn, tf,
                          stream_tile0=ffn_tiles_p, out_tile0=ffn_tiles_p, into=x)
        else:
            x = _ffn_call(x, mod, r3(ln_ffn1), f1_in, f1_out, l, 0, tm_ffn, tf)
        q, k, v, og, cq, ckv, kpe, gates, *ckv_stacks = _inproj_call(
            x, mod, r3(ln_mix), wqk, wv, wo, wa, r3(q_norm), r3(kv_norm), gate_bias, cos, sin, l, tm, dims,
            depth, mp, ckv_stacks)
        gates_r = jnp.transpose(gates[:, :2 * mh].reshape(m // CHUNK, CHUNK, 2 * mh), (0, 2, 1))
        nrm = r3(mlstm_norm)
        hm_p, p_c, n_p, m_p = _mlstm_call(q, k, v, og, gates, gates_r, zeros_c, zeros_n, zeros_m, nrm,
                                          l, 0, bp, sp // CHUNK, mh, 0, depth, p_c)
        hm_s, s_c, n_s, m_s = _mlstm_call(q, k, v, og, gates, gates_r, state_C, state_n, m0_s, nrm,
                                          l, mp // CHUNK, bs, ss // CHUNK, mh, l, depth, s_c)
        riders = [(ffn2_w_in, l), (ffn2_w_out, l)]
        if l + 1 < depth:
            riders += [(ffn1_w_in, l + 1), (ffn1_w_out, l + 1)]
        oa_p, f2_in, f2_out, *next_f1 = _attn_prompt_call(cq, ckv, kpe, wuq, wuk, wuv, cos, sin, l, bp, sp, tq, dims, riders)
        if next_f1:
            f1_in, f1_out = next_f1
        oa_s = _attn_sample_call(cq, ckv, kpe, cache_ckv, cache_kpe_t, wuq, wuk, wuv, cos, sin, l, mp // ss, bs, ss, dims)
        x = _outproj_call(x, hm_p, hm_s.reshape(ms, hdv), oa_p, oa_s, mod, w_out_b, l, tm, sp)
        if l + 1 < depth:
            x = _ffn_call(x, mod, r3(ln_ffn2), f2_in, f2_out, l, 6, tm_ffn, tf)
        else:
            fn = final_norm.reshape(1, d)
            y_prompt = _ffn_call(x, mod, r3(ln_ffn2), f2_in, f2_out, l, 6, tm_ffn, tf,
                                 n_tiles=ffn_tiles_p, final_gain=fn).reshape(bp, sp, d)
            y_sample = _ffn_call(x, mod, r3(ln_ffn2), f2_in, f2_out, l, 6, tm_ffn, tf, src_tile0=ffn_tiles_p,
                                 n_tiles=ffn_tiles_s, stream_tile0=ffn_tiles_p, final_gain=fn).reshape(bs, ss, d)
        outs["p_kpe"].append(kpe[:mp].reshape(bp, sp, rope))
        outs["s_kpe"].append(kpe[mp:].reshape(bs, ss, rope))
        outs["p_n"].append(n_p)
        outs["p_m"].append(m_p[..., 0])
        outs["s_n"].append(n_s)
        outs["s_m"].append(m_s[..., 0])

    st = {k: jnp.stack(v) for k, v in outs.items()}
    p_ckv = ckv_stacks[0].reshape(depth, bp, sp, kvl)
    s_ckv = ckv_stacks[1].reshape(depth, bs, ss, kvl)
    return (y_prompt, y_sample, p_ckv, st["p_kpe"], p_c, st["p_n"], st["p_m"],
            s_ckv, st["s_kpe"], s_c, st["s_n"], st["s_m"])
```

```python
import functools
import math

import jax
import jax.numpy as jnp
from jax import lax
from jax.experimental import pallas as pl
from jax.experimental.pallas import tpu as pltpu

CHUNK = 64
EPS = 1e-6
ROPE_THETA = 10000.0
N_MOD = 9
LANES = 128
VMEM_LIMIT = 58 * 1024 * 1024

BF = jnp.bfloat16
F32 = jnp.float32


def _cparams(*sem):
    return pltpu.CompilerParams(dimension_semantics=sem, vmem_limit_bytes=VMEM_LIMIT)


def _dot(a, b):
    return jnp.dot(a, b, preferred_element_type=F32)


def _dot_nt(a, b):
    return lax.dot_general(a, b, (((1,), (1,)), ((), ())), preferred_element_type=F32)


def _dot_tn(a, b):
    return lax.dot_general(a, b, (((0,), (0,)), ((), ())), preferred_element_type=F32)


def _rms(x, g):
    ms = jnp.mean(x * x, axis=-1, keepdims=True)
    return x * lax.rsqrt(ms + EPS) * g


def _silu(x):
    return x * jax.nn.sigmoid(x)


def _rope128(pe, cos, sin):
    lane = lax.broadcasted_iota(jnp.int32, pe.shape, 1)
    first_half = jnp.bitwise_and(lane, 63) < 32
    swapped = jnp.where(first_half, pltpu.roll(pe, 96, 1), pltpu.roll(pe, 32, 1))
    return pe * cos + swapped * sin


def _mod_kernel(c_ref, w_ref, b_ref, o_ref):
    a = _silu(c_ref[...]).astype(BF)
    o_ref[...] = _dot(a, w_ref[...].astype(BF)) + b_ref[...]


def _mod_call(cg, mod_w, mod_b):
    depth, d, nd = mod_w.shape
    g = cg.shape[0]
    tn = 1024
    return pl.pallas_call(
        _mod_kernel,
        grid=(depth, nd // tn),
        in_specs=[
            pl.BlockSpec((g, d), lambda l, j: (0, 0)),
            pl.BlockSpec((None, d, tn), lambda l, j: (l, 0, j)),
            pl.BlockSpec((None, 1, tn), lambda l, j: (l, 0, j)),
        ],
        out_specs=pl.BlockSpec((None, g, tn), lambda l, j: (l, 0, j)),
        out_shape=jax.ShapeDtypeStruct((depth, g, nd), F32),
        compiler_params=_cparams("parallel", "parallel"),
        name="adaln_mod",
    )(cg, mod_w, mod_b.reshape(depth, 1, nd))


def _norm_mod_to(h_ref, x_ref, ln_ref, sh_ref, sc_ref):
    ln = ln_ref[...]
    for g in range(x_ref.shape[0] // CHUNK):
        rows = pl.ds(g * CHUNK, CHUNK)
        y = _rms(x_ref[rows, :], ln)
        h_ref[rows, :] = (y * (1.0 + sc_ref[g:g + 1, :]) + sh_ref[g:g + 1, :]).astype(h_ref.dtype)


def _ffn_kernel(*refs, final_norm, aliased):
    x_ref, sh_ref, sc_ref, gt_ref, ln_ref, wg_ref, wu_ref, wo_ref = refs[:8]
    fg_ref = refs[8] if final_norm else None
    o_ref, h_sc = refs[8 + final_norm + aliased:]
    j = pl.program_id(1)
    nj = pl.num_programs(1)

    def up_down(h):
        a = (_silu(_dot(h, wg_ref[...])) * _dot(h, wu_ref[...])).astype(BF)
        return _dot(a, wo_ref[...])

    def residual(acc):
        for g in range(x_ref.shape[0] // CHUNK):
            rows = pl.ds(g * CHUNK, CHUNK)
            y = x_ref[rows, :] + (0.5 * gt_ref[g:g + 1, :]) * acc[g * CHUNK:(g + 1) * CHUNK, :]
            o_ref[rows, :] = _rms(y, fg_ref[...]) if final_norm else y

    @pl.when(j == 0)
    def _():
        _norm_mod_to(h_sc, x_ref, ln_ref, sh_ref, sc_ref)
        o_ref[...] = up_down(h_sc[...])

    @pl.when(jnp.logical_and(j > 0, j < nj - 1))
    def _():
        o_ref[...] += up_down(h_sc[...])

    @pl.when(jnp.logical_and(j > 0, j == nj - 1))
    def _():
        residual(o_ref[...] + up_down(h_sc[...]))

    @pl.when(nj == 1)
    def _():
        residual(o_ref[...])


def _ffn_call(x, mod, ln, w_in, w_out, layer, k0, tm, tf, *, src_tile0=0, n_tiles=None, stream_tile0=0,
              out_rows=None, out_tile0=0, into=None, final_gain=None):
    d = x.shape[1]
    f = w_out.shape[0]
    ng = tm // CHUNK
    nf = f // tf
    n_tiles = x.shape[0] // tm if n_tiles is None else n_tiles
    out_rows = (into.shape[0] if into is not None else n_tiles * tm) if out_rows is None else out_rows

    def mod_spec(k):
        return pl.BlockSpec((None, ng, d), lambda i, j: (layer, i + stream_tile0, k))

    in_specs = [
        pl.BlockSpec((tm, d), lambda i, j: (i + src_tile0, 0)),
        mod_spec(k0), mod_spec(k0 + 1), mod_spec(k0 + 2),
        pl.BlockSpec((None, 1, d), lambda i, j: (layer, 0, 0)),
        pl.BlockSpec((d, tf), lambda i, j: (0, j)),
        pl.BlockSpec((d, tf), lambda i, j: (0, j + nf)),
        pl.BlockSpec((tf, d), lambda i, j: (j, 0)),
    ]
    operands = [x, mod, mod, mod, ln, w_in, w_in, w_out]
    if final_gain is not None:
        in_specs.append(pl.BlockSpec((1, d), lambda i, j: (0, 0)))
        operands.append(final_gain)
    aliases = {}
    if into is not None:
        aliases = {len(operands): 0}
        in_specs.append(pl.BlockSpec(memory_space=pl.ANY))
        operands.append(into)
    kern = functools.partial(_ffn_kernel, final_norm=final_gain is not None, aliased=into is not None)
    return pl.pallas_call(
        kern,
        grid=(n_tiles, nf),
        in_specs=in_specs,
        out_specs=pl.BlockSpec((tm, d), lambda i, j: (i + out_tile0, 0)),
        out_shape=jax.ShapeDtypeStruct((out_rows, d), F32),
        scratch_shapes=[pltpu.VMEM((tm, d), BF)],
        input_output_aliases=aliases,
        compiler_params=_cparams("parallel", "arbitrary"),
        name="ffn",
    )(*operands)


def _inproj_kernel(*refs, n_heads, k_scale, n_aliased, n_prompt_tiles):
    (x_ref, sh_ref, sc_ref, ln_ref, wqk_ref, wv_ref, wo_ref, wa_ref, qn_ref, kvn_ref, gb_ref,
     cos_ref, sin_ref) = refs[:13]
    (q_ref, k_ref, v_ref, og_ref, cq_ref, ckv_ref, kpe_ref, gt_ref, ckvp_ref, ckvs_ref,
     h_sc) = refs[13 + n_aliased:]
    _norm_mod_to(h_sc, x_ref, ln_ref, sh_ref, sc_ref)
    h = h_sc[...]
    hdk = q_ref.shape[1]
    za = _dot(h, wa_ref[...])
    og_ref[...] = jax.nn.sigmoid(_dot(h, wo_ref[...]))
    zqk = _dot(h, wqk_ref[...])
    q_ref[...] = zqk[:, :hdk].astype(BF)
    k_ref[...] = (zqk[:, hdk:] * k_scale).astype(BF)
    v_ref[...] = _dot(h, wv_ref[...]).astype(BF)
    ql = cq_ref.shape[1]
    kvl = ckv_ref.shape[1]
    cq_ref[...] = _rms(za[:, :ql], qn_ref[...]).astype(BF)
    ckv = _rms(za[:, ql:ql + kvl], kvn_ref[...])
    ckv_ref[...] = ckv
    pe = _rope128(za[:, ql + kvl:ql + kvl + LANES], cos_ref[...], sin_ref[...])
    kpe_ref[...] = pe[:, :kpe_ref.shape[1]]
    zg = za[:, ql + kvl + LANES:] + gb_ref[...]
    lane = lax.broadcasted_iota(jnp.int32, zg.shape, 1)
    log_sig = jnp.minimum(zg, 0.0) - jnp.log1p(jnp.exp(-jnp.abs(zg)))
    gt_ref[...] = jnp.where(lane < n_heads, zg, log_sig)

    i = pl.program_id(0)

    @pl.when(i < n_prompt_tiles)
    def _():
        ckvp_ref[...] = ckv_ref[...]

    @pl.when(i >= n_prompt_tiles)
    def _():
        ckvs_ref[...] = ckv_ref[...]


def _inproj_call(x, mod, ln, wqk, wv, wo, wa, q_norm, kv_norm, gate_bias, cos, sin, layer, tm, dims,
                 n_layers, rows_prompt, ckv_stacks=None):
    m, d = x.shape
    ng = tm // CHUNK
    hdk, hdv, ql, kvl, rope = dims["hdk"], dims["hdv"], dims["ql"], dims["kvl"], dims["rope"]
    npt = rows_prompt // tm
    aliased = [] if ckv_stacks is None else list(ckv_stacks)

    def mod_spec(k):
        return pl.BlockSpec((None, ng, d), lambda i: (layer, i, k))

    def wspec(w):
        return pl.BlockSpec((None,) + w.shape[1:], lambda i: (layer, 0, 0))

    def row(n):
        return pl.BlockSpec((tm, n), lambda i: (i, 0))

    kern = functools.partial(_inproj_kernel, n_heads=dims["mh"], k_scale=dims["dk"] ** -0.5,
                             n_aliased=len(aliased), n_prompt_tiles=npt)
    return pl.pallas_call(
        kern,
        grid=(m // tm,),
        in_specs=[
            row(d), mod_spec(3), mod_spec(4), wspec(ln), wspec(wqk), wspec(wv), wspec(wo), wspec(wa),
            wspec(q_norm), wspec(kv_norm), wspec(gate_bias), row(LANES), row(LANES),
        ] + [pl.BlockSpec(memory_space=pl.ANY)] * len(aliased),
        out_specs=[
            row(hdk), row(hdk), row(hdv), row(hdv), row(ql), row(kvl), row(rope), row(LANES),
            pl.BlockSpec((None, tm, kvl), lambda i: (layer, jnp.minimum(i, npt - 1), 0)),
            pl.BlockSpec((None, tm, kvl), lambda i: (layer, jnp.maximum(i - npt, 0), 0)),
        ],
        out_shape=[
            jax.ShapeDtypeStruct((m, hdk), BF), jax.ShapeDtypeStruct((m, hdk), BF),
            jax.ShapeDtypeStruct((m, hdv), BF), jax.ShapeDtypeStruct((m, hdv), F32),
            jax.ShapeDtypeStruct((m, ql), BF), jax.ShapeDtypeStruct((m, kvl), F32),
            jax.ShapeDtypeStruct((m, rope), F32), jax.ShapeDtypeStruct((m, LANES), F32),
            jax.ShapeDtypeStruct((n_layers, rows_prompt, kvl), F32),
            jax.ShapeDtypeStruct((n_layers, m - rows_prompt, kvl), F32),
        ],
        scratch_shapes=[pltpu.VMEM((tm, d), BF)],
        input_output_aliases={13 + a: 8 + a for a in range(len(aliased))},
        compiler_params=_cparams("arbitrary"),
        name="in_proj",
    )(x, mod, mod, ln, wqk, wv, wo, wa, q_norm, kv_norm, gate_bias, cos, sin, *aliased)


def _mlstm_kernel(*refs, n_heads, bt):
    tok = [refs[6 * u:6 * u + 6] for u in range(bt)]
    c0_ref, n0_ref, m0_ref, nrm_ref = refs[6 * bt:6 * bt + 4]
    hm_ref, c_ref, n_ref, m_ref = refs[-4:]
    c_idx = pl.program_id(1)

    @pl.when(c_idx == 0)
    def _():
        c_ref[...] = c0_ref[...]
        n_ref[...] = n0_ref[...]
        m_ref[...] = m0_ref[...]

    L = tok[0][0].shape[0]
    dk = tok[0][0].shape[1] // n_heads
    dv = tok[0][2].shape[1] // n_heads
    chains = [(u, h) for u in range(bt) for h in range(n_heads)]
    nc = len(chains)

    def stack(fn):
        return jnp.concatenate([fn(u, h) for u, h in chains], axis=0)

    def per_chain_last(col):
        last = col.reshape(nc, L, 1)[:, L - 1:L, :]
        return jnp.broadcast_to(last, (nc, L, 1)).reshape(nc * L, 1)

    t_idx = jnp.bitwise_and(lax.broadcasted_iota(jnp.int32, (nc * L, L), 0), L - 1)
    s_idx = lax.broadcasted_iota(jnp.int32, (nc * L, L), 1)
    causal = s_idx <= t_idx
    ig_c = stack(lambda u, h: tok[u][4][:, h:h + 1])
    lf_c = stack(lambda u, h: tok[u][4][:, n_heads + h:n_heads + h + 1])
    ig_r = stack(lambda u, h: jnp.broadcast_to(tok[u][5][h:h + 1, :], (L, L)))
    lf_r = stack(lambda u, h: jnp.broadcast_to(tok[u][5][n_heads + h:n_heads + h + 1, :], (L, L)))
    m_prev = stack(lambda u, h: jnp.broadcast_to(m_ref[u, h:h + 1, 0:1], (L, 1)))
    b_c = jnp.sum(jnp.where(causal, lf_r, 0.0), axis=1, keepdims=True)
    b_r = jnp.sum(jnp.where(causal, 0.0, lf_c).reshape(nc, L, L), axis=1, keepdims=True)
    b_r = jnp.broadcast_to(b_r, (nc, L, L)).reshape(nc * L, L) + lf_r
    b_last = per_chain_last(b_c)
    dmat = jnp.where(causal, b_c - b_r + ig_r, -jnp.inf)
    g_c = b_c + m_prev
    m_t = jnp.maximum(g_c, jnp.max(dmat, axis=1, keepdims=True))
    w_intra = jnp.exp(dmat - m_t)
    w_inter = jnp.exp(g_c - m_t)
    m_new = per_chain_last(m_t)
    wa = jnp.exp(b_last - b_c + ig_c - m_new)
    decay = jnp.exp(b_last + m_prev - m_new)

    qs = {c: tok[c[0]][0][:, c[1] * dk:(c[1] + 1) * dk] for c in chains}
    ks = {c: tok[c[0]][1][:, c[1] * dk:(c[1] + 1) * dk] for c in chains}
    vs = {c: tok[c[0]][2][:, c[1] * dv:(c[1] + 1) * dv] for c in chains}
    c_prev = {c: c_ref[c[0], c[1]] for c in chains}
    n_prev = {c: n_ref[c[0], c[1]:c[1] + 1, :] for c in chains}
    s = stack(lambda u, h: _dot_nt(qs[u, h], ks[u, h])) * w_intra
    qc = stack(lambda u, h: _dot_nt(qs[u, h], c_prev[u, h].astype(BF)))
    s_bf = s.astype(BF)
    sv = jnp.concatenate([_dot(s_bf[i * L:(i + 1) * L, :], vs[c]) for i, c in enumerate(chains)], axis=0)
    qn = jnp.sum(stack(lambda u, h: qs[u, h].astype(F32) * n_prev[u, h]), axis=1, keepdims=True)
    num = sv + w_inter * qc
    nq = jnp.sum(s, axis=1, keepdims=True) + w_inter * qn
    r = 1.0 / jnp.maximum(jnp.abs(nq), jnp.exp(-m_t))
    y = num * stack(lambda u, h: tok[u][3][:, h * dv:(h + 1) * dv])
    x = y * (r * lax.rsqrt(jnp.mean(y * y, axis=-1, keepdims=True) * (r * r) + EPS))
    wk = wa * stack(lambda u, h: ks[u, h].astype(F32))
    wk_bf = wk.astype(BF)
    for i, (u, h) in enumerate(chains):
        rows = slice(i * L, (i + 1) * L)
        hm_ref[u, :, h * dv:(h + 1) * dv] = (x[rows, :] * nrm_ref[:, h * dv:(h + 1) * dv]).astype(hm_ref.dtype)
        dec = decay[i * L:i * L + 1, :]
        c_ref[u, h] = dec * c_prev[u, h] + _dot_tn(vs[u, h], wk_bf[rows, :])
        n_ref[u, h:h + 1, :] = dec * n_prev[u, h] + jnp.sum(wk[rows, :], axis=0, keepdims=True)
        m_ref[u, h:h + 1, :] = jnp.broadcast_to(m_new[i * L:i * L + 1, :], (1, m_ref.shape[2]))


def _mlstm_call(q, k, v, og, gates_c, gates_r, c0, n0, m0, norm, layer, row0, n_seq, n_chunks, n_heads,
                state_layer, n_layers, c_stack=None):
    hdk = q.shape[1]
    hdv = v.shape[1]
    dv, dk = c0.shape[-2:]
    g2 = gates_r.shape[1]
    bt = _pick_tile(n_seq, 4)
    aliased = [] if c_stack is None else [c_stack]

    def chunked(a):
        return a.reshape(a.shape[0] // CHUNK, CHUNK, a.shape[1])

    arrays = [chunked(q), chunked(k), chunked(v), chunked(og), chunked(gates_c), gates_r]
    shapes = [(CHUNK, hdk), (CHUNK, hdk), (CHUNK, hdv), (CHUNK, hdv), (CHUNK, LANES), (g2, CHUNK)]

    def tok(u, shape):
        return pl.BlockSpec((None,) + shape, lambda t, c: (row0 + (t * bt + u) * n_chunks + c, 0, 0))

    def state(shape):
        return pl.BlockSpec((bt,) + shape, lambda t, c: (t,) + (0,) * len(shape))

    def stacked(shape, which):
        return pl.BlockSpec((None, bt) + shape, lambda t, c: (which, t) + (0,) * len(shape))

    in_specs = [tok(u, shape) for u in range(bt) for shape in shapes]
    in_specs += [stacked((n_heads, dv, dk), state_layer), stacked((n_heads, dk), state_layer),
                 stacked((n_heads, LANES), state_layer), pl.BlockSpec((None, 1, hdv), lambda t, c: (layer, 0, 0))]
    in_specs += [pl.BlockSpec(memory_space=pl.ANY)] * len(aliased)
    kern = functools.partial(_mlstm_kernel, n_heads=n_heads, bt=bt)
    return pl.pallas_call(
        kern,
        grid=(n_seq // bt, n_chunks),
        in_specs=in_specs,
        out_specs=[
            pl.BlockSpec((None, bt, CHUNK, hdv), lambda t, c: (c, t, 0, 0)),
            stacked((n_heads, dv, dk), layer), state((n_heads, dk)), state((n_heads, LANES)),
        ],
        out_shape=[
            jax.ShapeDtypeStruct((n_chunks, n_seq, CHUNK, hdv), BF),
            jax.ShapeDtypeStruct((n_layers, n_seq, n_heads, dv, dk), F32),
            jax.ShapeDtypeStruct((n_seq, n_heads, dk), F32),
            jax.ShapeDtypeStruct((n_seq, n_heads, LANES), F32),
        ],
        input_output_aliases={6 * bt + 4: 1} if aliased else {},
        compiler_params=_cparams("parallel", "arbitrary"),
        name="mlstm",
    )(*(arrays * bt), c0, n0, m0, norm, *aliased)


def _build_queries(q_sc, cq_ref, wuq_ref, wuk_ref, cos_ref, sin_ref, n_heads, nope, rope, kvl):
    tq = cq_ref.shape[0]
    qa = _dot(cq_ref[...], wuq_ref[...])
    cos = cos_ref[...]
    sin = sin_ref[...]
    for h in range(n_heads):
        qn = qa[:, h * nope:(h + 1) * nope].astype(BF)
        q_sc[h * tq:(h + 1) * tq, 0:kvl] = _dot(qn, wuk_ref[h]).astype(BF)
    per = LANES // rope
    for p in range(n_heads // per):
        base = n_heads * nope + p * LANES
        pe = _rope128(qa[:, base:base + LANES], cos, sin)
        for u in range(per):
            h = p * per + u
            q_sc[h * tq:(h + 1) * tq, kvl:kvl + rope] = pe[:, u * rope:(u + 1) * rope].astype(BF)


def _lane_tile(x, width):
    if width <= LANES:
        return x[:, :width]
    return jnp.concatenate([x] * (width // LANES), axis=1)


def _flash_step(q_sc, keys_next, s_next, s_cur, vals_cur, m_sc, l_sc, acc_sc, sm_scale, rs, mask_chunks=None,
                inline=False):
    n_groups = q_sc.shape[0] // rs

    def group(r, carry):
        r0 = pl.multiple_of(r * rs, rs)
        rows = pl.ds(r0, rs)
        if s_next is not None:
            s_next[rows, :] = _dot_nt(q_sc[rows, :], keys_next)
        if s_cur is not None:
            kb, kvl = vals_cur.shape
            s = s_cur[rows, :] * sm_scale
            if mask_chunks is not None:
                q_chunks, q_chunk0, k_chunk0 = mask_chunks
                chunk_bits = CHUNK.bit_length() - 1
                row = lax.broadcasted_iota(jnp.int32, (rs, kb), 0) + r0
                q_chunk = jnp.bitwise_and(jnp.right_shift(row, chunk_bits), q_chunks - 1) + q_chunk0
                k_chunk = jnp.right_shift(lax.broadcasted_iota(jnp.int32, (rs, kb), 1), chunk_bits) + k_chunk0
                s = jnp.where(k_chunk <= q_chunk, s, -jnp.inf)
            m_prev = m_sc[rows, :]
            m_new = jnp.maximum(m_prev, jnp.max(s, axis=1, keepdims=True))
            alpha = jnp.exp(m_prev - m_new)
            p = jnp.exp(s - _lane_tile(m_new, kb))
            l_sc[rows, :] = alpha * l_sc[rows, :] + jnp.sum(p, axis=1, keepdims=True)
            acc_sc[rows, :] = _lane_tile(alpha, kvl) * acc_sc[rows, :] + _dot(p.astype(BF), vals_cur)
            m_sc[rows, :] = m_new
        return carry

    lax.fori_loop(0, n_groups, group, 0, unroll=True if inline else (2 if n_groups % 2 == 0 else 1))


def _flash_init(m_sc, l_sc, acc_sc):
    m_sc[...] = jnp.full(m_sc.shape, -jnp.inf, F32)
    l_sc[...] = jnp.zeros(l_sc.shape, F32)
    acc_sc[...] = jnp.zeros(acc_sc.shape, F32)


def _flash_finish(o_ref, l_sc, acc_sc, wuv_ref, n_heads, tq):
    vd = wuv_ref.shape[2]
    kvl = acc_sc.shape[1]
    for h in range(n_heads):
        rows = pl.ds(h * tq, tq)
        o_lat = (acc_sc[rows, :] / _lane_tile(l_sc[rows, :], kvl)).astype(BF)
        o_ref[:, h * vd:(h + 1) * vd] = _dot(o_lat, wuv_ref[h]).astype(o_ref.dtype)


def _attn_prompt_kernel(*refs, n_heads, nope, rope, sm_scale, rs, n_riders):
    cq_ref, ckv_ref, kpe_ref, wuq_ref, wuk_ref, wuv_ref, cos_ref, sin_ref = refs[:8]
    rider_in = refs[8:8 + n_riders]
    o_ref = refs[8 + n_riders]
    rider_out = refs[9 + n_riders:9 + 2 * n_riders]
    kcat, q_sc, m_sc, l_sc, acc_sc, s_a, s_b = refs[9 + 2 * n_riders:]
    i = pl.program_id(1)
    tq = cq_ref.shape[0]
    kvl = ckv_ref.shape[1]
    kb = s_a.shape[1]

    for src, dst in zip(rider_in, rider_out):
        dst[...] = src[...].astype(dst.dtype)

    @pl.when(i == 0)
    def _():
        kcat[:, 0:kvl] = ckv_ref[...].astype(BF)
        kcat[:, kvl:kvl + rope] = kpe_ref[...].astype(BF)

    _build_queries(q_sc, cq_ref, wuq_ref, wuk_ref, cos_ref, sin_ref, n_heads, nope, rope, kvl)
    _flash_init(m_sc, l_sc, acc_sc)

    def step(j_next, s_next, j_cur, s_cur, masked=False):
        keys = None if s_next is None else kcat[pl.ds(pl.multiple_of(j_next * kb, kb), kb), :]
        vals = None if s_cur is None else kcat[pl.ds(pl.multiple_of(j_cur * kb, kb), kb), 0:kvl]
        mask = (tq // CHUNK, i * (tq // CHUNK), j_cur * (kb // CHUNK)) if masked else None
        _flash_step(q_sc, keys, s_next, s_cur, vals, m_sc, l_sc, acc_sc, sm_scale, rs, mask,
                    inline=(s_cur is None) or masked)

    last = (i * tq) // kb
    step(0, s_a, None, None)

    def pair(t, carry):
        j = 2 * t
        step(j + 1, s_b, j, s_a)
        step(j + 2, s_a, j + 1, s_b)
        return carry

    lax.fori_loop(0, last // 2, pair, 0)

    @pl.when(last % 2 == 0)
    def _():
        step(None, None, last, s_a, masked=True)
        _flash_finish(o_ref, l_sc, acc_sc, wuv_ref, n_heads, tq)

    @pl.when(last % 2 == 1)
    def _():
        step(last, s_b, last - 1, s_a)
        step(None, None, last, s_b, masked=True)
        _flash_finish(o_ref, l_sc, acc_sc, wuv_ref, n_heads, tq)


def _rider_block(rows, cols, n_steps):
    for n_row_blocks in range(n_steps, 0, -1):
        if n_steps % n_row_blocks or rows % n_row_blocks or cols % (n_steps // n_row_blocks):
            continue
        br, bc = rows // n_row_blocks, cols // (n_steps // n_row_blocks)
        if br % 16 == 0 and bc % LANES == 0:
            return br, bc
    raise ValueError(f"no aligned {n_steps}-block tiling of ({rows}, {cols})")


def _attn_prompt_call(cq, ckv, kpe, wuq, wuk, wuv, cos, sin, layer, n_seq, seq, tq, dims, riders=()):
    ah, nope, rope, kvl, vd = dims["ah"], dims["nope"], dims["rope"], dims["kvl"], dims["vd"]
    ql = cq.shape[1]
    nq = seq // tq
    kb = _pick_tile(seq, 512)
    assert tq % CHUNK == 0 and (tq // CHUNK) & (tq // CHUNK - 1) == 0 and kb % tq == 0
    rs = _pick_tile(ah * tq, 512)
    kern = functools.partial(_attn_prompt_kernel, n_heads=ah, nope=nope, rope=rope,
                             sm_scale=(nope + rope) ** -0.5, rs=rs, n_riders=len(riders))

    def wspec(w):
        return pl.BlockSpec((None,) + w.shape[1:], lambda b, i: (layer,) + (0,) * (w.ndim - 1))

    rider_in, rider_out, rider_shapes = [], [], []
    for arr, arr_layer in riders:
        _, rows, cols = arr.shape
        br, bc = _rider_block(rows, cols, n_seq * nq)
        ncb = cols // bc
        rider_in.append(pl.BlockSpec((None, br, bc), functools.partial(
            lambda b, i, ncb, arr_layer: (arr_layer, (b * nq + i) // ncb, (b * nq + i) % ncb), ncb=ncb, arr_layer=arr_layer)))
        rider_out.append(pl.BlockSpec((br, bc), functools.partial(
            lambda b, i, ncb: ((b * nq + i) // ncb, (b * nq + i) % ncb), ncb=ncb)))
        rider_shapes.append(jax.ShapeDtypeStruct((rows, cols), BF))

    return pl.pallas_call(
        kern,
        grid=(n_seq, nq),
        in_specs=[
            pl.BlockSpec((tq, ql), lambda b, i: (b * nq + i, 0)),
            pl.BlockSpec((seq, kvl), lambda b, i: (b, 0)),
            pl.BlockSpec((seq, rope), lambda b, i: (b, 0)),
            wspec(wuq), wspec(wuk), wspec(wuv),
            pl.BlockSpec((tq, LANES), lambda b, i: (b * nq + i, 0)),
            pl.BlockSpec((tq, LANES), lambda b, i: (b * nq + i, 0)),
        ] + rider_in,
        out_specs=[pl.BlockSpec((tq, ah * vd), lambda b, i: (b * nq + i, 0))] + rider_out,
        out_shape=[jax.ShapeDtypeStruct((n_seq * seq, ah * vd), BF)] + rider_shapes,
        scratch_shapes=[
            pltpu.VMEM((seq, kvl + rope), BF),
            pltpu.VMEM((ah * tq, kvl + rope), BF),
            pltpu.VMEM((ah * tq, LANES), F32),
            pltpu.VMEM((ah * tq, LANES), F32),
            pltpu.VMEM((ah * tq, kvl), F32),
            pltpu.VMEM((ah * tq, kb), F32),
            pltpu.VMEM((ah * tq, kb), F32),
        ],
        compiler_params=_cparams("parallel", "arbitrary"),
        name="mla_prompt",
    )(cq, ckv, kpe, wuq, wuk, wuv, cos, sin, *[arr for arr, _ in riders])


def _attn_sample_kernel(cq_ref, ckv_ref, kpe_ref, pckv_ref, pkpe_ref, wuq_ref, wuk_ref, wuv_ref, cos_ref, sin_ref,
                        o_ref, kcat, q_sc, m_sc, l_sc, acc_sc, s_a, s_b, s_n, *, n_heads, nope, rope, sm_scale):
    tq = cq_ref.shape[0]
    kvl = ckv_ref.shape[1]
    past = pckv_ref.shape[0]
    kb = s_a.shape[1]
    kcat[0:past, 0:kvl] = pckv_ref[...].astype(BF)
    kcat[0:past, kvl:kvl + rope] = pkpe_ref[...].T.astype(BF)
    kcat[past:past + tq, 0:kvl] = ckv_ref[...].astype(BF)
    kcat[past:past + tq, kvl:kvl + rope] = kpe_ref[...].astype(BF)
    _build_queries(q_sc, cq_ref, wuq_ref, wuk_ref, cos_ref, sin_ref, n_heads, nope, rope, kvl)
    _flash_init(m_sc, l_sc, acc_sc)
    rs = q_sc.shape[0]

    blocks = [(j * kb, kb, (s_a, s_b)[j % 2]) for j in range(past // kb)] + [(past, tq, s_n)]
    for cur, nxt in zip([None] + blocks, blocks + [None]):
        keys, s_next = (None, None) if nxt is None else (kcat[nxt[0]:nxt[0] + nxt[1], :], nxt[2])
        vals, s_cur = (None, None) if cur is None else (kcat[cur[0]:cur[0] + cur[1], 0:kvl], cur[2])
        _flash_step(q_sc, keys, s_next, s_cur, vals, m_sc, l_sc, acc_sc, sm_scale, rs)
    _flash_finish(o_ref, l_sc, acc_sc, wuv_ref, n_heads, tq)


def _attn_sample_call(cq, ckv, kpe, cache_ckv, cache_kpe, wuq, wuk, wuv, cos, sin, layer, row0, n_seq, tq, dims):
    ah, nope, rope, kvl, vd = dims["ah"], dims["nope"], dims["rope"], dims["kvl"], dims["vd"]
    ql = cq.shape[1]
    past = cache_ckv.shape[2]
    kb = math.gcd(past, 256)
    kern = functools.partial(_attn_sample_kernel, n_heads=ah, nope=nope, rope=rope,
                             sm_scale=(nope + rope) ** -0.5)

    def wspec(w):
        return pl.BlockSpec((None,) + w.shape[1:], lambda b: (layer,) + (0,) * (w.ndim - 1))

    def tok(n):
        return pl.BlockSpec((tq, n), lambda b: (row0 + b, 0))

    return pl.pallas_call(
        kern,
        grid=(n_seq,),
        in_specs=[
            tok(ql), tok(kvl), tok(rope),
            pl.BlockSpec((None, None, past, kvl), lambda b: (layer, b, 0, 0)),
            pl.BlockSpec((None, None, rope, past), lambda b: (layer, b, 0, 0)),
            wspec(wuq), wspec(wuk), wspec(wuv), tok(LANES), tok(LANES),
        ],
        out_specs=pl.BlockSpec((tq, ah * vd), lambda b: (b, 0)),
        out_shape=jax.ShapeDtypeStruct((n_seq * tq, ah * vd), BF),
        scratch_shapes=[
            pltpu.VMEM((past + tq, kvl + rope), BF),
            pltpu.VMEM((ah * tq, kvl + rope), BF),
            pltpu.VMEM((ah * tq, LANES), F32),
            pltpu.VMEM((ah * tq, LANES), F32),
            pltpu.VMEM((ah * tq, kvl), F32),
            pltpu.VMEM((ah * tq, kb), F32),
            pltpu.VMEM((ah * tq, kb), F32),
            pltpu.VMEM((ah * tq, tq), F32),
        ],
        compiler_params=_cparams("parallel"),
        name="mla_sample",
    )(cq, ckv, kpe, cache_ckv, cache_kpe, wuq, wuk, wuv, cos, sin)


def _outproj_kernel(x_ref, hmp_ref, hms_ref, oap_ref, oas_ref, gt_ref, wm_ref, wa_ref, o_ref, *, n_prompt_tiles):
    i = pl.program_id(0)
    tm = x_ref.shape[0]

    def project(hm, oa):
        y = _dot(hm, wm_ref[...]) + _dot(oa, wa_ref[...])
        for g in range(tm // CHUNK):
            rows = pl.ds(g * CHUNK, CHUNK)
            o_ref[rows, :] = x_ref[rows, :] + gt_ref[g:g + 1, :] * y[g * CHUNK:(g + 1) * CHUNK, :]

    @pl.when(i < n_prompt_tiles)
    def _():
        project(hmp_ref[...].reshape(tm, hmp_ref.shape[-1]), oap_ref[...])

    @pl.when(i >= n_prompt_tiles)
    def _():
        project(hms_ref[...], oas_ref[...])


def _outproj_call(x, hm_p, hm_s, oa_p, oa_s, mod, w_out, layer, tm, seq):
    m, d = x.shape
    wm = hm_s.shape[1]
    wa = oa_s.shape[1]
    assert wm == wa and seq % tm == 0
    ng = tm // CHUNK
    npt = oa_p.shape[0] // tm
    tps = seq // tm

    def prompt_tile(i):
        return jnp.minimum(i, npt - 1)

    def sample_tile(i):
        return jnp.maximum(i - npt, 0)

    return pl.pallas_call(
        functools.partial(_outproj_kernel, n_prompt_tiles=npt),
        grid=(m // tm,),
        in_specs=[
            pl.BlockSpec((tm, d), lambda i: (i, 0)),
            pl.BlockSpec((ng, None, CHUNK, wm), lambda i: (prompt_tile(i) % tps, prompt_tile(i) // tps, 0, 0)),
            pl.BlockSpec((tm, wm), lambda i: (sample_tile(i), 0)),
            pl.BlockSpec((tm, wa), lambda i: (prompt_tile(i), 0)),
            pl.BlockSpec((tm, wa), lambda i: (sample_tile(i), 0)),
            pl.BlockSpec((None, ng, d), lambda i: (layer, i, 5)),
            pl.BlockSpec((None, wm, d), lambda i: (layer, 0, 0)),
            pl.BlockSpec((None, wa, d), lambda i: (layer, 1, 0)),
        ],
        out_specs=pl.BlockSpec((tm, d), lambda i: (i, 0)),
        out_shape=jax.ShapeDtypeStruct((m, d), F32),
        compiler_params=_cparams("parallel"),
        name="out_proj",
    )(x, hm_p, hm_s, oa_p, oa_s, mod, w_out, w_out)


def _rope_tables(pos, rope):
    half = rope // 2
    freqs = ROPE_THETA ** (-jnp.arange(half, dtype=F32) / half)
    ang = pos.astype(F32)[:, None] * freqs[None, :]
    cos = jnp.cos(ang)
    sin = jnp.sin(ang)
    reps = LANES // rope
    return jnp.tile(jnp.concatenate([cos, cos], axis=1), (1, reps)), jnp.tile(jnp.concatenate([-sin, sin], axis=1), (1, reps))


def _pick_tile(m, cap):
    t = cap
    while m % t:
        t //= 2
    return t


def kernel(x_prompt, x_sample, c_prompt, c_sample, cache_ckv, cache_kpe, state_C, state_n, state_m, mod_w, mod_b, ln_ffn1, ffn1_w_in, ffn1_w_out, ln_mix, w_in, mlstm_b_i, mlstm_b_f, mlstm_norm, q_norm, w_uq, kv_norm, w_uk, w_uv, w_out, ln_ffn2, ffn2_w_in, ffn2_w_out, final_norm):
    bp, sp, d = x_prompt.shape
    bs, ss, _ = x_sample.shape
    depth = mod_w.shape[0]
    past = cache_ckv.shape[2]
    mh, dv, dk = state_C.shape[2:]
    kvl, ah, nope = w_uk.shape[1:]
    vd = w_uv.shape[3]
    rope = cache_kpe.shape[3]
    ql = q_norm.shape[1]
    hdk, hdv = mh * dk, mh * dv
    dff = ffn1_w_out.shape[1]
    dims = dict(mh=mh, dk=dk, dv=dv, hdk=hdk, hdv=hdv, ql=ql, kvl=kvl, rope=rope, ah=ah, nope=nope, vd=vd)
    assert sp % CHUNK == 0 and ss == CHUNK and LANES % rope == 0 and ah % (LANES // rope) == 0
    mp, ms = bp * sp, bs * ss
    m = mp + ms
    tm = _pick_tile(math.gcd(mp, ms), 512)
    tm_ffn = _pick_tile(math.gcd(mp, ms), 1024)
    ffn_tiles_p, ffn_tiles_s = mp // tm_ffn, ms // tm_ffn
    tf = _pick_tile(dff, 512)
    tq = _pick_tile(sp, 256)

    cg = jnp.concatenate([jnp.repeat(c_prompt, sp // CHUNK, axis=0), jnp.repeat(c_sample, ss // CHUNK, axis=0)], axis=0)
    pos = jnp.concatenate([jnp.tile(jnp.arange(sp), bp), jnp.tile(past + jnp.arange(ss), bs)])
    cos, sin = _rope_tables(pos, rope)

    o_mq, o_mk, o_mv, o_mo = 0, hdk, 2 * hdk, 2 * hdk + hdv
    o_mi = o_mo + hdv
    o_mf, o_qa = o_mi + mh, o_mi + 2 * mh
    o_kva, o_pe = o_qa + ql, o_qa + ql + kvl
    wqk = w_in[:, :, o_mq:o_mv].astype(BF)
    wv = w_in[:, :, o_mv:o_mo].astype(BF)
    wo = w_in[:, :, o_mo:o_mi].astype(BF)
    zpad = lambda n: jnp.zeros((depth, d, n), w_in.dtype)
    wa = jnp.concatenate([w_in[:, :, o_qa:o_pe + rope], zpad(LANES - rope), w_in[:, :, o_mi:o_qa], zpad(LANES - 2 * mh)], axis=2).astype(BF)
    gate_bias = jnp.concatenate([mlstm_b_i, mlstm_b_f, jnp.zeros((depth, LANES - 2 * mh), F32)], axis=1).reshape(depth, 1, LANES)
    wuq4 = w_uq.reshape(depth, ql, ah, nope + rope)
    wuq = jnp.concatenate([wuq4[..., :nope].reshape(depth, ql, ah * nope), wuq4[..., nope:].reshape(depth, ql, ah * rope)], axis=2).astype(BF)
    wuk = jnp.transpose(w_uk, (0, 2, 3, 1)).astype(BF)
    wuv = jnp.transpose(w_uv, (0, 2, 1, 3)).astype(BF)
    w_out_b = w_out.astype(BF)
    f1_in, f1_out = ffn1_w_in[0].astype(BF), ffn1_w_out[0].astype(BF)
    r3 = lambda a: a.reshape(depth, 1, a.shape[1])

    mod = _mod_call(cg, mod_w, mod_b)

    zeros_c = jnp.zeros((1, bp, mh, dv, dk), F32)
    zeros_n = jnp.zeros((1, bp, mh, dk), F32)
    zeros_m = jnp.zeros((1, bp, mh, LANES), F32)
    m0_s = jnp.broadcast_to(state_m[..., None], state_m.shape + (LANES,))
    cache_kpe_t = jnp.swapaxes(cache_kpe, 2, 3)

    outs = {k: [] for k in ("p_kpe", "p_n", "p_m", "s_kpe", "s_n", "s_m")}
    ckv_stacks = (jnp.zeros((depth, mp, kvl), F32), jnp.zeros((depth, ms, kvl), F32))
    p_c = jnp.zeros((depth, bp, mh, dv, dk), F32)
    s_c = jnp.zeros((depth, bs, mh, dv, dk), F32)
    for l in range(depth):
        if l == 0:
            x = _ffn_call(x_prompt.reshape(mp, d), mod, r3(ln_ffn1), f1_in, f1_out, l, 0, tm_ffn, tf,
                          into=jnp.zeros((m, d), F32))
            x = _ffn_call(x_sample.reshape(ms, d), mod, r3(ln_ffn1), f1_in, f1_out, l, 0, tm_ffn, tf,
                          stream_tile0=ffn_tiles_p, out_tile0=ffn_tiles_p, into=x)
        else:
            x = _ffn_call(x, mod, r3(ln_ffn1), f1_in, f1_out, l, 0, tm_ffn, tf)
        q, k, v, og, cq, ckv, kpe, gates, *ckv_stacks = _inproj_call(
            x, mod, r3(ln_mix), wqk, wv, wo, wa, r3(q_norm), r3(kv_norm), gate_bias, cos, sin, l, tm, dims,
            depth, mp, ckv_stacks)
        gates_r = jnp.transpose(gates[:, :2 * mh].reshape(m // CHUNK, CHUNK, 2 * mh), (0, 2, 1))
        nrm = r3(mlstm_norm)
        hm_p, p_c, n_p, m_p = _mlstm_call(q, k, v, og, gates, gates_r, zeros_c, zeros_n, zeros_m, nrm,
                                          l, 0, bp, sp // CHUNK, mh, 0, depth, p_c)
        hm_s, s_c, n_s, m_s = _mlstm_call(q, k, v, og, gates, gates_r, state_C, state_n, m0_s, nrm,
                                          l, mp // CHUNK, bs, ss // CHUNK, mh, l, depth, s_c)
        riders = [(ffn2_w_in, l), (ffn2_w_out, l)]
        if l + 1 < depth:
            riders += [(ffn1_w_in, l + 1), (ffn1_w_out, l + 1)]
        oa_p, f2_in, f2_out, *next_f1 = _attn_prompt_call(cq, ckv, kpe, wuq, wuk, wuv, cos, sin, l, bp, sp, tq, dims, riders)
        if next_f1:
            f1_in, f1_out = next_f1
        oa_s = _attn_sample_call(cq, ckv, kpe, cache_ckv, cache_kpe_t, wuq, wuk, wuv, cos, sin, l, mp // ss, bs, ss, dims)
        x = _outproj_call(x, hm_p, hm_s.reshape(ms, hdv), oa_p, oa_s, mod, w_out_b, l, tm, sp)
        if l + 1 < depth:
            x = _ffn_call(x, mod, r3(ln_ffn2), f2_in, f2_out, l, 6, tm_ffn, tf)
        else:
            fn = final_norm.reshape(1, d)
            y_prompt = _ffn_call(x, mod, r3(ln_ffn2), f2_in, f2_out, l, 6, tm_ffn, tf,
                                 n_tiles=ffn_tiles_p, final_gain=fn).reshape(bp, sp, d)
            y_sample = _ffn_call(x, mod, r3(ln_ffn2), f2_in, f2_out, l, 6, tm_ffn, tf, src_tile0=ffn_tiles_p,
                                 n_tiles=ffn_tiles_s, stream_tile0=ffn_tiles_p, final_gain=fn).reshape(bs, ss, d)
        outs["p_kpe"].append(kpe[:mp].reshape(bp, sp, rope))
        outs["s_kpe"].append(kpe[mp:].reshape(bs, ss, rope))
        outs["p_n"].append(n_p)
        outs["p_m"].append(m_p[..., 0])
        outs["s_n"].append(n_s)
        outs["s_m"].append(m_s[..., 0])

    st = {k: jnp.stack(v) for k, v in outs.items()}
    p_ckv = ckv_stacks[0].reshape(depth, bp, sp, kvl)
    s_ckv = ckv_stacks[1].reshape(depth, bs, ss, kvl)
    return (y_prompt, y_sample, p_ckv, st["p_kpe"], p_c, st["p_n"], st["p_m"],
            s_ckv, st["s_kpe"], s_c, st["s_n"], st["s_m"])
```

```python
import functools
import math

import jax
import jax.numpy as jnp
from jax import lax
from jax.experimental import pallas as pl
from jax.experimental.pallas import tpu as pltpu

CHUNK = 64
EPS = 1e-6
ROPE_THETA = 10000.0
N_MOD = 9
LANES = 128
VMEM_LIMIT = 58 * 1024 * 1024

BF = jnp.bfloat16
F32 = jnp.float32


def _cparams(*sem):
    return pltpu.CompilerParams(dimension_semantics=sem, vmem_limit_bytes=VMEM_LIMIT)


def _dot(a, b):
    return jnp.dot(a, b, preferred_element_type=F32)


def _dot_nt(a, b):
    return lax.dot_general(a, b, (((1,), (1,)), ((), ())), preferred_element_type=F32)


def _dot_tn(a, b):
    return lax.dot_general(a, b, (((0,), (0,)), ((), ())), preferred_element_type=F32)


def _rms(x, g):
    ms = jnp.mean(x * x, axis=-1, keepdims=True)
    return x * lax.rsqrt(ms + EPS) * g


def _silu(x):
    return x * jax.nn.sigmoid(x)


def _rope128(pe, cos, sin):
    lane = lax.broadcasted_iota(jnp.int32, pe.shape, 1)
    first_half = jnp.bitwise_and(lane, 63) < 32
    swapped = jnp.where(first_half, pltpu.roll(pe, 96, 1), pltpu.roll(pe, 32, 1))
    return pe * cos + swapped * sin


def _mod_kernel(c_ref, w_ref, b_ref, o_ref):
    a = _silu(c_ref[...]).astype(BF)
    o_ref[...] = _dot(a, w_ref[...].astype(BF)) + b_ref[...]


def _mod_call(cg, mod_w, mod_b):
    depth, d, nd = mod_w.shape
    g = cg.shape[0]
    tn = 1024
    return pl.pallas_call(
        _mod_kernel,
        grid=(depth, nd // tn),
        in_specs=[
            pl.BlockSpec((g, d), lambda l, j: (0, 0)),
            pl.BlockSpec((None, d, tn), lambda l, j: (l, 0, j)),
            pl.BlockSpec((None, 1, tn), lambda l, j: (l, 0, j)),
        ],
        out_specs=pl.BlockSpec((None, g, tn), lambda l, j: (l, 0, j)),
        out_shape=jax.ShapeDtypeStruct((depth, g, nd), F32),
        compiler_params=_cparams("parallel", "parallel"),
        name="adaln_mod",
    )(cg, mod_w, mod_b.reshape(depth, 1, nd))


def _norm_mod_to(h_ref, x_ref, ln_ref, sh_ref, sc_ref):
    ln = ln_ref[...]
    for g in range(x_ref.shape[0] // CHUNK):
        rows = pl.ds(g * CHUNK, CHUNK)
        y = _rms(x_ref[rows, :], ln)
        h_ref[rows, :] = (y * (1.0 + sc_ref[g:g + 1, :]) + sh_ref[g:g + 1, :]).astype(h_ref.dtype)


def _ffn_kernel(*refs, final_norm, aliased):
    x_ref, sh_ref, sc_ref, gt_ref, ln_ref, wg_ref, wu_ref, wo_ref = refs[:8]
    fg_ref = refs[8] if final_norm else None
    o_ref, h_sc = refs[8 + final_norm + aliased:]
    j = pl.program_id(1)
    nj = pl.num_programs(1)

    def up_down(h):
        a = (_silu(_dot(h, wg_ref[...])) * _dot(h, wu_ref[...])).astype(BF)
        return _dot(a, wo_ref[...])

    def residual(acc):
        for g in range(x_ref.shape[0] // CHUNK):
            rows = pl.ds(g * CHUNK, CHUNK)
            y = x_ref[rows, :] + (0.5 * gt_ref[g:g + 1, :]) * acc[g * CHUNK:(g + 1) * CHUNK, :]
            o_ref[rows, :] = _rms(y, fg_ref[...]) if final_norm else y

    @pl.when(j == 0)
    def _():
        _norm_mod_to(h_sc, x_ref, ln_ref, sh_ref, sc_ref)
        o_ref[...] = up_down(h_sc[...])

    @pl.when(jnp.logical_and(j > 0, j < nj - 1))
    def _():
        o_ref[...] += up_down(h_sc[...])

    @pl.when(jnp.logical_and(j > 0, j == nj - 1))
    def _():
        residual(o_ref[...] + up_down(h_sc[...]))

    @pl.when(nj == 1)
    def _():
        residual(o_ref[...])


def _ffn_call(x, mod, ln, w_in, w_out, layer, k0, tm, tf, *, src_tile0=0, n_tiles=None, stream_tile0=0,
              out_rows=None, out_tile0=0, into=None, final_gain=None):
    d = x.shape[1]
    f = w_out.shape[0]
    ng = tm // CHUNK
    nf = f // tf
    n_tiles = x.shape[0] // tm if n_tiles is None else n_tiles
    out_rows = (into.shape[0] if into is not None else n_tiles * tm) if out_rows is None else out_rows

    def mod_spec(k):
        return pl.BlockSpec((None, ng, d), lambda i, j: (layer, i + stream_tile0, k))

    in_specs = [
        pl.BlockSpec((tm, d), lambda i, j: (i + src_tile0, 0)),
        mod_spec(k0), mod_spec(k0 + 1), mod_spec(k0 + 2),
        pl.BlockSpec((None, 1, d), lambda i, j: (layer, 0, 0)),
        pl.BlockSpec((d, tf), lambda i, j: (0, j)),
        pl.BlockSpec((d, tf), lambda i, j: (0, j + nf)),
        pl.BlockSpec((tf, d), lambda i, j: (j, 0)),
    ]
    operands = [x, mod, mod, mod, ln, w_in, w_in, w_out]
    if final_gain is not None:
        in_specs.append(pl.BlockSpec((1, d), lambda i, j: (0, 0)))
        operands.append(final_gain)
    aliases = {}
    if into is not None:
        aliases = {len(operands): 0}
        in_specs.append(pl.BlockSpec(memory_space=pl.ANY))
        operands.append(into)
    kern = functools.partial(_ffn_kernel, final_norm=final_gain is not None, aliased=into is not None)
    return pl.pallas_call(
        kern,
        grid=(n_tiles, nf),
        in_specs=in_specs,
        out_specs=pl.BlockSpec((tm, d), lambda i, j: (i + out_tile0, 0)),
        out_shape=jax.ShapeDtypeStruct((out_rows, d), F32),
        scratch_shapes=[pltpu.VMEM((tm, d), BF)],
        input_output_aliases=aliases,
        compiler_params=_cparams("parallel", "arbitrary"),
        name="ffn",
    )(*operands)


def _inproj_kernel(*refs, n_heads, k_scale, n_aliased, n_prompt_tiles):
    (x_ref, sh_ref, sc_ref, ln_ref, wqk_ref, wv_ref, wo_ref, wa_ref, qn_ref, kvn_ref, gb_ref,
     cos_ref, sin_ref) = refs[:13]
    (q_ref, k_ref, v_ref, og_ref, cq_ref, ckv_ref, kpe_ref, gt_ref, ckvp_ref, ckvs_ref,
     h_sc) = refs[13 + n_aliased:]
    _norm_mod_to(h_sc, x_ref, ln_ref, sh_ref, sc_ref)
    h = h_sc[...]
    hdk = q_ref.shape[1]
    za = _dot(h, wa_ref[...])
    og_ref[...] = jax.nn.sigmoid(_dot(h, wo_ref[...]))
    zqk = _dot(h, wqk_ref[...])
    q_ref[...] = zqk[:, :hdk].astype(BF)
    k_ref[...] = (zqk[:, hdk:] * k_scale).astype(BF)
    v_ref[...] = _dot(h, wv_ref[...]).astype(BF)
    ql = cq_ref.shape[1]
    kvl = ckv_ref.shape[1]
    cq_ref[...] = _rms(za[:, :ql], qn_ref[...]).astype(BF)
    ckv = _rms(za[:, ql:ql + kvl], kvn_ref[...])
    ckv_ref[...] = ckv
    pe = _rope128(za[:, ql + kvl:ql + kvl + LANES], cos_ref[...], sin_ref[...])
    kpe_ref[...] = pe[:, :kpe_ref.shape[1]]
    zg = za[:, ql + kvl + LANES:] + gb_ref[...]
    lane = lax.broadcasted_iota(jnp.int32, zg.shape, 1)
    log_sig = jnp.minimum(zg, 0.0) - jnp.log1p(jnp.exp(-jnp.abs(zg)))
    gt_ref[...] = jnp.where(lane < n_heads, zg, log_sig)

    i = pl.program_id(0)

    @pl.when(i < n_prompt_tiles)
    def _():
        ckvp_ref[...] = ckv_ref[...]

    @pl.when(i >= n_prompt_tiles)
    def _():
        ckvs_ref[...] = ckv_ref[...]


def _inproj_call(x, mod, ln, wqk, wv, wo, wa, q_norm, kv_norm, gate_bias, cos, sin, layer, tm, dims,
                 n_layers, rows_prompt, ckv_stacks=None):
    m, d = x.shape
    ng = tm // CHUNK
    hdk, hdv, ql, kvl, rope = dims["hdk"], dims["hdv"], dims["ql"], dims["kvl"], dims["rope"]
    npt = rows_prompt // tm
    aliased = [] if ckv_stacks is None else list(ckv_stacks)

    def mod_spec(k):
        return pl.BlockSpec((None, ng, d), lambda i: (layer, i, k))

    def wspec(w):
        return pl.BlockSpec((None,) + w.shape[1:], lambda i: (layer, 0, 0))

    def row(n):
        return pl.BlockSpec((tm, n), lambda i: (i, 0))

    kern = functools.partial(_inproj_kernel, n_heads=dims["mh"], k_scale=dims["dk"] ** -0.5,
                             n_aliased=len(aliased), n_prompt_tiles=npt)
    return pl.pallas_call(
        kern,
        grid=(m // tm,),
        in_specs=[
            row(d), mod_spec(3), mod_spec(4), wspec(ln), wspec(wqk), wspec(wv), wspec(wo), wspec(wa),
            wspec(q_norm), wspec(kv_norm), wspec(gate_bias), row(LANES), row(LANES),
        ] + [pl.BlockSpec(memory_space=pl.ANY)] * len(aliased),
        out_specs=[
            row(hdk), row(hdk), row(hdv), row(hdv), row(ql), row(kvl), row(rope), row(LANES),
            pl.BlockSpec((None, tm, kvl), lambda i: (layer, jnp.minimum(i, npt - 1), 0)),
            pl.BlockSpec((None, tm, kvl), lambda i: (layer, jnp.maximum(i - npt, 0), 0)),
        ],
        out_shape=[
            jax.ShapeDtypeStruct((m, hdk), BF), jax.ShapeDtypeStruct((m, hdk), BF),
            jax.ShapeDtypeStruct((m, hdv), BF), jax.ShapeDtypeStruct((m, hdv), F32),
            jax.ShapeDtypeStruct((m, ql), BF), jax.ShapeDtypeStruct((m, kvl), F32),
            jax.ShapeDtypeStruct((m, rope), F32), jax.ShapeDtypeStruct((m, LANES), F32),
            jax.ShapeDtypeStruct((n_layers, rows_prompt, kvl), F32),
            jax.ShapeDtypeStruct((n_layers, m - rows_prompt, kvl), F32),
        ],
        scratch_shapes=[pltpu.VMEM((tm, d), BF)],
        input_output_aliases={13 + a: 8 + a for a in range(len(aliased))},
        compiler_params=_cparams("arbitrary"),
        name="in_proj",
    )(x, mod, mod, ln, wqk, wv, wo, wa, q_norm, kv_norm, gate_bias, cos, sin, *aliased)


def _mlstm_kernel(*refs, n_heads, bt):
    tok = [refs[6 * u:6 * u + 6] for u in range(bt)]
    c0_ref, n0_ref, m0_ref, nrm_ref = refs[6 * bt:6 * bt + 4]
    hm_ref, c_ref, n_ref, m_ref = refs[-4:]
    c_idx = pl.program_id(1)

    @pl.when(c_idx == 0)
    def _():
        c_ref[...] = c0_ref[...]
        n_ref[...] = n0_ref[...]
        m_ref[...] = m0_ref[...]

    L = tok[0][0].shape[0]
    dk = tok[0][0].shape[1] // n_heads
    dv = tok[0][2].shape[1] // n_heads
    chains = [(u, h) for u in range(bt) for h in range(n_heads)]
    nc = len(chains)

    def stack(fn):
        return jnp.concatenate([fn(u, h) for u, h in chains], axis=0)

    def per_chain_last(col):
        last = col.reshape(nc, L, 1)[:, L - 1:L, :]
        return jnp.broadcast_to(last, (nc, L, 1)).reshape(nc * L, 1)

    t_idx = jnp.bitwise_and(lax.broadcasted_iota(jnp.int32, (nc * L, L), 0), L - 1)
    s_idx = lax.broadcasted_iota(jnp.int32, (nc * L, L), 1)
    causal = s_idx <= t_idx
    ig_c = stack(lambda u, h: tok[u][4][:, h:h + 1])
    lf_c = stack(lambda u, h: tok[u][4][:, n_heads + h:n_heads + h + 1])
    ig_r = stack(lambda u, h: jnp.broadcast_to(tok[u][5][h:h + 1, :], (L, L)))
    lf_r = stack(lambda u, h: jnp.broadcast_to(tok[u][5][n_heads + h:n_heads + h + 1, :], (L, L)))
    m_prev = stack(lambda u, h: jnp.broadcast_to(m_ref[u, h:h + 1, 0:1], (L, 1)))
    b_c = jnp.sum(jnp.where(causal, lf_r, 0.0), axis=1, keepdims=True)
    b_r = jnp.sum(jnp.where(causal, 0.0, lf_c).reshape(nc, L, L), axis=1, keepdims=True)
    b_r = jnp.broadcast_to(b_r, (nc, L, L)).reshape(nc * L, L) + lf_r
    b_last = per_chain_last(b_c)
    dmat = jnp.where(causal, b_c - b_r + ig_r, -jnp.inf)
    g_c = b_c + m_prev
    m_t = jnp.maximum(g_c, jnp.max(dmat, axis=1, keepdims=True))
    w_intra = jnp.exp(dmat - m_t)
    w_inter = jnp.exp(g_c - m_t)
    m_new = per_chain_last(m_t)
    wa = jnp.exp(b_last - b_c + ig_c - m_new)
    decay = jnp.exp(b_last + m_prev - m_new)

    qs = {c: tok[c[0]][0][:, c[1] * dk:(c[1] + 1) * dk] for c in chains}
    ks = {c: tok[c[0]][1][:, c[1] * dk:(c[1] + 1) * dk] for c in chains}
    vs = {c: tok[c[0]][2][:, c[1] * dv:(c[1] + 1) * dv] for c in chains}
    c_prev = {c: c_ref[c[0], c[1]] for c in chains}
    n_prev = {c: n_ref[c[0], c[1]:c[1] + 1, :] for c in chains}
    s = stack(lambda u, h: _dot_nt(qs[u, h], ks[u, h])) * w_intra
    qc = stack(lambda u, h: _dot_nt(qs[u, h], c_prev[u, h].astype(BF)))
    s_bf = s.astype(BF)
    sv = jnp.concatenate([_dot(s_bf[i * L:(i + 1) * L, :], vs[c]) for i, c in enumerate(chains)], axis=0)
    qn = jnp.sum(stack(lambda u, h: qs[u, h].astype(F32) * n_prev[u, h]), axis=1, keepdims=True)
    num = sv + w_inter * qc
    nq = jnp.sum(s, axis=1, keepdims=True) + w_inter * qn
    r = 1.0 / jnp.maximum(jnp.abs(nq), jnp.exp(-m_t))
    y = num * stack(lambda u, h: tok[u][3][:, h * dv:(h + 1) * dv])
    x = y * (r * lax.rsqrt(jnp.mean(y * y, axis=-1, keepdims=True) * (r * r) + EPS))
    wk = wa * stack(lambda u, h: ks[u, h].astype(F32))
    wk_bf = wk.astype(BF)
    for i, (u, h) in enumerate(chains):
        rows = slice(i * L, (i + 1) * L)
        hm_ref[u, :, h * dv:(h + 1) * dv] = (x[rows, :] * nrm_ref[:, h * dv:(h + 1) * dv]).astype(hm_ref.dtype)
        dec = decay[i * L:i * L + 1, :]
        c_ref[u, h] = dec * c_prev[u, h] + _dot_tn(vs[u, h], wk_bf[rows, :])
        n_ref[u, h:h + 1, :] = dec * n_prev[u, h] + jnp.sum(wk[rows, :], axis=0, keepdims=True)
        m_ref[u, h:h + 1, :] = jnp.broadcast_to(m_new[i * L:i * L + 1, :], (1, m_ref.shape[2]))


def _mlstm_call(q, k, v, og, gates_c, gates_r, c0, n0, m0, norm, layer, row0, n_seq, n_chunks, n_heads,
                state_layer, n_layers, c_stack=None):
    hdk = q.shape[1]
    hdv = v.shape[1]
    dv, dk = c0.shape[-2:]
    g2 = gates_r.shape[1]
    bt = _pick_tile(n_seq, 4)
    aliased = [] if c_stack is None else [c_stack]

    def chunked(a):
        return a.reshape(a.shape[0] // CHUNK, CHUNK, a.shape[1])

    arrays = [chunked(q), chunked(k), chunked(v), chunked(og), chunked(gates_c), gates_r]
    shapes = [(CHUNK, hdk), (CHUNK, hdk), (CHUNK, hdv), (CHUNK, hdv), (CHUNK, LANES), (g2, CHUNK)]

    def tok(u, shape):
        return pl.BlockSpec((None,) + shape, lambda t, c: (row0 + (t * bt + u) * n_chunks + c, 0, 0))

    def state(shape):
        return pl.BlockSpec((bt,) + shape, lambda t, c: (t,) + (0,) * len(shape))

    def stacked(shape, which):
        return pl.BlockSpec((None, bt) + shape, lambda t, c: (which, t) + (0,) * len(shape))

    in_specs = [tok(u, shape) for u in range(bt) for shape in shapes]
    in_specs += [stacked((n_heads, dv, dk), state_layer), stacked((n_heads, dk), state_layer),
                 stacked((n_heads, LANES), state_layer), pl.BlockSpec((None, 1, hdv), lambda t, c: (layer, 0, 0))]
    in_specs += [pl.BlockSpec(memory_space=pl.ANY)] * len(aliased)
    kern = functools.partial(_mlstm_kernel, n_heads=n_heads, bt=bt)
    return pl.pallas_call(
        kern,
        grid=(n_seq // bt, n_chunks),
        in_specs=in_specs,
        out_specs=[
            pl.BlockSpec((None, bt, CHUNK, hdv), lambda t, c: (c, t, 0, 0)),
            stacked((n_heads, dv, dk), layer), state((n_heads, dk)), state((n_heads, LANES)),
        ],
        out_shape=[
            jax.ShapeDtypeStruct((n_chunks, n_seq, CHUNK, hdv), BF),
            jax.ShapeDtypeStruct((n_layers, n_seq, n_heads, dv, dk), F32),
            jax.ShapeDtypeStruct((n_seq, n_heads, dk), F32),
            jax.ShapeDtypeStruct((n_seq, n_heads, LANES), F32),
        ],
        input_output_aliases={6 * bt + 4: 1} if aliased else {},
        compiler_params=_cparams("parallel", "arbitrary"),
        name="mlstm",
    )(*(arrays * bt), c0, n0, m0, norm, *aliased)


def _build_queries(q_sc, cq_ref, wuq_ref, wuk_ref, cos_ref, sin_ref, n_heads, nope, rope, kvl):
    tq = cq_ref.shape[0]
    qa = _dot(cq_ref[...], wuq_ref[...])
    cos = cos_ref[...]
    sin = sin_ref[...]
    for h in range(n_heads):
        qn = qa[:, h * nope:(h + 1) * nope].astype(BF)
        q_sc[h * tq:(h + 1) * tq, 0:kvl] = _dot(qn, wuk_ref[h]).astype(BF)
    per = LANES // rope
    for p in range(n_heads // per):
        base = n_heads * nope + p * LANES
        pe = _rope128(qa[:, base:base + LANES], cos, sin)
        for u in range(per):
            h = p * per + u
            q_sc[h * tq:(h + 1) * tq, kvl:kvl + rope] = pe[:, u * rope:(u + 1) * rope].astype(BF)


def _lane_tile(x, width):
    if width <= LANES:
        return x[:, :width]
    return jnp.concatenate([x] * (width // LANES), axis=1)


def _flash_step(q_sc, keys_next, s_next, s_cur, vals_cur, m_sc, l_sc, acc_sc, sm_scale, rs, mask_chunks=None,
                inline=False):
    n_groups = q_sc.shape[0] // rs

    def group(r, carry):
        r0 = pl.multiple_of(r * rs, rs)
        rows = pl.ds(r0, rs)
        if s_next is not None:
            s_next[rows, :] = _dot_nt(q_sc[rows, :], keys_next)
        if s_cur is not None:
            kb, kvl = vals_cur.shape
            s = s_cur[rows, :] * sm_scale
            if mask_chunks is not None:
                q_chunks, q_chunk0, k_chunk0 = mask_chunks
                chunk_bits = CHUNK.bit_length() - 1
                row = lax.broadcasted_iota(jnp.int32, (rs, kb), 0) + r0
                q_chunk = jnp.bitwise_and(jnp.right_shift(row, chunk_bits), q_chunks - 1) + q_chunk0
                k_chunk = jnp.right_shift(lax.broadcasted_iota(jnp.int32, (rs, kb), 1), chunk_bits) + k_chunk0
                s = jnp.where(k_chunk <= q_chunk, s, -jnp.inf)
            m_prev = m_sc[rows, :]
            m_new = jnp.maximum(m_prev, jnp.max(s, axis=1, keepdims=True))
            alpha = jnp.exp(m_prev - m_new)
            p = jnp.exp(s - _lane_tile(m_new, kb))
            l_sc[rows, :] = alpha * l_sc[rows, :] + jnp.sum(p, axis=1, keepdims=True)
            acc_sc[rows, :] = _lane_tile(alpha, kvl) * acc_sc[rows, :] + _dot(p.astype(BF), vals_cur)
            m_sc[rows, :] = m_new
        return carry

    lax.fori_loop(0, n_groups, group, 0, unroll=True if inline else (2 if n_groups % 2 == 0 else 1))


def _flash_init(m_sc, l_sc, acc_sc):
    m_sc[...] = jnp.full(m_sc.shape, -jnp.inf, F32)
    l_sc[...] = jnp.zeros(l_sc.shape, F32)
    acc_sc[...] = jnp.zeros(acc_sc.shape, F32)


def _flash_finish(o_ref, l_sc, acc_sc, wuv_ref, n_heads, tq):
    vd = wuv_ref.shape[2]
    kvl = acc_sc.shape[1]
    for h in range(n_heads):
        rows = pl.ds(h * tq, tq)
        o_lat = (acc_sc[rows, :] / _lane_tile(l_sc[rows, :], kvl)).astype(BF)
        o_ref[:, h * vd:(h + 1) * vd] = _dot(o_lat, wuv_ref[h]).astype(o_ref.dtype)


def _attn_prompt_kernel(*refs, n_heads, nope, rope, sm_scale, rs, n_riders):
    cq_ref, ckv_ref, kpe_ref, wuq_ref, wuk_ref, wuv_ref, cos_ref, sin_ref = refs[:8]
    rider_in = refs[8:8 + n_riders]
    o_ref = refs[8 + n_riders]
    rider_out = refs[9 + n_riders:9 + 2 * n_riders]
    kcat, q_sc, m_sc, l_sc, acc_sc, s_a, s_b = refs[9 + 2 * n_riders:]
    i = pl.program_id(1)
    tq = cq_ref.shape[0]
    kvl = ckv_ref.shape[1]
    kb = s_a.shape[1]

    @pl.when(i == 0)
    def _():
        kcat[:, 0:kvl] = ckv_ref[...].astype(BF)
        kcat[:, kvl:kvl + rope] = kpe_ref[...].astype(BF)

    for src, dst in zip(rider_in, rider_out):
        dst[...] = src[...].astype(dst.dtype)

    _build_queries(q_sc, cq_ref, wuq_ref, wuk_ref, cos_ref, sin_ref, n_heads, nope, rope, kvl)
    _flash_init(m_sc, l_sc, acc_sc)

    def step(j_next, s_next, j_cur, s_cur, masked=False):
        keys = None if s_next is None else kcat[pl.ds(pl.multiple_of(j_next * kb, kb), kb), :]
        vals = None if s_cur is None else kcat[pl.ds(pl.multiple_of(j_cur * kb, kb), kb), 0:kvl]
        mask = (tq // CHUNK, i * (tq // CHUNK), j_cur * (kb // CHUNK)) if masked else None
        _flash_step(q_sc, keys, s_next, s_cur, vals, m_sc, l_sc, acc_sc, sm_scale, rs, mask,
                    inline=(s_cur is None) or masked)

    last = (i * tq) // kb
    step(0, s_a, None, None)

    def pair(t, carry):
        j = 2 * t
        step(j + 1, s_b, j, s_a)
        step(j + 2, s_a, j + 1, s_b)
        return carry

    lax.fori_loop(0, last // 2, pair, 0)

    @pl.when(last % 2 == 0)
    def _():
        step(None, None, last, s_a, masked=True)
        _flash_finish(o_ref, l_sc, acc_sc, wuv_ref, n_heads, tq)

    @pl.when(last % 2 == 1)
    def _():
        step(last, s_b, last - 1, s_a)
        step(None, None, last, s_b, masked=True)
        _flash_finish(o_ref, l_sc, acc_sc, wuv_ref, n_heads, tq)


def _rider_block(rows, cols, n_steps):
    for n_row_blocks in range(n_steps, 0, -1):
        if n_steps % n_row_blocks or rows % n_row_blocks or cols % (n_steps // n_row_blocks):
            continue
        br, bc = rows // n_row_blocks, cols // (n_steps // n_row_blocks)
        if br % 16 == 0 and bc % LANES == 0:
            return br, bc
    raise ValueError(f"no aligned {n_steps}-block tiling of ({rows}, {cols})")


def _attn_prompt_call(cq, ckv, kpe, wuq, wuk, wuv, cos, sin, layer, n_seq, seq, tq, dims, riders=()):
    ah, nope, rope, kvl, vd = dims["ah"], dims["nope"], dims["rope"], dims["kvl"], dims["vd"]
    ql = cq.shape[1]
    nq = seq // tq
    kb = _pick_tile(seq, 512)
    assert tq % CHUNK == 0 and (tq // CHUNK) & (tq // CHUNK - 1) == 0 and kb % tq == 0
    rs = _pick_tile(ah * tq, 512)
    kern = functools.partial(_attn_prompt_kernel, n_heads=ah, nope=nope, rope=rope,
                             sm_scale=(nope + rope) ** -0.5, rs=rs, n_riders=len(riders))

    def wspec(w):
        return pl.BlockSpec((None,) + w.shape[1:], lambda b, i: (layer,) + (0,) * (w.ndim - 1))

    rider_in, rider_out, rider_shapes = [], [], []
    for arr, arr_layer in riders:
        _, rows, cols = arr.shape
        br, bc = _rider_block(rows, cols, n_seq * nq)
        ncb = cols // bc
        rider_in.append(pl.BlockSpec((None, br, bc), functools.partial(
            lambda b, i, ncb, arr_layer: (arr_layer, (b * nq + i) // ncb, (b * nq + i) % ncb), ncb=ncb, arr_layer=arr_layer)))
        rider_out.append(pl.BlockSpec((br, bc), functools.partial(
            lambda b, i, ncb: ((b * nq + i) // ncb, (b * nq + i) % ncb), ncb=ncb)))
        rider_shapes.append(jax.ShapeDtypeStruct((rows, cols), BF))

    return pl.pallas_call(
        kern,
        grid=(n_seq, nq),
        in_specs=[
            pl.BlockSpec((tq, ql), lambda b, i: (b * nq + i, 0)),
            pl.BlockSpec((seq, kvl), lambda b, i: (b, 0)),
            pl.BlockSpec((seq, rope), lambda b, i: (b, 0)),
            wspec(wuq), wspec(wuk), wspec(wuv),
            pl.BlockSpec((tq, LANES), lambda b, i: (b * nq + i, 0)),
            pl.BlockSpec((tq, LANES), lambda b, i: (b * nq + i, 0)),
        ] + rider_in,
        out_specs=[pl.BlockSpec((tq, ah * vd), lambda b, i: (b * nq + i, 0))] + rider_out,
        out_shape=[jax.ShapeDtypeStruct((n_seq * seq, ah * vd), BF)] + rider_shapes,
        scratch_shapes=[
            pltpu.VMEM((seq, kvl + rope), BF),
            pltpu.VMEM((ah * tq, kvl + rope), BF),
            pltpu.VMEM((ah * tq, LANES), F32),
            pltpu.VMEM((ah * tq, LANES), F32),
            pltpu.VMEM((ah * tq, kvl), F32),
            pltpu.VMEM((ah * tq, kb), F32),
            pltpu.VMEM((ah * tq, kb), F32),
        ],
        compiler_params=_cparams("parallel", "arbitrary"),
        name="mla_prompt",
    )(cq, ckv, kpe, wuq, wuk, wuv, cos, sin, *[arr for arr, _ in riders])


def _attn_sample_kernel(cq_ref, ckv_ref, kpe_ref, pckv_ref, pkpe_ref, wuq_ref, wuk_ref, wuv_ref, cos_ref, sin_ref,
                        o_ref, kcat, q_sc, m_sc, l_sc, acc_sc, s_a, s_b, s_n, *, n_heads, nope, rope, sm_scale):
    tq = cq_ref.shape[0]
    kvl = ckv_ref.shape[1]
    past = pckv_ref.shape[0]
    kb = s_a.shape[1]
    kcat[0:past, 0:kvl] = pckv_ref[...].astype(BF)
    kcat[0:past, kvl:kvl + rope] = pkpe_ref[...].T.astype(BF)
    kcat[past:past + tq, 0:kvl] = ckv_ref[...].astype(BF)
    kcat[past:past + tq, kvl:kvl + rope] = kpe_ref[...].astype(BF)
    _build_queries(q_sc, cq_ref, wuq_ref, wuk_ref, cos_ref, sin_ref, n_heads, nope, rope, kvl)
    _flash_init(m_sc, l_sc, acc_sc)
    rs = q_sc.shape[0]

    blocks = [(j * kb, kb, (s_a, s_b)[j % 2]) for j in range(past // kb)] + [(past, tq, s_n)]
    for cur, nxt in zip([None] + blocks, blocks + [None]):
        keys, s_next = (None, None) if nxt is None else (kcat[nxt[0]:nxt[0] + nxt[1], :], nxt[2])
        vals, s_cur = (None, None) if cur is None else (kcat[cur[0]:cur[0] + cur[1], 0:kvl], cur[2])
        _flash_step(q_sc, keys, s_next, s_cur, vals, m_sc, l_sc, acc_sc, sm_scale, rs)
    _flash_finish(o_ref, l_sc, acc_sc, wuv_ref, n_heads, tq)


def _attn_sample_call(cq, ckv, kpe, cache_ckv, cache_kpe, wuq, wuk, wuv, cos, sin, layer, row0, n_seq, tq, dims):
    ah, nope, rope, kvl, vd = dims["ah"], dims["nope"], dims["rope"], dims["kvl"], dims["vd"]
    ql = cq.shape[1]
    past = cache_ckv.shape[2]
    kb = math.gcd(past, 512)
    kern = functools.partial(_attn_sample_kernel, n_heads=ah, nope=nope, rope=rope,
                             sm_scale=(nope + rope) ** -0.5)

    def wspec(w):
        return pl.BlockSpec((None,) + w.shape[1:], lambda b: (layer,) + (0,) * (w.ndim - 1))

    def tok(n):
        return pl.BlockSpec((tq, n), lambda b: (row0 + b, 0))

    return pl.pallas_call(
        kern,
        grid=(n_seq,),
        in_specs=[
            tok(ql), tok(kvl), tok(rope),
            pl.BlockSpec((None, None, past, kvl), lambda b: (layer, b, 0, 0)),
            pl.BlockSpec((None, None, rope, past), lambda b: (layer, b, 0, 0)),
            wspec(wuq), wspec(wuk), wspec(wuv), tok(LANES), tok(LANES),
        ],
        out_specs=pl.BlockSpec((tq, ah * vd), lambda b: (b, 0)),
        out_shape=jax.ShapeDtypeStruct((n_seq * tq, ah * vd), BF),
        scratch_shapes=[
            pltpu.VMEM((past + tq, kvl + rope), BF),
            pltpu.VMEM((ah * tq, kvl + rope), BF),
            pltpu.VMEM((ah * tq, LANES), F32),
            pltpu.VMEM((ah * tq, LANES), F32),
            pltpu.VMEM((ah * tq, kvl), F32),
            pltpu.VMEM((ah * tq, kb), F32),
            pltpu.VMEM((ah * tq, kb), F32),
            pltpu.VMEM((ah * tq, tq), F32),
        ],
        compiler_params=_cparams("parallel"),
        name="mla_sample",
    )(cq, ckv, kpe, cache_ckv, cache_kpe, wuq, wuk, wuv, cos, sin)


def _outproj_kernel(x_ref, hmp_ref, hms_ref, oap_ref, oas_ref, gt_ref, wm_ref, wa_ref, o_ref, *, n_prompt_tiles):
    i = pl.program_id(0)
    tm = x_ref.shape[0]

    def project(hm, oa):
        y = _dot(hm, wm_ref[...]) + _dot(oa, wa_ref[...])
        for g in range(tm // CHUNK):
            rows = pl.ds(g * CHUNK, CHUNK)
            o_ref[rows, :] = x_ref[rows, :] + gt_ref[g:g + 1, :] * y[g * CHUNK:(g + 1) * CHUNK, :]

    @pl.when(i < n_prompt_tiles)
    def _():
        project(hmp_ref[...].reshape(tm, hmp_ref.shape[-1]), oap_ref[...])

    @pl.when(i >= n_prompt_tiles)
    def _():
        project(hms_ref[...], oas_ref[...])


def _outproj_call(x, hm_p, hm_s, oa_p, oa_s, mod, w_out, layer, tm, seq):
    m, d = x.shape
    wm = hm_s.shape[1]
    wa = oa_s.shape[1]
    assert wm == wa and seq % tm == 0
    ng = tm // CHUNK
    npt = oa_p.shape[0] // tm
    tps = seq // tm

    def prompt_tile(i):
        return jnp.minimum(i, npt - 1)

    def sample_tile(i):
        return jnp.maximum(i - npt, 0)

    return pl.pallas_call(
        functools.partial(_outproj_kernel, n_prompt_tiles=npt),
        grid=(m // tm,),
        in_specs=[
            pl.BlockSpec((tm, d), lambda i: (i, 0)),
            pl.BlockSpec((ng, None, CHUNK, wm), lambda i: (prompt_tile(i) % tps, prompt_tile(i) // tps, 0, 0)),
            pl.BlockSpec((tm, wm), lambda i: (sample_tile(i), 0)),
            pl.BlockSpec((tm, wa), lambda i: (prompt_tile(i), 0)),
            pl.BlockSpec((tm, wa), lambda i: (sample_tile(i), 0)),
            pl.BlockSpec((None, ng, d), lambda i: (layer, i, 5)),
            pl.BlockSpec((None, wm, d), lambda i: (layer, 0, 0)),
            pl.BlockSpec((None, wa, d), lambda i: (layer, 1, 0)),
        ],
        out_specs=pl.BlockSpec((tm, d), lambda i: (i, 0)),
        out_shape=jax.ShapeDtypeStruct((m, d), F32),
        compiler_params=_cparams("parallel"),
        name="out_proj",
    )(x, hm_p, hm_s, oa_p, oa_s, mod, w_out, w_out)


def _rope_tables(pos, rope):
    half = rope // 2
    freqs = ROPE_THETA ** (-jnp.arange(half, dtype=F32) / half)
    ang = pos.astype(F32)[:, None] * freqs[None, :]
    cos = jnp.cos(ang)
    sin = jnp.sin(ang)
    reps = LANES // rope
    return jnp.tile(jnp.concatenate([cos, cos], axis=1), (1, reps)), jnp.tile(jnp.concatenate([-sin, sin], axis=1), (1, reps))


def _pick_tile(m, cap):
    t = cap
    while m % t:
        t //= 2
    return t


def kernel(x_prompt, x_sample, c_prompt, c_sample, cache_ckv, cache_kpe, state_C, state_n, state_m, mod_w, mod_b, ln_ffn1, ffn1_w_in, ffn1_w_out, ln_mix, w_in, mlstm_b_i, mlstm_b_f, mlstm_norm, q_norm, w_uq, kv_norm, w_uk, w_uv, w_out, ln_ffn2, ffn2_w_in, ffn2_w_out, final_norm):
    bp, sp, d = x_prompt.shape
    bs, ss, _ = x_sample.shape
    depth = mod_w.shape[0]
    past = cache_ckv.shape[2]
    mh, dv, dk = state_C.shape[2:]
    kvl, ah, nope = w_uk.shape[1:]
    vd = w_uv.shape[3]
    rope = cache_kpe.shape[3]
    ql = q_norm.shape[1]
    hdk, hdv = mh * dk, mh * dv
    dff = ffn1_w_out.shape[1]
    dims = dict(mh=mh, dk=dk, dv=dv, hdk=hdk, hdv=hdv, ql=ql, kvl=kvl, rope=rope, ah=ah, nope=nope, vd=vd)
    assert sp % CHUNK == 0 and ss == CHUNK and LANES % rope == 0 and ah % (LANES // rope) == 0
    mp, ms = bp * sp, bs * ss
    m = mp + ms
    tm = _pick_tile(math.gcd(mp, ms), 512)
    tm_ffn = _pick_tile(math.gcd(mp, ms), 1024)
    ffn_tiles_p, ffn_tiles_s = mp // tm_ffn, ms // tm_ffn
    tf = _pick_tile(dff, 512)
    tq = _pick_tile(sp, 256)

    cg = jnp.concatenate([jnp.repeat(c_prompt, sp // CHUNK, axis=0), jnp.repeat(c_sample, ss // CHUNK, axis=0)], axis=0)
    pos = jnp.concatenate([jnp.tile(jnp.arange(sp), bp), jnp.tile(past + jnp.arange(ss), bs)])
    cos, sin = _rope_tables(pos, rope)

    o_mq, o_mk, o_mv, o_mo = 0, hdk, 2 * hdk, 2 * hdk + hdv
    o_mi = o_mo + hdv
    o_mf, o_qa = o_mi + mh, o_mi + 2 * mh
    o_kva, o_pe = o_qa + ql, o_qa + ql + kvl
    wqk = w_in[:, :, o_mq:o_mv].astype(BF)
    wv = w_in[:, :, o_mv:o_mo].astype(BF)
    wo = w_in[:, :, o_mo:o_mi].astype(BF)
    zpad = lambda n: jnp.zeros((depth, d, n), w_in.dtype)
    wa = jnp.concatenate([w_in[:, :, o_qa:o_pe + rope], zpad(LANES - rope), w_in[:, :, o_mi:o_qa], zpad(LANES - 2 * mh)], axis=2).astype(BF)
    gate_bias = jnp.concatenate([mlstm_b_i, mlstm_b_f, jnp.zeros((depth, LANES - 2 * mh), F32)], axis=1).reshape(depth, 1, LANES)
    wuq4 = w_uq.reshape(depth, ql, ah, nope + rope)
    wuq = jnp.concatenate([wuq4[..., :nope].reshape(depth, ql, ah * nope), wuq4[..., nope:].reshape(depth, ql, ah * rope)], axis=2).astype(BF)
    wuk = jnp.transpose(w_uk, (0, 2, 3, 1)).astype(BF)
    wuv = jnp.transpose(w_uv, (0, 2, 1, 3)).astype(BF)
    w_out_b = w_out.astype(BF)
    f1_in, f1_out = ffn1_w_in[0].astype(BF), ffn1_w_out[0].astype(BF)
    r3 = lambda a: a.reshape(depth, 1, a.shape[1])

    mod = _mod_call(cg, mod_w, mod_b)

    zeros_c = jnp.zeros((1, bp, mh, dv, dk), F32)
    zeros_n = jnp.zeros((1, bp, mh, dk), F32)
    zeros_m = jnp.zeros((1, bp, mh, LANES), F32)
    m0_s = jnp.broadcast_to(state_m[..., None], state_m.shape + (LANES,))
    cache_kpe_t = jnp.swapaxes(cache_kpe, 2, 3)

    outs = {k: [] for k in ("p_kpe", "p_n", "p_m", "s_kpe", "s_n", "s_m")}
    ckv_stacks = (jnp.zeros((depth, mp, kvl), F32), jnp.zeros((depth, ms, kvl), F32))
    p_c = jnp.zeros((depth, bp, mh, dv, dk), F32)
    s_c = jnp.zeros((depth, bs, mh, dv, dk), F32)
    for l in range(depth):
        if l == 0:
            x = _ffn_call(x_prompt.reshape(mp, d), mod, r3(ln_ffn1), f1_in, f1_out, l, 0, tm_ffn, tf,
                          into=jnp.zeros((m, d), F32))
            x = _ffn_call(x_sample.reshape(ms, d), mod, r3(ln_ffn1), f1_in, f1_out, l, 0, tm_ffn, tf,
                          stream_tile0=ffn_tiles_p, out_tile0=ffn_tiles_p, into=x)
        else:
            x = _ffn_call(x, mod, r3(ln_ffn1), f1_in, f1_out, l, 0, tm_ffn, tf)
        q, k, v, og, cq, ckv, kpe, gates, *ckv_stacks = _inproj_call(
            x, mod, r3(ln_mix), wqk, wv, wo, wa, r3(q_norm), r3(kv_norm), gate_bias, cos, sin, l, tm, dims,
            depth, mp, ckv_stacks)
        gates_r = jnp.transpose(gates[:, :2 * mh].reshape(m // CHUNK, CHUNK, 2 * mh), (0, 2, 1))
        nrm = r3(mlstm_norm)
        hm_p, p_c, n_p, m_p = _mlstm_call(q, k, v, og, gates, gates_r, zeros_c, zeros_n, zeros_m, nrm,
                                          l, 0, bp, sp // CHUNK, mh, 0, depth, p_c)
        hm_s, s_c, n_s, m_s = _mlstm_call(q, k, v, og, gates, gates_r, state_C, state_n, m0_s, nrm,
                                          l, mp // CHUNK, bs, ss // CHUNK, mh, l, depth, s_c)
        riders = [(ffn2_w_in, l), (ffn2_w_out, l)]
        if l + 1 < depth:
            riders += [(ffn1_w_in, l + 1), (ffn1_w_out, l + 1)]
        oa_p, f2_in, f2_out, *next_f1 = _attn_prompt_call(cq, ckv, kpe, wuq, wuk, wuv, cos, sin, l, bp, sp, tq, dims, riders)
        if next_f1:
            f1_in, f1_out = next_f1
        oa_s = _attn_sample_call(cq, ckv, kpe, cache_ckv, cache_kpe_t, wuq, wuk, wuv, cos, sin, l, mp // ss, bs, ss, dims)
        x = _outproj_call(x, hm_p, hm_s.reshape(ms, hdv), oa_p, oa_s, mod, w_out_b, l, tm, sp)
        if l + 1 < depth:
            x = _ffn_call(x, mod, r3(ln_ffn2), f2_in, f2_out, l, 6, tm_ffn, tf)
        else:
            fn = final_norm.reshape(1, d)
            y_prompt = _ffn_call(x, mod, r3(ln_ffn2), f2_in, f2_out, l, 6, tm_ffn, tf,
                                 n_tiles=ffn_tiles_p, final_gain=fn).reshape(bp, sp, d)
            y_sample = _ffn_call(x, mod, r3(ln_ffn2), f2_in, f2_out, l, 6, tm_ffn, tf, src_tile0=ffn_tiles_p,
                                 n_tiles=ffn_tiles_s, stream_tile0=ffn_tiles_p, final_gain=fn).reshape(bs, ss, d)
        outs["p_kpe"].append(kpe[:mp].reshape(bp, sp, rope))
        outs["s_kpe"].append(kpe[mp:].reshape(bs, ss, rope))
        outs["p_n"].append(n_p)
        outs["p_m"].append(m_p[..., 0])
        outs["s_n"].append(n_s)
        outs["s_m"].append(m_s[..., 0])

    st = {k: jnp.stack(v) for k, v in outs.items()}
    p_ckv = ckv_stacks[0].reshape(depth, bp, sp, kvl)
    s_ckv = ckv_stacks[1].reshape(depth, bs, ss, kvl)
    return (y_prompt, y_sample, p_ckv, st["p_kpe"], p_c, st["p_n"], st["p_m"],
            s_ckv, st["s_kpe"], s_c, st["s_n"], st["s_m"])
```

```python
import functools
import math

import jax
import jax.numpy as jnp
from jax import lax
from jax.experimental import pallas as pl
from jax.experimental.pallas import tpu as pltpu

CHUNK = 64
EPS = 1e-6
ROPE_THETA = 10000.0
LANES = 128
VMEM_LIMIT = 58 * 1024 * 1024

BF = jnp.bfloat16
F32 = jnp.float32


def _cparams(*sem):
    return pltpu.CompilerParams(dimension_semantics=sem, vmem_limit_bytes=VMEM_LIMIT)


def _dot(a, b):
    return jnp.dot(a, b, preferred_element_type=F32)


def _dot_nt(a, b):
    return lax.dot_general(a, b, (((1,), (1,)), ((), ())), preferred_element_type=F32)


def _dot_tn(a, b):
    return lax.dot_general(a, b, (((0,), (0,)), ((), ())), preferred_element_type=F32)


def _rms(x, g):
    ms = jnp.mean(x * x, axis=-1, keepdims=True)
    return x * lax.rsqrt(ms + EPS) * g


def _silu(x):
    return x * jax.nn.sigmoid(x)


def _rope128(pe, cos, sin):
    lane = lax.broadcasted_iota(jnp.int32, pe.shape, 1)
    first_half = jnp.bitwise_and(lane, 63) < 32
    swapped = jnp.where(first_half, pltpu.roll(pe, 96, 1), pltpu.roll(pe, 32, 1))
    return pe * cos + swapped * sin


def _mod_kernel(c_ref, w_ref, b_ref, o_ref):
    a = _silu(c_ref[...]).astype(BF)
    o_ref[...] = _dot(a, w_ref[...].astype(BF)) + b_ref[...]


def _mod_call(cg, mod_w, mod_b):
    depth, d, nd = mod_w.shape
    g = cg.shape[0]
    tn = 1024
    return pl.pallas_call(
        _mod_kernel,
        grid=(depth, nd // tn),
        in_specs=[
            pl.BlockSpec((g, d), lambda l, j: (0, 0)),
            pl.BlockSpec((None, d, tn), lambda l, j: (l, 0, j)),
            pl.BlockSpec((None, 1, tn), lambda l, j: (l, 0, j)),
        ],
        out_specs=pl.BlockSpec((None, g, tn), lambda l, j: (l, 0, j)),
        out_shape=jax.ShapeDtypeStruct((depth, g, nd), F32),
        compiler_params=_cparams("parallel", "parallel"),
        name="adaln_mod",
    )(cg, mod_w, mod_b.reshape(depth, 1, nd))


def _norm_mod_to(h_ref, x_ref, ln_ref, sh_ref, sc_ref):
    ln = ln_ref[...]
    for g in range(x_ref.shape[0] // CHUNK):
        rows = pl.ds(g * CHUNK, CHUNK)
        y = _rms(x_ref[rows, :], ln)
        h_ref[rows, :] = (y * (1.0 + sc_ref[g:g + 1, :]) + sh_ref[g:g + 1, :]).astype(h_ref.dtype)


def _ffn_kernel(*refs, final_norm, aliased):
    x_ref, sh_ref, sc_ref, gt_ref, ln_ref, wg_ref, wu_ref, wo_ref = refs[:8]
    fg_ref = refs[8] if final_norm else None
    o_ref, h_sc = refs[8 + final_norm + aliased:]
    j = pl.program_id(1)
    nj = pl.num_programs(1)

    def up_down(h):
        a = (_silu(_dot(h, wg_ref[...])) * _dot(h, wu_ref[...])).astype(BF)
        return _dot(a, wo_ref[...])

    def residual(acc):
        for g in range(x_ref.shape[0] // CHUNK):
            rows = pl.ds(g * CHUNK, CHUNK)
            y = x_ref[rows, :] + (0.5 * gt_ref[g:g + 1, :]) * acc[g * CHUNK:(g + 1) * CHUNK, :]
            o_ref[rows, :] = _rms(y, fg_ref[...]) if final_norm else y

    @pl.when(j == 0)
    def _():
        _norm_mod_to(h_sc, x_ref, ln_ref, sh_ref, sc_ref)
        o_ref[...] = up_down(h_sc[...])

    @pl.when(jnp.logical_and(j > 0, j < nj - 1))
    def _():
        o_ref[...] += up_down(h_sc[...])

    @pl.when(jnp.logical_and(j > 0, j == nj - 1))
    def _():
        residual(o_ref[...] + up_down(h_sc[...]))

    @pl.when(nj == 1)
    def _():
        residual(o_ref[...])


def _ffn_call(x, mod, ln, w_in, w_out, layer, k0, tm, tf, *, src_tile0=0, n_tiles=None, stream_tile0=0,
              out_rows=None, out_tile0=0, into=None, final_gain=None):
    d = x.shape[1]
    f = w_out.shape[0]
    ng = tm // CHUNK
    nf = f // tf
    n_tiles = x.shape[0] // tm if n_tiles is None else n_tiles
    out_rows = (into.shape[0] if into is not None else n_tiles * tm) if out_rows is None else out_rows

    def mod_spec(k):
        return pl.BlockSpec((None, ng, d), lambda i, j: (layer, i + stream_tile0, k))

    in_specs = [
        pl.BlockSpec((tm, d), lambda i, j: (i + src_tile0, 0)),
        mod_spec(k0), mod_spec(k0 + 1), mod_spec(k0 + 2),
        pl.BlockSpec((None, 1, d), lambda i, j: (layer, 0, 0)),
        pl.BlockSpec((d, tf), lambda i, j: (0, j)),
        pl.BlockSpec((d, tf), lambda i, j: (0, j + nf)),
        pl.BlockSpec((tf, d), lambda i, j: (j, 0)),
    ]
    operands = [x, mod, mod, mod, ln, w_in, w_in, w_out]
    if final_gain is not None:
        in_specs.append(pl.BlockSpec((1, d), lambda i, j: (0, 0)))
        operands.append(final_gain)
    aliases = {}
    if into is not None:
        aliases = {len(operands): 0}
        in_specs.append(pl.BlockSpec(memory_space=pl.ANY))
        operands.append(into)
    kern = functools.partial(_ffn_kernel, final_norm=final_gain is not None, aliased=into is not None)
    return pl.pallas_call(
        kern,
        grid=(n_tiles, nf),
        in_specs=in_specs,
        out_specs=pl.BlockSpec((tm, d), lambda i, j: (i + out_tile0, 0)),
        out_shape=jax.ShapeDtypeStruct((out_rows, d), F32),
        scratch_shapes=[pltpu.VMEM((tm, d), BF)],
        input_output_aliases=aliases,
        compiler_params=_cparams("parallel", "arbitrary"),
        name="ffn",
    )(*operands)


def _inproj_kernel(*refs, n_heads, k_scale, n_aliased, n_prompt_tiles):
    (x_ref, sh_ref, sc_ref, ln_ref, wqk_ref, wv_ref, wo_ref, wa_ref, qn_ref, kvn_ref, gb_ref,
     cos_ref, sin_ref) = refs[:13]
    (q_ref, k_ref, v_ref, og_ref, cq_ref, ckv_ref, kpe_ref, gt_ref, ckvp_ref, ckvs_ref,
     h_sc) = refs[13 + n_aliased:]
    _norm_mod_to(h_sc, x_ref, ln_ref, sh_ref, sc_ref)
    h = h_sc[...]
    hdk = q_ref.shape[1]
    za = _dot(h, wa_ref[...])
    og_ref[...] = jax.nn.sigmoid(_dot(h, wo_ref[...]))
    zqk = _dot(h, wqk_ref[...])
    q_ref[...] = zqk[:, :hdk].astype(BF)
    k_ref[...] = (zqk[:, hdk:] * k_scale).astype(BF)
    v_ref[...] = _dot(h, wv_ref[...]).astype(BF)
    ql = cq_ref.shape[1]
    kvl = ckv_ref.shape[1]
    cq_ref[...] = _rms(za[:, :ql], qn_ref[...]).astype(BF)
    ckv = _rms(za[:, ql:ql + kvl], kvn_ref[...])
    ckv_ref[...] = ckv
    pe = _rope128(za[:, ql + kvl:ql + kvl + LANES], cos_ref[...], sin_ref[...])
    kpe_ref[...] = pe[:, :kpe_ref.shape[1]]
    zg = za[:, ql + kvl + LANES:] + gb_ref[...]
    lane = lax.broadcasted_iota(jnp.int32, zg.shape, 1)
    log_sig = jnp.minimum(zg, 0.0) - jnp.log1p(jnp.exp(-jnp.abs(zg)))
    gt_ref[...] = jnp.where(lane < n_heads, zg, log_sig)

    i = pl.program_id(0)

    @pl.when(i < n_prompt_tiles)
    def _():
        ckvp_ref[...] = ckv_ref[...]

    @pl.when(i >= n_prompt_tiles)
    def _():
        ckvs_ref[...] = ckv_ref[...]


def _inproj_call(x, mod, ln, wqk, wv, wo, wa, q_norm, kv_norm, gate_bias, cos, sin, layer, tm, dims,
                 n_layers, rows_prompt, ckv_stacks=None):
    m, d = x.shape
    ng = tm // CHUNK
    hdk, hdv, ql, kvl, rope = dims["hdk"], dims["hdv"], dims["ql"], dims["kvl"], dims["rope"]
    npt = rows_prompt // tm
    aliased = [] if ckv_stacks is None else list(ckv_stacks)

    def mod_spec(k):
        return pl.BlockSpec((None, ng, d), lambda i: (layer, i, k))

    def wspec(w):
        return pl.BlockSpec((None,) + w.shape[1:], lambda i: (layer, 0, 0))

    def row(n):
        return pl.BlockSpec((tm, n), lambda i: (i, 0))

    kern = functools.partial(_inproj_kernel, n_heads=dims["mh"], k_scale=dims["dk"] ** -0.5,
                             n_aliased=len(aliased), n_prompt_tiles=npt)
    return pl.pallas_call(
        kern,
        grid=(m // tm,),
        in_specs=[
            row(d), mod_spec(3), mod_spec(4), wspec(ln), wspec(wqk), wspec(wv), wspec(wo), wspec(wa),
            wspec(q_norm), wspec(kv_norm), wspec(gate_bias), row(LANES), row(LANES),
        ] + [pl.BlockSpec(memory_space=pl.ANY)] * len(aliased),
        out_specs=[
            row(hdk), row(hdk), row(hdv), row(hdv), row(ql), row(kvl), row(rope), row(LANES),
            pl.BlockSpec((None, tm, kvl), lambda i: (layer, jnp.minimum(i, npt - 1), 0)),
            pl.BlockSpec((None, tm, kvl), lambda i: (layer, jnp.maximum(i - npt, 0), 0)),
        ],
        out_shape=[
            jax.ShapeDtypeStruct((m, hdk), BF), jax.ShapeDtypeStruct((m, hdk), BF),
            jax.ShapeDtypeStruct((m, hdv), BF), jax.ShapeDtypeStruct((m, hdv), F32),
            jax.ShapeDtypeStruct((m, ql), BF), jax.ShapeDtypeStruct((m, kvl), F32),
            jax.ShapeDtypeStruct((m, rope), F32), jax.ShapeDtypeStruct((m, LANES), F32),
            jax.ShapeDtypeStruct((n_layers, rows_prompt, kvl), F32),
            jax.ShapeDtypeStruct((n_layers, m - rows_prompt, kvl), F32),
        ],
        scratch_shapes=[pltpu.VMEM((tm, d), BF)],
        input_output_aliases={13 + a: 8 + a for a in range(len(aliased))},
        compiler_params=_cparams("arbitrary"),
        name="in_proj",
    )(x, mod, mod, ln, wqk, wv, wo, wa, q_norm, kv_norm, gate_bias, cos, sin, *aliased)


def _mlstm_kernel(*refs, n_heads, bt):
    tok = [refs[6 * u:6 * u + 6] for u in range(bt)]
    c0_ref, n0_ref, m0_ref, nrm_ref = refs[6 * bt:6 * bt + 4]
    hm_ref, c_ref, n_ref, m_ref = refs[-4:]
    c_idx = pl.program_id(1)

    @pl.when(c_idx == 0)
    def _():
        c_ref[...] = c0_ref[...]
        n_ref[...] = n0_ref[...]
        m_ref[...] = m0_ref[...]

    L = tok[0][0].shape[0]
    dk = tok[0][0].shape[1] // n_heads
    dv = tok[0][2].shape[1] // n_heads
    chains = [(u, h) for u in range(bt) for h in range(n_heads)]
    nc = len(chains)

    def stack(fn):
        return jnp.concatenate([fn(u, h) for u, h in chains], axis=0)

    def per_chain_last(col):
        last = col.reshape(nc, L, 1)[:, L - 1:L, :]
        return jnp.broadcast_to(last, (nc, L, 1)).reshape(nc * L, 1)

    t_idx = jnp.bitwise_and(lax.broadcasted_iota(jnp.int32, (nc * L, L), 0), L - 1)
    s_idx = lax.broadcasted_iota(jnp.int32, (nc * L, L), 1)
    causal = s_idx <= t_idx
    ig_c = stack(lambda u, h: tok[u][4][:, h:h + 1])
    lf_c = stack(lambda u, h: tok[u][4][:, n_heads + h:n_heads + h + 1])
    ig_r = stack(lambda u, h: jnp.broadcast_to(tok[u][5][h:h + 1, :], (L, L)))
    lf_r = stack(lambda u, h: jnp.broadcast_to(tok[u][5][n_heads + h:n_heads + h + 1, :], (L, L)))
    m_prev = stack(lambda u, h: jnp.broadcast_to(m_ref[u, h:h + 1, 0:1], (L, 1)))
    b_c = jnp.sum(jnp.where(causal, lf_r, 0.0), axis=1, keepdims=True)
    b_r = jnp.sum(jnp.where(causal, 0.0, lf_c).reshape(nc, L, L), axis=1, keepdims=True)
    b_r = jnp.broadcast_to(b_r, (nc, L, L)).reshape(nc * L, L) + lf_r
    b_last = per_chain_last(b_c)
    dmat = jnp.where(causal, b_c - b_r + ig_r, -jnp.inf)
    g_c = b_c + m_prev
    m_t = jnp.maximum(g_c, jnp.max(dmat, axis=1, keepdims=True))
    w_intra = jnp.exp(dmat - m_t)
    w_inter = jnp.exp(g_c - m_t)
    m_new = per_chain_last(m_t)
    wa = jnp.exp(b_last - b_c + ig_c - m_new)
    decay = jnp.exp(b_last + m_prev - m_new)

    qs = {c: tok[c[0]][0][:, c[1] * dk:(c[1] + 1) * dk] for c in chains}
    ks = {c: tok[c[0]][1][:, c[1] * dk:(c[1] + 1) * dk] for c in chains}
    vs = {c: tok[c[0]][2][:, c[1] * dv:(c[1] + 1) * dv] for c in chains}
    c_prev = {c: c_ref[c[0], c[1]] for c in chains}
    n_prev = {c: n_ref[c[0], c[1]:c[1] + 1, :] for c in chains}
    s = stack(lambda u, h: _dot_nt(qs[u, h], ks[u, h])) * w_intra
    qc = stack(lambda u, h: _dot_nt(qs[u, h], c_prev[u, h].astype(BF)))
    s_bf = s.astype(BF)
    sv = jnp.concatenate([_dot(s_bf[i * L:(i + 1) * L, :], vs[c]) for i, c in enumerate(chains)], axis=0)
    qn = jnp.sum(stack(lambda u, h: qs[u, h].astype(F32) * n_prev[u, h]), axis=1, keepdims=True)
    num = sv + w_inter * qc
    nq = jnp.sum(s, axis=1, keepdims=True) + w_inter * qn
    r = 1.0 / jnp.maximum(jnp.abs(nq), jnp.exp(-m_t))
    y = num * stack(lambda u, h: tok[u][3][:, h * dv:(h + 1) * dv])
    x = y * (r * lax.rsqrt(jnp.mean(y * y, axis=-1, keepdims=True) * (r * r) + EPS))
    wk = wa * stack(lambda u, h: ks[u, h].astype(F32))
    wk_bf = wk.astype(BF)
    for i, (u, h) in enumerate(chains):
        rows = slice(i * L, (i + 1) * L)
        hm_ref[u, :, h * dv:(h + 1) * dv] = (x[rows, :] * nrm_ref[:, h * dv:(h + 1) * dv]).astype(hm_ref.dtype)
        dec = decay[i * L:i * L + 1, :]
        c_ref[u, h] = dec * c_prev[u, h] + _dot_tn(vs[u, h], wk_bf[rows, :])
        n_ref[u, h:h + 1, :] = dec * n_prev[u, h] + jnp.sum(wk[rows, :], axis=0, keepdims=True)
        m_ref[u, h:h + 1, :] = jnp.broadcast_to(m_new[i * L:i * L + 1, :], (1, m_ref.shape[2]))


def _mlstm_call(q, k, v, og, gates_c, gates_r, c0, n0, m0, norm, layer, row0, n_seq, n_chunks, n_heads,
                state_layer, n_layers, c_stack=None):
    hdk = q.shape[1]
    hdv = v.shape[1]
    dv, dk = c0.shape[-2:]
    g2 = gates_r.shape[1]
    bt = _pick_tile(n_seq, 4)
    aliased = [] if c_stack is None else [c_stack]

    def chunked(a):
        return a.reshape(a.shape[0] // CHUNK, CHUNK, a.shape[1])

    arrays = [chunked(q), chunked(k), chunked(v), chunked(og), chunked(gates_c), gates_r]
    shapes = [(CHUNK, hdk), (CHUNK, hdk), (CHUNK, hdv), (CHUNK, hdv), (CHUNK, LANES), (g2, CHUNK)]

    def tok(u, shape):
        return pl.BlockSpec((None,) + shape, lambda t, c: (row0 + (t * bt + u) * n_chunks + c, 0, 0))

    def state(shape):
        return pl.BlockSpec((bt,) + shape, lambda t, c: (t,) + (0,) * len(shape))

    def stacked(shape, which):
        return pl.BlockSpec((None, bt) + shape, lambda t, c: (which, t) + (0,) * len(shape))

    in_specs = [tok(u, shape) for u in range(bt) for shape in shapes]
    in_specs += [stacked((n_heads, dv, dk), state_layer), stacked((n_heads, dk), state_layer),
                 stacked((n_heads, LANES), state_layer), pl.BlockSpec((None, 1, hdv), lambda t, c: (layer, 0, 0))]
    in_specs += [pl.BlockSpec(memory_space=pl.ANY)] * len(aliased)
    kern = functools.partial(_mlstm_kernel, n_heads=n_heads, bt=bt)
    return pl.pallas_call(
        kern,
        grid=(n_seq // bt, n_chunks),
        in_specs=in_specs,
        out_specs=[
            pl.BlockSpec((None, bt, CHUNK, hdv), lambda t, c: (c, t, 0, 0)),
            stacked((n_heads, dv, dk), layer), state((n_heads, dk)), state((n_heads, LANES)),
        ],
        out_shape=[
            jax.ShapeDtypeStruct((n_chunks, n_seq, CHUNK, hdv), BF),
            jax.ShapeDtypeStruct((n_layers, n_seq, n_heads, dv, dk), F32),
            jax.ShapeDtypeStruct((n_seq, n_heads, dk), F32),
            jax.ShapeDtypeStruct((n_seq, n_heads, LANES), F32),
        ],
        input_output_aliases={6 * bt + 4: 1} if aliased else {},
        compiler_params=_cparams("parallel", "arbitrary"),
        name="mlstm",
    )(*(arrays * bt), c0, n0, m0, norm, *aliased)


def _build_queries(q_sc, cq_ref, wuq_ref, wuk_ref, cos_ref, sin_ref, n_heads, nope, rope, kvl):
    tq = cq_ref.shape[0]
    qa = _dot(cq_ref[...], wuq_ref[...])
    cos = cos_ref[...]
    sin = sin_ref[...]
    for h in range(n_heads):
        qn = qa[:, h * nope:(h + 1) * nope].astype(BF)
        q_sc[h * tq:(h + 1) * tq, 0:kvl] = _dot(qn, wuk_ref[h]).astype(BF)
    per = LANES // rope
    for p in range(n_heads // per):
        base = n_heads * nope + p * LANES
        pe = _rope128(qa[:, base:base + LANES], cos, sin)
        for u in range(per):
            h = p * per + u
            q_sc[h * tq:(h + 1) * tq, kvl:kvl + rope] = pe[:, u * rope:(u + 1) * rope].astype(BF)


def _lane_tile(x, width):
    if width <= LANES:
        return x[:, :width]
    return jnp.concatenate([x] * (width // LANES), axis=1)


def _flash_step(q_sc, keys_next, s_next, s_cur, vals_cur, m_sc, l_sc, acc_sc, sm_scale, rs, mask_chunks=None,
                inline=False):
    n_groups = q_sc.shape[0] // rs

    def group(r, carry):
        r0 = pl.multiple_of(r * rs, rs)
        rows = pl.ds(r0, rs)
        if s_next is not None:
            s_next[rows, :] = _dot_nt(q_sc[rows, :], keys_next)
        if s_cur is not None:
            kb, kvl = vals_cur.shape
            s = s_cur[rows, :] * sm_scale
            if mask_chunks is not None:
                q_chunks, q_chunk0, k_chunk0 = mask_chunks
                chunk_bits = CHUNK.bit_length() - 1
                row = lax.broadcasted_iota(jnp.int32, (rs, kb), 0) + r0
                q_chunk = jnp.bitwise_and(jnp.right_shift(row, chunk_bits), q_chunks - 1) + q_chunk0
                k_chunk = jnp.right_shift(lax.broadcasted_iota(jnp.int32, (rs, kb), 1), chunk_bits) + k_chunk0
                s = jnp.where(k_chunk <= q_chunk, s, -jnp.inf)
            m_prev = m_sc[rows, :]
            m_new = jnp.maximum(m_prev, jnp.max(s, axis=1, keepdims=True))
            alpha = jnp.exp(m_prev - m_new)
            p = jnp.exp(s - _lane_tile(m_new, kb))
            l_sc[rows, :] = alpha * l_sc[rows, :] + jnp.sum(p, axis=1, keepdims=True)
            acc_sc[rows, :] = _lane_tile(alpha, kvl) * acc_sc[rows, :] + _dot(p.astype(BF), vals_cur)
            m_sc[rows, :] = m_new
        return carry

    lax.fori_loop(0, n_groups, group, 0, unroll=True if inline else (2 if n_groups % 2 == 0 else 1))


def _flash_init(m_sc, l_sc, acc_sc):
    m_sc[...] = jnp.full(m_sc.shape, -jnp.inf, F32)
    l_sc[...] = jnp.zeros(l_sc.shape, F32)
    acc_sc[...] = jnp.zeros(acc_sc.shape, F32)


def _flash_finish(o_ref, l_sc, acc_sc, wuv_ref, n_heads, tq):
    vd = wuv_ref.shape[2]
    kvl = acc_sc.shape[1]
    for h in range(n_heads):
        rows = pl.ds(h * tq, tq)
        o_lat = (acc_sc[rows, :] / _lane_tile(l_sc[rows, :], kvl)).astype(BF)
        o_ref[:, h * vd:(h + 1) * vd] = _dot(o_lat, wuv_ref[h]).astype(o_ref.dtype)


def _attn_prompt_kernel(*refs, n_heads, nope, rope, sm_scale, rs, n_riders):
    cq_ref, ckv_ref, kpe_ref, wuq_ref, wuk_ref, wuv_ref, cos_ref, sin_ref = refs[:8]
    rider_in = refs[8:8 + n_riders]
    o_ref = refs[8 + n_riders]
    rider_out = refs[9 + n_riders:9 + 2 * n_riders]
    kcat, q_sc, m_sc, l_sc, acc_sc, s_a, s_b = refs[9 + 2 * n_riders:]
    i = pl.program_id(1)
    tq = cq_ref.shape[0]
    kvl = ckv_ref.shape[1]
    kb = s_a.shape[1]

    @pl.when(i == 0)
    def _():
        kcat[:, 0:kvl] = ckv_ref[...].astype(BF)
        kcat[:, kvl:kvl + rope] = kpe_ref[...].astype(BF)

    for src, dst in zip(rider_in, rider_out):
        dst[...] = src[...].astype(dst.dtype)

    _build_queries(q_sc, cq_ref, wuq_ref, wuk_ref, cos_ref, sin_ref, n_heads, nope, rope, kvl)
    _flash_init(m_sc, l_sc, acc_sc)

    def step(j_next, s_next, j_cur, s_cur):
        keys = None if s_next is None else kcat[pl.ds(pl.multiple_of(j_next * kb, kb), kb), :]
        vals = None if s_cur is None else kcat[pl.ds(pl.multiple_of(j_cur * kb, kb), kb), 0:kvl]
        _flash_step(q_sc, keys, s_next, s_cur, vals, m_sc, l_sc, acc_sc, sm_scale, rs, inline=s_cur is None)

    last = (i * tq) // kb
    step(0, s_a, None, None)

    def pair(t, carry):
        j = 2 * t
        step(j + 1, s_b, j, s_a)
        step(j + 2, s_a, j + 1, s_b)
        return carry

    lax.fori_loop(0, last // 2, pair, 0)

    def diagonal(s_ref):
        def run(width):
            vals = kcat[pl.ds(pl.multiple_of(last * kb, kb), width), 0:kvl]
            mask = (tq // CHUNK, i * (tq // CHUNK), last * (kb // CHUNK))
            _flash_step(q_sc, None, None, s_ref.at[:, pl.ds(0, width)], vals, m_sc, l_sc, acc_sc, sm_scale, rs,
                        mask, inline=True)
            _flash_finish(o_ref, l_sc, acc_sc, wuv_ref, n_heads, tq)

        if tq < kb:
            starts_block = (i * tq) % kb == 0
            pl.when(starts_block)(lambda: run(tq))
            pl.when(jnp.logical_not(starts_block))(lambda: run(kb))
        else:
            run(kb)

    @pl.when(last % 2 == 0)
    def _():
        diagonal(s_a)

    @pl.when(last % 2 == 1)
    def _():
        step(last, s_b, last - 1, s_a)
        diagonal(s_b)


def _rider_block(rows, cols, n_steps):
    for n_row_blocks in range(n_steps, 0, -1):
        if n_steps % n_row_blocks or rows % n_row_blocks or cols % (n_steps // n_row_blocks):
            continue
        br, bc = rows // n_row_blocks, cols // (n_steps // n_row_blocks)
        if br % 16 == 0 and bc % LANES == 0:
            return br, bc
    raise ValueError(f"no aligned {n_steps}-block tiling of ({rows}, {cols})")


def _attn_prompt_call(cq, ckv, kpe, wuq, wuk, wuv, cos, sin, layer, n_seq, seq, tq, dims, riders=()):
    ah, nope, rope, kvl, vd = dims["ah"], dims["nope"], dims["rope"], dims["kvl"], dims["vd"]
    ql = cq.shape[1]
    nq = seq // tq
    kb = _pick_tile(seq, 512)
    assert tq % CHUNK == 0 and (tq // CHUNK) & (tq // CHUNK - 1) == 0 and kb % tq == 0
    rs = _pick_tile(ah * tq, 512)
    kern = functools.partial(_attn_prompt_kernel, n_heads=ah, nope=nope, rope=rope,
                             sm_scale=(nope + rope) ** -0.5, rs=rs, n_riders=len(riders))

    def wspec(w):
        return pl.BlockSpec((None,) + w.shape[1:], lambda b, i: (layer,) + (0,) * (w.ndim - 1))

    rider_in, rider_out, rider_shapes = [], [], []
    for arr, arr_layer in riders:
        _, rows, cols = arr.shape
        br, bc = _rider_block(rows, cols, n_seq * nq)
        ncb = cols // bc
        rider_in.append(pl.BlockSpec((None, br, bc), functools.partial(
            lambda b, i, ncb, arr_layer: (arr_layer, (b * nq + i) // ncb, (b * nq + i) % ncb), ncb=ncb, arr_layer=arr_layer)))
        rider_out.append(pl.BlockSpec((br, bc), functools.partial(
            lambda b, i, ncb: ((b * nq + i) // ncb, (b * nq + i) % ncb), ncb=ncb)))
        rider_shapes.append(jax.ShapeDtypeStruct((rows, cols), BF))

    return pl.pallas_call(
        kern,
        grid=(n_seq, nq),
        in_specs=[
            pl.BlockSpec((tq, ql), lambda b, i: (b * nq + i, 0)),
            pl.BlockSpec((seq, kvl), lambda b, i: (b, 0)),
            pl.BlockSpec((seq, rope), lambda b, i: (b, 0)),
            wspec(wuq), wspec(wuk), wspec(wuv),
            pl.BlockSpec((tq, LANES), lambda b, i: (b * nq + i, 0)),
            pl.BlockSpec((tq, LANES), lambda b, i: (b * nq + i, 0)),
        ] + rider_in,
        out_specs=[pl.BlockSpec((tq, ah * vd), lambda b, i: (b * nq + i, 0))] + rider_out,
        out_shape=[jax.ShapeDtypeStruct((n_seq * seq, ah * vd), BF)] + rider_shapes,
        scratch_shapes=[
            pltpu.VMEM((seq, kvl + rope), BF),
            pltpu.VMEM((ah * tq, kvl + rope), BF),
            pltpu.VMEM((ah * tq, LANES), F32),
            pltpu.VMEM((ah * tq, LANES), F32),
            pltpu.VMEM((ah * tq, kvl), F32),
            pltpu.VMEM((ah * tq, kb), F32),
            pltpu.VMEM((ah * tq, kb), F32),
        ],
        compiler_params=_cparams("parallel", "arbitrary"),
        name="mla_prompt",
    )(cq, ckv, kpe, wuq, wuk, wuv, cos, sin, *[arr for arr, _ in riders])


def _attn_sample_kernel(cq_ref, ckv_ref, kpe_ref, pckv_ref, pkpe_ref, wuq_ref, wuk_ref, wuv_ref, cos_ref, sin_ref,
                        o_ref, kcat, q_sc, m_sc, l_sc, acc_sc, s_a, s_b, s_n, *, n_heads, nope, rope, sm_scale):
    tq = cq_ref.shape[0]
    kvl = ckv_ref.shape[1]
    past = pckv_ref.shape[0]
    kb = s_a.shape[1]
    kcat[0:past, 0:kvl] = pckv_ref[...].astype(BF)
    kcat[0:past, kvl:kvl + rope] = pkpe_ref[...].T.astype(BF)
    kcat[past:past + tq, 0:kvl] = ckv_ref[...].astype(BF)
    kcat[past:past + tq, kvl:kvl + rope] = kpe_ref[...].astype(BF)
    _build_queries(q_sc, cq_ref, wuq_ref, wuk_ref, cos_ref, sin_ref, n_heads, nope, rope, kvl)
    _flash_init(m_sc, l_sc, acc_sc)
    rs = q_sc.shape[0]

    blocks = [(j * kb, kb, (s_a, s_b)[j % 2]) for j in range(past // kb)] + [(past, tq, s_n)]
    for cur, nxt in zip([None] + blocks, blocks + [None]):
        keys, s_next = (None, None) if nxt is None else (kcat[nxt[0]:nxt[0] + nxt[1], :], nxt[2])
        vals, s_cur = (None, None) if cur is None else (kcat[cur[0]:cur[0] + cur[1], 0:kvl], cur[2])
        _flash_step(q_sc, keys, s_next, s_cur, vals, m_sc, l_sc, acc_sc, sm_scale, rs)
    _flash_finish(o_ref, l_sc, acc_sc, wuv_ref, n_heads, tq)


def _attn_sample_call(cq, ckv, kpe, cache_ckv, cache_kpe, wuq, wuk, wuv, cos, sin, layer, row0, n_seq, tq, dims):
    ah, nope, rope, kvl, vd = dims["ah"], dims["nope"], dims["rope"], dims["kvl"], dims["vd"]
    ql = cq.shape[1]
    past = cache_ckv.shape[2]
    kb = math.gcd(past, 512)
    kern = functools.partial(_attn_sample_kernel, n_heads=ah, nope=nope, rope=rope,
                             sm_scale=(nope + rope) ** -0.5)

    def wspec(w):
        return pl.BlockSpec((None,) + w.shape[1:], lambda b: (layer,) + (0,) * (w.ndim - 1))

    def tok(n):
        return pl.BlockSpec((tq, n), lambda b: (row0 + b, 0))

    return pl.pallas_call(
        kern,
        grid=(n_seq,),
        in_specs=[
            tok(ql), tok(kvl), tok(rope),
            pl.BlockSpec((None, None, past, kvl), lambda b: (layer, b, 0, 0)),
            pl.BlockSpec((None, None, rope, past), lambda b: (layer, b, 0, 0)),
            wspec(wuq), wspec(wuk), wspec(wuv), tok(LANES), tok(LANES),
        ],
        out_specs=pl.BlockSpec((tq, ah * vd), lambda b: (b, 0)),
        out_shape=jax.ShapeDtypeStruct((n_seq * tq, ah * vd), BF),
        scratch_shapes=[
            pltpu.VMEM((past + tq, kvl + rope), BF),
            pltpu.VMEM((ah * tq, kvl + rope), BF),
            pltpu.VMEM((ah * tq, LANES), F32),
            pltpu.VMEM((ah * tq, LANES), F32),
            pltpu.VMEM((ah * tq, kvl), F32),
            pltpu.VMEM((ah * tq, kb), F32),
            pltpu.VMEM((ah * tq, kb), F32),
            pltpu.VMEM((ah * tq, tq), F32),
        ],
        compiler_params=_cparams("parallel"),
        name="mla_sample",
    )(cq, ckv, kpe, cache_ckv, cache_kpe, wuq, wuk, wuv, cos, sin)


def _outproj_kernel(x_ref, hmp_ref, hms_ref, oap_ref, oas_ref, gt_ref, wm_ref, wa_ref, o_ref, *, n_prompt_tiles):
    i = pl.program_id(0)
    tm = x_ref.shape[0]

    def project(hm, oa):
        y = _dot(hm, wm_ref[...]) + _dot(oa, wa_ref[...])
        for g in range(tm // CHUNK):
            rows = pl.ds(g * CHUNK, CHUNK)
            o_ref[rows, :] = x_ref[rows, :] + gt_ref[g:g + 1, :] * y[g * CHUNK:(g + 1) * CHUNK, :]

    @pl.when(i < n_prompt_tiles)
    def _():
        project(hmp_ref[...].reshape(tm, hmp_ref.shape[-1]), oap_ref[...])

    @pl.when(i >= n_prompt_tiles)
    def _():
        project(hms_ref[...], oas_ref[...])


def _outproj_call(x, hm_p, hm_s, oa_p, oa_s, mod, w_out, layer, tm, seq):
    m, d = x.shape
    wm = hm_s.shape[1]
    wa = oa_s.shape[1]
    assert wm == wa and seq % tm == 0
    ng = tm // CHUNK
    npt = oa_p.shape[0] // tm
    tps = seq // tm

    def prompt_tile(i):
        return jnp.minimum(i, npt - 1)

    def sample_tile(i):
        return jnp.maximum(i - npt, 0)

    return pl.pallas_call(
        functools.partial(_outproj_kernel, n_prompt_tiles=npt),
        grid=(m // tm,),
        in_specs=[
            pl.BlockSpec((tm, d), lambda i: (i, 0)),
            pl.BlockSpec((ng, None, CHUNK, wm), lambda i: (prompt_tile(i) % tps, prompt_tile(i) // tps, 0, 0)),
            pl.BlockSpec((tm, wm), lambda i: (sample_tile(i), 0)),
            pl.BlockSpec((tm, wa), lambda i: (prompt_tile(i), 0)),
            pl.BlockSpec((tm, wa), lambda i: (sample_tile(i), 0)),
            pl.BlockSpec((None, ng, d), lambda i: (layer, i, 5)),
            pl.BlockSpec((None, wm, d), lambda i: (layer, 0, 0)),
            pl.BlockSpec((None, wa, d), lambda i: (layer, 1, 0)),
        ],
        out_specs=pl.BlockSpec((tm, d), lambda i: (i, 0)),
        out_shape=jax.ShapeDtypeStruct((m, d), F32),
        compiler_params=_cparams("parallel"),
        name="out_proj",
    )(x, hm_p, hm_s, oa_p, oa_s, mod, w_out, w_out)


def _rope_tables(pos, rope):
    half = rope // 2
    freqs = ROPE_THETA ** (-jnp.arange(half, dtype=F32) / half)
    ang = pos.astype(F32)[:, None] * freqs[None, :]
    cos = jnp.cos(ang)
    sin = jnp.sin(ang)
    reps = LANES // rope
    return jnp.tile(jnp.concatenate([cos, cos], axis=1), (1, reps)), jnp.tile(jnp.concatenate([-sin, sin], axis=1), (1, reps))


def _pick_tile(m, cap):
    t = cap
    while m % t:
        t //= 2
    return t


def kernel(x_prompt, x_sample, c_prompt, c_sample, cache_ckv, cache_kpe, state_C, state_n, state_m, mod_w, mod_b, ln_ffn1, ffn1_w_in, ffn1_w_out, ln_mix, w_in, mlstm_b_i, mlstm_b_f, mlstm_norm, q_norm, w_uq, kv_norm, w_uk, w_uv, w_out, ln_ffn2, ffn2_w_in, ffn2_w_out, final_norm):
    bp, sp, d = x_prompt.shape
    bs, ss, _ = x_sample.shape
    depth = mod_w.shape[0]
    past = cache_ckv.shape[2]
    mh, dv, dk = state_C.shape[2:]
    kvl, ah, nope = w_uk.shape[1:]
    vd = w_uv.shape[3]
    rope = cache_kpe.shape[3]
    ql = q_norm.shape[1]
    hdk, hdv = mh * dk, mh * dv
    dff = ffn1_w_out.shape[1]
    dims = dict(mh=mh, dk=dk, dv=dv, hdk=hdk, hdv=hdv, ql=ql, kvl=kvl, rope=rope, ah=ah, nope=nope, vd=vd)
    assert sp % CHUNK == 0 and ss == CHUNK and LANES % rope == 0 and ah % (LANES // rope) == 0
    mp, ms = bp * sp, bs * ss
    m = mp + ms
    tm = _pick_tile(math.gcd(mp, ms), 512)
    tm_ffn = _pick_tile(math.gcd(mp, ms), 1024)
    ffn_tiles_p, ffn_tiles_s = mp // tm_ffn, ms // tm_ffn
    tf = _pick_tile(dff, 512)
    tq = _pick_tile(sp, 256)

    cg = jnp.concatenate([jnp.repeat(c_prompt, sp // CHUNK, axis=0), jnp.repeat(c_sample, ss // CHUNK, axis=0)], axis=0)
    pos = jnp.concatenate([jnp.tile(jnp.arange(sp), bp), jnp.tile(past + jnp.arange(ss), bs)])
    cos, sin = _rope_tables(pos, rope)

    o_mq, o_mk, o_mv, o_mo = 0, hdk, 2 * hdk, 2 * hdk + hdv
    o_mi = o_mo + hdv
    o_mf, o_qa = o_mi + mh, o_mi + 2 * mh
    o_kva, o_pe = o_qa + ql, o_qa + ql + kvl
    wqk = w_in[:, :, o_mq:o_mv].astype(BF)
    wv = w_in[:, :, o_mv:o_mo].astype(BF)
    wo = w_in[:, :, o_mo:o_mi].astype(BF)
    zpad = lambda n: jnp.zeros((depth, d, n), w_in.dtype)
    wa = jnp.concatenate([w_in[:, :, o_qa:o_pe + rope], zpad(LANES - rope), w_in[:, :, o_mi:o_qa], zpad(LANES - 2 * mh)], axis=2).astype(BF)
    gate_bias = jnp.concatenate([mlstm_b_i, mlstm_b_f, jnp.zeros((depth, LANES - 2 * mh), F32)], axis=1).reshape(depth, 1, LANES)
    wuq4 = w_uq.reshape(depth, ql, ah, nope + rope)
    wuq = jnp.concatenate([wuq4[..., :nope].reshape(depth, ql, ah * nope), wuq4[..., nope:].reshape(depth, ql, ah * rope)], axis=2).astype(BF)
    wuk = jnp.transpose(w_uk, (0, 2, 3, 1)).astype(BF)
    wuv = jnp.transpose(w_uv, (0, 2, 1, 3)).astype(BF)
    w_out_b = w_out.astype(BF)
    f1_in, f1_out = ffn1_w_in[0].astype(BF), ffn1_w_out[0].astype(BF)
    r3 = lambda a: a.reshape(depth, 1, a.shape[1])

    mod = _mod_call(cg, mod_w, mod_b)

    zeros_c = jnp.zeros((1, bp, mh, dv, dk), F32)
    zeros_n = jnp.zeros((1, bp, mh, dk), F32)
    zeros_m = jnp.zeros((1, bp, mh, LANES), F32)
    m0_s = jnp.broadcast_to(state_m[..., None], state_m.shape + (LANES,))
    cache_kpe_t = jnp.swapaxes(cache_kpe, 2, 3)

    outs = {k: [] for k in ("p_kpe", "p_n", "p_m", "s_kpe", "s_n", "s_m")}
    ckv_stacks = (jnp.zeros((depth, mp, kvl), F32), jnp.zeros((depth, ms, kvl), F32))
    p_c = jnp.zeros((depth, bp, mh, dv, dk), F32)
    s_c = jnp.zeros((depth, bs, mh, dv, dk), F32)
    for l in range(depth):
        if l == 0:
            x = _ffn_call(x_prompt.reshape(mp, d), mod, r3(ln_ffn1), f1_in, f1_out, l, 0, tm_ffn, tf,
                          into=jnp.zeros((m, d), F32))
            x = _ffn_call(x_sample.reshape(ms, d), mod, r3(ln_ffn1), f1_in, f1_out, l, 0, tm_ffn, tf,
                          stream_tile0=ffn_tiles_p, out_tile0=ffn_tiles_p, into=x)
        else:
            x = _ffn_call(x, mod, r3(ln_ffn1), f1_in, f1_out, l, 0, tm_ffn, tf)
        q, k, v, og, cq, ckv, kpe, gates, *ckv_stacks = _inproj_call(
            x, mod, r3(ln_mix), wqk, wv, wo, wa, r3(q_norm), r3(kv_norm), gate_bias, cos, sin, l, tm, dims,
            depth, mp, ckv_stacks)
        gates_r = jnp.transpose(gates[:, :2 * mh].reshape(m // CHUNK, CHUNK, 2 * mh), (0, 2, 1))
        nrm = r3(mlstm_norm)
        hm_p, p_c, n_p, m_p = _mlstm_call(q, k, v, og, gates, gates_r, zeros_c, zeros_n, zeros_m, nrm,
                                          l, 0, bp, sp // CHUNK, mh, 0, depth, p_c)
        hm_s, s_c, n_s, m_s = _mlstm_call(q, k, v, og, gates, gates_r, state_C, state_n, m0_s, nrm,
                                          l, mp // CHUNK, bs, ss // CHUNK, mh, l, depth, s_c)
        riders = [(ffn2_w_in, l), (ffn2_w_out, l)]
        if l + 1 < depth:
            riders += [(ffn1_w_in, l + 1), (ffn1_w_out, l + 1)]
        oa_p, f2_in, f2_out, *next_f1 = _attn_prompt_call(cq, ckv, kpe, wuq, wuk, wuv, cos, sin, l, bp, sp, tq, dims, riders)
        if next_f1:
            f1_in, f1_out = next_f1
        oa_s = _attn_sample_call(cq, ckv, kpe, cache_ckv, cache_kpe_t, wuq, wuk, wuv, cos, sin, l, mp // ss, bs, ss, dims)
        x = _outproj_call(x, hm_p, hm_s.reshape(ms, hdv), oa_p, oa_s, mod, w_out_b, l, tm, sp)
        if l + 1 < depth:
            x = _ffn_call(x, mod, r3(ln_ffn2), f2_in, f2_out, l, 6, tm_ffn, tf)
        else:
            fn = final_norm.reshape(1, d)
            y_prompt = _ffn_call(x, mod, r3(ln_ffn2), f2_in, f2_out, l, 6, tm_ffn, tf,
                                 n_tiles=ffn_tiles_p, final_gain=fn).reshape(bp, sp, d)
            y_sample = _ffn_call(x, mod, r3(ln_ffn2), f2_in, f2_out, l, 6, tm_ffn, tf, src_tile0=ffn_tiles_p,
                                 n_tiles=ffn_tiles_s, stream_tile0=ffn_tiles_p, final_gain=fn).reshape(bs, ss, d)
        outs["p_kpe"].append(kpe[:mp].reshape(bp, sp, rope))
        outs["s_kpe"].append(kpe[mp:].reshape(bs, ss, rope))
        outs["p_n"].append(n_p)
        outs["p_m"].append(m_p[..., 0])
        outs["s_n"].append(n_s)
        outs["s_m"].append(m_s[..., 0])

    st = {k: jnp.stack(v) for k, v in outs.items()}
    p_ckv = ckv_stacks[0].reshape(depth, bp, sp, kvl)
    s_ckv = ckv_stacks[1].reshape(depth, bs, ss, kvl)
    return (y_prompt, y_sample, p_ckv, st["p_kpe"], p_c, st["p_n"], st["p_m"],
            s_ckv, st["s_kpe"], s_c, st["s_n"], st["s_m"])
```

```python
import functools
import math

import jax
import jax.numpy as jnp
from jax import lax
from jax.experimental import pallas as pl
from jax.experimental.pallas import tpu as pltpu

CHUNK = 64
EPS = 1e-6
ROPE_THETA = 10000.0
LANES = 128
VMEM_LIMIT = 58 * 1024 * 1024

BF = jnp.bfloat16
F32 = jnp.float32


def _cparams(*sem):
    return pltpu.CompilerParams(dimension_semantics=sem, vmem_limit_bytes=VMEM_LIMIT)


def _dot(a, b):
    return jnp.dot(a, b, preferred_element_type=F32)


def _dot_nt(a, b):
    return lax.dot_general(a, b, (((1,), (1,)), ((), ())), preferred_element_type=F32)


def _dot_tn(a, b):
    return lax.dot_general(a, b, (((0,), (0,)), ((), ())), preferred_element_type=F32)


def _rms(x, g):
    ms = jnp.mean(x * x, axis=-1, keepdims=True)
    return x * lax.rsqrt(ms + EPS) * g


def _silu(x):
    return x * jax.nn.sigmoid(x)


def _rope128(pe, cos, sin):
    lane = lax.broadcasted_iota(jnp.int32, pe.shape, 1)
    first_half = jnp.bitwise_and(lane, 63) < 32
    swapped = jnp.where(first_half, pltpu.roll(pe, 96, 1), pltpu.roll(pe, 32, 1))
    return pe * cos + swapped * sin


def _mod_kernel(c_ref, w_ref, b_ref, o_ref):
    a = _silu(c_ref[...]).astype(BF)
    o_ref[...] = _dot(a, w_ref[...].astype(BF)) + b_ref[...]


def _mod_call(cg, mod_w, mod_b):
    depth, d, nd = mod_w.shape
    g = cg.shape[0]
    tn = _pick_tile(nd, 2048)
    return pl.pallas_call(
        _mod_kernel,
        grid=(depth, nd // tn),
        in_specs=[
            pl.BlockSpec((g, d), lambda l, j: (0, 0)),
            pl.BlockSpec((None, d, tn), lambda l, j: (l, 0, j)),
            pl.BlockSpec((None, 1, tn), lambda l, j: (l, 0, j)),
        ],
        out_specs=pl.BlockSpec((None, g, tn), lambda l, j: (l, 0, j)),
        out_shape=jax.ShapeDtypeStruct((depth, g, nd), F32),
        compiler_params=_cparams("parallel", "parallel"),
        name="adaln_mod",
    )(cg, mod_w, mod_b.reshape(depth, 1, nd))


def _norm_mod_to(h_ref, x_ref, ln_ref, sh_ref, sc_ref):
    ln = ln_ref[...]
    for g in range(x_ref.shape[0] // CHUNK):
        rows = pl.ds(g * CHUNK, CHUNK)
        y = _rms(x_ref[rows, :], ln)
        h_ref[rows, :] = (y * (1.0 + sc_ref[g:g + 1, :]) + sh_ref[g:g + 1, :]).astype(h_ref.dtype)


def _ffn_kernel(*refs, final_norm, aliased):
    x_ref, sh_ref, sc_ref, gt_ref, ln_ref, wg_ref, wu_ref, wo_ref = refs[:8]
    fg_ref = refs[8] if final_norm else None
    o_ref, h_sc = refs[8 + final_norm + aliased:]
    j = pl.program_id(1)
    nj = pl.num_programs(1)

    def up_down(h):
        a = (_silu(_dot(h, wg_ref[...])) * _dot(h, wu_ref[...])).astype(BF)
        return _dot(a, wo_ref[...])

    def residual(acc):
        for g in range(x_ref.shape[0] // CHUNK):
            rows = pl.ds(g * CHUNK, CHUNK)
            y = x_ref[rows, :] + (0.5 * gt_ref[g:g + 1, :]) * acc[g * CHUNK:(g + 1) * CHUNK, :]
            o_ref[rows, :] = _rms(y, fg_ref[...]) if final_norm else y

    @pl.when(j == 0)
    def _():
        _norm_mod_to(h_sc, x_ref, ln_ref, sh_ref, sc_ref)
        o_ref[...] = up_down(h_sc[...])

    @pl.when(jnp.logical_and(j > 0, j < nj - 1))
    def _():
        o_ref[...] += up_down(h_sc[...])

    @pl.when(jnp.logical_and(j > 0, j == nj - 1))
    def _():
        residual(o_ref[...] + up_down(h_sc[...]))

    @pl.when(nj == 1)
    def _():
        residual(o_ref[...])


def _ffn_call(x, mod, ln, w_in, w_out, layer, k0, tm, tf, *, src_tile0=0, n_tiles=None, stream_tile0=0,
              out_rows=None, out_tile0=0, into=None, final_gain=None):
    d = x.shape[1]
    f = w_out.shape[0]
    ng = tm // CHUNK
    nf = f // tf
    n_tiles = x.shape[0] // tm if n_tiles is None else n_tiles
    out_rows = (into.shape[0] if into is not None else n_tiles * tm) if out_rows is None else out_rows

    def mod_spec(k):
        return pl.BlockSpec((None, ng, d), lambda i, j: (layer, i + stream_tile0, k))

    in_specs = [
        pl.BlockSpec((tm, d), lambda i, j: (i + src_tile0, 0)),
        mod_spec(k0), mod_spec(k0 + 1), mod_spec(k0 + 2),
        pl.BlockSpec((None, 1, d), lambda i, j: (layer, 0, 0)),
        pl.BlockSpec((d, tf), lambda i, j: (0, j)),
        pl.BlockSpec((d, tf), lambda i, j: (0, j + nf)),
        pl.BlockSpec((tf, d), lambda i, j: (j, 0)),
    ]
    operands = [x, mod, mod, mod, ln, w_in, w_in, w_out]
    if final_gain is not None:
        in_specs.append(pl.BlockSpec((1, d), lambda i, j: (0, 0)))
        operands.append(final_gain)
    aliases = {}
    if into is not None:
        aliases = {len(operands): 0}
        in_specs.append(pl.BlockSpec(memory_space=pl.ANY))
        operands.append(into)
    kern = functools.partial(_ffn_kernel, final_norm=final_gain is not None, aliased=into is not None)
    return pl.pallas_call(
        kern,
        grid=(n_tiles, nf),
        in_specs=in_specs,
        out_specs=pl.BlockSpec((tm, d), lambda i, j: (i + out_tile0, 0)),
        out_shape=jax.ShapeDtypeStruct((out_rows, d), F32),
        scratch_shapes=[pltpu.VMEM((tm, d), BF)],
        input_output_aliases=aliases,
        compiler_params=_cparams("parallel", "arbitrary"),
        name="ffn",
    )(*operands)


def _inproj_kernel(*refs, n_heads, k_scale, n_aliased, n_prompt_tiles):
    (x_ref, sh_ref, sc_ref, ln_ref, wqk_ref, wv_ref, wo_ref, wa_ref, qn_ref, kvn_ref, gb_ref,
     cos_ref, sin_ref) = refs[:13]
    (q_ref, k_ref, v_ref, og_ref, cq_ref, ckv_ref, kpe_ref, gt_ref, ckvp_ref, ckvs_ref,
     h_sc) = refs[13 + n_aliased:]
    _norm_mod_to(h_sc, x_ref, ln_ref, sh_ref, sc_ref)
    h = h_sc[...]
    hdk = q_ref.shape[1]
    za = _dot(h, wa_ref[...])
    og_ref[...] = jax.nn.sigmoid(_dot(h, wo_ref[...]))
    zqk = _dot(h, wqk_ref[...])
    q_ref[...] = zqk[:, :hdk].astype(BF)
    k_ref[...] = (zqk[:, hdk:] * k_scale).astype(BF)
    v_ref[...] = _dot(h, wv_ref[...]).astype(BF)
    ql = cq_ref.shape[1]
    kvl = ckv_ref.shape[1]
    cq_ref[...] = _rms(za[:, :ql], qn_ref[...]).astype(BF)
    ckv = _rms(za[:, ql:ql + kvl], kvn_ref[...])
    ckv_ref[...] = ckv
    pe = _rope128(za[:, ql + kvl:ql + kvl + LANES], cos_ref[...], sin_ref[...])
    kpe_ref[...] = pe[:, :kpe_ref.shape[1]]
    zg = za[:, ql + kvl + LANES:] + gb_ref[...]
    lane = lax.broadcasted_iota(jnp.int32, zg.shape, 1)
    log_sig = jnp.minimum(zg, 0.0) - jnp.log1p(jnp.exp(-jnp.abs(zg)))
    gt_ref[...] = jnp.where(lane < n_heads, zg, log_sig)

    i = pl.program_id(0)

    @pl.when(i < n_prompt_tiles)
    def _():
        ckvp_ref[...] = ckv_ref[...]

    @pl.when(i >= n_prompt_tiles)
    def _():
        ckvs_ref[...] = ckv_ref[...]


def _inproj_call(x, mod, ln, wqk, wv, wo, wa, q_norm, kv_norm, gate_bias, cos, sin, layer, tm, dims,
                 n_layers, rows_prompt, ckv_stacks=None):
    m, d = x.shape
    ng = tm // CHUNK
    hdk, hdv, ql, kvl, rope = dims["hdk"], dims["hdv"], dims["ql"], dims["kvl"], dims["rope"]
    npt = rows_prompt // tm
    aliased = [] if ckv_stacks is None else list(ckv_stacks)

    def mod_spec(k):
        return pl.BlockSpec((None, ng, d), lambda i: (layer, i, k))

    def wspec(w):
        return pl.BlockSpec((None,) + w.shape[1:], lambda i: (layer, 0, 0))

    def row(n):
        return pl.BlockSpec((tm, n), lambda i: (i, 0))

    kern = functools.partial(_inproj_kernel, n_heads=dims["mh"], k_scale=dims["dk"] ** -0.5,
                             n_aliased=len(aliased), n_prompt_tiles=npt)
    return pl.pallas_call(
        kern,
        grid=(m // tm,),
        in_specs=[
            row(d), mod_spec(3), mod_spec(4), wspec(ln), wspec(wqk), wspec(wv), wspec(wo), wspec(wa),
            wspec(q_norm), wspec(kv_norm), wspec(gate_bias), row(LANES), row(LANES),
        ] + [pl.BlockSpec(memory_space=pl.ANY)] * len(aliased),
        out_specs=[
            row(hdk), row(hdk), row(hdv), row(hdv), row(ql), row(kvl), row(rope), row(LANES),
            pl.BlockSpec((None, tm, kvl), lambda i: (layer, jnp.minimum(i, npt - 1), 0)),
            pl.BlockSpec((None, tm, kvl), lambda i: (layer, jnp.maximum(i - npt, 0), 0)),
        ],
        out_shape=[
            jax.ShapeDtypeStruct((m, hdk), BF), jax.ShapeDtypeStruct((m, hdk), BF),
            jax.ShapeDtypeStruct((m, hdv), BF), jax.ShapeDtypeStruct((m, hdv), F32),
            jax.ShapeDtypeStruct((m, ql), BF), jax.ShapeDtypeStruct((m, kvl), F32),
            jax.ShapeDtypeStruct((m, rope), F32), jax.ShapeDtypeStruct((m, LANES), F32),
            jax.ShapeDtypeStruct((n_layers, rows_prompt, kvl), F32),
            jax.ShapeDtypeStruct((n_layers, m - rows_prompt, kvl), F32),
        ],
        scratch_shapes=[pltpu.VMEM((tm, d), BF)],
        input_output_aliases={13 + a: 8 + a for a in range(len(aliased))},
        compiler_params=_cparams("arbitrary"),
        name="in_proj",
    )(x, mod, mod, ln, wqk, wv, wo, wa, q_norm, kv_norm, gate_bias, cos, sin, *aliased)


def _mlstm_kernel(*refs, n_heads, bt):
    tok = [refs[6 * u:6 * u + 6] for u in range(bt)]
    c0_ref, n0_ref, m0_ref, nrm_ref = refs[6 * bt:6 * bt + 4]
    hm_ref, c_ref, n_ref, m_ref = refs[-4:]
    c_idx = pl.program_id(1)

    @pl.when(c_idx == 0)
    def _():
        c_ref[...] = c0_ref[...]
        n_ref[...] = n0_ref[...]
        m_ref[...] = m0_ref[...]

    L = tok[0][0].shape[0]
    dk = tok[0][0].shape[1] // n_heads
    dv = tok[0][2].shape[1] // n_heads
    chains = [(u, h) for u in range(bt) for h in range(n_heads)]
    nc = len(chains)

    def stack(fn):
        return jnp.concatenate([fn(u, h) for u, h in chains], axis=0)

    def per_chain_last(col):
        last = col.reshape(nc, L, 1)[:, L - 1:L, :]
        return jnp.broadcast_to(last, (nc, L, 1)).reshape(nc * L, 1)

    t_idx = jnp.bitwise_and(lax.broadcasted_iota(jnp.int32, (nc * L, L), 0), L - 1)
    s_idx = lax.broadcasted_iota(jnp.int32, (nc * L, L), 1)
    causal = s_idx <= t_idx
    ig_c = stack(lambda u, h: tok[u][4][:, h:h + 1])
    lf_c = stack(lambda u, h: tok[u][4][:, n_heads + h:n_heads + h + 1])
    ig_r = stack(lambda u, h: jnp.broadcast_to(tok[u][5][h:h + 1, :], (L, L)))
    lf_r = stack(lambda u, h: jnp.broadcast_to(tok[u][5][n_heads + h:n_heads + h + 1, :], (L, L)))
    m_prev = stack(lambda u, h: jnp.broadcast_to(m_ref[u, h:h + 1, 0:1], (L, 1)))
    b_c = jnp.sum(jnp.where(causal, lf_r, 0.0), axis=1, keepdims=True)
    b_r = jnp.sum(jnp.where(causal, 0.0, lf_c).reshape(nc, L, L), axis=1, keepdims=True)
    b_r = jnp.broadcast_to(b_r, (nc, L, L)).reshape(nc * L, L) + lf_r
    b_last = per_chain_last(b_c)
    dmat = jnp.where(causal, b_c - b_r + ig_r, -jnp.inf)
    g_c = b_c + m_prev
    m_t = jnp.maximum(g_c, jnp.max(dmat, axis=1, keepdims=True))
    w_intra = jnp.exp(dmat - m_t)
    w_inter = jnp.exp(g_c - m_t)
    m_new = per_chain_last(m_t)
    wa = jnp.exp(b_last - b_c + ig_c - m_new)
    decay = jnp.exp(b_last + m_prev - m_new)

    qs = {c: tok[c[0]][0][:, c[1] * dk:(c[1] + 1) * dk] for c in chains}
    ks = {c: tok[c[0]][1][:, c[1] * dk:(c[1] + 1) * dk] for c in chains}
    vs = {c: tok[c[0]][2][:, c[1] * dv:(c[1] + 1) * dv] for c in chains}
    c_prev = {c: c_ref[c[0], c[1]] for c in chains}
    n_prev = {c: n_ref[c[0], c[1]:c[1] + 1, :] for c in chains}
    s = stack(lambda u, h: _dot_nt(qs[u, h], ks[u, h])) * w_intra
    qc = stack(lambda u, h: _dot_nt(qs[u, h], c_prev[u, h].astype(BF)))
    s_bf = s.astype(BF)
    sv = jnp.concatenate([_dot(s_bf[i * L:(i + 1) * L, :], vs[c]) for i, c in enumerate(chains)], axis=0)
    qn = stack(lambda u, h: _dot_nt(qs[u, h], jnp.broadcast_to(n_prev[u, h], (8, dk)).astype(BF))[:, 0:1])
    num = sv + w_inter * qc
    nq = jnp.sum(s, axis=1, keepdims=True) + w_inter * qn
    r = 1.0 / jnp.maximum(jnp.abs(nq), jnp.exp(-m_t))
    y = num * stack(lambda u, h: tok[u][3][:, h * dv:(h + 1) * dv])
    x = y * (r * lax.rsqrt(jnp.mean(y * y, axis=-1, keepdims=True) * (r * r) + EPS))
    wk = wa * stack(lambda u, h: ks[u, h].astype(F32))
    wk_bf = wk.astype(BF)
    for i, (u, h) in enumerate(chains):
        rows = slice(i * L, (i + 1) * L)
        hm_ref[u, :, h * dv:(h + 1) * dv] = (x[rows, :] * nrm_ref[:, h * dv:(h + 1) * dv]).astype(hm_ref.dtype)
        dec = decay[i * L:i * L + 1, :]
        c_ref[u, h] = dec * c_prev[u, h] + _dot_tn(vs[u, h], wk_bf[rows, :])
        n_ref[u, h:h + 1, :] = dec * n_prev[u, h] + jnp.sum(wk[rows, :], axis=0, keepdims=True)
        m_ref[u, h:h + 1, :] = jnp.broadcast_to(m_new[i * L:i * L + 1, :], (1, m_ref.shape[2]))


def _mlstm_call(q, k, v, og, gates_c, gates_r, c0, n0, m0, norm, layer, row0, n_seq, n_chunks, n_heads,
                state_layer, n_layers, c_stack=None):
    hdk = q.shape[1]
    hdv = v.shape[1]
    dv, dk = c0.shape[-2:]
    g2 = gates_r.shape[1]
    bt = _pick_tile(n_seq, 4)
    aliased = [] if c_stack is None else [c_stack]

    def chunked(a):
        return a.reshape(a.shape[0] // CHUNK, CHUNK, a.shape[1])

    arrays = [chunked(q), chunked(k), chunked(v), chunked(og), chunked(gates_c), gates_r]
    shapes = [(CHUNK, hdk), (CHUNK, hdk), (CHUNK, hdv), (CHUNK, hdv), (CHUNK, LANES), (g2, CHUNK)]

    def tok(u, shape):
        return pl.BlockSpec((None,) + shape, lambda t, c: (row0 + (t * bt + u) * n_chunks + c, 0, 0))

    def state(shape):
        return pl.BlockSpec((bt,) + shape, lambda t, c: (t,) + (0,) * len(shape))

    def stacked(shape, which):
        return pl.BlockSpec((None, bt) + shape, lambda t, c: (which, t) + (0,) * len(shape))

    in_specs = [tok(u, shape) for u in range(bt) for shape in shapes]
    in_specs += [stacked((n_heads, dv, dk), state_layer), stacked((n_heads, dk), state_layer),
                 stacked((n_heads, LANES), state_layer), pl.BlockSpec((None, 1, hdv), lambda t, c: (layer, 0, 0))]
    in_specs += [pl.BlockSpec(memory_space=pl.ANY)] * len(aliased)
    kern = functools.partial(_mlstm_kernel, n_heads=n_heads, bt=bt)
    return pl.pallas_call(
        kern,
        grid=(n_seq // bt, n_chunks),
        in_specs=in_specs,
        out_specs=[
            pl.BlockSpec((None, bt, CHUNK, hdv), lambda t, c: (c, t, 0, 0)),
            stacked((n_heads, dv, dk), layer), state((n_heads, dk)), state((n_heads, LANES)),
        ],
        out_shape=[
            jax.ShapeDtypeStruct((n_chunks, n_seq, CHUNK, hdv), BF),
            jax.ShapeDtypeStruct((n_layers, n_seq, n_heads, dv, dk), F32),
            jax.ShapeDtypeStruct((n_seq, n_heads, dk), F32),
            jax.ShapeDtypeStruct((n_seq, n_heads, LANES), F32),
        ],
        input_output_aliases={6 * bt + 4: 1} if aliased else {},
        compiler_params=_cparams("parallel", "arbitrary"),
        name="mlstm",
    )(*(arrays * bt), c0, n0, m0, norm, *aliased)


def _build_queries(q_sc, cq_ref, wuq_ref, wuk_ref, cos_ref, sin_ref, n_heads, nope, rope, kvl):
    tq = cq_ref.shape[0]
    qa = _dot(cq_ref[...], wuq_ref[...])
    cos = cos_ref[...]
    sin = sin_ref[...]
    for h in range(n_heads):
        qn = qa[:, h * nope:(h + 1) * nope].astype(BF)
        q_sc[h * tq:(h + 1) * tq, 0:kvl] = _dot(qn, wuk_ref[h]).astype(BF)
    per = LANES // rope
    for p in range(n_heads // per):
        base = n_heads * nope + p * LANES
        pe = _rope128(qa[:, base:base + LANES], cos, sin)
        for u in range(per):
            h = p * per + u
            q_sc[h * tq:(h + 1) * tq, kvl:kvl + rope] = pe[:, u * rope:(u + 1) * rope].astype(BF)


def _lane_tile(x, width):
    if width <= LANES:
        return x[:, :width]
    return jnp.concatenate([x] * (width // LANES), axis=1)


def _flash_step(q_sc, keys_next, s_next, s_cur, vals_cur, m_sc, l_sc, acc_sc, sm_scale, rs, mask_chunks=None,
                inline=False):
    n_groups = q_sc.shape[0] // rs

    def group(r, carry):
        r0 = pl.multiple_of(r * rs, rs)
        rows = pl.ds(r0, rs)
        if s_next is not None:
            s_next[rows, :] = _dot_nt(q_sc[rows, :], keys_next)
        if s_cur is not None:
            kb, kvl = vals_cur.shape
            s = s_cur[rows, :] * sm_scale
            if mask_chunks is not None:
                q_chunks, q_chunk0, k_chunk0 = mask_chunks
                chunk_bits = CHUNK.bit_length() - 1
                row = lax.broadcasted_iota(jnp.int32, (rs, kb), 0) + r0
                q_chunk = jnp.bitwise_and(jnp.right_shift(row, chunk_bits), q_chunks - 1) + q_chunk0
                k_chunk = jnp.right_shift(lax.broadcasted_iota(jnp.int32, (rs, kb), 1), chunk_bits) + k_chunk0
                s = jnp.where(k_chunk <= q_chunk, s, -jnp.inf)
            m_prev = m_sc[rows, :]
            m_new = jnp.maximum(m_prev, jnp.max(s, axis=1, keepdims=True))
            alpha = jnp.exp(m_prev - m_new)
            p = jnp.exp(s - _lane_tile(m_new, kb))
            l_sc[rows, :] = alpha * l_sc[rows, :] + jnp.sum(p, axis=1, keepdims=True)
            acc_sc[rows, :] = _lane_tile(alpha, kvl) * acc_sc[rows, :] + _dot(p.astype(BF), vals_cur)
            m_sc[rows, :] = m_new
        return carry

    lax.fori_loop(0, n_groups, group, 0, unroll=True if inline else (2 if n_groups % 2 == 0 else 1))


def _flash_init(m_sc, l_sc, acc_sc):
    m_sc[...] = jnp.full(m_sc.shape, -jnp.inf, F32)
    l_sc[...] = jnp.zeros(l_sc.shape, F32)
    acc_sc[...] = jnp.zeros(acc_sc.shape, F32)


def _flash_finish(o_ref, l_sc, acc_sc, wuv_ref, n_heads, tq):
    vd = wuv_ref.shape[2]
    kvl = acc_sc.shape[1]
    for h in range(n_heads):
        rows = pl.ds(h * tq, tq)
        o_lat = (acc_sc[rows, :] / _lane_tile(l_sc[rows, :], kvl)).astype(BF)
        o_ref[:, h * vd:(h + 1) * vd] = _dot(o_lat, wuv_ref[h]).astype(o_ref.dtype)


def _attn_prompt_kernel(*refs, n_heads, nope, rope, sm_scale, rs, n_riders):
    cq_ref, ckv_ref, kpe_ref, wuq_ref, wuk_ref, wuv_ref, cos_ref, sin_ref = refs[:8]
    rider_in = refs[8:8 + n_riders]
    o_ref = refs[8 + n_riders]
    rider_out = refs[9 + n_riders:9 + 2 * n_riders]
    kcat, q_sc, m_sc, l_sc, acc_sc, s_a, s_b = refs[9 + 2 * n_riders:]
    i = pl.program_id(1)
    tq = cq_ref.shape[0]
    kvl = ckv_ref.shape[1]
    kb = s_a.shape[1]

    @pl.when(i == 0)
    def _():
        kcat[:, 0:kvl] = ckv_ref[...].astype(BF)
        kcat[:, kvl:kvl + rope] = kpe_ref[...].astype(BF)

    for src, dst in zip(rider_in, rider_out):
        dst[...] = src[...].astype(dst.dtype)

    _build_queries(q_sc, cq_ref, wuq_ref, wuk_ref, cos_ref, sin_ref, n_heads, nope, rope, kvl)
    _flash_init(m_sc, l_sc, acc_sc)

    def step(j_next, s_next, j_cur, s_cur):
        keys = None if s_next is None else kcat[pl.ds(pl.multiple_of(j_next * kb, kb), kb), :]
        vals = None if s_cur is None else kcat[pl.ds(pl.multiple_of(j_cur * kb, kb), kb), 0:kvl]
        _flash_step(q_sc, keys, s_next, s_cur, vals, m_sc, l_sc, acc_sc, sm_scale, rs, inline=s_cur is None)

    last = (i * tq) // kb
    step(0, s_a, None, None)

    def pair(t, carry):
        j = 2 * t
        step(j + 1, s_b, j, s_a)
        step(j + 2, s_a, j + 1, s_b)
        return carry

    lax.fori_loop(0, last // 2, pair, 0)

    def diagonal(s_ref):
        def run(width):
            vals = kcat[pl.ds(pl.multiple_of(last * kb, kb), width), 0:kvl]
            mask = (tq // CHUNK, i * (tq // CHUNK), last * (kb // CHUNK))
            _flash_step(q_sc, None, None, s_ref.at[:, pl.ds(0, width)], vals, m_sc, l_sc, acc_sc, sm_scale, rs,
                        mask, inline=True)
            _flash_finish(o_ref, l_sc, acc_sc, wuv_ref, n_heads, tq)

        if tq < kb:
            starts_block = (i * tq) % kb == 0
            pl.when(starts_block)(lambda: run(tq))
            pl.when(jnp.logical_not(starts_block))(lambda: run(kb))
        else:
            run(kb)

    @pl.when(last % 2 == 0)
    def _():
        diagonal(s_a)

    @pl.when(last % 2 == 1)
    def _():
        step(last, s_b, last - 1, s_a)
        diagonal(s_b)


def _rider_block(rows, cols, n_steps):
    for n_row_blocks in range(n_steps, 0, -1):
        if n_steps % n_row_blocks or rows % n_row_blocks or cols % (n_steps // n_row_blocks):
            continue
        br, bc = rows // n_row_blocks, cols // (n_steps // n_row_blocks)
        if br % 16 == 0 and bc % LANES == 0:
            return br, bc
    raise ValueError(f"no aligned {n_steps}-block tiling of ({rows}, {cols})")


def _attn_prompt_call(cq, ckv, kpe, wuq, wuk, wuv, cos, sin, layer, n_seq, seq, tq, dims, riders=()):
    ah, nope, rope, kvl, vd = dims["ah"], dims["nope"], dims["rope"], dims["kvl"], dims["vd"]
    ql = cq.shape[1]
    nq = seq // tq
    kb = _pick_tile(seq, 512)
    assert tq % CHUNK == 0 and (tq // CHUNK) & (tq // CHUNK - 1) == 0 and kb % tq == 0
    rs = _pick_tile(ah * tq, 512)
    kern = functools.partial(_attn_prompt_kernel, n_heads=ah, nope=nope, rope=rope,
                             sm_scale=(nope + rope) ** -0.5, rs=rs, n_riders=len(riders))

    def wspec(w):
        return pl.BlockSpec((None,) + w.shape[1:], lambda b, i: (layer,) + (0,) * (w.ndim - 1))

    rider_in, rider_out, rider_shapes = [], [], []
    for arr, arr_layer in riders:
        _, rows, cols = arr.shape
        br, bc = _rider_block(rows, cols, n_seq * nq)
        ncb = cols // bc
        rider_in.append(pl.BlockSpec((None, br, bc), functools.partial(
            lambda b, i, ncb, arr_layer: (arr_layer, (b * nq + i) // ncb, (b * nq + i) % ncb), ncb=ncb, arr_layer=arr_layer)))
        rider_out.append(pl.BlockSpec((br, bc), functools.partial(
            lambda b, i, ncb: ((b * nq + i) // ncb, (b * nq + i) % ncb), ncb=ncb)))
        rider_shapes.append(jax.ShapeDtypeStruct((rows, cols), BF))

    return pl.pallas_call(
        kern,
        grid=(n_seq, nq),
        in_specs=[
            pl.BlockSpec((tq, ql), lambda b, i: (b * nq + i, 0)),
            pl.BlockSpec((seq, kvl), lambda b, i: (b, 0)),
            pl.BlockSpec((seq, rope), lambda b, i: (b, 0)),
            wspec(wuq), wspec(wuk), wspec(wuv),
            pl.BlockSpec((tq, LANES), lambda b, i: (b * nq + i, 0)),
            pl.BlockSpec((tq, LANES), lambda b, i: (b * nq + i, 0)),
        ] + rider_in,
        out_specs=[pl.BlockSpec((tq, ah * vd), lambda b, i: (b * nq + i, 0))] + rider_out,
        out_shape=[jax.ShapeDtypeStruct((n_seq * seq, ah * vd), BF)] + rider_shapes,
        scratch_shapes=[
            pltpu.VMEM((seq, kvl + rope), BF),
            pltpu.VMEM((ah * tq, kvl + rope), BF),
            pltpu.VMEM((ah * tq, LANES), F32),
            pltpu.VMEM((ah * tq, LANES), F32),
            pltpu.VMEM((ah * tq, kvl), F32),
            pltpu.VMEM((ah * tq, kb), F32),
            pltpu.VMEM((ah * tq, kb), F32),
        ],
        compiler_params=_cparams("parallel", "arbitrary"),
        name="mla_prompt",
    )(cq, ckv, kpe, wuq, wuk, wuv, cos, sin, *[arr for arr, _ in riders])


def _attn_sample_kernel(cq_ref, ckv_ref, kpe_ref, pckv_ref, pkpe_ref, wuq_ref, wuk_ref, wuv_ref, cos_ref, sin_ref,
                        o_ref, kcat, q_sc, m_sc, l_sc, acc_sc, s_a, s_b, s_n, *, n_heads, nope, rope, sm_scale):
    tq = cq_ref.shape[0]
    kvl = ckv_ref.shape[1]
    past = pckv_ref.shape[0]
    kb = s_a.shape[1]
    kcat[0:past, 0:kvl] = pckv_ref[...].astype(BF)
    kcat[0:past, kvl:kvl + rope] = pkpe_ref[...].T.astype(BF)
    kcat[past:past + tq, 0:kvl] = ckv_ref[...].astype(BF)
    kcat[past:past + tq, kvl:kvl + rope] = kpe_ref[...].astype(BF)
    _build_queries(q_sc, cq_ref, wuq_ref, wuk_ref, cos_ref, sin_ref, n_heads, nope, rope, kvl)
    _flash_init(m_sc, l_sc, acc_sc)
    rs = q_sc.shape[0]

    blocks = [(j * kb, kb, (s_a, s_b)[j % 2]) for j in range(past // kb)] + [(past, tq, s_n)]
    for cur, nxt in zip([None] + blocks, blocks + [None]):
        keys, s_next = (None, None) if nxt is None else (kcat[nxt[0]:nxt[0] + nxt[1], :], nxt[2])
        vals, s_cur = (None, None) if cur is None else (kcat[cur[0]:cur[0] + cur[1], 0:kvl], cur[2])
        _flash_step(q_sc, keys, s_next, s_cur, vals, m_sc, l_sc, acc_sc, sm_scale, rs)
    _flash_finish(o_ref, l_sc, acc_sc, wuv_ref, n_heads, tq)


def _attn_sample_call(cq, ckv, kpe, cache_ckv, cache_kpe, wuq, wuk, wuv, cos, sin, layer, row0, n_seq, tq, dims):
    ah, nope, rope, kvl, vd = dims["ah"], dims["nope"], dims["rope"], dims["kvl"], dims["vd"]
    ql = cq.shape[1]
    past = cache_ckv.shape[2]
    kb = math.gcd(past, 512)
    kern = functools.partial(_attn_sample_kernel, n_heads=ah, nope=nope, rope=rope,
                             sm_scale=(nope + rope) ** -0.5)

    def wspec(w):
        return pl.BlockSpec((None,) + w.shape[1:], lambda b: (layer,) + (0,) * (w.ndim - 1))

    def tok(n):
        return pl.BlockSpec((tq, n), lambda b: (row0 + b, 0))

    return pl.pallas_call(
        kern,
        grid=(n_seq,),
        in_specs=[
            tok(ql), tok(kvl), tok(rope),
            pl.BlockSpec((None, None, past, kvl), lambda b: (layer, b, 0, 0)),
            pl.BlockSpec((None, None, rope, past), lambda b: (layer, b, 0, 0)),
            wspec(wuq), wspec(wuk), wspec(wuv), tok(LANES), tok(LANES),
        ],
        out_specs=pl.BlockSpec((tq, ah * vd), lambda b: (b, 0)),
        out_shape=jax.ShapeDtypeStruct((n_seq * tq, ah * vd), BF),
        scratch_shapes=[
            pltpu.VMEM((past + tq, kvl + rope), BF),
            pltpu.VMEM((ah * tq, kvl + rope), BF),
            pltpu.VMEM((ah * tq, LANES), F32),
            pltpu.VMEM((ah * tq, LANES), F32),
            pltpu.VMEM((ah * tq, kvl), F32),
            pltpu.VMEM((ah * tq, kb), F32),
            pltpu.VMEM((ah * tq, kb), F32),
            pltpu.VMEM((ah * tq, tq), F32),
        ],
        compiler_params=_cparams("parallel"),
        name="mla_sample",
    )(cq, ckv, kpe, cache_ckv, cache_kpe, wuq, wuk, wuv, cos, sin)


def _outproj_kernel(x_ref, hmp_ref, hms_ref, oap_ref, oas_ref, gt_ref, wm_ref, wa_ref, o_ref, *, n_prompt_tiles):
    i = pl.program_id(0)
    tm = x_ref.shape[0]

    def project(hm, oa):
        y = _dot(hm, wm_ref[...]) + _dot(oa, wa_ref[...])
        for g in range(tm // CHUNK):
            rows = pl.ds(g * CHUNK, CHUNK)
            o_ref[rows, :] = x_ref[rows, :] + gt_ref[g:g + 1, :] * y[g * CHUNK:(g + 1) * CHUNK, :]

    @pl.when(i < n_prompt_tiles)
    def _():
        project(hmp_ref[...].reshape(tm, hmp_ref.shape[-1]), oap_ref[...])

    @pl.when(i >= n_prompt_tiles)
    def _():
        project(hms_ref[...], oas_ref[...])


def _outproj_call(x, hm_p, hm_s, oa_p, oa_s, mod, w_out, layer, tm, seq):
    m, d = x.shape
    wm = hm_s.shape[1]
    wa = oa_s.shape[1]
    assert wm == wa and seq % tm == 0
    ng = tm // CHUNK
    npt = oa_p.shape[0] // tm
    tps = seq // tm

    def prompt_tile(i):
        return jnp.minimum(i, npt - 1)

    def sample_tile(i):
        return jnp.maximum(i - npt, 0)

    return pl.pallas_call(
        functools.partial(_outproj_kernel, n_prompt_tiles=npt),
        grid=(m // tm,),
        in_specs=[
            pl.BlockSpec((tm, d), lambda i: (i, 0)),
            pl.BlockSpec((ng, None, CHUNK, wm), lambda i: (prompt_tile(i) % tps, prompt_tile(i) // tps, 0, 0)),
            pl.BlockSpec((tm, wm), lambda i: (sample_tile(i), 0)),
            pl.BlockSpec((tm, wa), lambda i: (prompt_tile(i), 0)),
            pl.BlockSpec((tm, wa), lambda i: (sample_tile(i), 0)),
            pl.BlockSpec((None, ng, d), lambda i: (layer, i, 5)),
            pl.BlockSpec((None, wm, d), lambda i: (layer, 0, 0)),
            pl.BlockSpec((None, wa, d), lambda i: (layer, 1, 0)),
        ],
        out_specs=pl.BlockSpec((tm, d), lambda i: (i, 0)),
        out_shape=jax.ShapeDtypeStruct((m, d), F32),
        compiler_params=_cparams("parallel"),
        name="out_proj",
    )(x, hm_p, hm_s, oa_p, oa_s, mod, w_out, w_out)


def _rope_tables(pos, rope):
    half = rope // 2
    freqs = ROPE_THETA ** (-jnp.arange(half, dtype=F32) / half)
    ang = pos.astype(F32)[:, None] * freqs[None, :]
    cos = jnp.cos(ang)
    sin = jnp.sin(ang)
    reps = LANES // rope
    return jnp.tile(jnp.concatenate([cos, cos], axis=1), (1, reps)), jnp.tile(jnp.concatenate([-sin, sin], axis=1), (1, reps))


def _pick_tile(m, cap):
    t = cap
    while m % t:
        t //= 2
    return t


def kernel(x_prompt, x_sample, c_prompt, c_sample, cache_ckv, cache_kpe, state_C, state_n, state_m, mod_w, mod_b, ln_ffn1, ffn1_w_in, ffn1_w_out, ln_mix, w_in, mlstm_b_i, mlstm_b_f, mlstm_norm, q_norm, w_uq, kv_norm, w_uk, w_uv, w_out, ln_ffn2, ffn2_w_in, ffn2_w_out, final_norm):
    bp, sp, d = x_prompt.shape
    bs, ss, _ = x_sample.shape
    depth = mod_w.shape[0]
    past = cache_ckv.shape[2]
    mh, dv, dk = state_C.shape[2:]
    kvl, ah, nope = w_uk.shape[1:]
    vd = w_uv.shape[3]
    rope = cache_kpe.shape[3]
    ql = q_norm.shape[1]
    hdk, hdv = mh * dk, mh * dv
    dff = ffn1_w_out.shape[1]
    dims = dict(mh=mh, dk=dk, dv=dv, hdk=hdk, hdv=hdv, ql=ql, kvl=kvl, rope=rope, ah=ah, nope=nope, vd=vd)
    assert sp % CHUNK == 0 and ss == CHUNK and LANES % rope == 0 and ah % (LANES // rope) == 0
    mp, ms = bp * sp, bs * ss
    m = mp + ms
    tm = _pick_tile(math.gcd(mp, ms), 512)
    tm_ffn = _pick_tile(math.gcd(mp, ms), 1024)
    ffn_tiles_p, ffn_tiles_s = mp // tm_ffn, ms // tm_ffn
    tf = _pick_tile(dff, 512)
    tq = _pick_tile(sp, 256)

    cg = jnp.concatenate([jnp.repeat(c_prompt, sp // CHUNK, axis=0), jnp.repeat(c_sample, ss // CHUNK, axis=0)], axis=0)
    pos = jnp.concatenate([jnp.tile(jnp.arange(sp), bp), jnp.tile(past + jnp.arange(ss), bs)])
    cos, sin = _rope_tables(pos, rope)

    o_mq, o_mk, o_mv, o_mo = 0, hdk, 2 * hdk, 2 * hdk + hdv
    o_mi = o_mo + hdv
    o_mf, o_qa = o_mi + mh, o_mi + 2 * mh
    o_kva, o_pe = o_qa + ql, o_qa + ql + kvl
    wqk = w_in[:, :, o_mq:o_mv].astype(BF)
    wv = w_in[:, :, o_mv:o_mo].astype(BF)
    wo = w_in[:, :, o_mo:o_mi].astype(BF)
    zpad = lambda n: jnp.zeros((depth, d, n), w_in.dtype)
    wa = jnp.concatenate([w_in[:, :, o_qa:o_pe + rope], zpad(LANES - rope), w_in[:, :, o_mi:o_qa], zpad(LANES - 2 * mh)], axis=2).astype(BF)
    gate_bias = jnp.concatenate([mlstm_b_i, mlstm_b_f, jnp.zeros((depth, LANES - 2 * mh), F32)], axis=1).reshape(depth, 1, LANES)
    wuq4 = w_uq.reshape(depth, ql, ah, nope + rope)
    wuq = jnp.concatenate([wuq4[..., :nope].reshape(depth, ql, ah * nope), wuq4[..., nope:].reshape(depth, ql, ah * rope)], axis=2).astype(BF)
    wuk = jnp.transpose(w_uk, (0, 2, 3, 1)).astype(BF)
    wuv = jnp.transpose(w_uv, (0, 2, 1, 3)).astype(BF)
    w_out_b = w_out.astype(BF)
    f1_in, f1_out = ffn1_w_in[0].astype(BF), ffn1_w_out[0].astype(BF)
    r3 = lambda a: a.reshape(depth, 1, a.shape[1])

    mod = _mod_call(cg, mod_w, mod_b)

    zeros_c = jnp.zeros((1, bp, mh, dv, dk), F32)
    zeros_n = jnp.zeros((1, bp, mh, dk), F32)
    zeros_m = jnp.zeros((1, bp, mh, LANES), F32)
    m0_s = jnp.broadcast_to(state_m[..., None], state_m.shape + (LANES,))
    cache_kpe_t = jnp.swapaxes(cache_kpe, 2, 3)

    outs = {k: [] for k in ("p_kpe", "p_n", "p_m", "s_kpe", "s_n", "s_m")}
    ckv_stacks = (jnp.zeros((depth, mp, kvl), F32), jnp.zeros((depth, ms, kvl), F32))
    p_c = jnp.zeros((depth, bp, mh, dv, dk), F32)
    s_c = jnp.zeros((depth, bs, mh, dv, dk), F32)
    for l in range(depth):
        if l == 0:
            x = _ffn_call(x_prompt.reshape(mp, d), mod, r3(ln_ffn1), f1_in, f1_out, l, 0, tm_ffn, tf,
                          into=jnp.zeros((m, d), F32))
            x = _ffn_call(x_sample.reshape(ms, d), mod, r3(ln_ffn1), f1_in, f1_out, l, 0, tm_ffn, tf,
                          stream_tile0=ffn_tiles_p, out_tile0=ffn_tiles_p, into=x)
        else:
            x = _ffn_call(x, mod, r3(ln_ffn1), f1_in, f1_out, l, 0, tm_ffn, tf)
        q, k, v, og, cq, ckv, kpe, gates, *ckv_stacks = _inproj_call(
            x, mod, r3(ln_mix), wqk, wv, wo, wa, r3(q_norm), r3(kv_norm), gate_bias, cos, sin, l, tm, dims,
            depth, mp, ckv_stacks)
        gates_r = jnp.transpose(gates[:, :2 * mh].reshape(m // CHUNK, CHUNK, 2 * mh), (0, 2, 1))
        nrm = r3(mlstm_norm)
        hm_p, p_c, n_p, m_p = _mlstm_call(q, k, v, og, gates, gates_r, zeros_c, zeros_n, zeros_m, nrm,
                                          l, 0, bp, sp // CHUNK, mh, 0, depth, p_c)
        hm_s, s_c, n_s, m_s = _mlstm_call(q, k, v, og, gates, gates_r, state_C, state_n, m0_s, nrm,
                                          l, mp // CHUNK, bs, ss // CHUNK, mh, l, depth, s_c)
        riders = [(ffn2_w_in, l), (ffn2_w_out, l)]
        if l + 1 < depth:
            riders += [(ffn1_w_in, l + 1), (ffn1_w_out, l + 1)]
        oa_p, f2_in, f2_out, *next_f1 = _attn_prompt_call(cq, ckv, kpe, wuq, wuk, wuv, cos, sin, l, bp, sp, tq, dims, riders)
        if next_f1:
            f1_in, f1_out = next_f1
        oa_s = _attn_sample_call(cq, ckv, kpe, cache_ckv, cache_kpe_t, wuq, wuk, wuv, cos, sin, l, mp // ss, bs, ss, dims)
        x = _outproj_call(x, hm_p, hm_s.reshape(ms, hdv), oa_p, oa_s, mod, w_out_b, l, tm, sp)
        if l + 1 < depth:
            x = _ffn_call(x, mod, r3(ln_ffn2), f2_in, f2_out, l, 6, tm_ffn, tf)
        else:
            fn = final_norm.reshape(1, d)
            y_prompt = _ffn_call(x, mod, r3(ln_ffn2), f2_in, f2_out, l, 6, tm_ffn, tf,
                                 n_tiles=ffn_tiles_p, final_gain=fn).reshape(bp, sp, d)
            y_sample = _ffn_call(x, mod, r3(ln_ffn2), f2_in, f2_out, l, 6, tm_ffn, tf, src_tile0=ffn_tiles_p,
                                 n_tiles=ffn_tiles_s, stream_tile0=ffn_tiles_p, final_gain=fn).reshape(bs, ss, d)
        outs["p_kpe"].append(kpe[:mp].reshape(bp, sp, rope))
        outs["s_kpe"].append(kpe[mp:].reshape(bs, ss, rope))
        outs["p_n"].append(n_p)
        outs["p_m"].append(m_p[..., 0])
        outs["s_n"].append(n_s)
        outs["s_m"].append(m_s[..., 0])

    st = {k: jnp.stack(v) for k, v in outs.items()}
    p_ckv = ckv_stacks[0].reshape(depth, bp, sp, kvl)
    s_ckv = ckv_stacks[1].reshape(depth, bs, ss, kvl)
    return (y_prompt, y_sample, p_ckv, st["p_kpe"], p_c, st["p_n"], st["p_m"],
            s_ckv, st["s_kpe"], s_c, st["s_n"], st["s_m"])
```
